```python
import jax
import jax.numpy as jnp
from jax import lax
import numpy as np


D_MODEL = 1024
BATCH = 4
SEQ = 8192
DEPTH = 1

N_META = 16
CHUNK = 128
PAD_FRONT = CHUNK - N_META
D_MIX = D_MODEL
RET_HEADS = 4
RET_DK = D_MIX // 2 // RET_HEADS
RET_DV = D_MIX // 2 // RET_HEADS
MLSTM_HEADS = 4
MLSTM_DH = D_MIX // 2 // MLSTM_HEADS
CONV_K = 5
N_EXPERTS = 16
EC_CAPACITY = 2
D_FF = 128 * ((8 * D_MODEL // 3 + 127) // 128)
ROPE_BASE = 10000.0
EPS = 1e-6
NEG = -1e30
RET_QK = RET_HEADS * RET_DK
RET_V = RET_HEADS * RET_DV
ML_W = MLSTM_HEADS * MLSTM_DH
N_GATES = 4 * MLSTM_HEADS
PROJ_SIZES = (RET_QK, RET_QK, RET_V, RET_V, ML_W, ML_W, ML_W, ML_W, N_GATES)
PROJ_COLS = sum(PROJ_SIZES)
SPLIT_AT = tuple(int(s) for s in np.cumsum(PROJ_SIZES)[:-1])

kernel_name = 'hybrid_retention_mlstm_ec_block'


def _rmsnorm(x, g):
    xf = x.astype(jnp.float32)
    y = xf * lax.rsqrt(jnp.mean(xf * xf, axis=-1, keepdims=True) + EPS)
    return (y * g.astype(jnp.float32)).astype(x.dtype)


def _head_norm(y, g):
    mu = jnp.mean(y, axis=-1, keepdims=True)
    yc = y - mu
    var = jnp.mean(yc * yc, axis=-1, keepdims=True)
    out = (yc * lax.rsqrt(var + EPS)).reshape(y.shape[:2] + (-1,))
    return out * g.astype(jnp.float32)


def _rotary(x, pos):
    half = x.shape[-1] // 2
    inv = ROPE_BASE ** (-jnp.arange(half, dtype=jnp.float32) / half)
    ang = pos[:, None] * inv[None, :]
    cos = jnp.cos(ang)[None, :, None, :]
    sin = jnp.sin(ang)[None, :, None, :]
    x1, x2 = x[..., :half], x[..., half:]
    return jnp.concatenate([x1 * cos - x2 * sin, x2 * cos + x1 * sin], axis=-1)


def _pad_front(t, value=0.0):
    widths = [(0, 0)] * t.ndim
    widths[1] = (PAD_FRONT, 0)
    return jnp.pad(t, widths, constant_values=value)


def _flip(t):
    return jnp.flip(t, axis=1)


def _centred_conv(u, w, b):
    c = u.shape[-1]
    out = lax.conv_general_dilated(
        u, w.astype(u.dtype)[:, None, :], window_strides=(1,),
        padding=((CONV_K // 2, CONV_K // 2),),
        dimension_numbers=('NWC', 'WIO', 'NWC'), feature_group_count=c)
    return out + b.astype(u.dtype)


def _retention_dir(q, k, v, log_gamma):
    b_, t_, h_, dk = q.shape
    dv = v.shape[-1]
    nc = t_ // CHUNK
    q = q.reshape(b_, nc, CHUNK, h_, dk)
    k = k.reshape(b_, nc, CHUNK, h_, dk)
    v = v.reshape(b_, nc, CHUNK, h_, dv)
    idx = jnp.arange(CHUNK, dtype=jnp.float32)
    diff = idx[:, None] - idx[None, :]
    decay = jnp.where(diff >= 0, jnp.exp(log_gamma[:, None, None] * jnp.maximum(diff, 0.0)), 0.0)
    scores = jnp.einsum('bclhd,bcmhd->bchlm', q, k) * decay
    intra = jnp.einsum('bchlm,bcmhe->bclhe', scores, v)
    zeta = jnp.exp(log_gamma[:, None] * (CHUNK - 1.0 - idx)[None, :])
    upd = jnp.einsum('bclhd,bclhe,hl->bchde', k, v, zeta)
    g_chunk = jnp.exp(log_gamma * CHUNK)[:, None, None]

    def step(state, u_c):
        return g_chunk * state + u_c, state

    r0 = jnp.zeros((b_, h_, dk, dv), jnp.float32)
    _, r_prev = lax.scan(step, r0, jnp.moveaxis(upd, 1, 0))
    r_prev = jnp.moveaxis(r_prev, 0, 1)
    xi = jnp.exp(log_gamma[:, None] * (idx + 1.0)[None, :])
    cross = jnp.einsum('bclhd,bchde,hl->bclhe', q, r_prev, xi)
    return (intra + cross).reshape(b_, t_, h_, dv)


def _mlstm_dir(q, k, v, log_i, log_f):
    b_, t_, h_, d = q.shape
    nc = t_ // CHUNK
    q = q.reshape(b_, nc, CHUNK, h_, d)
    k = k.reshape(b_, nc, CHUNK, h_, d)
    v = v.reshape(b_, nc, CHUNK, h_, d)
    li = jnp.moveaxis(log_i.reshape(b_, nc, CHUNK, h_), 3, 2)
    lf = jnp.moveaxis(log_f.reshape(b_, nc, CHUNK, h_), 3, 2)
    bcum = jnp.cumsum(lf, axis=-1)
    b_last = bcum[..., -1]
    causal = jnp.tril(jnp.ones((CHUNK, CHUNK), dtype=bool))
    log_d = jnp.where(causal, bcum[..., :, None] - bcum[..., None, :] + li[..., None, :], NEG)
    log_u = b_last[..., None] - bcum + li
    a = jnp.max(log_u, axis=-1)
    w_u = jnp.exp(log_u - a[..., None])
    upd_c = jnp.einsum('bchl,bclhe,bclhd->bched', w_u, v, k)
    upd_n = jnp.einsum('bchl,bclhd->bchd', w_u, k)

    def step(carry, inp):
        c_st, n_st, m_st = carry
        uc, un, bl, ac = inp
        m_new = jnp.maximum(bl + m_st, ac)
        f = jnp.exp(bl + m_st - m_new)
        g = jnp.exp(ac - m_new)
        c_new = f[..., None, None] * c_st + g[..., None, None] * uc
        n_new = f[..., None] * n_st + g[..., None] * un
        return (c_new, n_new, m_new), (c_st, n_st, m_st)

    init = (jnp.zeros((b_, h_, d, d), jnp.float32),
            jnp.zeros((b_, h_, d), jnp.float32),
            jnp.zeros((b_, h_), jnp.float32))
    xs = (jnp.moveaxis(upd_c, 1, 0), jnp.moveaxis(upd_n, 1, 0),
          jnp.moveaxis(b_last, 1, 0), jnp.moveaxis(a, 1, 0))
    _, (c_prev, n_prev, m_prev) = lax.scan(step, init, xs)
    c_prev = jnp.moveaxis(c_prev, 0, 1)
    n_prev = jnp.moveaxis(n_prev, 0, 1)
    m_prev = jnp.moveaxis(m_prev, 0, 1)
    log_inter = bcum + m_prev[..., None]
    m_row = jnp.maximum(log_inter, jnp.max(log_d, axis=-1))
    s = jnp.einsum('bclhd,bcmhd->bchlm', q, k) * jnp.exp(log_d - m_row[..., None])
    w_inter = jnp.exp(log_inter - m_row)
    num = (jnp.einsum('bchlm,bcmhe->bclhe', s, v)
           + jnp.einsum('bclhd,bched,bchl->bclhe', q, c_prev, w_inter))
    den = jnp.sum(s, axis=-1) + jnp.einsum('bclhd,bchd,bchl->bchl', q, n_prev, w_inter)
    denom = jnp.maximum(jnp.abs(den), jnp.exp(-m_row))
    h = num / jnp.moveaxis(denom, 2, 3)[..., None]
    return h.reshape(b_, t_, h_, d)


def _token_mixer(h, norm_g, w_in, b_gates, conv_w, conv_b, ret_decay_logit,
                 ret_gn_g, mlstm_gn_g, w_out):
    b_, n_, _ = h.shape
    f32 = jnp.float32
    u = _rmsnorm(h, norm_g)
    proj = jnp.einsum('bnd,dk->bnk', u, w_in)
    rq, rk, rv, rg, mq, mk, mv, mo, gate_pre = jnp.split(proj, SPLIT_AT, axis=-1)

    pos = jnp.arange(n_, dtype=f32)
    rq = _rotary(rq.reshape(b_, n_, RET_HEADS, RET_DK).astype(f32), pos) * (RET_DK ** -0.5)
    rk = _rotary(rk.reshape(b_, n_, RET_HEADS, RET_DK).astype(f32), pos)
    rv = rv.reshape(b_, n_, RET_HEADS, RET_DV).astype(f32)
    rq, rk, rv = _pad_front(rq), _pad_front(rk), _pad_front(rv)
    log_gamma = jax.nn.log_sigmoid(ret_decay_logit.astype(f32))
    ret = (_retention_dir(rq, rk, rv, log_gamma[0])
           + _flip(_retention_dir(_flip(rq), _flip(rk), _flip(rv), log_gamma[1])))
    ret = ret[:, PAD_FRONT:]
    y_ret = _head_norm(ret, ret_gn_g) * jax.nn.silu(rg.astype(f32))

    qk = jax.nn.silu(_centred_conv(jnp.concatenate([mq, mk], axis=-1), conv_w, conv_b))
    mq, mk = jnp.split(qk, 2, axis=-1)
    mq = mq.reshape(b_, n_, MLSTM_HEADS, MLSTM_DH).astype(f32) * (MLSTM_DH ** -0.5)
    mk = mk.reshape(b_, n_, MLSTM_HEADS, MLSTM_DH).astype(f32)
    mv = mv.reshape(b_, n_, MLSTM_HEADS, MLSTM_DH).astype(f32)
    gts = gate_pre.astype(f32) + b_gates.astype(f32)
    log_i = gts[..., :2 * MLSTM_HEADS].reshape(b_, n_, 2, MLSTM_HEADS)
    log_f = jax.nn.log_sigmoid(gts[..., 2 * MLSTM_HEADS:]).reshape(b_, n_, 2, MLSTM_HEADS)
    mq, mk, mv = _pad_front(mq), _pad_front(mk), _pad_front(mv)
    log_i = _pad_front(log_i, NEG)
    log_f = _pad_front(log_f, 0.0)
    h_f = _mlstm_dir(mq, mk, mv, log_i[:, :, 0], log_f[:, :, 0])
    h_b = _flip(_mlstm_dir(_flip(mq), _flip(mk), _flip(mv),
                           _flip(log_i[:, :, 1]), _flip(log_f[:, :, 1])))
    hm = (h_f + h_b)[:, PAD_FRONT:]
    hm = jax.nn.sigmoid(mo.astype(f32)).reshape(b_, n_, MLSTM_HEADS, MLSTM_DH) * hm
    y_m = _head_norm(hm, mlstm_gn_g)

    y = jnp.concatenate([y_ret, y_m], axis=-1).astype(h.dtype)
    return jnp.einsum('bnk,kd->bnd', y, w_out)


def _expert_choice_ffn(h, norm_g, w_router, w_gate, w_up, w_down):
    b_, n_, _ = h.shape
    u = _rmsnorm(h, norm_g)
    aff = jax.nn.softmax(jnp.einsum('bnd,de->bne', u, w_router).astype(jnp.float32), axis=-1)
    cap = EC_CAPACITY * n_ // N_EXPERTS
    g, idx = lax.top_k(jnp.swapaxes(aff, 1, 2), cap)
    xs = jax.vmap(lambda ub, ib: ub[ib])(u, idx)
    hid = (jax.nn.silu(jnp.einsum('becd,edf->becf', xs, w_gate))
           * jnp.einsum('becd,edf->becf', xs, w_up))
    y = jnp.einsum('becf,efd->becd', hid, w_down) * g[..., None].astype(h.dtype)
    return jnp.zeros_like(h).at[jnp.arange(b_)[:, None, None], idx].add(y)


def setup_inputs(seed: int = 0) -> dict:
    key = jax.random.key(seed)
    ks = jax.random.split(key, 17)
    f32 = jnp.float32
    nrm = jax.random.normal
    x = nrm(ks[0], (BATCH, SEQ, D_MODEL), f32)
    meta_tokens = nrm(ks[1], (N_META, D_MODEL), f32)
    mix_norm_g = 1.0 + 0.05 * nrm(ks[2], (DEPTH, D_MODEL), f32)
    w_in = nrm(ks[3], (DEPTH, D_MODEL, PROJ_COLS), f32) * (D_MODEL ** -0.5)
    f_bias = jnp.tile(jnp.linspace(3.0, 6.0, MLSTM_HEADS, dtype=f32), 2)
    base_gates = jnp.concatenate([jnp.zeros((2 * MLSTM_HEADS,), f32), f_bias])
    b_gates = base_gates[None, :] + 0.1 * nrm(ks[4], (DEPTH, N_GATES), f32)
    conv_w = nrm(ks[5], (DEPTH, CONV_K, 2 * ML_W), f32) * (CONV_K ** -0.5)
    conv_b = 0.01 * nrm(ks[6], (DEPTH, 2 * ML_W), f32)
    scales = 5.0 + jnp.arange(RET_HEADS, dtype=f32)
    gamma_logit = jnp.log(2.0 ** scales - 1.0)
    ret_decay_logit = gamma_logit[None, None, :] + 0.05 * nrm(ks[7], (DEPTH, 2, RET_HEADS), f32)
    ret_gn_g = 1.0 + 0.05 * nrm(ks[8], (DEPTH, RET_V), f32)
    mlstm_gn_g = 1.0 + 0.05 * nrm(ks[9], (DEPTH, ML_W), f32)
    w_out = nrm(ks[10], (DEPTH, D_MIX, D_MODEL), f32) * (D_MIX ** -0.5)
    ffn_norm_g = 1.0 + 0.05 * nrm(ks[11], (DEPTH, D_MODEL), f32)
    w_router = nrm(ks[12], (DEPTH, D_MODEL, N_EXPERTS), f32) * (D_MODEL ** -0.5)
    w_gate = nrm(ks[13], (DEPTH, N_EXPERTS, D_MODEL, D_FF), f32) * (D_MODEL ** -0.5)
    w_up = nrm(ks[14], (DEPTH, N_EXPERTS, D_MODEL, D_FF), f32) * (D_MODEL ** -0.5)
    w_down = nrm(ks[15], (DEPTH, N_EXPERTS, D_FF, D_MODEL), f32) * (D_FF ** -0.5)
    final_norm_g = 1.0 + 0.05 * nrm(ks[16], (D_MODEL,), f32)
    return {'x': x, 'meta_tokens': meta_tokens, 'mix_norm_g': mix_norm_g, 'w_in': w_in,
            'b_gates': b_gates, 'conv_w': conv_w, 'conv_b': conv_b,
            'ret_decay_logit': ret_decay_logit, 'ret_gn_g': ret_gn_g, 'mlstm_gn_g': mlstm_gn_g,
            'w_out': w_out, 'ffn_norm_g': ffn_norm_g, 'w_router': w_router, 'w_gate': w_gate,
            'w_up': w_up, 'w_down': w_down, 'final_norm_g': final_norm_g}


def reference(x, meta_tokens, mix_norm_g, w_in, b_gates, conv_w, conv_b, ret_decay_logit,
              ret_gn_g, mlstm_gn_g, w_out, ffn_norm_g, w_router, w_gate, w_up, w_down,
              final_norm_g):
    b_ = x.shape[0]
    meta = jnp.broadcast_to(meta_tokens.astype(x.dtype)[None], (b_, N_META, x.shape[-1]))
    h = jnp.concatenate([meta, x], axis=1)
    for l in range(DEPTH):
        h = h + _token_mixer(h, mix_norm_g[l], w_in[l], b_gates[l], conv_w[l], conv_b[l],
                             ret_decay_logit[l], ret_gn_g[l], mlstm_gn_g[l], w_out[l])
        h = h + _expert_choice_ffn(h, ffn_norm_g[l], w_router[l], w_gate[l], w_up[l], w_down[l])
    h = _rmsnorm(h, final_norm_g)
    return h[:, N_META:]
```

```python
import functools

import jax
import jax.numpy as jnp
from jax import lax
from jax.experimental import pallas as pl
from jax.experimental.pallas import tpu as pltpu

F32 = jnp.float32
BF16 = jnp.bfloat16

LANES = 128
CHUNK = 128
N_META = 16
PAD = CHUNK - N_META
HEADS = 4
DH = 128
N_EXPERTS = 16
CONV_K = 5
HALO = 8
EPS = 1e-6
NEG = -1e30
ROPE_BASE = 10000.0
VMEM_LIMIT = 56 * 1024 * 1024


def _cparams(sem, vmem=VMEM_LIMIT, **kw):
    return pltpu.CompilerParams(dimension_semantics=sem, vmem_limit_bytes=vmem, **kw)


def _dot(a, b):
    return jnp.dot(a, b, preferred_element_type=F32)


def _dot_nt(a, b):
    return lax.dot_general(a, b, (((1,), (1,)), ((), ())), preferred_element_type=F32)


def _dot_tn(a, b):
    return lax.dot_general(a, b, (((0,), (0,)), ((), ())), preferred_element_type=F32)


def _split3(x):
    hi = x.astype(BF16)
    r = x - hi.astype(F32)
    mid = r.astype(BF16)
    lo = (r - mid.astype(F32)).astype(BF16)
    return hi, mid, lo


def _dot3_left(m_bf, x):
    hi, mid, lo = _split3(x)
    return _dot(m_bf, hi) + _dot(m_bf, mid) + _dot(m_bf, lo)


def _dot3_right(x, m_bf):
    hi, mid, lo = _split3(x)
    return _dot(hi, m_bf) + _dot(mid, m_bf) + _dot(lo, m_bf)


def _rms(x, g):
    return x * lax.rsqrt(jnp.mean(x * x, axis=-1, keepdims=True) + EPS) * g


def _log_sigmoid(x):
    return jnp.minimum(x, 0.0) - jnp.log1p(jnp.exp(-jnp.abs(x)))


def _silu(x):
    return x * (1.0 / (1.0 + jnp.exp(-x)))


def _sigmoid(x):
    return 1.0 / (1.0 + jnp.exp(-x))


def _row_tile(t):
    return 640 if t % 640 == 0 else CHUNK


def _in_proj_kernel(h_ref, hp_ref, hn_ref, ng_ref, wr_ref, wm_ref, wg_ref, wgt_ref,
                    bg_ref, bgt_ref, cw_ref, cb_ref, cos_ref, sin_ref,
                    rq_ref, rk_ref, rv_ref, rg_ref, mq_ref, mk_ref, mv_ref, mo_ref,
                    gc_ref, gr_ref, ext_ref, *, tr, n_tiles):
    i = pl.program_id(1)
    ng = ng_ref[...]
    u = _rms(h_ref[0], ng).astype(BF16)
    row = lax.broadcasted_iota(jnp.int32, (tr, 1), 0) + i * tr
    real = row >= PAD

    cosf = cos_ref[...]
    sinf = sin_ref[...]
    w = HEADS * DH
    pq = _dot(u, wr_ref[:, 0:w])
    pk = _dot(u, wr_ref[:, w:2 * w])
    for hd in range(HEADS):
        sl = slice(hd * DH, (hd + 1) * DH)
        xq = pq[:, sl]
        xk = pk[:, sl]
        rq_ref[0, :, sl] = ((xq * cosf + pltpu.roll(xq, DH // 2, 1) * sinf) * (DH ** -0.5)).astype(BF16)
        rk_ref[0, :, sl] = (xk * cosf + pltpu.roll(xk, DH // 2, 1) * sinf).astype(BF16)
    rv_ref[0] = _dot(u, wr_ref[:, 2 * w:3 * w]).astype(BF16)
    rg_ref[0] = _silu(_dot(u, wr_ref[:, 3 * w:4 * w])).astype(BF16)

    up = _rms(hp_ref[0], ng)
    un = _rms(hn_ref[0], ng)
    up = jnp.where(i == 0, 0.0, up)
    un = jnp.where(i == n_tiles - 1, 0.0, un)
    uh = jnp.concatenate([up, un], axis=0).astype(BF16)
    halo = _dot(uh, wm_ref[:, 0:2 * w])
    ext_ref[0:HALO, :] = halo[0:HALO]
    ext_ref[HALO:HALO + tr, :] = _dot(u, wm_ref[:, 0:2 * w])
    ext_ref[HALO + tr:2 * HALO + tr, :] = halo[HALO:2 * HALO]
    conv = cb_ref[...] + jnp.zeros((tr, 2 * w), F32)
    for k in range(CONV_K):
        off = HALO + k - CONV_K // 2
        conv = conv + cw_ref[k:k + 1, :] * ext_ref[off:off + tr, :]
    qk = jnp.where(real, _silu(conv), 0.0)
    mq_ref[0] = (qk[:, 0:w] * (DH ** -0.5)).astype(BF16)
    mk_ref[0] = qk[:, w:2 * w].astype(BF16)
    mv_ref[0] = _dot(u, wm_ref[:, 2 * w:3 * w]).astype(BF16)
    mo_ref[0] = _sigmoid(_dot(u, wm_ref[:, 3 * w:4 * w])).astype(BF16)

    lane = lax.broadcasted_iota(jnp.int32, (tr, LANES), 1)
    gts = _dot(u, wg_ref[...]) + bg_ref[...]
    gts = jnp.where(lane < 2 * HEADS, jnp.where(real, gts, NEG),
                    jnp.where(real & (lane < 4 * HEADS), _log_sigmoid(gts), 0.0))
    ci = lax.broadcasted_iota(jnp.int32, (CHUNK, CHUNK), 0)
    cj = lax.broadcasted_iota(jnp.int32, (CHUNK, CHUNK), 1)
    low = (ci >= cj).astype(BF16)
    upp = (ci <= cj).astype(BF16)
    lane_c = lax.broadcasted_iota(jnp.int32, (CHUNK, LANES), 1)
    for c in range(tr // CHUNK):
        g = gts[c * CHUNK:(c + 1) * CHUNK]
        lf = jnp.where(lane_c >= 2 * HEADS, g, 0.0)
        pre = _dot3_left(low, lf)
        suf = _dot3_left(upp, lf)
        full = jnp.where(lane_c < 2 * HEADS, g, jnp.where(lane_c < 3 * HEADS, pre, suf))
        for hd in range(HEADS):
            rolled = full if hd == 0 else pltpu.roll(full, LANES - hd, 1)
            gc_ref[0, hd, c * CHUNK:(c + 1) * CHUNK, :] = rolled

    rown = lax.broadcasted_iota(jnp.int32, (4 * HEADS, tr), 0)
    coln = lax.broadcasted_iota(jnp.int32, (1, tr), 1) + i * tr
    realr = coln >= PAD
    gtr = _dot_nt(wgt_ref[...], u) + bgt_ref[...]
    gtr = jnp.where(rown < 2 * HEADS, jnp.where(realr, gtr, NEG),
                    jnp.where(realr, _log_sigmoid(gtr), 0.0))
    rown_c = lax.broadcasted_iota(jnp.int32, (4 * HEADS, CHUNK), 0)
    for c in range(tr // CHUNK):
        g = gtr[:, c * CHUNK:(c + 1) * CHUNK]
        lf = jnp.where(rown_c >= 2 * HEADS, g, 0.0)
        pre = _dot3_right(lf, upp)
        suf = _dot3_right(lf, low)
        zero = jnp.zeros((4, CHUNK), F32)
        for hd in range(HEADS):
            blk = jnp.concatenate(
                [g[hd:hd + 1], g[HEADS + hd:HEADS + hd + 1],
                 pre[2 * HEADS + hd:2 * HEADS + hd + 1],
                 suf[3 * HEADS + hd:3 * HEADS + hd + 1], zero], axis=0)
            gr_ref[0, hd, :, c * CHUNK:(c + 1) * CHUNK] = blk


def _in_proj(h_pad, norm_g, w_in, b_gates, conv_w, conv_b, cosf, sinf):
    b, t, d = h_pad.shape
    tr = _row_tile(t)
    n_tiles = t // tr
    w = HEADS * DH
    wr = w_in[:, 0:4 * w].astype(BF16)
    wm = w_in[:, 4 * w:8 * w].astype(BF16)
    wg = jnp.pad(w_in[:, 8 * w:], ((0, 0), (0, LANES - 4 * HEADS))).astype(BF16)
    wgt = w_in[:, 8 * w:].T.astype(BF16)
    bg = jnp.pad(b_gates, (0, LANES - 4 * HEADS)).reshape(1, LANES)
    bgt = b_gates.reshape(4 * HEADS, 1)
    hb = tr // HALO
    last_h = t // HALO - 1
    seq = lambda bi, i: (bi, i, 0)
    const2 = lambda bi, i: (0, 0)
    out_bf = jax.ShapeDtypeStruct((b, t, w), BF16)
    kern = functools.partial(_in_proj_kernel, tr=tr, n_tiles=n_tiles)
    return pl.pallas_call(
        kern,
        grid=(b, n_tiles),
        in_specs=[
            pl.BlockSpec((1, tr, d), seq),
            pl.BlockSpec((1, HALO, d), lambda bi, i: (bi, jnp.maximum(i * hb - 1, 0), 0)),
            pl.BlockSpec((1, HALO, d), lambda bi, i: (bi, jnp.minimum((i + 1) * hb, last_h), 0)),
            pl.BlockSpec((1, d), const2),
            pl.BlockSpec((d, 4 * w), const2),
            pl.BlockSpec((d, 4 * w), const2),
            pl.BlockSpec((d, LANES), const2),
            pl.BlockSpec((4 * HEADS, d), const2),
            pl.BlockSpec((1, LANES), const2),
            pl.BlockSpec((4 * HEADS, 1), const2),
            pl.BlockSpec((CONV_K, 2 * w), const2),
            pl.BlockSpec((1, 2 * w), const2),
            pl.BlockSpec((tr, DH), lambda bi, i: (i, 0)),
            pl.BlockSpec((tr, DH), lambda bi, i: (i, 0)),
        ],
        out_specs=[pl.BlockSpec((1, tr, w), seq)] * 8 + [
            pl.BlockSpec((1, HEADS, tr, LANES), lambda bi, i: (bi, 0, i, 0)),
            pl.BlockSpec((1, HEADS, 8, tr), lambda bi, i: (bi, 0, 0, i)),
        ],
        out_shape=[out_bf] * 8 + [
            jax.ShapeDtypeStruct((b, HEADS, t, LANES), F32),
            jax.ShapeDtypeStruct((b, HEADS, 8, t), F32),
        ],
        scratch_shapes=[pltpu.VMEM((tr + 2 * HALO, 2 * w), F32)],
        compiler_params=_cparams(("parallel", "arbitrary")),
        name="in_proj",
    )(h_pad, h_pad, h_pad, norm_g.reshape(1, d), wr, wm, wg, wgt, bg, bgt,
      conv_w, conv_b.reshape(1, 2 * w), cosf, sinf)


def _retention_kernel(lg_ref, q_ref, k_ref, v_ref, o_ref, *, n_chunks):
    hd = pl.program_id(1)
    lgf = lg_ref[0, hd]
    lgb = lg_ref[1, hd]
    li = lax.broadcasted_iota(jnp.int32, (CHUNK, CHUNK), 0)
    mi = lax.broadcasted_iota(jnp.int32, (CHUNK, CHUNK), 1)
    diff = (li - mi).astype(F32)
    decay = (jnp.where(diff >= 0, jnp.exp(lgf * jnp.maximum(diff, 0.0)), 0.0)
             + jnp.where(diff <= 0, jnp.exp(lgb * jnp.maximum(-diff, 0.0)), 0.0))
    lcol = lax.broadcasted_iota(jnp.int32, (CHUNK, 1), 0).astype(F32)
    one = jnp.ones((1, 1), F32)
    zeta_f = jnp.exp(lgf * (CHUNK - 1.0 - lcol))
    xi_f = jnp.exp(lgf * (lcol + 1.0))
    g_f = jnp.exp(lgf * CHUNK * one)
    zeta_b = jnp.exp(lgb * lcol)
    xi_b = jnp.exp(lgb * (CHUNK - lcol))
    g_b = jnp.exp(lgb * CHUNK * one)

    def fwd(c, state):
        rows = pl.ds(pl.multiple_of(c * CHUNK, CHUNK), CHUNK)
        q = q_ref[0, rows, :]
        k = k_ref[0, rows, :]
        v = v_ref[0, rows, :]
        s = _dot_nt(q, k) * decay
        intra = _dot(s.astype(BF16), v)
        cross = _dot(q, state.astype(BF16)) * xi_f
        o_ref[0, rows, :] = intra + cross
        kz = (k.astype(F32) * zeta_f).astype(BF16)
        return g_f * state + _dot_tn(kz, v)

    lax.fori_loop(0, n_chunks, fwd, jnp.zeros((DH, DH), F32))

    def bwd(j, state):
        c = n_chunks - 1 - j
        rows = pl.ds(pl.multiple_of(c * CHUNK, CHUNK), CHUNK)
        q = q_ref[0, rows, :]
        k = k_ref[0, rows, :]
        v = v_ref[0, rows, :]
        cross = _dot(q, state.astype(BF16)) * xi_b
        o_ref[0, rows, :] = o_ref[0, rows, :] + cross
        kz = (k.astype(F32) * zeta_b).astype(BF16)
        return g_b * state + _dot_tn(kz, v)

    lax.fori_loop(0, n_chunks, bwd, jnp.zeros((DH, DH), F32))


def _retention(lg, rq, rk, rv):
    b, t, _ = rq.shape
    spec = pl.BlockSpec((1, t, DH), lambda bi, hi, lg_ref: (bi, 0, hi))
    return pl.pallas_call(
        functools.partial(_retention_kernel, n_chunks=t // CHUNK),
        grid_spec=pltpu.PrefetchScalarGridSpec(
            num_scalar_prefetch=1,
            grid=(b, HEADS),
            in_specs=[spec, spec, spec],
            out_specs=spec,
        ),
        out_shape=jax.ShapeDtypeStruct((b, t, HEADS * DH), F32),
        compiler_params=_cparams(("parallel", "arbitrary")),
        name="retention",
    )(lg, rq, rk, rv)


def _mlstm_kernel(q_ref, k_ref, v_ref, gc_ref, gr_ref, o_ref, *, n_chunks):
    li = lax.broadcasted_iota(jnp.int32, (CHUNK, CHUNK), 0)
    mi = lax.broadcasted_iota(jnp.int32, (CHUNK, CHUNK), 1)
    causal = li >= mi
    anti = li <= mi

    def step(c, carry, backward):
        ck, nrm, m_st = carry
        rows = pl.ds(pl.multiple_of(c * CHUNK, CHUNK), CHUNK)
        q = q_ref[0, rows, :]
        k = k_ref[0, rows, :]
        v = v_ref[0, rows, :]
        gcol = gc_ref[0, 0, rows, :]
        grow = gr_ref[0, 0, :, rows]
        if backward:
            li_col, bc_col = gcol[:, 4:5], gcol[:, 12:13]
            li_row, bc_row = grow[1:2, :], grow[3:4, :]
            b_last = bc_col[0:1, :]
            mask = anti
        else:
            li_col, bc_col = gcol[:, 0:1], gcol[:, 8:9]
            li_row, bc_row = grow[0:1, :], grow[2:3, :]
            b_last = bc_col[CHUNK - 1:CHUNK, :]
            mask = causal
        log_d = jnp.where(mask, bc_col - bc_row + li_row, NEG)
        log_inter = bc_col + m_st
        m_row = jnp.maximum(log_inter, jnp.max(log_d, axis=1, keepdims=True))
        s = _dot_nt(q, k) * jnp.exp(log_d - m_row)
        w_inter = jnp.exp(log_inter - m_row)
        num = _dot(s.astype(BF16), v) + _dot(q, ck.astype(BF16)) * w_inter
        qn = jnp.sum(q.astype(F32) * nrm, axis=1, keepdims=True)
        den = jnp.sum(s, axis=1, keepdims=True) + qn * w_inter
        denom = jnp.maximum(jnp.abs(den), jnp.exp(-m_row))
        hval = num / denom
        if backward:
            o_ref[0, rows, :] = o_ref[0, rows, :] + hval
        else:
            o_ref[0, rows, :] = hval
        log_u = b_last - bc_col + li_col
        a = jnp.max(log_u, axis=0, keepdims=True)
        w_u = jnp.exp(log_u - a)
        kw = k.astype(F32) * w_u
        upd = _dot_tn(kw.astype(BF16), v)
        upd_n = jnp.sum(kw, axis=0, keepdims=True)
        m_new = jnp.maximum(b_last + m_st, a)
        f = jnp.exp(b_last + m_st - m_new)
        g = jnp.exp(a - m_new)
        return f * ck + g * upd, f * nrm + g * upd_n, m_new

    init = (jnp.zeros((DH, DH), F32), jnp.zeros((1, DH), F32), jnp.zeros((1, 1), F32))
    lax.fori_loop(0, n_chunks, lambda c, s: step(c, s, False), init)
    lax.fori_loop(0, n_chunks, lambda j, s: step(n_chunks - 1 - j, s, True), init)


def _mlstm(mq, mk, mv, gcol, grow):
    b, t, _ = mq.shape
    spec = pl.BlockSpec((1, t, DH), lambda bi, hi: (bi, 0, hi))
    return pl.pallas_call(
        functools.partial(_mlstm_kernel, n_chunks=t // CHUNK),
        grid=(b, HEADS),
        in_specs=[spec, spec, spec,
                  pl.BlockSpec((1, 1, t, LANES), lambda bi, hi: (bi, hi, 0, 0)),
                  pl.BlockSpec((1, 1, 8, t), lambda bi, hi: (bi, hi, 0, 0))],
        out_specs=spec,
        out_shape=jax.ShapeDtypeStruct((b, t, HEADS * DH), F32),
        compiler_params=_cparams(("parallel", "arbitrary")),
        name="mlstm",
    )(mq, mk, mv, gcol, grow)


def _head_norm(y, g):
    outs = []
    for hd in range(HEADS):
        yh = y[:, hd * DH:(hd + 1) * DH]
        mu = jnp.mean(yh, axis=1, keepdims=True)
        yc = yh - mu
        var = jnp.mean(yc * yc, axis=1, keepdims=True)
        outs.append(yc * lax.rsqrt(var + EPS))
    return jnp.concatenate(outs, axis=1) * g


def _out_proj_kernel(h_ref, ret_ref, hm_ref, rg_ref, mo_ref, rgn_ref, mgn_ref, wo_ref,
                     fg_ref, wrt_ref, h1_ref, u2_ref, aff_ref, *, tr):
    i = pl.program_id(1)
    row = lax.broadcasted_iota(jnp.int32, (tr, 1), 0) + i * tr
    real = row >= PAD
    w = HEADS * DH
    y_ret = _head_norm(ret_ref[0], rgn_ref[...]) * rg_ref[0].astype(F32)
    y_m = _head_norm(mo_ref[0].astype(F32) * hm_ref[0], mgn_ref[...])
    y_ret = jnp.where(real, y_ret, 0.0).astype(BF16)
    y_m = jnp.where(real, y_m, 0.0).astype(BF16)
    h1 = h_ref[0] + _dot(y_ret, wo_ref[0:w, :]) + _dot(y_m, wo_ref[w:2 * w, :])
    h1_ref[0] = h1
    u2 = _rms(h1, fg_ref[...])
    u2_ref[0] = u2
    logits = _dot_nt(wrt_ref[...], u2.astype(BF16))
    mx = jnp.max(logits, axis=0, keepdims=True)
    ex = jnp.exp(logits - mx)
    aff = ex / jnp.sum(ex, axis=0, keepdims=True)
    coln = lax.broadcasted_iota(jnp.int32, (1, tr), 1) + i * tr
    aff_ref[0] = jnp.where(coln >= PAD, aff, -1.0)


def _out_proj(h_pad, ret, hm, rgs, mos, ret_gn_g, mlstm_gn_g, w_out, ffn_norm_g, w_router):
    b, t, d = h_pad.shape
    tr = _row_tile(t)
    w = HEADS * DH
    seq = lambda bi, i: (bi, i, 0)
    const2 = lambda bi, i: (0, 0)
    return pl.pallas_call(
        functools.partial(_out_proj_kernel, tr=tr),
        grid=(b, t // tr),
        in_specs=[
            pl.BlockSpec((1, tr, d), seq),
            pl.BlockSpec((1, tr, w), seq),
            pl.BlockSpec((1, tr, w), seq),
            pl.BlockSpec((1, tr, w), seq),
            pl.BlockSpec((1, tr, w), seq),
            pl.BlockSpec((1, w), const2),
            pl.BlockSpec((1, w), const2),
            pl.BlockSpec((2 * w, d), const2),
            pl.BlockSpec((1, d), const2),
            pl.BlockSpec((N_EXPERTS, d), const2),
        ],
        out_specs=[
            pl.BlockSpec((1, tr, d), seq),
            pl.BlockSpec((1, tr, d), seq),
            pl.BlockSpec((1, N_EXPERTS, tr), lambda bi, i: (bi, 0, i)),
        ],
        out_shape=[
            jax.ShapeDtypeStruct((b, t, d), F32),
            jax.ShapeDtypeStruct((b, t, d), F32),
            jax.ShapeDtypeStruct((b, N_EXPERTS, t), F32),
        ],
        compiler_params=_cparams(("parallel", "arbitrary")),
        name="out_proj",
    )(h_pad, ret, hm, rgs, mos, ret_gn_g.reshape(1, w), mlstm_gn_g.reshape(1, w),
      w_out.astype(BF16), ffn_norm_g.reshape(1, d), w_router.T.astype(BF16))


def _route_kernel(aff_ref, dest_ref, gate_ref, cnt_ref, sel_ref, pm_ref, dv_ref,
                  *, t, cap, c_pad):
    e_n = N_EXPERTS
    nblk = t // LANES
    bits = pltpu.bitcast(aff_ref[0], jnp.int32)
    capf = float(cap)

    def search(_, lohi):
        lo, hi = lohi
        mid = lo + ((hi - lo + 1) >> 1)
        cnt = jnp.sum((bits >= mid).astype(F32), axis=1, keepdims=True)
        ok = cnt >= capf
        return jnp.where(ok, mid, lo), jnp.where(ok, hi, mid - 1)

    lo0 = jnp.zeros((e_n, 1), jnp.int32)
    hi0 = jnp.full((e_n, 1), 0x3F800000, jnp.int32)
    thr, _ = lax.fori_loop(0, 31, search, (lo0, hi0))
    need = capf - jnp.sum((bits > thr).astype(F32), axis=1, keepdims=True)

    ci = lax.broadcasted_iota(jnp.int32, (LANES, LANES), 0)
    cj = lax.broadcasted_iota(jnp.int32, (LANES, LANES), 1)
    upp = (ci <= cj).astype(BF16)
    ei = lax.broadcasted_iota(jnp.int32, (e_n, e_n), 0)
    ej = lax.broadcasted_iota(jnp.int32, (e_n, e_n), 1)
    strict = (ej < ei).astype(BF16)

    carry = jnp.zeros((e_n, 1), F32)
    for j in range(nblk):
        sl = slice(j * LANES, (j + 1) * LANES)
        bb = bits[:, sl]
        eqf = (bb == thr).astype(F32)
        ceq = _dot(eqf.astype(BF16), upp) + carry
        carry = ceq[:, LANES - 1:LANES]
        sel_ref[:, sl] = jnp.where(bb > thr, 1.0, jnp.where(ceq <= need, eqf, 0.0))

    carry = jnp.zeros((e_n, 1), F32)
    tok = lax.broadcasted_iota(jnp.int32, (1, LANES), 1).astype(F32)
    for j in range(nblk):
        sl = slice(j * LANES, (j + 1) * LANES)
        selb = sel_ref[:, sl]
        selbf = selb.astype(BF16)
        pin = _dot(selbf, upp) + carry
        carry = pin[:, LANES - 1:LANES]
        pm_ref[:, sl] = jnp.where(selb > 0, pin, 0.0)
        rank = _dot(strict, selbf)
        dv_ref[:, sl] = (tok + float(j * LANES)) * float(N_EXPERTS) + rank
        cnt_row = jnp.sum(selb, axis=0, keepdims=True)
        cnt_ref[0, sl, :] = jnp.broadcast_to(cnt_row, (LANES, LANES)).T

    gate_ref[...] = jnp.zeros(gate_ref.shape, F32)
    lane_e = lax.broadcasted_iota(jnp.int32, (LANES, LANES), 1)

    def expert(e):
        def ctile(ct, _):
            c0 = pl.multiple_of(ct * LANES, LANES)
            want = (lax.broadcasted_iota(jnp.int32, (LANES, 1), 0) + (c0 + 1)).astype(F32)

            def blk(j, acc):
                acc_d, acc_g = acc
                cols = pl.ds(pl.multiple_of(j * LANES, LANES), LANES)
                hit = pm_ref[e:e + 1, cols] == want
                acc_d = acc_d + jnp.where(hit, dv_ref[e:e + 1, cols], 0.0)
                acc_g = acc_g + jnp.where(hit, aff_ref[0, e:e + 1, cols], 0.0)
                return acc_d, acc_g

            zero = jnp.zeros((LANES, LANES), F32)
            acc_d, acc_g = lax.fori_loop(0, nblk, blk, (zero, zero))
            drow = jnp.sum(acc_d.T, axis=0, keepdims=True)
            dest_ref[0, e:e + 1, pl.ds(c0, LANES)] = drow.astype(jnp.int32)
            gcol = jnp.sum(acc_g, axis=1, keepdims=True)
            rows = pl.ds(c0, LANES)
            gate_ref[0, rows, :] = jnp.where(lane_e == e, gcol, gate_ref[0, rows, :])
            return 0

        lax.fori_loop(0, c_pad // LANES, ctile, 0)

    for e in range(e_n):
        expert(e)


def _route(aff, cap, c_pad):
    b, e_n, t = aff.shape
    kern = functools.partial(_route_kernel, t=t, cap=cap, c_pad=c_pad)
    return pl.pallas_call(
        kern,
        grid=(b,),
        in_specs=[pl.BlockSpec((1, e_n, t), lambda bi: (bi, 0, 0))],
        out_specs=[
            pl.BlockSpec((1, e_n, c_pad), lambda bi: (bi, 0, 0)),
            pl.BlockSpec((1, c_pad, LANES), lambda bi: (bi, 0, 0)),
            pl.BlockSpec((1, t, LANES), lambda bi: (bi, 0, 0)),
        ],
        out_shape=[
            jax.ShapeDtypeStruct((b, e_n, c_pad), jnp.int32),
            jax.ShapeDtypeStruct((b, c_pad, LANES), F32),
            jax.ShapeDtypeStruct((b, t, LANES), F32),
        ],
        scratch_shapes=[pltpu.VMEM((e_n, t), F32)] * 3,
        compiler_params=_cparams(("arbitrary",)),
        name="route",
    )(aff)


def _expert_kernel(dest_ref, gate_ref, u_hbm, wg_ref, wu_ref, wd_ref, z_hbm,
                   x32_ref, xb_ref, acc_ref, sem_in, sem_out, *, t, nb, cap, c_pad, n_f):
    bi = pl.program_id(0)
    e = pl.program_id(1)
    f = pl.program_id(2)

    def gather_copy(c):
        tok = dest_ref[0, 0, c] >> 4
        return pltpu.make_async_copy(u_hbm.at[pl.ds(bi * t + tok, 1), :],
                                     x32_ref.at[pl.ds(c, 1), :], sem_in)

    def scatter_copy(c):
        v = dest_ref[0, 0, c]
        row = ((v & (N_EXPERTS - 1)) * nb + bi) * t + (v >> 4)
        return pltpu.make_async_copy(acc_ref.at[pl.ds(c, 1), :], z_hbm.at[pl.ds(row, 1), :], sem_out)

    @pl.when(f == 0)
    def _():
        def issue(c, _):
            gather_copy(c).start()
            return 0
        lax.fori_loop(0, cap, issue, 0)
        if c_pad > cap:
            x32_ref[cap:c_pad, :] = jnp.zeros((c_pad - cap, x32_ref.shape[1]), F32)

        def wait(c, _):
            gather_copy(c).wait()
            return 0
        lax.fori_loop(0, cap, wait, 0)
        xb_ref[...] = x32_ref[...].astype(BF16)

    x = xb_ref[...]
    hid = _silu(_dot(x, wg_ref[0])) * _dot(x, wu_ref[0])
    part = _dot(hid.astype(BF16), wd_ref[0])

    @pl.when(f == 0)
    def _():
        acc_ref[...] = part

    @pl.when(f > 0)
    def _():
        acc_ref[...] = acc_ref[...] + part

    @pl.when(f == n_f - 1)
    def _():
        lane = lax.broadcasted_iota(jnp.int32, (c_pad, LANES), 1)
        gcol = jnp.sum(jnp.where(lane == e, gate_ref[0], 0.0), axis=1, keepdims=True)
        acc_ref[...] = acc_ref[...] * gcol

        def issue(c, _):
            scatter_copy(c).start()
            return 0
        lax.fori_loop(0, cap, issue, 0)

        def wait(c, _):
            scatter_copy(c).wait()
            return 0
        lax.fori_loop(0, cap, wait, 0)


def _experts(dest, gate, u2, w_gate, w_up, w_down, cap):
    b, e_n, c_pad = dest.shape
    _, t, d = u2.shape
    d_ff = w_gate.shape[-1]
    fc = 256
    n_f = d_ff // fc
    kern = functools.partial(_expert_kernel, t=t, nb=b, cap=cap, c_pad=c_pad, n_f=n_f)
    return pl.pallas_call(
        kern,
        grid=(b, e_n, n_f),
        in_specs=[
            pl.BlockSpec((1, 1, c_pad), lambda bi, e, f: (bi * e_n + e, 0, 0), memory_space=pltpu.SMEM),
            pl.BlockSpec((1, c_pad, LANES), lambda bi, e, f: (bi, 0, 0)),
            pl.BlockSpec(memory_space=pl.ANY),
            pl.BlockSpec((1, d, fc), lambda bi, e, f: (e, 0, f)),
            pl.BlockSpec((1, d, fc), lambda bi, e, f: (e, 0, f)),
            pl.BlockSpec((1, fc, d), lambda bi, e, f: (e, f, 0)),
        ],
        out_specs=pl.BlockSpec(memory_space=pl.ANY),
        out_shape=jax.ShapeDtypeStruct((e_n * b * t, d), F32),
        scratch_shapes=[
            pltpu.VMEM((c_pad, d), F32),
            pltpu.VMEM((c_pad, d), BF16),
            pltpu.VMEM((c_pad, d), F32),
            pltpu.SemaphoreType.DMA,
            pltpu.SemaphoreType.DMA,
        ],
        compiler_params=_cparams(("arbitrary", "arbitrary", "arbitrary")),
        name="experts",
    )(dest.reshape(b * e_n, 1, c_pad), gate, u2.reshape(b * t, d),
      w_gate.astype(BF16), w_up.astype(BF16), w_down.astype(BF16))


def _combine_kernel(mc_ref, h1_ref, z_ref, cnt_ref, fg_ref, o_ref, acc_ref, *, n_layers, n_tiles):
    bi = pl.program_id(0)
    i = pl.program_id(1)
    j = pl.program_id(2)

    @pl.when(j == 0)
    def _():
        acc_ref[...] = h1_ref[0]

    @pl.when(j < mc_ref[bi * n_tiles + i])
    def _():
        take = cnt_ref[0][:, 0:1] > j.astype(F32)
        acc_ref[...] = acc_ref[...] + jnp.where(take, z_ref[0, 0], 0.0)

    @pl.when(j == n_layers - 1)
    def _():
        o_ref[0] = _rms(acc_ref[...], fg_ref[...])


def _combine(maxcnt, h1, z, cnt, final_norm_g):
    b, t, d = h1.shape
    n_layers = z.shape[0]
    tr = _row_tile(t)
    n_tiles = t // tr

    def z_map(bi, i, j, mc):
        return (jnp.maximum(jnp.minimum(j, mc[bi * n_tiles + i] - 1), 0), bi, i, 0)

    return pl.pallas_call(
        functools.partial(_combine_kernel, n_layers=n_layers, n_tiles=n_tiles),
        grid_spec=pltpu.PrefetchScalarGridSpec(
            num_scalar_prefetch=1,
            grid=(b, n_tiles, n_layers),
            in_specs=[
                pl.BlockSpec((1, tr, d), lambda bi, i, j, mc: (bi, i, 0)),
                pl.BlockSpec((1, 1, tr, d), z_map),
                pl.BlockSpec((1, tr, LANES), lambda bi, i, j, mc: (bi, i, 0)),
                pl.BlockSpec((1, d), lambda bi, i, j, mc: (0, 0)),
            ],
            out_specs=pl.BlockSpec((1, tr, d), lambda bi, i, j, mc: (bi, i, 0)),
            scratch_shapes=[pltpu.VMEM((tr, d), F32)],
        ),
        out_shape=jax.ShapeDtypeStruct((b, t, d), F32),
        compiler_params=_cparams(("parallel", "arbitrary", "arbitrary")),
        name="combine",
    )(maxcnt, h1, z, cnt, final_norm_g.reshape(1, d))


def _maxcnt_kernel(cnt_ref, o_ref):
    m = jnp.max(cnt_ref[0], axis=0, keepdims=True)
    o_ref[0, 0] = jnp.broadcast_to(m, (8, LANES)).astype(jnp.int32)


def _maxcnt(cnt):
    b, t, _ = cnt.shape
    tr = _row_tile(t)
    n_tiles = t // tr
    out = pl.pallas_call(
        _maxcnt_kernel,
        grid=(b, n_tiles),
        in_specs=[pl.BlockSpec((1, tr, LANES), lambda bi, i: (bi, i, 0))],
        out_specs=pl.BlockSpec((1, 1, 8, LANES), lambda bi, i: (bi, i, 0, 0)),
        out_shape=jax.ShapeDtypeStruct((b, n_tiles, 8, LANES), jnp.int32),
        compiler_params=_cparams(("parallel", "arbitrary")),
        name="maxcnt",
    )(cnt)
    return out[:, :, 0, 0].reshape(b * n_tiles)


def kernel(x, meta_tokens, mix_norm_g, w_in, b_gates, conv_w, conv_b, ret_decay_logit, ret_gn_g,
           mlstm_gn_g, w_out, ffn_norm_g, w_router, w_gate, w_up, w_down, final_norm_g):
    b, s, d = x.shape
    t = CHUNK + s
    n = N_META + s
    cap = 2 * n // N_EXPERTS
    c_pad = -(-cap // LANES) * LANES

    meta = jnp.broadcast_to(meta_tokens.astype(x.dtype)[None], (b, N_META, d))
    h_pad = jnp.concatenate([jnp.zeros((b, PAD, d), x.dtype), meta, x], axis=1)

    half = DH // 2
    pos = jnp.arange(t, dtype=F32) - PAD
    inv = ROPE_BASE ** (-jnp.arange(half, dtype=F32) / half)
    ang = pos[:, None] * inv[None, :]
    cosf = jnp.concatenate([jnp.cos(ang), jnp.cos(ang)], axis=1)
    sinf = jnp.concatenate([-jnp.sin(ang), jnp.sin(ang)], axis=1)

    assert mix_norm_g.shape[0] == 1, "single-layer block only"
    rq, rk, rv, rgs, mq, mk, mv, mos, gcol, grow = _in_proj(
        h_pad, mix_norm_g[0], w_in[0], b_gates[0], conv_w[0], conv_b[0], cosf, sinf)
    lg = jax.nn.log_sigmoid(ret_decay_logit[0].astype(F32))
    ret = _retention(lg, rq, rk, rv)
    hm = _mlstm(mq, mk, mv, gcol, grow)
    h1, u2, aff = _out_proj(h_pad, ret, hm, rgs, mos, ret_gn_g[0], mlstm_gn_g[0], w_out[0],
                            ffn_norm_g[0], w_router[0])
    dest, gate, cnt = _route(aff, cap, c_pad)
    z = _experts(dest, gate, u2, w_gate[0], w_up[0], w_down[0], cap)
    z = z.reshape(N_EXPERTS, b, t, d)
    out = _combine(_maxcnt(cnt), h1, z, cnt, final_norm_g)
    return out[:, CHUNK:]
```

```python
import functools

import jax
import jax.numpy as jnp
from jax import lax
from jax.experimental import pallas as pl
from jax.experimental.pallas import tpu as pltpu
from jax.experimental.pallas import tpu_sc as plsc

F32 = jnp.float32
BF16 = jnp.bfloat16

LANES = 128
CHUNK = 128
N_META = 16
PAD = CHUNK - N_META
HEADS = 4
DH = 128
N_EXPERTS = 16
CONV_K = 5
HALO = 8
SUB = 8
SC_WINDOW = 128
EPS = 1e-6
NEG = -1e30
ROPE_BASE = 10000.0
VMEM_LIMIT = 56 * 1024 * 1024


def _cparams(sem, vmem=VMEM_LIMIT, **kw):
    return pltpu.CompilerParams(dimension_semantics=sem, vmem_limit_bytes=vmem, **kw)


def _dot(a, b):
    return jnp.dot(a, b, preferred_element_type=F32)


def _dot_nt(a, b):
    return lax.dot_general(a, b, (((1,), (1,)), ((), ())), preferred_element_type=F32)


def _dot_tn(a, b):
    return lax.dot_general(a, b, (((0,), (0,)), ((), ())), preferred_element_type=F32)


def _split3(x):
    hi = x.astype(BF16)
    r = x - hi.astype(F32)
    mid = r.astype(BF16)
    lo = (r - mid.astype(F32)).astype(BF16)
    return hi, mid, lo


def _dot3_left(m_bf, x):
    hi, mid, lo = _split3(x)
    return _dot(m_bf, hi) + _dot(m_bf, mid) + _dot(m_bf, lo)


def _dot3_right(x, m_bf):
    hi, mid, lo = _split3(x)
    return _dot(hi, m_bf) + _dot(mid, m_bf) + _dot(lo, m_bf)


def _rms(x, g):
    return x * lax.rsqrt(jnp.mean(x * x, axis=-1, keepdims=True) + EPS) * g


def _log_sigmoid(x):
    return jnp.minimum(x, 0.0) - jnp.log1p(jnp.exp(-jnp.abs(x)))


def _silu(x):
    return x * (1.0 / (1.0 + jnp.exp(-x)))


def _sigmoid(x):
    return 1.0 / (1.0 + jnp.exp(-x))


def _row_tile(t):
    return 640 if t % 640 == 0 else CHUNK


def _store_segments(dst_ref, src_ref, n_rows):
    nseg = src_ref.shape[1] // LANES

    def body(i, _):
        r0 = pl.multiple_of(i * SUB, SUB)
        for j in range(nseg):
            q0 = pl.multiple_of((i * nseg + j) * SUB, SUB)
            dst_ref[pl.ds(q0, SUB), :] = src_ref[pl.ds(r0, SUB), j * LANES:(j + 1) * LANES]
        return 0

    lax.fori_loop(0, n_rows // SUB, body, 0)


def _segment_ids(rows, nseg):
    r = rows.reshape(-1, 1, SUB)
    j = jnp.arange(nseg, dtype=jnp.int32).reshape(1, nseg, 1)
    return (((r >> 3) * nseg + j) * SUB + (r & (SUB - 1))).reshape(-1)


def _in_proj_kernel(h_ref, hp_ref, hn_ref, ng_ref, wr_ref, wm_ref, wg_ref, wgt_ref,
                    bg_ref, bgt_ref, cw_ref, cb_ref, cos_ref, sin_ref,
                    rq_ref, rk_ref, rv_ref, rg_ref, mq_ref, mk_ref, mv_ref, mo_ref,
                    gc_ref, gr_ref, ext_ref, *, tr, n_tiles):
    i = pl.program_id(1)
    ng = ng_ref[...]
    u = _rms(h_ref[0], ng).astype(BF16)
    row = lax.broadcasted_iota(jnp.int32, (tr, 1), 0) + i * tr
    real = row >= PAD

    cosf = cos_ref[...]
    sinf = sin_ref[...]
    w = HEADS * DH
    pq = _dot(u, wr_ref[:, 0:w])
    pk = _dot(u, wr_ref[:, w:2 * w])
    for hd in range(HEADS):
        sl = slice(hd * DH, (hd + 1) * DH)
        xq = pq[:, sl]
        xk = pk[:, sl]
        rq_ref[0, :, sl] = ((xq * cosf + pltpu.roll(xq, DH // 2, 1) * sinf) * (DH ** -0.5)).astype(BF16)
        rk_ref[0, :, sl] = (xk * cosf + pltpu.roll(xk, DH // 2, 1) * sinf).astype(BF16)
    rv_ref[0] = _dot(u, wr_ref[:, 2 * w:3 * w]).astype(BF16)
    rg_ref[0] = _silu(_dot(u, wr_ref[:, 3 * w:4 * w])).astype(BF16)

    up = _rms(hp_ref[0], ng)
    un = _rms(hn_ref[0], ng)
    up = jnp.where(i == 0, 0.0, up)
    un = jnp.where(i == n_tiles - 1, 0.0, un)
    uh = jnp.concatenate([up, un], axis=0).astype(BF16)
    halo = _dot(uh, wm_ref[:, 0:2 * w])
    ext_ref[0:HALO, :] = halo[0:HALO]
    ext_ref[HALO:HALO + tr, :] = _dot(u, wm_ref[:, 0:2 * w])
    ext_ref[HALO + tr:2 * HALO + tr, :] = halo[HALO:2 * HALO]
    conv = cb_ref[...] + jnp.zeros((tr, 2 * w), F32)
    for k in range(CONV_K):
        off = HALO + k - CONV_K // 2
        conv = conv + cw_ref[k:k + 1, :] * ext_ref[off:off + tr, :]
    qk = jnp.where(real, _silu(conv), 0.0)
    mq_ref[0] = (qk[:, 0:w] * (DH ** -0.5)).astype(BF16)
    mk_ref[0] = qk[:, w:2 * w].astype(BF16)
    mv_ref[0] = _dot(u, wm_ref[:, 2 * w:3 * w]).astype(BF16)
    mo_ref[0] = _sigmoid(_dot(u, wm_ref[:, 3 * w:4 * w])).astype(BF16)

    lane = lax.broadcasted_iota(jnp.int32, (tr, LANES), 1)
    gts = _dot(u, wg_ref[...]) + bg_ref[...]
    gts = jnp.where(lane < 2 * HEADS, jnp.where(real, gts, NEG),
                    jnp.where(real & (lane < 4 * HEADS), _log_sigmoid(gts), 0.0))
    ci = lax.broadcasted_iota(jnp.int32, (CHUNK, CHUNK), 0)
    cj = lax.broadcasted_iota(jnp.int32, (CHUNK, CHUNK), 1)
    low = (ci >= cj).astype(BF16)
    upp = (ci <= cj).astype(BF16)
    lane_c = lax.broadcasted_iota(jnp.int32, (CHUNK, LANES), 1)
    for c in range(tr // CHUNK):
        g = gts[c * CHUNK:(c + 1) * CHUNK]
        lf = jnp.where(lane_c >= 2 * HEADS, g, 0.0)
        pre = _dot3_left(low, lf)
        suf = _dot3_left(upp, lf)
        full = jnp.where(lane_c < 2 * HEADS, g, jnp.where(lane_c < 3 * HEADS, pre, suf))
        for hd in range(HEADS):
            rolled = full if hd == 0 else pltpu.roll(full, LANES - hd, 1)
            gc_ref[0, hd, c * CHUNK:(c + 1) * CHUNK, :] = rolled

    rown = lax.broadcasted_iota(jnp.int32, (4 * HEADS, tr), 0)
    coln = lax.broadcasted_iota(jnp.int32, (1, tr), 1) + i * tr
    realr = coln >= PAD
    gtr = _dot_nt(wgt_ref[...], u) + bgt_ref[...]
    gtr = jnp.where(rown < 2 * HEADS, jnp.where(realr, gtr, NEG),
                    jnp.where(realr, _log_sigmoid(gtr), 0.0))
    rown_c = lax.broadcasted_iota(jnp.int32, (4 * HEADS, CHUNK), 0)
    for c in range(tr // CHUNK):
        g = gtr[:, c * CHUNK:(c + 1) * CHUNK]
        lf = jnp.where(rown_c >= 2 * HEADS, g, 0.0)
        pre = _dot3_right(lf, upp)
        suf = _dot3_right(lf, low)
        zero = jnp.zeros((4, CHUNK), F32)
        for hd in range(HEADS):
            blk = jnp.concatenate(
                [g[hd:hd + 1], g[HEADS + hd:HEADS + hd + 1],
                 pre[2 * HEADS + hd:2 * HEADS + hd + 1],
                 suf[3 * HEADS + hd:3 * HEADS + hd + 1], zero], axis=0)
            gr_ref[0, hd, :, c * CHUNK:(c + 1) * CHUNK] = blk


def _in_proj(h_pad, norm_g, w_in, b_gates, conv_w, conv_b, cosf, sinf):
    b, t, d = h_pad.shape
    tr = _row_tile(t)
    n_tiles = t // tr
    w = HEADS * DH
    wr = w_in[:, 0:4 * w].astype(BF16)
    wm = w_in[:, 4 * w:8 * w].astype(BF16)
    wg = jnp.pad(w_in[:, 8 * w:], ((0, 0), (0, LANES - 4 * HEADS))).astype(BF16)
    wgt = w_in[:, 8 * w:].T.astype(BF16)
    bg = jnp.pad(b_gates, (0, LANES - 4 * HEADS)).reshape(1, LANES)
    bgt = b_gates.reshape(4 * HEADS, 1)
    hb = tr // HALO
    last_h = t // HALO - 1
    seq = lambda bi, i: (bi, i, 0)
    const2 = lambda bi, i: (0, 0)
    out_bf = jax.ShapeDtypeStruct((b, t, w), BF16)
    kern = functools.partial(_in_proj_kernel, tr=tr, n_tiles=n_tiles)
    return pl.pallas_call(
        kern,
        grid=(b, n_tiles),
        in_specs=[
            pl.BlockSpec((1, tr, d), seq),
            pl.BlockSpec((1, HALO, d), lambda bi, i: (bi, jnp.maximum(i * hb - 1, 0), 0)),
            pl.BlockSpec((1, HALO, d), lambda bi, i: (bi, jnp.minimum((i + 1) * hb, last_h), 0)),
            pl.BlockSpec((1, d), const2),
            pl.BlockSpec((d, 4 * w), const2),
            pl.BlockSpec((d, 4 * w), const2),
            pl.BlockSpec((d, LANES), const2),
            pl.BlockSpec((4 * HEADS, d), const2),
            pl.BlockSpec((1, LANES), const2),
            pl.BlockSpec((4 * HEADS, 1), const2),
            pl.BlockSpec((CONV_K, 2 * w), const2),
            pl.BlockSpec((1, 2 * w), const2),
            pl.BlockSpec((tr, DH), lambda bi, i: (i, 0)),
            pl.BlockSpec((tr, DH), lambda bi, i: (i, 0)),
        ],
        out_specs=[pl.BlockSpec((1, tr, w), seq)] * 8 + [
            pl.BlockSpec((1, HEADS, tr, LANES), lambda bi, i: (bi, 0, i, 0)),
            pl.BlockSpec((1, HEADS, 8, tr), lambda bi, i: (bi, 0, 0, i)),
        ],
        out_shape=[out_bf] * 8 + [
            jax.ShapeDtypeStruct((b, HEADS, t, LANES), F32),
            jax.ShapeDtypeStruct((b, HEADS, 8, t), F32),
        ],
        scratch_shapes=[pltpu.VMEM((tr + 2 * HALO, 2 * w), F32)],
        compiler_params=_cparams(("parallel", "arbitrary")),
        name="in_proj",
    )(h_pad, h_pad, h_pad, norm_g.reshape(1, d), wr, wm, wg, wgt, bg, bgt,
      conv_w, conv_b.reshape(1, 2 * w), cosf, sinf)


def _retention_kernel(lg_ref, q_ref, k_ref, v_ref, o_ref, *, n_chunks):
    hd = pl.program_id(1)
    lgf = lg_ref[0, hd]
    lgb = lg_ref[1, hd]
    li = lax.broadcasted_iota(jnp.int32, (CHUNK, CHUNK), 0)
    mi = lax.broadcasted_iota(jnp.int32, (CHUNK, CHUNK), 1)
    diff = (li - mi).astype(F32)
    decay = (jnp.where(diff >= 0, jnp.exp(lgf * jnp.maximum(diff, 0.0)), 0.0)
             + jnp.where(diff <= 0, jnp.exp(lgb * jnp.maximum(-diff, 0.0)), 0.0))
    lcol = lax.broadcasted_iota(jnp.int32, (CHUNK, 1), 0).astype(F32)
    one = jnp.ones((1, 1), F32)
    zeta_f = jnp.exp(lgf * (CHUNK - 1.0 - lcol))
    xi_f = jnp.exp(lgf * (lcol + 1.0))
    g_f = jnp.exp(lgf * CHUNK * one)
    zeta_b = jnp.exp(lgb * lcol)
    xi_b = jnp.exp(lgb * (CHUNK - lcol))
    g_b = jnp.exp(lgb * CHUNK * one)

    def fwd(c, state):
        rows = pl.ds(pl.multiple_of(c * CHUNK, CHUNK), CHUNK)
        q = q_ref[0, rows, :]
        k = k_ref[0, rows, :]
        v = v_ref[0, rows, :]
        s = _dot_nt(q, k) * decay
        intra = _dot(s.astype(BF16), v)
        cross = _dot(q, state.astype(BF16)) * xi_f
        o_ref[0, rows, :] = intra + cross
        kz = (k.astype(F32) * zeta_f).astype(BF16)
        return g_f * state + _dot_tn(kz, v)

    lax.fori_loop(0, n_chunks, fwd, jnp.zeros((DH, DH), F32))

    def bwd(j, state):
        c = n_chunks - 1 - j
        rows = pl.ds(pl.multiple_of(c * CHUNK, CHUNK), CHUNK)
        q = q_ref[0, rows, :]
        k = k_ref[0, rows, :]
        v = v_ref[0, rows, :]
        cross = _dot(q, state.astype(BF16)) * xi_b
        o_ref[0, rows, :] = o_ref[0, rows, :] + cross
        kz = (k.astype(F32) * zeta_b).astype(BF16)
        return g_b * state + _dot_tn(kz, v)

    lax.fori_loop(0, n_chunks, bwd, jnp.zeros((DH, DH), F32))


def _retention(lg, rq, rk, rv):
    b, t, _ = rq.shape
    spec = pl.BlockSpec((1, t, DH), lambda bi, hi, lg_ref: (bi, 0, hi))
    return pl.pallas_call(
        functools.partial(_retention_kernel, n_chunks=t // CHUNK),
        grid_spec=pltpu.PrefetchScalarGridSpec(
            num_scalar_prefetch=1,
            grid=(b, HEADS),
            in_specs=[spec, spec, spec],
            out_specs=spec,
        ),
        out_shape=jax.ShapeDtypeStruct((b, t, HEADS * DH), F32),
        compiler_params=_cparams(("parallel", "arbitrary")),
        name="retention",
    )(lg, rq, rk, rv)


def _mlstm_kernel(q_ref, k_ref, v_ref, gc_ref, gr_ref, o_ref, *, n_chunks):
    li = lax.broadcasted_iota(jnp.int32, (CHUNK, CHUNK), 0)
    mi = lax.broadcasted_iota(jnp.int32, (CHUNK, CHUNK), 1)
    causal = li >= mi
    anti = li <= mi

    def step(c, carry, backward):
        ck, nrm, m_st = carry
        rows = pl.ds(pl.multiple_of(c * CHUNK, CHUNK), CHUNK)
        q = q_ref[0, rows, :]
        k = k_ref[0, rows, :]
        v = v_ref[0, rows, :]
        gcol = gc_ref[0, 0, rows, :]
        grow = gr_ref[0, 0, :, rows]
        if backward:
            li_col, bc_col = gcol[:, 4:5], gcol[:, 12:13]
            li_row, bc_row = grow[1:2, :], grow[3:4, :]
            b_last = bc_col[0:1, :]
            mask = anti
        else:
            li_col, bc_col = gcol[:, 0:1], gcol[:, 8:9]
            li_row, bc_row = grow[0:1, :], grow[2:3, :]
            b_last = bc_col[CHUNK - 1:CHUNK, :]
            mask = causal
        log_d = jnp.where(mask, bc_col - bc_row + li_row, NEG)
        log_inter = bc_col + m_st
        m_row = jnp.maximum(log_inter, jnp.max(log_d, axis=1, keepdims=True))
        s = _dot_nt(q, k) * jnp.exp(log_d - m_row)
        w_inter = jnp.exp(log_inter - m_row)
        num = _dot(s.astype(BF16), v) + _dot(q, ck.astype(BF16)) * w_inter
        qn = jnp.sum(q.astype(F32) * nrm, axis=1, keepdims=True)
        den = jnp.sum(s, axis=1, keepdims=True) + qn * w_inter
        denom = jnp.maximum(jnp.abs(den), jnp.exp(-m_row))
        hval = num / denom
        if backward:
            o_ref[0, rows, :] = o_ref[0, rows, :] + hval
        else:
            o_ref[0, rows, :] = hval
        log_u = b_last - bc_col + li_col
        a = jnp.max(log_u, axis=0, keepdims=True)
        w_u = jnp.exp(log_u - a)
        kw = k.astype(F32) * w_u
        upd = _dot_tn(kw.astype(BF16), v)
        upd_n = jnp.sum(kw, axis=0, keepdims=True)
        m_new = jnp.maximum(b_last + m_st, a)
        f = jnp.exp(b_last + m_st - m_new)
        g = jnp.exp(a - m_new)
        return f * ck + g * upd, f * nrm + g * upd_n, m_new

    init = (jnp.zeros((DH, DH), F32), jnp.zeros((1, DH), F32), jnp.zeros((1, 1), F32))
    lax.fori_loop(0, n_chunks, lambda c, s: step(c, s, False), init)
    lax.fori_loop(0, n_chunks, lambda j, s: step(n_chunks - 1 - j, s, True), init)


def _mlstm(mq, mk, mv, gcol, grow):
    b, t, _ = mq.shape
    spec = pl.BlockSpec((1, t, DH), lambda bi, hi: (bi, 0, hi))
    return pl.pallas_call(
        functools.partial(_mlstm_kernel, n_chunks=t // CHUNK),
        grid=(b, HEADS),
        in_specs=[spec, spec, spec,
                  pl.BlockSpec((1, 1, t, LANES), lambda bi, hi: (bi, hi, 0, 0)),
                  pl.BlockSpec((1, 1, 8, t), lambda bi, hi: (bi, hi, 0, 0))],
        out_specs=spec,
        out_shape=jax.ShapeDtypeStruct((b, t, HEADS * DH), F32),
        compiler_params=_cparams(("parallel", "arbitrary")),
        name="mlstm",
    )(mq, mk, mv, gcol, grow)


def _head_norm(y, g):
    outs = []
    for hd in range(HEADS):
        yh = y[:, hd * DH:(hd + 1) * DH]
        mu = jnp.mean(yh, axis=1, keepdims=True)
        yc = yh - mu
        var = jnp.mean(yc * yc, axis=1, keepdims=True)
        outs.append(yc * lax.rsqrt(var + EPS))
    return jnp.concatenate(outs, axis=1) * g


def _out_proj_kernel(h_ref, ret_ref, hm_ref, rg_ref, mo_ref, rgn_ref, mgn_ref, wo_ref,
                     fg_ref, wrt_ref, h1_ref, u2_ref, aff_ref, u2m_ref, *, tr):
    i = pl.program_id(1)
    row = lax.broadcasted_iota(jnp.int32, (tr, 1), 0) + i * tr
    real = row >= PAD
    w = HEADS * DH
    y_ret = _head_norm(ret_ref[0], rgn_ref[...]) * rg_ref[0].astype(F32)
    y_m = _head_norm(mo_ref[0].astype(F32) * hm_ref[0], mgn_ref[...])
    y_ret = jnp.where(real, y_ret, 0.0).astype(BF16)
    y_m = jnp.where(real, y_m, 0.0).astype(BF16)
    h1 = h_ref[0] + _dot(y_ret, wo_ref[0:w, :]) + _dot(y_m, wo_ref[w:2 * w, :])
    h1_ref[0] = h1
    u2 = _rms(h1, fg_ref[...])
    u2m_ref[...] = u2
    _store_segments(u2_ref, u2m_ref, tr)
    logits = _dot_nt(wrt_ref[...], u2.astype(BF16))
    mx = jnp.max(logits, axis=0, keepdims=True)
    ex = jnp.exp(logits - mx)
    aff = ex / jnp.sum(ex, axis=0, keepdims=True)
    coln = lax.broadcasted_iota(jnp.int32, (1, tr), 1) + i * tr
    aff_ref[0] = jnp.where(coln >= PAD, aff, -1.0)


def _out_proj(h_pad, ret, hm, rgs, mos, ret_gn_g, mlstm_gn_g, w_out, ffn_norm_g, w_router):
    b, t, d = h_pad.shape
    tr = _row_tile(t)
    n_tiles = t // tr
    nseg = d // LANES
    w = HEADS * DH
    seq = lambda bi, i: (bi, i, 0)
    const2 = lambda bi, i: (0, 0)
    return pl.pallas_call(
        functools.partial(_out_proj_kernel, tr=tr),
        grid=(b, n_tiles),
        in_specs=[
            pl.BlockSpec((1, tr, d), seq),
            pl.BlockSpec((1, tr, w), seq),
            pl.BlockSpec((1, tr, w), seq),
            pl.BlockSpec((1, tr, w), seq),
            pl.BlockSpec((1, tr, w), seq),
            pl.BlockSpec((1, w), const2),
            pl.BlockSpec((1, w), const2),
            pl.BlockSpec((2 * w, d), const2),
            pl.BlockSpec((1, d), const2),
            pl.BlockSpec((N_EXPERTS, d), const2),
        ],
        out_specs=[
            pl.BlockSpec((1, tr, d), seq),
            pl.BlockSpec((tr * nseg, LANES), lambda bi, i: (bi * n_tiles + i, 0)),
            pl.BlockSpec((1, N_EXPERTS, tr), lambda bi, i: (bi, 0, i)),
        ],
        out_shape=[
            jax.ShapeDtypeStruct((b, t, d), F32),
            jax.ShapeDtypeStruct((b * t * nseg, LANES), F32),
            jax.ShapeDtypeStruct((b, N_EXPERTS, t), F32),
        ],
        scratch_shapes=[pltpu.VMEM((tr, d), F32)],
        compiler_params=_cparams(("parallel", "arbitrary")),
        name="out_proj",
    )(h_pad, ret, hm, rgs, mos, ret_gn_g.reshape(1, w), mlstm_gn_g.reshape(1, w),
      w_out.astype(BF16), ffn_norm_g.reshape(1, d), w_router.T.astype(BF16))


def _select_kernel(aff_ref, pm_ref, dv_ref, cnt_ref, bt_ref, sel_ref, *, t, cap):
    e_n = N_EXPERTS
    nblk = t // LANES
    bits = pltpu.bitcast(aff_ref[0], jnp.int32)
    capf = float(cap)

    def search(_, lohi):
        lo, hi = lohi
        mid = lo + ((hi - lo + 1) >> 1)
        cnt = jnp.sum((bits >= mid).astype(F32), axis=1, keepdims=True)
        ok = cnt >= capf
        return jnp.where(ok, mid, lo), jnp.where(ok, hi, mid - 1)

    lo0 = jnp.zeros((e_n, 1), jnp.int32)
    hi0 = jnp.full((e_n, 1), 0x3F800000, jnp.int32)
    thr, _ = lax.fori_loop(0, 31, search, (lo0, hi0))
    need = capf - jnp.sum((bits > thr).astype(F32), axis=1, keepdims=True)

    ci = lax.broadcasted_iota(jnp.int32, (LANES, LANES), 0)
    cj = lax.broadcasted_iota(jnp.int32, (LANES, LANES), 1)
    upp = (ci <= cj).astype(BF16)
    ei = lax.broadcasted_iota(jnp.int32, (e_n, e_n), 0)
    ej = lax.broadcasted_iota(jnp.int32, (e_n, e_n), 1)
    strict = (ej < ei).astype(BF16)

    carry = jnp.zeros((e_n, 1), F32)
    for j in range(nblk):
        sl = slice(j * LANES, (j + 1) * LANES)
        bb = bits[:, sl]
        eqf = (bb == thr).astype(F32)
        ceq = _dot(eqf.astype(BF16), upp) + carry
        carry = ceq[:, LANES - 1:LANES]
        sel_ref[:, sl] = jnp.where(bb > thr, 1.0, jnp.where(ceq <= need, eqf, 0.0))

    carry = jnp.zeros((e_n, 1), F32)
    tok = lax.broadcasted_iota(jnp.int32, (1, LANES), 1).astype(F32)
    lane_b = lax.broadcasted_iota(jnp.int32, (e_n, LANES), 1)
    btot = jnp.zeros((e_n, LANES), F32)
    for j in range(nblk):
        sl = slice(j * LANES, (j + 1) * LANES)
        selb = sel_ref[:, sl]
        selbf = selb.astype(BF16)
        pin = _dot(selbf, upp) + carry
        carry = pin[:, LANES - 1:LANES]
        btot = jnp.where(lane_b == j, carry, btot)
        pm_ref[0, :, sl] = jnp.where(selb > 0, pin, 0.0)
        rank = _dot(strict, selbf)
        dv_ref[0, :, sl] = (tok + float(j * LANES)) * float(N_EXPERTS) + rank
        cnt_row = jnp.sum(selb, axis=0, keepdims=True)
        cnt_ref[0, sl, :] = jnp.broadcast_to(cnt_row, (LANES, LANES)).T
    bt_ref[0] = jnp.where(lane_b < nblk, btot, carry).astype(jnp.int32)


def _select(aff, cap):
    b, e_n, t = aff.shape
    spec = pl.BlockSpec((1, e_n, t), lambda bi: (bi, 0, 0))
    return pl.pallas_call(
        functools.partial(_select_kernel, t=t, cap=cap),
        grid=(b,),
        in_specs=[spec],
        out_specs=[
            spec,
            spec,
            pl.BlockSpec((1, t, LANES), lambda bi: (bi, 0, 0)),
            pl.BlockSpec((1, e_n, LANES), lambda bi: (bi, 0, 0)),
        ],
        out_shape=[
            jax.ShapeDtypeStruct((b, e_n, t), F32),
            jax.ShapeDtypeStruct((b, e_n, t), F32),
            jax.ShapeDtypeStruct((b, t, LANES), F32),
            jax.ShapeDtypeStruct((b, e_n, LANES), jnp.int32),
        ],
        scratch_shapes=[pltpu.VMEM((e_n, t), F32)],
        compiler_params=_cparams(("parallel",)),
        name="select",
    )(aff)


def _compact_kernel(bt_ref, pm_ref, dv_ref, aff_ref, src_ref, dst_ref, gate_ref,
                    *, t, cap, c_pad, nb):
    bi = pl.program_id(0)
    e_n = N_EXPERTS
    nblk = t // LANES
    gate_ref[...] = jnp.zeros(gate_ref.shape, F32)
    lane_e = lax.broadcasted_iota(jnp.int32, (LANES, LANES), 1)
    slot_l = lax.broadcasted_iota(jnp.int32, (1, LANES), 1)

    def expert(e):
        base = (bi * e_n + e) * LANES

        def ctile(ct, j0):
            c0 = pl.multiple_of(ct * LANES, LANES)
            j_lo = lax.while_loop(
                lambda j: (j < nblk) & (bt_ref[base + jnp.minimum(j, nblk - 1)] <= c0),
                lambda j: j + 1, j0)
            j_hi = lax.while_loop(
                lambda j: (j < nblk) & (bt_ref[base + jnp.minimum(j, nblk - 1)] < c0 + LANES),
                lambda j: j + 1, j_lo)
            j_end = jnp.minimum(j_hi + 1, nblk)
            want = (lax.broadcasted_iota(jnp.int32, (LANES, 1), 0) + (c0 + 1)).astype(F32)

            def blk(j, acc):
                acc_d, acc_g = acc
                cols = pl.ds(pl.multiple_of(j * LANES, LANES), LANES)
                hit = pm_ref[0, e:e + 1, cols] == want
                acc_d = acc_d + jnp.where(hit, dv_ref[0, e:e + 1, cols], 0.0)
                acc_g = acc_g + jnp.where(hit, aff_ref[0, e:e + 1, cols], 0.0)
                return acc_d, acc_g

            zero = jnp.zeros((LANES, LANES), F32)
            acc_d, acc_g = lax.fori_loop(j_lo, j_end, blk, (zero, zero))
            v = jnp.sum(acc_d.T, axis=0, keepdims=True).astype(jnp.int32)
            tok = v >> 4
            rank = v & (N_EXPERTS - 1)
            p = slot_l + (c0 - cap)
            is_pad = p >= 0
            src_ref[0, e:e + 1, pl.ds(c0, LANES)] = bi * t + jnp.where(is_pad, 0, tok)
            dst_ref[0, e:e + 1, pl.ds(c0, LANES)] = jnp.where(
                is_pad, N_EXPERTS * nb * t + (bi * e_n + e) * (c_pad - cap) + p,
                (rank * nb + bi) * t + tok)
            gcol = jnp.sum(acc_g, axis=1, keepdims=True)
            rows = pl.ds(c0, LANES)
            gate_ref[0, rows, :] = jnp.where(lane_e == e, gcol, gate_ref[0, rows, :])
            return j_lo

        lax.fori_loop(0, c_pad // LANES, ctile, 0)

    for e in range(e_n):
        expert(e)


def _compact(btot, pm, dv, aff, cap, c_pad):
    b, e_n, t = aff.shape
    spec = pl.BlockSpec((1, e_n, t), lambda bi, bt: (bi, 0, 0))
    ospec = pl.BlockSpec((1, e_n, c_pad), lambda bi, bt: (bi, 0, 0))
    return pl.pallas_call(
        functools.partial(_compact_kernel, t=t, cap=cap, c_pad=c_pad, nb=b),
        grid_spec=pltpu.PrefetchScalarGridSpec(
            num_scalar_prefetch=1,
            grid=(b,),
            in_specs=[spec, spec, spec],
            out_specs=[ospec, ospec, pl.BlockSpec((1, c_pad, LANES), lambda bi, bt: (bi, 0, 0))],
        ),
        out_shape=[
            jax.ShapeDtypeStruct((b, e_n, c_pad), jnp.int32),
            jax.ShapeDtypeStruct((b, e_n, c_pad), jnp.int32),
            jax.ShapeDtypeStruct((b, c_pad, LANES), F32),
        ],
        compiler_params=_cparams(("parallel",)),
        name="compact",
    )(btot.reshape(b * e_n * LANES), pm, dv, aff)


def _sc_mesh():
    return plsc.VectorSubcoreMesh(core_axis_name="core", subcore_axis_name="subcore")


def _row_gather(table, idx):
    m = idx.shape[0]
    d = table.shape[1]

    @functools.partial(pl.kernel, out_type=jax.ShapeDtypeStruct((m, d), table.dtype),
                       mesh=_sc_mesh(), scratch_types=[])
    def gather_kernel(x_hbm, i_hbm, o_hbm):
        def body(i_vmem, o_vmem):
            pltpu.sync_copy(x_hbm.at[i_vmem.at[0]], o_vmem)

        pltpu.emit_pipeline(
            body,
            grid=(m // SC_WINDOW,),
            in_specs=[pl.BlockSpec((1, SC_WINDOW), lambda i: (0, i))],
            out_specs=[pl.BlockSpec((SC_WINDOW, d), lambda i: (i, 0))],
            core_axis_name=("core", "subcore"),
            dimension_semantics=(pltpu.PARALLEL,),
        )(i_hbm, o_hbm)

    return gather_kernel(table, idx.reshape(1, m))


def _row_scatter(rows, idx, n_out):
    m, d = rows.shape

    @functools.partial(pl.kernel, out_type=jax.ShapeDtypeStruct((n_out, d), rows.dtype),
                       mesh=_sc_mesh(), scratch_types=[])
    def scatter_kernel(x_hbm, i_hbm, o_hbm):
        def body(x_vmem, i_vmem):
            pltpu.sync_copy(x_vmem, o_hbm.at[i_vmem.at[0]])

        pltpu.emit_pipeline(
            body,
            grid=(m // SC_WINDOW,),
            in_specs=[pl.BlockSpec((SC_WINDOW, d), lambda i: (i, 0)),
                      pl.BlockSpec((1, SC_WINDOW), lambda i: (0, i))],
            out_specs=[],
            core_axis_name=("core", "subcore"),
            dimension_semantics=(pltpu.PARALLEL,),
        )(x_hbm, i_hbm)

    return scatter_kernel(rows, idx.reshape(1, m))


def _expert_kernel(gate_ref, x_ref, wg_ref, wu_ref, wd_ref, y_ref, xb_ref, acc_ref, *, c_pad, n_f):
    e = pl.program_id(1)
    f = pl.program_id(2)
    nseg = xb_ref.shape[1] // LANES

    @pl.when(f == 0)
    def _():
        def body(i, _):
            for j in range(nseg):
                lo = x_ref[pl.ds(pl.multiple_of((2 * i * nseg + j) * SUB, SUB), SUB), :]
                hi = x_ref[pl.ds(pl.multiple_of(((2 * i + 1) * nseg + j) * SUB, SUB), SUB), :]
                xb_ref[pl.ds(pl.multiple_of(i * 2 * SUB, 2 * SUB), 2 * SUB), j * LANES:(j + 1) * LANES] = (
                    jnp.concatenate([lo, hi], axis=0).astype(BF16))
            return 0

        lax.fori_loop(0, c_pad // (2 * SUB), body, 0)

    x = xb_ref[...]
    hid = _silu(_dot(x, wg_ref[0])) * _dot(x, wu_ref[0])
    part = _dot(hid.astype(BF16), wd_ref[0])

    @pl.when(f == 0)
    def _():
        acc_ref[...] = part

    @pl.when(f > 0)
    def _():
        acc_ref[...] = acc_ref[...] + part

    @pl.when(f == n_f - 1)
    def _():
        lane = lax.broadcasted_iota(jnp.int32, (c_pad, LANES), 1)
        gcol = jnp.sum(jnp.where(lane == e, gate_ref[0], 0.0), axis=1, keepdims=True)
        acc_ref[...] = acc_ref[...] * gcol
        _store_segments(y_ref, acc_ref, c_pad)


def _experts(gate, xs, w_gate, w_up, w_down):
    b, c_pad, _ = gate.shape
    e_n, d, d_ff = w_gate.shape
    nseg = d // LANES
    fc = 256
    n_f = d_ff // fc
    blk = pl.BlockSpec((c_pad * nseg, LANES), lambda bi, e, f: (bi * e_n + e, 0))
    return pl.pallas_call(
        functools.partial(_expert_kernel, c_pad=c_pad, n_f=n_f),
        grid=(b, e_n, n_f),
        in_specs=[
            pl.BlockSpec((1, c_pad, LANES), lambda bi, e, f: (bi, 0, 0)),
            blk,
            pl.BlockSpec((1, d, fc), lambda bi, e, f: (e, 0, f)),
            pl.BlockSpec((1, d, fc), lambda bi, e, f: (e, 0, f)),
            pl.BlockSpec((1, fc, d), lambda bi, e, f: (e, f, 0)),
        ],
        out_specs=blk,
        out_shape=jax.ShapeDtypeStruct(xs.shape, F32),
        scratch_shapes=[pltpu.VMEM((c_pad, d), BF16), pltpu.VMEM((c_pad, d), F32)],
        compiler_params=_cparams(("parallel", "arbitrary", "arbitrary")),
        name="experts",
    )(gate, xs, w_gate.astype(BF16), w_up.astype(BF16), w_down.astype(BF16))


def _combine_kernel(mc_ref, h1_ref, z_ref, cnt_ref, fg_ref, o_ref, acc_ref, *, n_layers, n_tiles):
    bi = pl.program_id(0)
    i = pl.program_id(1)
    j = pl.program_id(2)

    @pl.when(j == 0)
    def _():
        acc_ref[...] = h1_ref[0]

    @pl.when(j < mc_ref[bi * n_tiles + i])
    def _():
        nseg = acc_ref.shape[1] // LANES
        jf = j.astype(F32)

        def body(g, _):
            rows = pl.ds(pl.multiple_of(g * SUB, SUB), SUB)
            take = cnt_ref[0, rows, 0:1] > jf
            for k in range(nseg):
                seg = z_ref[pl.ds(pl.multiple_of((g * nseg + k) * SUB, SUB), SUB), :]
                cols = slice(k * LANES, (k + 1) * LANES)
                acc_ref[rows, cols] = acc_ref[rows, cols] + jnp.where(take, seg, 0.0)
            return 0

        lax.fori_loop(0, acc_ref.shape[0] // SUB, body, 0)

    @pl.when(j == n_layers - 1)
    def _():
        o_ref[0] = _rms(acc_ref[...], fg_ref[...])


def _combine(maxcnt, h1, z, cnt, final_norm_g):
    b, t, d = h1.shape
    n_layers = N_EXPERTS
    tr = _row_tile(t)
    n_tiles = t // tr

    def z_map(bi, i, j, mc):
        layer = jnp.maximum(jnp.minimum(j, mc[bi * n_tiles + i] - 1), 0)
        return ((layer * b + bi) * n_tiles + i, 0)

    return pl.pallas_call(
        functools.partial(_combine_kernel, n_layers=n_layers, n_tiles=n_tiles),
        grid_spec=pltpu.PrefetchScalarGridSpec(
            num_scalar_prefetch=1,
            grid=(b, n_tiles, n_layers),
            in_specs=[
                pl.BlockSpec((1, tr, d), lambda bi, i, j, mc: (bi, i, 0)),
                pl.BlockSpec((tr * (d // LANES), LANES), z_map),
                pl.BlockSpec((1, tr, LANES), lambda bi, i, j, mc: (bi, i, 0)),
                pl.BlockSpec((1, d), lambda bi, i, j, mc: (0, 0)),
            ],
            out_specs=pl.BlockSpec((1, tr, d), lambda bi, i, j, mc: (bi, i, 0)),
            scratch_shapes=[pltpu.VMEM((tr, d), F32)],
        ),
        out_shape=jax.ShapeDtypeStruct((b, t, d), F32),
        compiler_params=_cparams(("parallel", "arbitrary", "arbitrary")),
        name="combine",
    )(maxcnt, h1, z, cnt, final_norm_g.reshape(1, d))


def _maxcnt_kernel(cnt_ref, o_ref):
    m = jnp.max(cnt_ref[0], axis=0, keepdims=True)
    o_ref[0, 0] = jnp.broadcast_to(m, (8, LANES)).astype(jnp.int32)


def _maxcnt(cnt):
    b, t, _ = cnt.shape
    tr = _row_tile(t)
    n_tiles = t // tr
    out = pl.pallas_call(
        _maxcnt_kernel,
        grid=(b, n_tiles),
        in_specs=[pl.BlockSpec((1, tr, LANES), lambda bi, i: (bi, i, 0))],
        out_specs=pl.BlockSpec((1, 1, 8, LANES), lambda bi, i: (bi, i, 0, 0)),
        out_shape=jax.ShapeDtypeStruct((b, n_tiles, 8, LANES), jnp.int32),
        compiler_params=_cparams(("parallel", "arbitrary")),
        name="maxcnt",
    )(cnt)
    return out[:, :, 0, 0].reshape(b * n_tiles)


def kernel(x, meta_tokens, mix_norm_g, w_in, b_gates, conv_w, conv_b, ret_decay_logit, ret_gn_g,
           mlstm_gn_g, w_out, ffn_norm_g, w_router, w_gate, w_up, w_down, final_norm_g):
    b, s, d = x.shape
    t = CHUNK + s
    n = N_META + s
    cap = 2 * n // N_EXPERTS
    c_pad = -(-cap // LANES) * LANES

    meta = jnp.broadcast_to(meta_tokens.astype(x.dtype)[None], (b, N_META, d))
    h_pad = jnp.concatenate([jnp.zeros((b, PAD, d), x.dtype), meta, x], axis=1)

    half = DH // 2
    pos = jnp.arange(t, dtype=F32) - PAD
    inv = ROPE_BASE ** (-jnp.arange(half, dtype=F32) / half)
    ang = pos[:, None] * inv[None, :]
    cosf = jnp.concatenate([jnp.cos(ang), jnp.cos(ang)], axis=1)
    sinf = jnp.concatenate([-jnp.sin(ang), jnp.sin(ang)], axis=1)

    assert mix_norm_g.shape[0] == 1, "single-layer block only"
    rq, rk, rv, rgs, mq, mk, mv, mos, gcol, grow = _in_proj(
        h_pad, mix_norm_g[0], w_in[0], b_gates[0], conv_w[0], conv_b[0], cosf, sinf)
    lg = jax.nn.log_sigmoid(ret_decay_logit[0].astype(F32))
    ret = _retention(lg, rq, rk, rv)
    hm = _mlstm(mq, mk, mv, gcol, grow)
    h1, u2, aff = _out_proj(h_pad, ret, hm, rgs, mos, ret_gn_g[0], mlstm_gn_g[0], w_out[0],
                            ffn_norm_g[0], w_router[0])
    pm, dv, cnt, btot = _select(aff, cap)
    src, dst, gate = _compact(btot, pm, dv, aff, cap, c_pad)
    nseg = d // LANES
    xs = _row_gather(u2, _segment_ids(src.reshape(-1), nseg))
    ys = _experts(gate, xs, w_gate[0], w_up[0], w_down[0])
    n_rows = N_EXPERTS * b * t + b * N_EXPERTS * (c_pad - cap)
    z = _row_scatter(ys, _segment_ids(dst.reshape(-1), nseg), -(-n_rows // SUB) * SUB * nseg)
    out = _combine(_maxcnt(cnt), h1, z, cnt, final_norm_g)
    return out[:, CHUNK:]
```

```python
import functools

import jax
import jax.numpy as jnp
from jax import lax
from jax.experimental import pallas as pl
from jax.experimental.pallas import tpu as pltpu
from jax.experimental.pallas import tpu_sc as plsc

F32 = jnp.float32
BF16 = jnp.bfloat16

LANES = 128
CHUNK = 128
N_META = 16
PAD = CHUNK - N_META
HEADS = 4
DH = 128
N_EXPERTS = 16
CONV_K = 5
HALO = 8
SUB = 8
SC_WINDOW = 128
EPS = 1e-6
NEG = -1e30
ROPE_BASE = 10000.0
VMEM_LIMIT = 56 * 1024 * 1024


def _cparams(sem, vmem=VMEM_LIMIT, **kw):
    return pltpu.CompilerParams(dimension_semantics=sem, vmem_limit_bytes=vmem, **kw)


def _dot(a, b):
    return jnp.dot(a, b, preferred_element_type=F32)


def _dot_nt(a, b):
    return lax.dot_general(a, b, (((1,), (1,)), ((), ())), preferred_element_type=F32)


def _dot_tn(a, b):
    return lax.dot_general(a, b, (((0,), (0,)), ((), ())), preferred_element_type=F32)


def _split3(x):
    hi = x.astype(BF16)
    r = x - hi.astype(F32)
    mid = r.astype(BF16)
    lo = (r - mid.astype(F32)).astype(BF16)
    return hi, mid, lo


def _dot3_left(m_bf, x):
    hi, mid, lo = _split3(x)
    return _dot(m_bf, hi) + _dot(m_bf, mid) + _dot(m_bf, lo)


def _dot3_right(x, m_bf):
    hi, mid, lo = _split3(x)
    return _dot(hi, m_bf) + _dot(mid, m_bf) + _dot(lo, m_bf)


def _rms(x, g):
    return x * lax.rsqrt(jnp.mean(x * x, axis=-1, keepdims=True) + EPS) * g


def _log_sigmoid(x):
    return jnp.minimum(x, 0.0) - jnp.log1p(jnp.exp(-jnp.abs(x)))


def _silu(x):
    return x * (1.0 / (1.0 + jnp.exp(-x)))


def _sigmoid(x):
    return 1.0 / (1.0 + jnp.exp(-x))


def _row_tile(t):
    return 640 if t % 640 == 0 else CHUNK


def _chunk_unroll(n_chunks):
    return 5 if n_chunks % 5 == 0 else 1


def _to_segments(dst_ref, x):
    r, d = x.shape
    for j in range(d // LANES):
        dst_ref[:, j] = x[:, j * LANES:(j + 1) * LANES].reshape(r // SUB, SUB, LANES)


def _from_segments(src_ref):
    g, nseg = src_ref.shape[0], src_ref.shape[1]
    return jnp.concatenate([src_ref[:, j].reshape(g * SUB, LANES) for j in range(nseg)], axis=1)


def _segment_ids(rows, nseg):
    r = jnp.tile(rows.reshape(-1, SUB), (1, nseg))
    j = (jnp.arange(nseg * SUB, dtype=jnp.int32) // SUB).reshape(1, nseg * SUB)
    return (((r >> 3) * nseg + j) * SUB + (r & (SUB - 1))).reshape(-1)


def _in_proj_kernel(h_ref, hp_ref, hn_ref, ng_ref, wr_ref, wm_ref, wg_ref, wgt_ref,
                    bg_ref, bgt_ref, cw_ref, cb_ref, cos_ref, sin_ref,
                    rq_ref, rk_ref, rv_ref, rg_ref, mq_ref, mk_ref, mv_ref, mo_ref,
                    gc_ref, gr_ref, ext_ref, *, tr, n_tiles):
    i = pl.program_id(1)
    ng = ng_ref[...]
    u = _rms(h_ref[0], ng).astype(BF16)
    row = lax.broadcasted_iota(jnp.int32, (tr, 1), 0) + i * tr
    real = row >= PAD

    cosf = cos_ref[...]
    sinf = sin_ref[...]
    w = HEADS * DH
    pq = _dot(u, wr_ref[:, 0:w])
    pk = _dot(u, wr_ref[:, w:2 * w])
    for hd in range(HEADS):
        sl = slice(hd * DH, (hd + 1) * DH)
        xq = pq[:, sl]
        xk = pk[:, sl]
        rq_ref[0, :, sl] = ((xq * cosf + pltpu.roll(xq, DH // 2, 1) * sinf) * (DH ** -0.5)).astype(BF16)
        rk_ref[0, :, sl] = (xk * cosf + pltpu.roll(xk, DH // 2, 1) * sinf).astype(BF16)
    rv_ref[0] = _dot(u, wr_ref[:, 2 * w:3 * w]).astype(BF16)
    rg_ref[0] = _silu(_dot(u, wr_ref[:, 3 * w:4 * w])).astype(BF16)

    up = _rms(hp_ref[0], ng)
    un = _rms(hn_ref[0], ng)
    up = jnp.where(i == 0, 0.0, up)
    un = jnp.where(i == n_tiles - 1, 0.0, un)
    uh = jnp.concatenate([up, un], axis=0).astype(BF16)
    halo = _dot(uh, wm_ref[:, 0:2 * w])
    ext_ref[0:HALO, :] = halo[0:HALO]
    ext_ref[HALO:HALO + tr, :] = _dot(u, wm_ref[:, 0:2 * w])
    ext_ref[HALO + tr:2 * HALO + tr, :] = halo[HALO:2 * HALO]
    conv = cb_ref[...] + jnp.zeros((tr, 2 * w), F32)
    for k in range(CONV_K):
        off = HALO + k - CONV_K // 2
        conv = conv + cw_ref[k:k + 1, :] * ext_ref[off:off + tr, :]
    qk = jnp.where(real, _silu(conv), 0.0)
    mq_ref[0] = (qk[:, 0:w] * (DH ** -0.5)).astype(BF16)
    mk_ref[0] = qk[:, w:2 * w].astype(BF16)
    mv_ref[0] = _dot(u, wm_ref[:, 2 * w:3 * w]).astype(BF16)
    mo_ref[0] = _sigmoid(_dot(u, wm_ref[:, 3 * w:4 * w])).astype(BF16)

    lane = lax.broadcasted_iota(jnp.int32, (tr, LANES), 1)
    gts = _dot(u, wg_ref[...]) + bg_ref[...]
    gts = jnp.where(lane < 2 * HEADS, jnp.where(real, gts, NEG),
                    jnp.where(real & (lane < 4 * HEADS), _log_sigmoid(gts), 0.0))
    ci = lax.broadcasted_iota(jnp.int32, (CHUNK, CHUNK), 0)
    cj = lax.broadcasted_iota(jnp.int32, (CHUNK, CHUNK), 1)
    low = (ci >= cj).astype(BF16)
    upp = (ci <= cj).astype(BF16)
    lane_c = lax.broadcasted_iota(jnp.int32, (CHUNK, LANES), 1)
    for c in range(tr // CHUNK):
        g = gts[c * CHUNK:(c + 1) * CHUNK]
        lf = jnp.where(lane_c >= 2 * HEADS, g, 0.0)
        pre = _dot3_left(low, lf)
        suf = _dot3_left(upp, lf)
        full = jnp.where(lane_c < 2 * HEADS, g, jnp.where(lane_c < 3 * HEADS, pre, suf))
        for hd in range(HEADS):
            rolled = full if hd == 0 else pltpu.roll(full, LANES - hd, 1)
            gc_ref[0, hd, c * CHUNK:(c + 1) * CHUNK, :] = rolled

    rown = lax.broadcasted_iota(jnp.int32, (4 * HEADS, tr), 0)
    coln = lax.broadcasted_iota(jnp.int32, (1, tr), 1) + i * tr
    realr = coln >= PAD
    gtr = _dot_nt(wgt_ref[...], u) + bgt_ref[...]
    gtr = jnp.where(rown < 2 * HEADS, jnp.where(realr, gtr, NEG),
                    jnp.where(realr, _log_sigmoid(gtr), 0.0))
    rown_c = lax.broadcasted_iota(jnp.int32, (4 * HEADS, CHUNK), 0)
    for c in range(tr // CHUNK):
        g = gtr[:, c * CHUNK:(c + 1) * CHUNK]
        lf = jnp.where(rown_c >= 2 * HEADS, g, 0.0)
        pre = _dot3_right(lf, upp)
        suf = _dot3_right(lf, low)
        zero = jnp.zeros((4, CHUNK), F32)
        for hd in range(HEADS):
            blk = jnp.concatenate(
                [g[hd:hd + 1], g[HEADS + hd:HEADS + hd + 1],
                 pre[2 * HEADS + hd:2 * HEADS + hd + 1],
                 suf[3 * HEADS + hd:3 * HEADS + hd + 1], zero], axis=0)
            gr_ref[0, hd, :, c * CHUNK:(c + 1) * CHUNK] = blk


def _in_proj(h_pad, norm_g, w_in, b_gates, conv_w, conv_b, cosf, sinf):
    b, t, d = h_pad.shape
    tr = _row_tile(t)
    n_tiles = t // tr
    w = HEADS * DH
    wr = w_in[:, 0:4 * w].astype(BF16)
    wm = w_in[:, 4 * w:8 * w].astype(BF16)
    wg = jnp.pad(w_in[:, 8 * w:], ((0, 0), (0, LANES - 4 * HEADS))).astype(BF16)
    wgt = w_in[:, 8 * w:].T.astype(BF16)
    bg = jnp.pad(b_gates, (0, LANES - 4 * HEADS)).reshape(1, LANES)
    bgt = b_gates.reshape(4 * HEADS, 1)
    hb = tr // HALO
    last_h = t // HALO - 1
    seq = lambda bi, i: (bi, i, 0)
    const2 = lambda bi, i: (0, 0)
    out_bf = jax.ShapeDtypeStruct((b, t, w), BF16)
    kern = functools.partial(_in_proj_kernel, tr=tr, n_tiles=n_tiles)
    return pl.pallas_call(
        kern,
        grid=(b, n_tiles),
        in_specs=[
            pl.BlockSpec((1, tr, d), seq),
            pl.BlockSpec((1, HALO, d), lambda bi, i: (bi, jnp.maximum(i * hb - 1, 0), 0)),
            pl.BlockSpec((1, HALO, d), lambda bi, i: (bi, jnp.minimum((i + 1) * hb, last_h), 0)),
            pl.BlockSpec((1, d), const2),
            pl.BlockSpec((d, 4 * w), const2),
            pl.BlockSpec((d, 4 * w), const2),
            pl.BlockSpec((d, LANES), const2),
            pl.BlockSpec((4 * HEADS, d), const2),
            pl.BlockSpec((1, LANES), const2),
            pl.BlockSpec((4 * HEADS, 1), const2),
            pl.BlockSpec((CONV_K, 2 * w), const2),
            pl.BlockSpec((1, 2 * w), const2),
            pl.BlockSpec((tr, DH), lambda bi, i: (i, 0)),
            pl.BlockSpec((tr, DH), lambda bi, i: (i, 0)),
        ],
        out_specs=[pl.BlockSpec((1, tr, w), seq)] * 8 + [
            pl.BlockSpec((1, HEADS, tr, LANES), lambda bi, i: (bi, 0, i, 0)),
            pl.BlockSpec((1, HEADS, 8, tr), lambda bi, i: (bi, 0, 0, i)),
        ],
        out_shape=[out_bf] * 8 + [
            jax.ShapeDtypeStruct((b, HEADS, t, LANES), F32),
            jax.ShapeDtypeStruct((b, HEADS, 8, t), F32),
        ],
        scratch_shapes=[pltpu.VMEM((tr + 2 * HALO, 2 * w), F32)],
        compiler_params=_cparams(("parallel", "arbitrary")),
        name="in_proj",
    )(h_pad, h_pad, h_pad, norm_g.reshape(1, d), wr, wm, wg, wgt, bg, bgt,
      conv_w, conv_b.reshape(1, 2 * w), cosf, sinf)


def _retention_kernel(lg_ref, q_ref, k_ref, v_ref, o_ref, *, n_chunks):
    hd = pl.program_id(1)
    lgf = lg_ref[0, hd]
    lgb = lg_ref[1, hd]
    li = lax.broadcasted_iota(jnp.int32, (CHUNK, CHUNK), 0)
    mi = lax.broadcasted_iota(jnp.int32, (CHUNK, CHUNK), 1)
    diff = (li - mi).astype(F32)
    decay = (jnp.where(diff >= 0, jnp.exp(lgf * jnp.maximum(diff, 0.0)), 0.0)
             + jnp.where(diff <= 0, jnp.exp(lgb * jnp.maximum(-diff, 0.0)), 0.0))
    lcol = lax.broadcasted_iota(jnp.int32, (CHUNK, 1), 0).astype(F32)
    one = jnp.ones((1, 1), F32)
    zeta_f = jnp.exp(lgf * (CHUNK - 1.0 - lcol))
    xi_f = jnp.exp(lgf * (lcol + 1.0))
    g_f = jnp.exp(lgf * CHUNK * one)
    zeta_b = jnp.exp(lgb * lcol)
    xi_b = jnp.exp(lgb * (CHUNK - lcol))
    g_b = jnp.exp(lgb * CHUNK * one)

    def fwd(c, state):
        rows = pl.ds(pl.multiple_of(c * CHUNK, CHUNK), CHUNK)
        q = q_ref[0, rows, :]
        k = k_ref[0, rows, :]
        v = v_ref[0, rows, :]
        s = _dot_nt(q, k) * decay
        intra = _dot(s.astype(BF16), v)
        cross = _dot(q, state.astype(BF16)) * xi_f
        o_ref[0, rows, :] = intra + cross
        kz = (k.astype(F32) * zeta_f).astype(BF16)
        return g_f * state + _dot_tn(kz, v)

    unroll = _chunk_unroll(n_chunks)
    lax.fori_loop(0, n_chunks, fwd, jnp.zeros((DH, DH), F32), unroll=unroll)

    def bwd(j, state):
        c = n_chunks - 1 - j
        rows = pl.ds(pl.multiple_of(c * CHUNK, CHUNK), CHUNK)
        q = q_ref[0, rows, :]
        k = k_ref[0, rows, :]
        v = v_ref[0, rows, :]
        cross = _dot(q, state.astype(BF16)) * xi_b
        o_ref[0, rows, :] = o_ref[0, rows, :] + cross
        kz = (k.astype(F32) * zeta_b).astype(BF16)
        return g_b * state + _dot_tn(kz, v)

    lax.fori_loop(0, n_chunks, bwd, jnp.zeros((DH, DH), F32), unroll=unroll)


def _retention(lg, rq, rk, rv):
    b, t, _ = rq.shape
    spec = pl.BlockSpec((1, t, DH), lambda bi, hi, lg_ref: (bi, 0, hi))
    return pl.pallas_call(
        functools.partial(_retention_kernel, n_chunks=t // CHUNK),
        grid_spec=pltpu.PrefetchScalarGridSpec(
            num_scalar_prefetch=1,
            grid=(b, HEADS),
            in_specs=[spec, spec, spec],
            out_specs=spec,
        ),
        out_shape=jax.ShapeDtypeStruct((b, t, HEADS * DH), F32),
        compiler_params=_cparams(("parallel", "arbitrary")),
        name="retention",
    )(lg, rq, rk, rv)


def _mlstm_kernel(q_ref, k_ref, v_ref, gc_ref, gr_ref, o_ref, *, n_chunks):
    li = lax.broadcasted_iota(jnp.int32, (CHUNK, CHUNK), 0)
    mi = lax.broadcasted_iota(jnp.int32, (CHUNK, CHUNK), 1)
    causal = li >= mi
    anti = li <= mi

    def step(c, carry, backward):
        ck, nrm, m_st = carry
        rows = pl.ds(pl.multiple_of(c * CHUNK, CHUNK), CHUNK)
        q = q_ref[0, rows, :]
        k = k_ref[0, rows, :]
        v = v_ref[0, rows, :]
        gcol = gc_ref[0, 0, rows, :]
        grow = gr_ref[0, 0, :, rows]
        if backward:
            li_col, bc_col = gcol[:, 4:5], gcol[:, 12:13]
            li_row, bc_row = grow[1:2, :], grow[3:4, :]
            b_last = bc_col[0:1, :]
            mask = anti
        else:
            li_col, bc_col = gcol[:, 0:1], gcol[:, 8:9]
            li_row, bc_row = grow[0:1, :], grow[2:3, :]
            b_last = bc_col[CHUNK - 1:CHUNK, :]
            mask = causal
        log_d = jnp.where(mask, bc_col - bc_row + li_row, NEG)
        log_inter = bc_col + m_st
        m_row = jnp.maximum(log_inter, jnp.max(log_d, axis=1, keepdims=True))
        s = _dot_nt(q, k) * jnp.exp(log_d - m_row)
        w_inter = jnp.exp(log_inter - m_row)
        num = _dot(s.astype(BF16), v) + _dot(q, ck.astype(BF16)) * w_inter
        qn = jnp.sum(q.astype(F32) * nrm, axis=1, keepdims=True)
        den = jnp.sum(s, axis=1, keepdims=True) + qn * w_inter
        denom = jnp.maximum(jnp.abs(den), jnp.exp(-m_row))
        hval = num / denom
        if backward:
            o_ref[0, rows, :] = o_ref[0, rows, :] + hval
        else:
            o_ref[0, rows, :] = hval
        log_u = b_last - bc_col + li_col
        a = jnp.max(log_u, axis=0, keepdims=True)
        w_u = jnp.exp(log_u - a)
        kw = k.astype(F32) * w_u
        upd = _dot_tn(kw.astype(BF16), v)
        upd_n = jnp.sum(kw, axis=0, keepdims=True)
        m_new = jnp.maximum(b_last + m_st, a)
        f = jnp.exp(b_last + m_st - m_new)
        g = jnp.exp(a - m_new)
        return f * ck + g * upd, f * nrm + g * upd_n, m_new

    init = (jnp.zeros((DH, DH), F32), jnp.zeros((1, DH), F32), jnp.zeros((1, 1), F32))
    lax.fori_loop(0, n_chunks, lambda c, s: step(c, s, False), init)
    lax.fori_loop(0, n_chunks, lambda j, s: step(n_chunks - 1 - j, s, True), init)


def _mlstm(mq, mk, mv, gcol, grow):
    b, t, _ = mq.shape
    spec = pl.BlockSpec((1, t, DH), lambda bi, hi: (bi, 0, hi))
    return pl.pallas_call(
        functools.partial(_mlstm_kernel, n_chunks=t // CHUNK),
        grid=(b, HEADS),
        in_specs=[spec, spec, spec,
                  pl.BlockSpec((1, 1, t, LANES), lambda bi, hi: (bi, hi, 0, 0)),
                  pl.BlockSpec((1, 1, 8, t), lambda bi, hi: (bi, hi, 0, 0))],
        out_specs=spec,
        out_shape=jax.ShapeDtypeStruct((b, t, HEADS * DH), F32),
        compiler_params=_cparams(("parallel", "arbitrary")),
        name="mlstm",
    )(mq, mk, mv, gcol, grow)


def _head_norm(y, g):
    outs = []
    for hd in range(HEADS):
        yh = y[:, hd * DH:(hd + 1) * DH]
        mu = jnp.mean(yh, axis=1, keepdims=True)
        yc = yh - mu
        var = jnp.mean(yc * yc, axis=1, keepdims=True)
        outs.append(yc * lax.rsqrt(var + EPS))
    return jnp.concatenate(outs, axis=1) * g


def _out_proj_kernel(h_ref, ret_ref, hm_ref, rg_ref, mo_ref, rgn_ref, mgn_ref, wo_ref,
                     fg_ref, wrt_ref, h1_ref, u2_ref, aff_ref, *, tr):
    i = pl.program_id(1)
    row = lax.broadcasted_iota(jnp.int32, (tr, 1), 0) + i * tr
    real = row >= PAD
    w = HEADS * DH
    y_ret = _head_norm(ret_ref[0], rgn_ref[...]) * rg_ref[0].astype(F32)
    y_m = _head_norm(mo_ref[0].astype(F32) * hm_ref[0], mgn_ref[...])
    y_ret = jnp.where(real, y_ret, 0.0).astype(BF16)
    y_m = jnp.where(real, y_m, 0.0).astype(BF16)
    h1 = h_ref[0] + _dot(y_ret, wo_ref[0:w, :]) + _dot(y_m, wo_ref[w:2 * w, :])
    h1_ref[0] = h1
    u2 = _rms(h1, fg_ref[...])
    _to_segments(u2_ref, u2)
    logits = _dot_nt(wrt_ref[...], u2.astype(BF16))
    mx = jnp.max(logits, axis=0, keepdims=True)
    ex = jnp.exp(logits - mx)
    aff = ex / jnp.sum(ex, axis=0, keepdims=True)
    coln = lax.broadcasted_iota(jnp.int32, (1, tr), 1) + i * tr
    aff_ref[0] = jnp.where(coln >= PAD, aff, -1.0)


def _out_proj(h_pad, ret, hm, rgs, mos, ret_gn_g, mlstm_gn_g, w_out, ffn_norm_g, w_router):
    b, t, d = h_pad.shape
    tr = _row_tile(t)
    n_tiles = t // tr
    nseg = d // LANES
    w = HEADS * DH
    seq = lambda bi, i: (bi, i, 0)
    const2 = lambda bi, i: (0, 0)
    return pl.pallas_call(
        functools.partial(_out_proj_kernel, tr=tr),
        grid=(b, n_tiles),
        in_specs=[
            pl.BlockSpec((1, tr, d), seq),
            pl.BlockSpec((1, tr, w), seq),
            pl.BlockSpec((1, tr, w), seq),
            pl.BlockSpec((1, tr, w), seq),
            pl.BlockSpec((1, tr, w), seq),
            pl.BlockSpec((1, w), const2),
            pl.BlockSpec((1, w), const2),
            pl.BlockSpec((2 * w, d), const2),
            pl.BlockSpec((1, d), const2),
            pl.BlockSpec((N_EXPERTS, d), const2),
        ],
        out_specs=[
            pl.BlockSpec((1, tr, d), seq),
            pl.BlockSpec((tr // SUB, nseg, SUB, LANES), lambda bi, i: (bi * n_tiles + i, 0, 0, 0)),
            pl.BlockSpec((1, N_EXPERTS, tr), lambda bi, i: (bi, 0, i)),
        ],
        out_shape=[
            jax.ShapeDtypeStruct((b, t, d), F32),
            jax.ShapeDtypeStruct((b * t // SUB, nseg, SUB, LANES), F32),
            jax.ShapeDtypeStruct((b, N_EXPERTS, t), F32),
        ],
        compiler_params=_cparams(("parallel", "arbitrary")),
        name="out_proj",
    )(h_pad, ret, hm, rgs, mos, ret_gn_g.reshape(1, w), mlstm_gn_g.reshape(1, w),
      w_out.astype(BF16), ffn_norm_g.reshape(1, d), w_router.T.astype(BF16))


def _select_kernel(aff_ref, pm_ref, dv_ref, cnt_ref, bt_ref, sel_ref, *, t, cap):
    e_n = N_EXPERTS
    nblk = t // LANES
    bits = pltpu.bitcast(aff_ref[0], jnp.int32)
    capf = float(cap)

    def search(_, lohi):
        lo, hi = lohi
        mid = lo + ((hi - lo + 1) >> 1)
        cnt = jnp.sum((bits >= mid).astype(F32), axis=1, keepdims=True)
        ok = cnt >= capf
        return jnp.where(ok, mid, lo), jnp.where(ok, hi, mid - 1)

    lo0 = jnp.zeros((e_n, 1), jnp.int32)
    hi0 = jnp.full((e_n, 1), 0x3F800000, jnp.int32)
    thr, _ = lax.fori_loop(0, 31, search, (lo0, hi0))
    need = capf - jnp.sum((bits > thr).astype(F32), axis=1, keepdims=True)

    ci = lax.broadcasted_iota(jnp.int32, (LANES, LANES), 0)
    cj = lax.broadcasted_iota(jnp.int32, (LANES, LANES), 1)
    upp = (ci <= cj).astype(BF16)
    ei = lax.broadcasted_iota(jnp.int32, (e_n, e_n), 0)
    ej = lax.broadcasted_iota(jnp.int32, (e_n, e_n), 1)
    strict = (ej < ei).astype(BF16)

    carry = jnp.zeros((e_n, 1), F32)
    for j in range(nblk):
        sl = slice(j * LANES, (j + 1) * LANES)
        bb = bits[:, sl]
        eqf = (bb == thr).astype(F32)
        ceq = _dot(eqf.astype(BF16), upp) + carry
        carry = ceq[:, LANES - 1:LANES]
        sel_ref[:, sl] = jnp.where(bb > thr, 1.0, jnp.where(ceq <= need, eqf, 0.0))

    carry = jnp.zeros((e_n, 1), F32)
    tok = lax.broadcasted_iota(jnp.int32, (1, LANES), 1).astype(F32)
    lane_b = lax.broadcasted_iota(jnp.int32, (e_n, LANES), 1)
    btot = jnp.zeros((e_n, LANES), F32)
    for j in range(nblk):
        sl = slice(j * LANES, (j + 1) * LANES)
        selb = sel_ref[:, sl]
        selbf = selb.astype(BF16)
        pin = _dot(selbf, upp) + carry
        carry = pin[:, LANES - 1:LANES]
        btot = jnp.where(lane_b == j, carry, btot)
        pm_ref[0, :, sl] = jnp.where(selb > 0, pin, 0.0)
        rank = _dot(strict, selbf)
        dv_ref[0, :, sl] = (tok + float(j * LANES)) * float(N_EXPERTS) + rank
        cnt_row = jnp.sum(selb, axis=0, keepdims=True)
        cnt_ref[0, sl, :] = jnp.broadcast_to(cnt_row, (LANES, LANES)).T
    bt_ref[0] = jnp.where(lane_b < nblk, btot, carry).astype(jnp.int32)


def _select(aff, cap):
    b, e_n, t = aff.shape
    spec = pl.BlockSpec((1, e_n, t), lambda bi: (bi, 0, 0))
    return pl.pallas_call(
        functools.partial(_select_kernel, t=t, cap=cap),
        grid=(b,),
        in_specs=[spec],
        out_specs=[
            spec,
            spec,
            pl.BlockSpec((1, t, LANES), lambda bi: (bi, 0, 0)),
            pl.BlockSpec((1, e_n, LANES), lambda bi: (bi, 0, 0)),
        ],
        out_shape=[
            jax.ShapeDtypeStruct((b, e_n, t), F32),
            jax.ShapeDtypeStruct((b, e_n, t), F32),
            jax.ShapeDtypeStruct((b, t, LANES), F32),
            jax.ShapeDtypeStruct((b, e_n, LANES), jnp.int32),
        ],
        scratch_shapes=[pltpu.VMEM((e_n, t), F32)],
        compiler_params=_cparams(("parallel",)),
        name="select",
    )(aff)


def _compact_kernel(bt_ref, pm_ref, dv_ref, aff_ref, src_ref, dst_ref, gate_ref,
                    *, t, cap, c_pad, nb):
    bi = pl.program_id(0)
    e_n = N_EXPERTS
    nblk = t // LANES
    gate_ref[...] = jnp.zeros(gate_ref.shape, F32)
    lane_e = lax.broadcasted_iota(jnp.int32, (LANES, LANES), 1)
    slot_l = lax.broadcasted_iota(jnp.int32, (1, LANES), 1)

    def expert(e):
        base = (bi * e_n + e) * LANES

        def ctile(ct, j0):
            c0 = pl.multiple_of(ct * LANES, LANES)
            j_lo = lax.while_loop(
                lambda j: (j < nblk) & (bt_ref[base + jnp.minimum(j, nblk - 1)] <= c0),
                lambda j: j + 1, j0)
            j_hi = lax.while_loop(
                lambda j: (j < nblk) & (bt_ref[base + jnp.minimum(j, nblk - 1)] < c0 + LANES),
                lambda j: j + 1, j_lo)
            j_end = jnp.minimum(j_hi + 1, nblk)
            want = (lax.broadcasted_iota(jnp.int32, (LANES, 1), 0) + (c0 + 1)).astype(F32)

            def blk(j, acc):
                acc_d, acc_g = acc
                cols = pl.ds(pl.multiple_of(j * LANES, LANES), LANES)
                hit = pm_ref[0, e:e + 1, cols] == want
                acc_d = acc_d + jnp.where(hit, dv_ref[0, e:e + 1, cols], 0.0)
                acc_g = acc_g + jnp.where(hit, aff_ref[0, e:e + 1, cols], 0.0)
                return acc_d, acc_g

            zero = jnp.zeros((LANES, LANES), F32)
            acc_d, acc_g = lax.fori_loop(j_lo, j_end, blk, (zero, zero))
            v = jnp.sum(acc_d.T, axis=0, keepdims=True).astype(jnp.int32)
            tok = v >> 4
            rank = v & (N_EXPERTS - 1)
            p = slot_l + (c0 - cap)
            is_pad = p >= 0
            src_ref[0, e:e + 1, pl.ds(c0, LANES)] = bi * t + jnp.where(is_pad, 0, tok)
            dst_ref[0, e:e + 1, pl.ds(c0, LANES)] = jnp.where(
                is_pad, N_EXPERTS * nb * t + (bi * e_n + e) * (c_pad - cap) + p,
                (rank * nb + bi) * t + tok)
            gcol = jnp.sum(acc_g, axis=1, keepdims=True)
            rows = pl.ds(c0, LANES)
            gate_ref[0, rows, :] = jnp.where(lane_e == e, gcol, gate_ref[0, rows, :])
            return j_lo

        lax.fori_loop(0, c_pad // LANES, ctile, 0)

    for e in range(e_n):
        expert(e)


def _compact(btot, pm, dv, aff, cap, c_pad):
    b, e_n, t = aff.shape
    spec = pl.BlockSpec((1, e_n, t), lambda bi, bt: (bi, 0, 0))
    ospec = pl.BlockSpec((1, e_n, c_pad), lambda bi, bt: (bi, 0, 0))
    return pl.pallas_call(
        functools.partial(_compact_kernel, t=t, cap=cap, c_pad=c_pad, nb=b),
        grid_spec=pltpu.PrefetchScalarGridSpec(
            num_scalar_prefetch=1,
            grid=(b,),
            in_specs=[spec, spec, spec],
            out_specs=[ospec, ospec, pl.BlockSpec((1, c_pad, LANES), lambda bi, bt: (bi, 0, 0))],
        ),
        out_shape=[
            jax.ShapeDtypeStruct((b, e_n, c_pad), jnp.int32),
            jax.ShapeDtypeStruct((b, e_n, c_pad), jnp.int32),
            jax.ShapeDtypeStruct((b, c_pad, LANES), F32),
        ],
        compiler_params=_cparams(("parallel",)),
        name="compact",
    )(btot.reshape(b * e_n * LANES), pm, dv, aff)


def _sc_mesh():
    return plsc.VectorSubcoreMesh(core_axis_name="core", subcore_axis_name="subcore")


def _row_gather(table, idx):
    m = idx.shape[0]
    d = table.shape[1]

    @functools.partial(pl.kernel, out_type=jax.ShapeDtypeStruct((m, d), table.dtype),
                       mesh=_sc_mesh(), scratch_types=[])
    def gather_kernel(x_hbm, i_hbm, o_hbm):
        def body(i_vmem, o_vmem):
            pltpu.sync_copy(x_hbm.at[i_vmem.at[0]], o_vmem)

        pltpu.emit_pipeline(
            body,
            grid=(m // SC_WINDOW,),
            in_specs=[pl.BlockSpec((1, SC_WINDOW), lambda i: (0, i))],
            out_specs=[pl.BlockSpec((SC_WINDOW, d), lambda i: (i, 0))],
            core_axis_name=("core", "subcore"),
            dimension_semantics=(pltpu.PARALLEL,),
        )(i_hbm, o_hbm)

    return gather_kernel(table, idx.reshape(1, m))


def _row_scatter(rows, idx, n_out):
    m, d = rows.shape

    @functools.partial(pl.kernel, out_type=jax.ShapeDtypeStruct((n_out, d), rows.dtype),
                       mesh=_sc_mesh(), scratch_types=[])
    def scatter_kernel(x_hbm, i_hbm, o_hbm):
        def body(x_vmem, i_vmem):
            pltpu.sync_copy(x_vmem, o_hbm.at[i_vmem.at[0]])

        pltpu.emit_pipeline(
            body,
            grid=(m // SC_WINDOW,),
            in_specs=[pl.BlockSpec((SC_WINDOW, d), lambda i: (i, 0)),
                      pl.BlockSpec((1, SC_WINDOW), lambda i: (0, i))],
            out_specs=[],
            core_axis_name=("core", "subcore"),
            dimension_semantics=(pltpu.PARALLEL,),
        )(x_hbm, i_hbm)

    return scatter_kernel(rows, idx.reshape(1, m))


def _expert_kernel(gate_ref, x_ref, wg_ref, wu_ref, wd_ref, y_ref, xb_ref, acc_ref, *, c_pad, n_f):
    e = pl.program_id(1)
    f = pl.program_id(2)

    @pl.when(f == 0)
    def _():
        xb_ref[...] = _from_segments(x_ref).astype(BF16)

    x = xb_ref[...]
    hid = _silu(_dot(x, wg_ref[0].astype(BF16))) * _dot(x, wu_ref[0].astype(BF16))
    part = _dot(hid.astype(BF16), wd_ref[0].astype(BF16))

    @pl.when(f == 0)
    def _():
        acc_ref[...] = part

    @pl.when(f > 0)
    def _():
        acc_ref[...] = acc_ref[...] + part

    @pl.when(f == n_f - 1)
    def _():
        lane = lax.broadcasted_iota(jnp.int32, (c_pad, LANES), 1)
        gcol = jnp.sum(jnp.where(lane == e, gate_ref[0], 0.0), axis=1, keepdims=True)
        _to_segments(y_ref, acc_ref[...] * gcol)


def _experts(gate, xs, w_gate, w_up, w_down):
    b, c_pad, _ = gate.shape
    e_n, d, d_ff = w_gate.shape
    nseg = d // LANES
    fc = 256
    n_f = d_ff // fc
    blk = pl.BlockSpec((c_pad // SUB, nseg, SUB, LANES), lambda bi, e, f: (bi * e_n + e, 0, 0, 0))
    return pl.pallas_call(
        functools.partial(_expert_kernel, c_pad=c_pad, n_f=n_f),
        grid=(b, e_n, n_f),
        in_specs=[
            pl.BlockSpec((1, c_pad, LANES), lambda bi, e, f: (bi, 0, 0)),
            blk,
            pl.BlockSpec((1, d, fc), lambda bi, e, f: (e, 0, f)),
            pl.BlockSpec((1, d, fc), lambda bi, e, f: (e, 0, f)),
            pl.BlockSpec((1, fc, d), lambda bi, e, f: (e, f, 0)),
        ],
        out_specs=blk,
        out_shape=jax.ShapeDtypeStruct(xs.shape, F32),
        scratch_shapes=[pltpu.VMEM((c_pad, d), BF16), pltpu.VMEM((c_pad, d), F32)],
        compiler_params=_cparams(("parallel", "arbitrary", "arbitrary")),
        name="experts",
    )(gate, xs, w_gate, w_up, w_down)


def _combine_kernel(mc_ref, h1_ref, z_ref, cnt_ref, fg_ref, o_ref, acc_ref, *, n_layers, n_tiles):
    bi = pl.program_id(0)
    i = pl.program_id(1)
    j = pl.program_id(2)

    @pl.when(j == 0)
    def _():
        _to_segments(acc_ref, h1_ref[0])

    @pl.when(j < mc_ref[bi * n_tiles + i])
    def _():
        g = acc_ref.shape[0]
        take = cnt_ref[0].reshape(g, 1, SUB, LANES) > j.astype(F32)
        acc_ref[...] = acc_ref[...] + jnp.where(take, z_ref[...], 0.0)

    @pl.when(j == n_layers - 1)
    def _():
        o_ref[0] = _rms(_from_segments(acc_ref), fg_ref[...])


def _combine(maxcnt, h1, z, cnt, final_norm_g):
    b, t, d = h1.shape
    n_layers = N_EXPERTS
    tr = _row_tile(t)
    n_tiles = t // tr

    def z_map(bi, i, j, mc):
        layer = jnp.maximum(jnp.minimum(j, mc[bi * n_tiles + i] - 1), 0)
        return ((layer * b + bi) * n_tiles + i, 0, 0, 0)

    seg_block = (tr // SUB, d // LANES, SUB, LANES)
    return pl.pallas_call(
        functools.partial(_combine_kernel, n_layers=n_layers, n_tiles=n_tiles),
        grid_spec=pltpu.PrefetchScalarGridSpec(
            num_scalar_prefetch=1,
            grid=(b, n_tiles, n_layers),
            in_specs=[
                pl.BlockSpec((1, tr, d), lambda bi, i, j, mc: (bi, i, 0)),
                pl.BlockSpec(seg_block, z_map),
                pl.BlockSpec((1, tr, LANES), lambda bi, i, j, mc: (bi, i, 0)),
                pl.BlockSpec((1, d), lambda bi, i, j, mc: (0, 0)),
            ],
            out_specs=pl.BlockSpec((1, tr, d), lambda bi, i, j, mc: (bi, i, 0)),
            scratch_shapes=[pltpu.VMEM(seg_block, F32)],
        ),
        out_shape=jax.ShapeDtypeStruct((b, t, d), F32),
        compiler_params=_cparams(("parallel", "arbitrary", "arbitrary")),
        name="combine",
    )(maxcnt, h1, z, cnt, final_norm_g.reshape(1, d))


def _maxcnt_kernel(cnt_ref, o_ref):
    m = jnp.max(cnt_ref[0], axis=0, keepdims=True)
    o_ref[0, 0] = jnp.broadcast_to(m, (8, LANES)).astype(jnp.int32)


def _maxcnt(cnt):
    b, t, _ = cnt.shape
    tr = _row_tile(t)
    n_tiles = t // tr
    out = pl.pallas_call(
        _maxcnt_kernel,
        grid=(b, n_tiles),
        in_specs=[pl.BlockSpec((1, tr, LANES), lambda bi, i: (bi, i, 0))],
        out_specs=pl.BlockSpec((1, 1, 8, LANES), lambda bi, i: (bi, i, 0, 0)),
        out_shape=jax.ShapeDtypeStruct((b, n_tiles, 8, LANES), jnp.int32),
        compiler_params=_cparams(("parallel", "arbitrary")),
        name="maxcnt",
    )(cnt)
    return out[:, :, 0, 0].reshape(b * n_tiles)


def kernel(x, meta_tokens, mix_norm_g, w_in, b_gates, conv_w, conv_b, ret_decay_logit, ret_gn_g,
           mlstm_gn_g, w_out, ffn_norm_g, w_router, w_gate, w_up, w_down, final_norm_g):
    b, s, d = x.shape
    t = CHUNK + s
    n = N_META + s
    cap = 2 * n // N_EXPERTS
    c_pad = -(-cap // LANES) * LANES

    meta = jnp.broadcast_to(meta_tokens.astype(x.dtype)[None], (b, N_META, d))
    h_pad = jnp.concatenate([jnp.zeros((b, PAD, d), x.dtype), meta, x], axis=1)

    half = DH // 2
    pos = jnp.arange(t, dtype=F32) - PAD
    inv = ROPE_BASE ** (-jnp.arange(half, dtype=F32) / half)
    ang = pos[:, None] * inv[None, :]
    cosf = jnp.concatenate([jnp.cos(ang), jnp.cos(ang)], axis=1)
    sinf = jnp.concatenate([-jnp.sin(ang), jnp.sin(ang)], axis=1)

    assert mix_norm_g.shape[0] == 1, "single-layer block only"
    rq, rk, rv, rgs, mq, mk, mv, mos, gcol, grow = _in_proj(
        h_pad, mix_norm_g[0], w_in[0], b_gates[0], conv_w[0], conv_b[0], cosf, sinf)
    lg = jax.nn.log_sigmoid(ret_decay_logit[0].astype(F32))
    ret = _retention(lg, rq, rk, rv)
    hm = _mlstm(mq, mk, mv, gcol, grow)
    h1, u2, aff = _out_proj(h_pad, ret, hm, rgs, mos, ret_gn_g[0], mlstm_gn_g[0], w_out[0],
                            ffn_norm_g[0], w_router[0])
    pm, dv, cnt, btot = _select(aff, cap)
    src, dst, gate = _compact(btot, pm, dv, aff, cap, c_pad)
    nseg = d // LANES
    tiled = lambda a: a.reshape(-1, nseg, SUB, LANES)
    flat = lambda a: a.reshape(-1, LANES)
    xs = _row_gather(flat(u2), _segment_ids(src.reshape(-1), nseg))
    ys = _experts(gate, tiled(xs), w_gate[0], w_up[0], w_down[0])
    n_rows = N_EXPERTS * b * t + b * N_EXPERTS * (c_pad - cap)
    z = _row_scatter(flat(ys), _segment_ids(dst.reshape(-1), nseg), -(-n_rows // SUB) * SUB * nseg)
    out = _combine(_maxcnt(cnt), h1, tiled(z), cnt, final_norm_g)
    return out[:, CHUNK:]
```

```python
import functools

import jax
import jax.numpy as jnp
from jax import lax
from jax.experimental import pallas as pl
from jax.experimental.pallas import tpu as pltpu
from jax.experimental.pallas import tpu_sc as plsc

F32 = jnp.float32
BF16 = jnp.bfloat16

LANES = 128
CHUNK = 128
N_META = 16
PAD = CHUNK - N_META
HEADS = 4
DH = 128
N_EXPERTS = 16
CONV_K = 5
HALO = 8
SUB = 8
SC_WINDOW = 128
EPS = 1e-6
NEG = -1e30
ROPE_BASE = 10000.0
VMEM_LIMIT = 56 * 1024 * 1024


def _cparams(sem, vmem=VMEM_LIMIT, **kw):
    return pltpu.CompilerParams(dimension_semantics=sem, vmem_limit_bytes=vmem, **kw)


def _dot(a, b):
    return jnp.dot(a, b, preferred_element_type=F32)


def _dot_nt(a, b):
    return lax.dot_general(a, b, (((1,), (1,)), ((), ())), preferred_element_type=F32)


def _dot_tn(a, b):
    return lax.dot_general(a, b, (((0,), (0,)), ((), ())), preferred_element_type=F32)


def _split3(x):
    hi = x.astype(BF16)
    r = x - hi.astype(F32)
    mid = r.astype(BF16)
    lo = (r - mid.astype(F32)).astype(BF16)
    return hi, mid, lo


def _dot3_left(m_bf, x):
    hi, mid, lo = _split3(x)
    return _dot(m_bf, hi) + _dot(m_bf, mid) + _dot(m_bf, lo)


def _dot3_right(x, m_bf):
    hi, mid, lo = _split3(x)
    return _dot(hi, m_bf) + _dot(mid, m_bf) + _dot(lo, m_bf)


def _rms(x, g):
    return x * lax.rsqrt(jnp.mean(x * x, axis=-1, keepdims=True) + EPS) * g


def _log_sigmoid(x):
    return jnp.minimum(x, 0.0) - jnp.log1p(jnp.exp(-jnp.abs(x)))


def _silu(x):
    return x * (1.0 / (1.0 + jnp.exp(-x)))


def _sigmoid(x):
    return 1.0 / (1.0 + jnp.exp(-x))


def _row_tile(t):
    return 640 if t % 640 == 0 else CHUNK


def _chunk_unroll(n_chunks):
    return 5 if n_chunks % 5 == 0 else 1


def _to_segments(dst_ref, x):
    r, d = x.shape
    for j in range(d // LANES):
        dst_ref[:, j] = x[:, j * LANES:(j + 1) * LANES].reshape(r // SUB, SUB, LANES)


def _from_segments(src_ref):
    g, nseg = src_ref.shape[0], src_ref.shape[1]
    return jnp.concatenate([src_ref[:, j].reshape(g * SUB, LANES) for j in range(nseg)], axis=1)


def _in_proj_kernel(h_ref, hp_ref, hn_ref, ng_ref, wr_ref, wm_ref, wg_ref, wgt_ref,
                    bg_ref, bgt_ref, cw_ref, cb_ref, cos_ref, sin_ref,
                    rq_ref, rk_ref, rv_ref, rg_ref, mq_ref, mk_ref, mv_ref, mo_ref,
                    kwf_ref, kwb_ref, gc_ref, gr_ref, ext_ref, *, tr, n_tiles):
    i = pl.program_id(1)
    ng = ng_ref[...]
    u = _rms(h_ref[0], ng).astype(BF16)
    row = lax.broadcasted_iota(jnp.int32, (tr, 1), 0) + i * tr
    real = row >= PAD

    cosf = cos_ref[...]
    sinf = sin_ref[...]
    w = HEADS * DH
    pq = _dot(u, wr_ref[:, 0:w])
    pk = _dot(u, wr_ref[:, w:2 * w])
    for hd in range(HEADS):
        sl = slice(hd * DH, (hd + 1) * DH)
        xq = pq[:, sl]
        xk = pk[:, sl]
        rq_ref[0, :, sl] = ((xq * cosf + pltpu.roll(xq, DH // 2, 1) * sinf) * (DH ** -0.5)).astype(BF16)
        rk_ref[0, :, sl] = (xk * cosf + pltpu.roll(xk, DH // 2, 1) * sinf).astype(BF16)
    rv_ref[0] = _dot(u, wr_ref[:, 2 * w:3 * w]).astype(BF16)
    rg_ref[0] = _silu(_dot(u, wr_ref[:, 3 * w:4 * w])).astype(BF16)

    up = _rms(hp_ref[0], ng)
    un = _rms(hn_ref[0], ng)
    up = jnp.where(i == 0, 0.0, up)
    un = jnp.where(i == n_tiles - 1, 0.0, un)
    uh = jnp.concatenate([up, un], axis=0).astype(BF16)
    halo = _dot(uh, wm_ref[:, 0:2 * w])
    ext_ref[0:HALO, :] = halo[0:HALO]
    ext_ref[HALO:HALO + tr, :] = _dot(u, wm_ref[:, 0:2 * w])
    ext_ref[HALO + tr:2 * HALO + tr, :] = halo[HALO:2 * HALO]
    conv = cb_ref[...] + jnp.zeros((tr, 2 * w), F32)
    for k in range(CONV_K):
        off = HALO + k - CONV_K // 2
        conv = conv + cw_ref[k:k + 1, :] * ext_ref[off:off + tr, :]
    qk = jnp.where(real, _silu(conv), 0.0)
    mq_ref[0] = (qk[:, 0:w] * (DH ** -0.5)).astype(BF16)
    mk_ref[0] = qk[:, w:2 * w].astype(BF16)
    mv_ref[0] = _dot(u, wm_ref[:, 2 * w:3 * w]).astype(BF16)
    mo_ref[0] = _sigmoid(_dot(u, wm_ref[:, 3 * w:4 * w])).astype(BF16)

    nd = 2 * HEADS
    lane = lax.broadcasted_iota(jnp.int32, (tr, LANES), 1)
    gts = _dot(u, wg_ref[...]) + bg_ref[...]
    gts = jnp.where(lane < nd, jnp.where(real, gts, NEG),
                    jnp.where(real & (lane < 2 * nd), _log_sigmoid(gts), 0.0))
    rown = lax.broadcasted_iota(jnp.int32, (2 * nd, tr), 0)
    coln = lax.broadcasted_iota(jnp.int32, (1, tr), 1) + i * tr
    realr = coln >= PAD
    gtr = _dot_nt(wgt_ref[...], u) + bgt_ref[...]
    gtr = jnp.where(rown < nd, jnp.where(realr, gtr, NEG),
                    jnp.where(realr, _log_sigmoid(gtr), 0.0))
    ci = lax.broadcasted_iota(jnp.int32, (CHUNK, CHUNK), 0)
    cj = lax.broadcasted_iota(jnp.int32, (CHUNK, CHUNK), 1)
    causal = ci >= cj
    anti = ci <= cj
    low = causal.astype(BF16)
    upp = anti.astype(BF16)
    lane_c = lax.broadcasted_iota(jnp.int32, (CHUNK, LANES), 1)
    lane_1 = lax.broadcasted_iota(jnp.int32, (1, LANES), 1)
    rown_c = lax.broadcasted_iota(jnp.int32, (2 * nd, CHUNK), 0)
    kconv = qk[:, w:2 * w]
    for c in range(tr // CHUNK):
        rs = slice(c * CHUNK, (c + 1) * CHUNK)
        g = gts[rs]
        lf = jnp.where(lane_c >= nd, g, 0.0)
        pre = _dot3_left(low, lf)
        suf = _dot3_left(upp, lf)
        bc = pltpu.roll(jnp.where(lane_c < nd + HEADS, pre, suf), LANES - nd, 1)
        blast = jnp.where(lane_1 < HEADS, bc[CHUNK - 1:CHUNK], bc[0:1])
        log_u = blast - bc + g
        a = jnp.max(log_u, axis=0, keepdims=True)
        wu = jnp.exp(log_u - a)
        gl = gtr[:, rs]
        lfr = jnp.where(rown_c >= nd, gl, 0.0)
        prer = _dot3_right(lfr, upp)
        sufr = _dot3_right(lfr, low)
        rr = gl[0:nd] - jnp.concatenate([prer[nd:nd + HEADS], sufr[nd + HEADS:2 * nd]], axis=0)
        for hd in range(HEADS):
            hs = slice(hd * DH, (hd + 1) * DH)
            cols = []
            for dr, (msk, kw_ref) in enumerate(((causal, kwf_ref), (anti, kwb_ref))):
                i8 = dr * HEADS + hd
                rmax = jnp.max(jnp.where(msk, rr[i8:i8 + 1, :], NEG), axis=1, keepdims=True)
                cols += [rmax, bc[:, i8:i8 + 1]]
                kw_ref[0, rs, hs] = (kconv[rs, hs] * wu[:, i8:i8 + 1]).astype(BF16)
            gc_ref[0, hd, rs, :] = jnp.where(
                lane_c == 0, cols[0], jnp.where(lane_c == 1, cols[1], jnp.where(lane_c == 2, cols[2], cols[3])))
            scal = [jnp.broadcast_to(x[:, j:j + 1], (1, CHUNK))
                    for j in (hd, HEADS + hd) for x in (a, blast)]
            gr_ref[0, hd, :, rs] = jnp.concatenate(
                [rr[hd:hd + 1], rr[HEADS + hd:HEADS + hd + 1]] + scal + [jnp.zeros((2, CHUNK), F32)], axis=0)


def _in_proj(h_pad, norm_g, w_in, b_gates, conv_w, conv_b, cosf, sinf):
    b, t, d = h_pad.shape
    tr = _row_tile(t)
    n_tiles = t // tr
    w = HEADS * DH
    wr = w_in[:, 0:4 * w].astype(BF16)
    wm = w_in[:, 4 * w:8 * w].astype(BF16)
    wg = jnp.pad(w_in[:, 8 * w:], ((0, 0), (0, LANES - 4 * HEADS))).astype(BF16)
    wgt = w_in[:, 8 * w:].T.astype(BF16)
    bg = jnp.pad(b_gates, (0, LANES - 4 * HEADS)).reshape(1, LANES)
    bgt = b_gates.reshape(4 * HEADS, 1)
    hb = tr // HALO
    last_h = t // HALO - 1
    seq = lambda bi, i: (bi, i, 0)
    const2 = lambda bi, i: (0, 0)
    out_bf = jax.ShapeDtypeStruct((b, t, w), BF16)
    kern = functools.partial(_in_proj_kernel, tr=tr, n_tiles=n_tiles)
    return pl.pallas_call(
        kern,
        grid=(b, n_tiles),
        in_specs=[
            pl.BlockSpec((1, tr, d), seq),
            pl.BlockSpec((1, HALO, d), lambda bi, i: (bi, jnp.maximum(i * hb - 1, 0), 0)),
            pl.BlockSpec((1, HALO, d), lambda bi, i: (bi, jnp.minimum((i + 1) * hb, last_h), 0)),
            pl.BlockSpec((1, d), const2),
            pl.BlockSpec((d, 4 * w), const2),
            pl.BlockSpec((d, 4 * w), const2),
            pl.BlockSpec((d, LANES), const2),
            pl.BlockSpec((4 * HEADS, d), const2),
            pl.BlockSpec((1, LANES), const2),
            pl.BlockSpec((4 * HEADS, 1), const2),
            pl.BlockSpec((CONV_K, 2 * w), const2),
            pl.BlockSpec((1, 2 * w), const2),
            pl.BlockSpec((tr, DH), lambda bi, i: (i, 0)),
            pl.BlockSpec((tr, DH), lambda bi, i: (i, 0)),
        ],
        out_specs=[pl.BlockSpec((1, tr, w), seq)] * 10 + [
            pl.BlockSpec((1, HEADS, tr, LANES), lambda bi, i: (bi, 0, i, 0)),
            pl.BlockSpec((1, HEADS, 8, tr), lambda bi, i: (bi, 0, 0, i)),
        ],
        out_shape=[out_bf] * 10 + [
            jax.ShapeDtypeStruct((b, HEADS, t, LANES), F32),
            jax.ShapeDtypeStruct((b, HEADS, 8, t), F32),
        ],
        scratch_shapes=[pltpu.VMEM((tr + 2 * HALO, 2 * w), F32)],
        compiler_params=_cparams(("parallel", "arbitrary")),
        name="in_proj",
    )(h_pad, h_pad, h_pad, norm_g.reshape(1, d), wr, wm, wg, wgt, bg, bgt,
      conv_w, conv_b.reshape(1, 2 * w), cosf, sinf)


def _retention_kernel(lg_ref, q_ref, k_ref, v_ref, o_ref, *, n_chunks):
    hd = pl.program_id(1)
    lgf = lg_ref[0, hd]
    lgb = lg_ref[1, hd]
    li = lax.broadcasted_iota(jnp.int32, (CHUNK, CHUNK), 0)
    mi = lax.broadcasted_iota(jnp.int32, (CHUNK, CHUNK), 1)
    diff = (li - mi).astype(F32)
    decay = (jnp.where(diff >= 0, jnp.exp(lgf * jnp.maximum(diff, 0.0)), 0.0)
             + jnp.where(diff <= 0, jnp.exp(lgb * jnp.maximum(-diff, 0.0)), 0.0))
    lcol = lax.broadcasted_iota(jnp.int32, (CHUNK, 1), 0).astype(F32)
    one = jnp.ones((1, 1), F32)
    zeta_f = jnp.exp(lgf * (CHUNK - 1.0 - lcol))
    xi_f = jnp.exp(lgf * (lcol + 1.0))
    g_f = jnp.exp(lgf * CHUNK * one)
    zeta_b = jnp.exp(lgb * lcol)
    xi_b = jnp.exp(lgb * (CHUNK - lcol))
    g_b = jnp.exp(lgb * CHUNK * one)

    def fwd(c, state):
        rows = pl.ds(pl.multiple_of(c * CHUNK, CHUNK), CHUNK)
        q = q_ref[0, rows, :]
        k = k_ref[0, rows, :]
        v = v_ref[0, rows, :]
        s = _dot_nt(q, k) * decay
        intra = _dot(s.astype(BF16), v)
        cross = _dot(q, state.astype(BF16)) * xi_f
        o_ref[0, rows, :] = intra + cross
        kz = (k.astype(F32) * zeta_f).astype(BF16)
        return g_f * state + _dot_tn(kz, v)

    unroll = _chunk_unroll(n_chunks)
    lax.fori_loop(0, n_chunks, fwd, jnp.zeros((DH, DH), F32), unroll=unroll)

    def bwd(j, state):
        c = n_chunks - 1 - j
        rows = pl.ds(pl.multiple_of(c * CHUNK, CHUNK), CHUNK)
        q = q_ref[0, rows, :]
        k = k_ref[0, rows, :]
        v = v_ref[0, rows, :]
        cross = _dot(q, state.astype(BF16)) * xi_b
        o_ref[0, rows, :] = o_ref[0, rows, :] + cross
        kz = (k.astype(F32) * zeta_b).astype(BF16)
        return g_b * state + _dot_tn(kz, v)

    lax.fori_loop(0, n_chunks, bwd, jnp.zeros((DH, DH), F32), unroll=unroll)


def _retention(lg, rq, rk, rv):
    b, t, _ = rq.shape
    spec = pl.BlockSpec((1, t, DH), lambda bi, hi, lg_ref: (bi, 0, hi))
    return pl.pallas_call(
        functools.partial(_retention_kernel, n_chunks=t // CHUNK),
        grid_spec=pltpu.PrefetchScalarGridSpec(
            num_scalar_prefetch=1,
            grid=(b, HEADS),
            in_specs=[spec, spec, spec],
            out_specs=spec,
        ),
        out_shape=jax.ShapeDtypeStruct((b, t, HEADS * DH), F32),
        compiler_params=_cparams(("parallel", "arbitrary")),
        name="retention",
    )(lg, rq, rk, rv)


def _mlstm_kernel(q_ref, k_ref, v_ref, kwf_ref, kwb_ref, gc_ref, gr_ref, o_ref, *, n_chunks):
    li = lax.broadcasted_iota(jnp.int32, (CHUNK, CHUNK), 0)
    mi = lax.broadcasted_iota(jnp.int32, (CHUNK, CHUNK), 1)
    causal = li >= mi
    anti = li <= mi

    ones = jnp.ones((CHUNK, DH), BF16)

    def step(c, carry, backward):
        st, m_st = carry
        rows = pl.ds(pl.multiple_of(c * CHUNK, CHUNK), CHUNK)
        q = q_ref[0, rows, :]
        k = k_ref[0, rows, :]
        v_aug = jnp.concatenate([v_ref[0, rows, :], ones], axis=1)
        gcol = gc_ref[0, 0, rows, :]
        grow = gr_ref[0, 0, :, rows]
        if backward:
            kw = kwb_ref[0, rows, :]
            r_max, bc = gcol[:, 2:3], gcol[:, 3:4]
            r_row, a, b_last = grow[1:2, :], grow[4:5, 0:1], grow[5:6, 0:1]
            mask = anti
        else:
            kw = kwf_ref[0, rows, :]
            r_max, bc = gcol[:, 0:1], gcol[:, 1:2]
            r_row, a, b_last = grow[0:1, :], grow[2:3, 0:1], grow[3:4, 0:1]
            mask = causal
        mm = jnp.maximum(jnp.broadcast_to(r_max, (CHUNK, DH)), m_st)
        s = _dot_nt(q, k) * jnp.where(mask, jnp.exp(r_row - mm), 0.0)
        w_inter = jnp.exp(m_st - mm)
        w2 = jnp.concatenate([w_inter, w_inter], axis=1)
        tot = _dot(s.astype(BF16), v_aug) + _dot(q, st.astype(BF16)) * w2
        floor = jnp.exp(-(jnp.broadcast_to(bc, (CHUNK, DH)) + mm))
        hval = tot[:, 0:DH] / jnp.maximum(jnp.abs(tot[:, DH:2 * DH]), floor)
        if backward:
            o_ref[0, rows, :] = o_ref[0, rows, :] + hval
        else:
            o_ref[0, rows, :] = hval
        upd = _dot_tn(kw, v_aug)
        m_new = jnp.maximum(b_last + m_st, a)
        f = jnp.exp(b_last + m_st - m_new)
        g = jnp.exp(a - m_new)
        return f * st + g * upd, m_new

    init = (jnp.zeros((DH, 2 * DH), F32), jnp.zeros((1, 1), F32))
    unroll = _chunk_unroll(n_chunks)
    lax.fori_loop(0, n_chunks, lambda c, s: step(c, s, False), init, unroll=unroll)
    lax.fori_loop(0, n_chunks, lambda j, s: step(n_chunks - 1 - j, s, True), init, unroll=unroll)


def _mlstm(mq, mk, mv, kwf, kwb, gcol, grow):
    b, t, _ = mq.shape
    spec = pl.BlockSpec((1, t, DH), lambda bi, hi: (bi, 0, hi))
    return pl.pallas_call(
        functools.partial(_mlstm_kernel, n_chunks=t // CHUNK),
        grid=(b, HEADS),
        in_specs=[spec, spec, spec, spec, spec,
                  pl.BlockSpec((1, 1, t, LANES), lambda bi, hi: (bi, hi, 0, 0)),
                  pl.BlockSpec((1, 1, 8, t), lambda bi, hi: (bi, hi, 0, 0))],
        out_specs=spec,
        out_shape=jax.ShapeDtypeStruct((b, t, HEADS * DH), F32),
        compiler_params=_cparams(("parallel", "arbitrary")),
        name="mlstm",
    )(mq, mk, mv, kwf, kwb, gcol, grow)


def _head_norm(y, g):
    outs = []
    for hd in range(HEADS):
        yh = y[:, hd * DH:(hd + 1) * DH]
        mu = jnp.mean(yh, axis=1, keepdims=True)
        yc = yh - mu
        var = jnp.mean(yc * yc, axis=1, keepdims=True)
        outs.append(yc * lax.rsqrt(var + EPS))
    return jnp.concatenate(outs, axis=1) * g


def _out_proj_kernel(h_ref, ret_ref, hm_ref, rg_ref, mo_ref, rgn_ref, mgn_ref, wo_ref,
                     fg_ref, wrt_ref, h1_ref, u2_ref, aff_ref, *, tr):
    i = pl.program_id(1)
    row = lax.broadcasted_iota(jnp.int32, (tr, 1), 0) + i * tr
    real = row >= PAD
    w = HEADS * DH
    y_ret = _head_norm(ret_ref[0], rgn_ref[...]) * rg_ref[0].astype(F32)
    y_m = _head_norm(mo_ref[0].astype(F32) * hm_ref[0], mgn_ref[...])
    y_ret = jnp.where(real, y_ret, 0.0).astype(BF16)
    y_m = jnp.where(real, y_m, 0.0).astype(BF16)
    h1 = h_ref[0] + _dot(y_ret, wo_ref[0:w, :]) + _dot(y_m, wo_ref[w:2 * w, :])
    h1_ref[0] = h1
    u2 = _rms(h1, fg_ref[...])
    _to_segments(u2_ref, u2)
    logits = _dot_nt(wrt_ref[...], u2.astype(BF16))
    mx = jnp.max(logits, axis=0, keepdims=True)
    ex = jnp.exp(logits - mx)
    aff = ex / jnp.sum(ex, axis=0, keepdims=True)
    coln = lax.broadcasted_iota(jnp.int32, (1, tr), 1) + i * tr
    aff_ref[0] = jnp.where(coln >= PAD, aff, -1.0)


def _out_proj(h_pad, ret, hm, rgs, mos, ret_gn_g, mlstm_gn_g, w_out, ffn_norm_g, w_router):
    b, t, d = h_pad.shape
    tr = _row_tile(t)
    n_tiles = t // tr
    nseg = d // LANES
    w = HEADS * DH
    seq = lambda bi, i: (bi, i, 0)
    const2 = lambda bi, i: (0, 0)
    return pl.pallas_call(
        functools.partial(_out_proj_kernel, tr=tr),
        grid=(b, n_tiles),
        in_specs=[
            pl.BlockSpec((1, tr, d), seq),
            pl.BlockSpec((1, tr, w), seq),
            pl.BlockSpec((1, tr, w), seq),
            pl.BlockSpec((1, tr, w), seq),
            pl.BlockSpec((1, tr, w), seq),
            pl.BlockSpec((1, w), const2),
            pl.BlockSpec((1, w), const2),
            pl.BlockSpec((2 * w, d), const2),
            pl.BlockSpec((1, d), const2),
            pl.BlockSpec((N_EXPERTS, d), const2),
        ],
        out_specs=[
            pl.BlockSpec((1, tr, d), seq),
            pl.BlockSpec((tr // SUB, nseg, SUB, LANES), lambda bi, i: (bi * n_tiles + i, 0, 0, 0)),
            pl.BlockSpec((1, N_EXPERTS, tr), lambda bi, i: (bi, 0, i)),
        ],
        out_shape=[
            jax.ShapeDtypeStruct((b, t, d), F32),
            jax.ShapeDtypeStruct((b * t // SUB, nseg, SUB, LANES), F32),
            jax.ShapeDtypeStruct((b, N_EXPERTS, t), F32),
        ],
        compiler_params=_cparams(("parallel", "arbitrary")),
        name="out_proj",
    )(h_pad, ret, hm, rgs, mos, ret_gn_g.reshape(1, w), mlstm_gn_g.reshape(1, w),
      w_out.astype(BF16), ffn_norm_g.reshape(1, d), w_router.T.astype(BF16))


def _select_kernel(aff_ref, pm_ref, dv_ref, cnt_ref, bt_ref, mc_ref, sel_ref, *, t, cap, tr):
    e_n = N_EXPERTS
    nblk = t // LANES
    bits = pltpu.bitcast(aff_ref[0], jnp.int32)
    capf = float(cap)

    def search(_, lohi):
        lo, hi = lohi
        mid = lo + ((hi - lo + 1) >> 1)
        cnt = jnp.sum((bits >= mid).astype(F32), axis=1, keepdims=True)
        ok = cnt >= capf
        return jnp.where(ok, mid, lo), jnp.where(ok, hi, mid - 1)

    lo0 = jnp.zeros((e_n, 1), jnp.int32)
    hi0 = jnp.full((e_n, 1), 0x3F800000, jnp.int32)
    thr, _ = lax.fori_loop(0, 31, search, (lo0, hi0))
    need = capf - jnp.sum((bits > thr).astype(F32), axis=1, keepdims=True)

    ci = lax.broadcasted_iota(jnp.int32, (LANES, LANES), 0)
    cj = lax.broadcasted_iota(jnp.int32, (LANES, LANES), 1)
    upp = (ci <= cj).astype(BF16)
    ei = lax.broadcasted_iota(jnp.int32, (e_n, e_n), 0)
    ej = lax.broadcasted_iota(jnp.int32, (e_n, e_n), 1)
    strict = (ej < ei).astype(BF16)

    carry = jnp.zeros((e_n, 1), F32)
    for j in range(nblk):
        sl = slice(j * LANES, (j + 1) * LANES)
        bb = bits[:, sl]
        eqf = (bb == thr).astype(F32)
        ceq = _dot(eqf.astype(BF16), upp) + carry
        carry = ceq[:, LANES - 1:LANES]
        sel_ref[:, sl] = jnp.where(bb > thr, 1.0, jnp.where(ceq <= need, eqf, 0.0))

    carry = jnp.zeros((e_n, 1), F32)
    tok = lax.broadcasted_iota(jnp.int32, (1, LANES), 1).astype(F32)
    lane_b = lax.broadcasted_iota(jnp.int32, (e_n, LANES), 1)
    btot = jnp.zeros((e_n, LANES), F32)
    tile_max = jnp.zeros((SUB, LANES), F32)
    lane_t = lax.broadcasted_iota(jnp.int32, (SUB, LANES), 1)
    for j in range(nblk):
        sl = slice(j * LANES, (j + 1) * LANES)
        selb = sel_ref[:, sl]
        selbf = selb.astype(BF16)
        pin = _dot(selbf, upp) + carry
        carry = pin[:, LANES - 1:LANES]
        btot = jnp.where(lane_b == j, carry, btot)
        pm_ref[0, :, sl] = jnp.where(selb > 0, pin, 0.0)
        rank = _dot(strict, selbf)
        dv_ref[0, :, sl] = (tok + float(j * LANES)) * float(N_EXPERTS) + rank
        cnt_row = jnp.sum(selb, axis=0, keepdims=True)
        cnt_ref[0, sl, :] = jnp.broadcast_to(cnt_row, (LANES, LANES)).T
        tile_max = jnp.where(lane_t == (j * LANES) // tr,
                             jnp.maximum(tile_max, jnp.max(cnt_row, axis=1, keepdims=True)), tile_max)
    bt_ref[0] = jnp.where(lane_b < nblk, btot, carry).astype(jnp.int32)
    mc_ref[0] = tile_max.astype(jnp.int32)


def _select(aff, cap):
    b, e_n, t = aff.shape
    tr = _row_tile(t)
    assert t // tr <= LANES
    spec = pl.BlockSpec((1, e_n, t), lambda bi: (bi, 0, 0))
    return pl.pallas_call(
        functools.partial(_select_kernel, t=t, cap=cap, tr=tr),
        grid=(b,),
        in_specs=[spec],
        out_specs=[
            spec,
            spec,
            pl.BlockSpec((1, t, LANES), lambda bi: (bi, 0, 0)),
            pl.BlockSpec((1, e_n, LANES), lambda bi: (bi, 0, 0)),
            pl.BlockSpec((1, SUB, LANES), lambda bi: (bi, 0, 0)),
        ],
        out_shape=[
            jax.ShapeDtypeStruct((b, e_n, t), F32),
            jax.ShapeDtypeStruct((b, e_n, t), F32),
            jax.ShapeDtypeStruct((b, t, LANES), F32),
            jax.ShapeDtypeStruct((b, e_n, LANES), jnp.int32),
            jax.ShapeDtypeStruct((b, SUB, LANES), jnp.int32),
        ],
        scratch_shapes=[pltpu.VMEM((e_n, t), F32)],
        compiler_params=_cparams(("parallel",)),
        name="select",
    )(aff)


def _compact_kernel(bt_ref, pm_ref, dv_ref, aff_ref, src_ref, dst_ref, gate_ref,
                    *, t, cap, c_pad, nseg):
    bi = pl.program_id(0)
    e_n = N_EXPERTS
    nblk = t // LANES
    nct = c_pad // LANES
    gate_ref[...] = jnp.zeros(gate_ref.shape, F32)
    lane_e = lax.broadcasted_iota(jnp.int32, (LANES, LANES), 1)
    slot_l = lax.broadcasted_iota(jnp.int32, (1, LANES), 1)

    sub_i = lax.broadcasted_iota(jnp.int32, (SUB, LANES), 0)
    lane_i = lax.broadcasted_iota(jnp.int32, (SUB, LANES), 1)
    pick = (lane_i >> 4) == sub_i
    pl_i = lax.broadcasted_iota(jnp.int32, (LANES, LANES), 0)
    pl_j = lax.broadcasted_iota(jnp.int32, (LANES, LANES), 1)
    perm = ((pl_i & 15) == ((pl_j >> 6) << 3) + (pl_j & 7)).astype(BF16)
    seg_j = (lane_i & 63) >> 3

    def segment_tile(rows):
        spread = _dot3_right(jnp.where(pick, rows.astype(F32), 0.0), perm).astype(jnp.int32)
        return (((spread >> 3) * nseg + seg_j) << 3) + (spread & (SUB - 1))

    def expert(e):
        base = (bi * e_n + e) * LANES

        def ctile(ct, j0):
            c0 = pl.multiple_of(ct * LANES, LANES)
            j_lo = lax.while_loop(
                lambda j: (j < nblk) & (bt_ref[base + jnp.minimum(j, nblk - 1)] <= c0),
                lambda j: j + 1, j0)
            j_hi = lax.while_loop(
                lambda j: (j < nblk) & (bt_ref[base + jnp.minimum(j, nblk - 1)] < c0 + LANES),
                lambda j: j + 1, j_lo)
            j_end = jnp.minimum(j_hi + 1, nblk)
            want = (lax.broadcasted_iota(jnp.int32, (LANES, 1), 0) + (c0 + 1)).astype(F32)

            def blk(j, acc):
                acc_d, acc_g = acc
                cols = pl.ds(pl.multiple_of(j * LANES, LANES), LANES)
                hit = pm_ref[0, e:e + 1, cols] == want
                acc_d = acc_d + jnp.where(hit, dv_ref[0, e:e + 1, cols], 0.0)
                acc_g = acc_g + jnp.where(hit, aff_ref[0, e:e + 1, cols], 0.0)
                return acc_d, acc_g

            zero = jnp.zeros((LANES, LANES), F32)
            acc_d, acc_g = lax.fori_loop(j_lo, j_end, blk, (zero, zero))
            v = jnp.sum(acc_d.T, axis=0, keepdims=True).astype(jnp.int32)
            tok = v >> 4
            rank = v & (N_EXPERTS - 1)
            p = slot_l + (c0 - cap)
            is_pad = p >= 0
            src = bi * t + jnp.where(is_pad, 0, tok)
            dst = jnp.where(is_pad, N_EXPERTS * t + e * (c_pad - cap) + p, rank * t + tok)
            out_rows = pl.ds(pl.multiple_of((e * nct + ct) * SUB, SUB), SUB)
            src_ref[0, out_rows, :] = segment_tile(src)
            dst_ref[0, out_rows, :] = segment_tile(dst)
            gcol = jnp.sum(acc_g, axis=1, keepdims=True)
            rows = pl.ds(c0, LANES)
            gate_ref[0, rows, :] = jnp.where(lane_e == e, gcol, gate_ref[0, rows, :])
            return j_lo

        lax.fori_loop(0, c_pad // LANES, ctile, 0)

    for e in range(e_n):
        expert(e)


def _compact(btot, pm, dv, aff, cap, c_pad, nseg):
    b, e_n, t = aff.shape
    assert nseg == SUB
    seg_rows = e_n * c_pad * nseg // LANES
    spec = pl.BlockSpec((1, e_n, t), lambda bi, bt: (bi, 0, 0))
    ospec = pl.BlockSpec((1, seg_rows, LANES), lambda bi, bt: (bi, 0, 0))
    return pl.pallas_call(
        functools.partial(_compact_kernel, t=t, cap=cap, c_pad=c_pad, nseg=nseg),
        grid_spec=pltpu.PrefetchScalarGridSpec(
            num_scalar_prefetch=1,
            grid=(b,),
            in_specs=[spec, spec, spec],
            out_specs=[ospec, ospec, pl.BlockSpec((1, c_pad, LANES), lambda bi, bt: (bi, 0, 0))],
        ),
        out_shape=[
            jax.ShapeDtypeStruct((b, seg_rows, LANES), jnp.int32),
            jax.ShapeDtypeStruct((b, seg_rows, LANES), jnp.int32),
            jax.ShapeDtypeStruct((b, c_pad, LANES), F32),
        ],
        compiler_params=_cparams(("parallel",)),
        name="compact",
    )(btot.reshape(b * e_n * LANES), pm, dv, aff)


def _sc_mesh():
    return plsc.VectorSubcoreMesh(core_axis_name="core", subcore_axis_name="subcore")


def _row_gather(table, idx):
    m = idx.shape[0]
    d = table.shape[1]

    @functools.partial(pl.kernel, out_type=jax.ShapeDtypeStruct((m, d), table.dtype),
                       mesh=_sc_mesh(), scratch_types=[])
    def gather_kernel(x_hbm, i_hbm, o_hbm):
        def body(i_vmem, o_vmem):
            pltpu.sync_copy(x_hbm.at[i_vmem.at[0]], o_vmem)

        pltpu.emit_pipeline(
            body,
            grid=(m // SC_WINDOW,),
            in_specs=[pl.BlockSpec((1, SC_WINDOW), lambda i: (0, i))],
            out_specs=[pl.BlockSpec((SC_WINDOW, d), lambda i: (i, 0))],
            core_axis_name=("core", "subcore"),
            dimension_semantics=(pltpu.PARALLEL,),
        )(i_hbm, o_hbm)

    return gather_kernel(table, idx.reshape(1, m))


def _row_scatter(rows, idx, n_out):
    m, d = rows.shape

    @functools.partial(pl.kernel, out_type=jax.ShapeDtypeStruct((n_out, d), rows.dtype),
                       mesh=_sc_mesh(), scratch_types=[])
    def scatter_kernel(x_hbm, i_hbm, o_hbm):
        def body(x_vmem, i_vmem):
            pltpu.sync_copy(x_vmem, o_hbm.at[i_vmem.at[0]])

        pltpu.emit_pipeline(
            body,
            grid=(m // SC_WINDOW,),
            in_specs=[pl.BlockSpec((SC_WINDOW, d), lambda i: (i, 0)),
                      pl.BlockSpec((1, SC_WINDOW), lambda i: (0, i))],
            out_specs=[],
            core_axis_name=("core", "subcore"),
            dimension_semantics=(pltpu.PARALLEL,),
        )(x_hbm, i_hbm)

    return scatter_kernel(rows, idx.reshape(1, m))


def _expert_kernel(gate_ref, x_ref, wg_ref, wu_ref, wd_ref, y_ref, xb_ref, acc_ref, *, c_pad, n_f):
    e = pl.program_id(0)
    f = pl.program_id(1)

    @pl.when(f == 0)
    def _():
        xb_ref[...] = _from_segments(x_ref).astype(BF16)

    x = xb_ref[...]
    hid = _silu(_dot(x, wg_ref[0].astype(BF16))) * _dot(x, wu_ref[0].astype(BF16))
    part = _dot(hid.astype(BF16), wd_ref[0].astype(BF16))

    @pl.when(f == 0)
    def _():
        acc_ref[...] = part

    @pl.when(f > 0)
    def _():
        acc_ref[...] = acc_ref[...] + part

    @pl.when(f == n_f - 1)
    def _():
        lane = lax.broadcasted_iota(jnp.int32, (c_pad, LANES), 1)
        gcol = jnp.sum(jnp.where(lane == e, gate_ref[0], 0.0), axis=1, keepdims=True)
        _to_segments(y_ref, acc_ref[...] * gcol)


def _experts(gate, xs, w_gate, w_up, w_down):
    _, c_pad, _ = gate.shape
    e_n, d, d_ff = w_gate.shape
    nseg = d // LANES
    fc = 256
    n_f = d_ff // fc
    blk = pl.BlockSpec((c_pad // SUB, nseg, SUB, LANES), lambda e, f: (e, 0, 0, 0))
    return pl.pallas_call(
        functools.partial(_expert_kernel, c_pad=c_pad, n_f=n_f),
        grid=(e_n, n_f),
        in_specs=[
            pl.BlockSpec((1, c_pad, LANES), lambda e, f: (0, 0, 0)),
            blk,
            pl.BlockSpec((1, d, fc), lambda e, f: (e, 0, f)),
            pl.BlockSpec((1, d, fc), lambda e, f: (e, 0, f)),
            pl.BlockSpec((1, fc, d), lambda e, f: (e, f, 0)),
        ],
        out_specs=blk,
        out_shape=jax.ShapeDtypeStruct(xs.shape, F32),
        scratch_shapes=[pltpu.VMEM((c_pad, d), BF16), pltpu.VMEM((c_pad, d), F32)],
        compiler_params=_cparams(("parallel", "arbitrary")),
        name="experts",
    )(gate, xs, w_gate, w_up, w_down)


def _combine_kernel(mc_ref, h1_ref, cnt_ref, fg_ref, *rest, n_layers, nb):
    z_refs, o_ref, acc_ref = rest[:nb], rest[nb], rest[nb + 1]
    bi = pl.program_id(0)
    i = pl.program_id(1)
    j = pl.program_id(2)

    @pl.when(j == 0)
    def _():
        _to_segments(acc_ref, h1_ref[0])

    for k in range(nb):
        @pl.when((bi == k) & (j < mc_ref[bi * LANES + i]))
        def _():
            g = acc_ref.shape[0]
            take = cnt_ref[0].reshape(g, 1, SUB, LANES) > j.astype(F32)
            acc_ref[...] = acc_ref[...] + jnp.where(take, z_refs[k][...], 0.0)

    @pl.when(j == n_layers - 1)
    def _():
        o_ref[0] = _rms(_from_segments(acc_ref), fg_ref[...])


def _combine(maxcnt, h1, zs, cnt, final_norm_g):
    b, t, d = h1.shape
    n_layers = N_EXPERTS
    tr = _row_tile(t)
    n_tiles = t // tr

    def z_map(k):
        def index(bi, i, j, mc):
            layer = jnp.maximum(jnp.minimum(j, mc[bi * LANES + i] - 1), 0)
            return (jnp.where(bi == k, layer * n_tiles + i, 0), 0, 0, 0)
        return index

    seg_block = (tr // SUB, d // LANES, SUB, LANES)
    return pl.pallas_call(
        functools.partial(_combine_kernel, n_layers=n_layers, nb=b),
        grid_spec=pltpu.PrefetchScalarGridSpec(
            num_scalar_prefetch=1,
            grid=(b, n_tiles, n_layers),
            in_specs=[
                pl.BlockSpec((1, tr, d), lambda bi, i, j, mc: (bi, i, 0)),
                pl.BlockSpec((1, tr, LANES), lambda bi, i, j, mc: (bi, i, 0)),
                pl.BlockSpec((1, d), lambda bi, i, j, mc: (0, 0)),
            ] + [pl.BlockSpec(seg_block, z_map(k)) for k in range(b)],
            out_specs=pl.BlockSpec((1, tr, d), lambda bi, i, j, mc: (bi, i, 0)),
            scratch_shapes=[pltpu.VMEM(seg_block, F32)],
        ),
        out_shape=jax.ShapeDtypeStruct((b, t, d), F32),
        compiler_params=_cparams(("parallel", "arbitrary", "arbitrary")),
        name="combine",
    )(maxcnt, h1, cnt, final_norm_g.reshape(1, d), *zs)


def kernel(x, meta_tokens, mix_norm_g, w_in, b_gates, conv_w, conv_b, ret_decay_logit, ret_gn_g,
           mlstm_gn_g, w_out, ffn_norm_g, w_router, w_gate, w_up, w_down, final_norm_g):
    b, s, d = x.shape
    t = CHUNK + s
    n = N_META + s
    cap = 2 * n // N_EXPERTS
    c_pad = -(-cap // LANES) * LANES

    meta = jnp.broadcast_to(meta_tokens.astype(x.dtype)[None], (b, N_META, d))
    h_pad = jnp.concatenate([jnp.zeros((b, PAD, d), x.dtype), meta, x], axis=1)

    half = DH // 2
    pos = jnp.arange(t, dtype=F32) - PAD
    inv = ROPE_BASE ** (-jnp.arange(half, dtype=F32) / half)
    ang = pos[:, None] * inv[None, :]
    cosf = jnp.concatenate([jnp.cos(ang), jnp.cos(ang)], axis=1)
    sinf = jnp.concatenate([-jnp.sin(ang), jnp.sin(ang)], axis=1)

    assert mix_norm_g.shape[0] == 1, "single-layer block only"
    rq, rk, rv, rgs, mq, mk, mv, mos, kwf, kwb, gcol, grow = _in_proj(
        h_pad, mix_norm_g[0], w_in[0], b_gates[0], conv_w[0], conv_b[0], cosf, sinf)
    lg = jax.nn.log_sigmoid(ret_decay_logit[0].astype(F32))
    ret = _retention(lg, rq, rk, rv)
    hm = _mlstm(mq, mk, mv, kwf, kwb, gcol, grow)
    h1, u2, aff = _out_proj(h_pad, ret, hm, rgs, mos, ret_gn_g[0], mlstm_gn_g[0], w_out[0],
                            ffn_norm_g[0], w_router[0])
    nseg = d // LANES
    pm, dv, cnt, btot, maxcnt = _select(aff, cap)
    src, dst, gate = _compact(btot, pm, dv, aff, cap, c_pad, nseg)
    tiled = lambda a: a.reshape(-1, nseg, SUB, LANES)
    flat = lambda a: a.reshape(-1, LANES)
    z_rows = -(-(N_EXPERTS * t + N_EXPERTS * (c_pad - cap)) // SUB) * SUB
    zs = []
    for bi in range(b):
        xs = _row_gather(flat(u2), src[bi].reshape(-1))
        ys = _experts(gate[bi:bi + 1], tiled(xs), w_gate[0], w_up[0], w_down[0])
        zs.append(tiled(_row_scatter(flat(ys), dst[bi].reshape(-1), z_rows * nseg)))
    out = _combine(maxcnt[:, 0, :].reshape(-1), h1, zs, cnt, final_norm_g)
    return out[:, CHUNK:]
```

```python
import functools

import jax
import jax.numpy as jnp
from jax import lax
from jax.experimental import pallas as pl
from jax.experimental.pallas import tpu as pltpu
from jax.experimental.pallas import tpu_sc as plsc

F32 = jnp.float32
BF16 = jnp.bfloat16

LANES = 128
CHUNK = 128
N_META = 16
PAD = CHUNK - N_META
HEADS = 4
DH = 128
N_EXPERTS = 16
CONV_K = 5
HALO = 8
SUB = 8
SC_WINDOW = 128
EPS = 1e-6
NEG = -1e30
ROPE_BASE = 10000.0
VMEM_LIMIT = 56 * 1024 * 1024


def _cparams(sem, vmem=VMEM_LIMIT, **kw):
    return pltpu.CompilerParams(dimension_semantics=sem, vmem_limit_bytes=vmem, **kw)


def _dot(a, b):
    return jnp.dot(a, b, preferred_element_type=F32)


def _dot_nt(a, b):
    return lax.dot_general(a, b, (((1,), (1,)), ((), ())), preferred_element_type=F32)


def _dot_tn(a, b):
    return lax.dot_general(a, b, (((0,), (0,)), ((), ())), preferred_element_type=F32)


def _split3(x):
    hi = x.astype(BF16)
    r = x - hi.astype(F32)
    mid = r.astype(BF16)
    lo = (r - mid.astype(F32)).astype(BF16)
    return hi, mid, lo


def _dot3_left(m_bf, x):
    hi, mid, lo = _split3(x)
    return _dot(m_bf, hi) + _dot(m_bf, mid) + _dot(m_bf, lo)


def _dot3_right(x, m_bf):
    hi, mid, lo = _split3(x)
    return _dot(hi, m_bf) + _dot(mid, m_bf) + _dot(lo, m_bf)


def _rms(x, g):
    return x * lax.rsqrt(jnp.mean(x * x, axis=-1, keepdims=True) + EPS) * g


def _log_sigmoid(x):
    return jnp.minimum(x, 0.0) - jnp.log1p(jnp.exp(-jnp.abs(x)))


def _silu(x):
    return x * (1.0 / (1.0 + jnp.exp(-x)))


def _sigmoid(x):
    return 1.0 / (1.0 + jnp.exp(-x))


def _row_tile(t):
    return 640 if t % 640 == 0 else CHUNK


def _chunk_unroll(n_chunks):
    return 13 if n_chunks % 13 == 0 else 1


def _to_segments(dst_ref, x):
    r, w = x.shape
    for j in range(w // LANES):
        dst_ref[:, j] = x[:, j * LANES:(j + 1) * LANES].reshape(r // SUB, SUB, LANES)


def _from_segments(src_ref):
    g, nseg = src_ref.shape[0], src_ref.shape[1]
    return jnp.concatenate([src_ref[:, j].reshape(g * SUB, LANES) for j in range(nseg)], axis=1)


def _bf16_bits(x):
    return pltpu.bitcast(x.astype(BF16).astype(F32), jnp.int32)


def _pack_pairs(x):
    half = x.shape[1] // 2
    return _bf16_bits(x[:, :half]) | lax.shift_right_logical(_bf16_bits(x[:, half:]), 16)


def _unpack_hi(w):
    return pltpu.bitcast(w & -65536, F32)


def _unpack_lo(w):
    return pltpu.bitcast(w << 16, F32)


def _in_proj_kernel(h_ref, hp_ref, hn_ref, ng_ref, wr_ref, wm_ref, wg_ref, wgt_ref,
                    bg_ref, bgt_ref, cw_ref, cb_ref, cos_ref, sin_ref,
                    rq_ref, rk_ref, rv_ref, rg_ref, mq_ref, mk_ref, mv_ref, mo_ref,
                    kwf_ref, kwb_ref, gc_ref, gr_ref, ext_ref, *, tr, n_tiles):
    i = pl.program_id(1)
    ng = ng_ref[...]
    u = _rms(h_ref[0], ng).astype(BF16)
    row = lax.broadcasted_iota(jnp.int32, (tr, 1), 0) + i * tr
    real = row >= PAD

    cosf = cos_ref[...]
    sinf = sin_ref[...]
    w = HEADS * DH
    pq = _dot(u, wr_ref[:, 0:w])
    pk = _dot(u, wr_ref[:, w:2 * w])
    for hd in range(HEADS):
        sl = slice(hd * DH, (hd + 1) * DH)
        xq = pq[:, sl]
        xk = pk[:, sl]
        rq_ref[0, :, sl] = ((xq * cosf + pltpu.roll(xq, DH // 2, 1) * sinf) * (DH ** -0.5)).astype(BF16)
        rk_ref[0, :, sl] = (xk * cosf + pltpu.roll(xk, DH // 2, 1) * sinf).astype(BF16)
    rv_ref[0] = _dot(u, wr_ref[:, 2 * w:3 * w]).astype(BF16)
    rg_ref[0] = _silu(_dot(u, wr_ref[:, 3 * w:4 * w])).astype(BF16)

    up = _rms(hp_ref[0], ng)
    un = _rms(hn_ref[0], ng)
    up = jnp.where(i == 0, 0.0, up)
    un = jnp.where(i == n_tiles - 1, 0.0, un)
    uh = jnp.concatenate([up, un], axis=0).astype(BF16)
    halo = _dot(uh, wm_ref[:, 0:2 * w])
    ext_ref[0:HALO, :] = halo[0:HALO]
    ext_ref[HALO:HALO + tr, :] = _dot(u, wm_ref[:, 0:2 * w])
    ext_ref[HALO + tr:2 * HALO + tr, :] = halo[HALO:2 * HALO]
    conv = cb_ref[...] + jnp.zeros((tr, 2 * w), F32)
    for k in range(CONV_K):
        off = HALO + k - CONV_K // 2
        conv = conv + cw_ref[k:k + 1, :] * ext_ref[off:off + tr, :]
    qk = jnp.where(real, _silu(conv), 0.0)
    mq_ref[0] = (qk[:, 0:w] * (DH ** -0.5)).astype(BF16)
    mk_ref[0] = qk[:, w:2 * w].astype(BF16)
    mv_ref[0] = _dot(u, wm_ref[:, 2 * w:3 * w]).astype(BF16)
    mo_ref[0] = _sigmoid(_dot(u, wm_ref[:, 3 * w:4 * w])).astype(BF16)

    nd = 2 * HEADS
    lane = lax.broadcasted_iota(jnp.int32, (tr, LANES), 1)
    gts = _dot(u, wg_ref[...]) + bg_ref[...]
    gts = jnp.where(lane < nd, jnp.where(real, gts, NEG),
                    jnp.where(real & (lane < 2 * nd), _log_sigmoid(gts), 0.0))
    rown = lax.broadcasted_iota(jnp.int32, (2 * nd, tr), 0)
    coln = lax.broadcasted_iota(jnp.int32, (1, tr), 1) + i * tr
    realr = coln >= PAD
    gtr = _dot_nt(wgt_ref[...], u) + bgt_ref[...]
    gtr = jnp.where(rown < nd, jnp.where(realr, gtr, NEG),
                    jnp.where(realr, _log_sigmoid(gtr), 0.0))
    ci = lax.broadcasted_iota(jnp.int32, (CHUNK, CHUNK), 0)
    cj = lax.broadcasted_iota(jnp.int32, (CHUNK, CHUNK), 1)
    causal = ci >= cj
    anti = ci <= cj
    low = causal.astype(BF16)
    upp = anti.astype(BF16)
    lane_c = lax.broadcasted_iota(jnp.int32, (CHUNK, LANES), 1)
    lane_1 = lax.broadcasted_iota(jnp.int32, (1, LANES), 1)
    rown_c = lax.broadcasted_iota(jnp.int32, (2 * nd, CHUNK), 0)
    kconv = qk[:, w:2 * w]
    for c in range(tr // CHUNK):
        rs = slice(c * CHUNK, (c + 1) * CHUNK)
        g = gts[rs]
        lf = jnp.where(lane_c >= nd, g, 0.0)
        pre = _dot3_left(low, lf)
        suf = _dot3_left(upp, lf)
        bc = pltpu.roll(jnp.where(lane_c < nd + HEADS, pre, suf), LANES - nd, 1)
        blast = jnp.where(lane_1 < HEADS, bc[CHUNK - 1:CHUNK], bc[0:1])
        log_u = blast - bc + g
        a = jnp.max(log_u, axis=0, keepdims=True)
        wu = jnp.exp(log_u - a)
        gl = gtr[:, rs]
        lfr = jnp.where(rown_c >= nd, gl, 0.0)
        prer = _dot3_right(lfr, upp)
        sufr = _dot3_right(lfr, low)
        rr = gl[0:nd] - jnp.concatenate([prer[nd:nd + HEADS], sufr[nd + HEADS:2 * nd]], axis=0)
        for hd in range(HEADS):
            hs = slice(hd * DH, (hd + 1) * DH)
            cols = []
            for dr, (msk, kw_ref) in enumerate(((causal, kwf_ref), (anti, kwb_ref))):
                i8 = dr * HEADS + hd
                rmax = jnp.max(jnp.where(msk, rr[i8:i8 + 1, :], NEG), axis=1, keepdims=True)
                cols += [rmax, bc[:, i8:i8 + 1]]
                kw_ref[0, rs, hs] = (kconv[rs, hs] * wu[:, i8:i8 + 1]).astype(BF16)
            gc_ref[0, hd, rs, :] = jnp.where(
                lane_c == 0, cols[0], jnp.where(lane_c == 1, cols[1], jnp.where(lane_c == 2, cols[2], cols[3])))
            scal = [jnp.broadcast_to(x[:, j:j + 1], (1, CHUNK))
                    for j in (hd, HEADS + hd) for x in (a, blast)]
            gr_ref[0, hd, :, rs] = jnp.concatenate(
                [rr[hd:hd + 1], rr[HEADS + hd:HEADS + hd + 1]] + scal + [jnp.zeros((2, CHUNK), F32)], axis=0)


def _in_proj(h_pad, norm_g, w_in, b_gates, conv_w, conv_b, cosf, sinf):
    b, t, d = h_pad.shape
    tr = _row_tile(t)
    n_tiles = t // tr
    w = HEADS * DH
    wr = w_in[:, 0:4 * w].astype(BF16)
    wm = w_in[:, 4 * w:8 * w].astype(BF16)
    wg = jnp.pad(w_in[:, 8 * w:], ((0, 0), (0, LANES - 4 * HEADS))).astype(BF16)
    wgt = w_in[:, 8 * w:].T.astype(BF16)
    bg = jnp.pad(b_gates, (0, LANES - 4 * HEADS)).reshape(1, LANES)
    bgt = b_gates.reshape(4 * HEADS, 1)
    hb = tr // HALO
    last_h = t // HALO - 1
    seq = lambda bi, i: (bi, i, 0)
    const2 = lambda bi, i: (0, 0)
    out_bf = jax.ShapeDtypeStruct((b, t, w), BF16)
    kern = functools.partial(_in_proj_kernel, tr=tr, n_tiles=n_tiles)
    return pl.pallas_call(
        kern,
        grid=(b, n_tiles),
        in_specs=[
            pl.BlockSpec((1, tr, d), seq),
            pl.BlockSpec((1, HALO, d), lambda bi, i: (bi, jnp.maximum(i * hb - 1, 0), 0)),
            pl.BlockSpec((1, HALO, d), lambda bi, i: (bi, jnp.minimum((i + 1) * hb, last_h), 0)),
            pl.BlockSpec((1, d), const2),
            pl.BlockSpec((d, 4 * w), const2),
            pl.BlockSpec((d, 4 * w), const2),
            pl.BlockSpec((d, LANES), const2),
            pl.BlockSpec((4 * HEADS, d), const2),
            pl.BlockSpec((1, LANES), const2),
            pl.BlockSpec((4 * HEADS, 1), const2),
            pl.BlockSpec((CONV_K, 2 * w), const2),
            pl.BlockSpec((1, 2 * w), const2),
            pl.BlockSpec((tr, DH), lambda bi, i: (i, 0)),
            pl.BlockSpec((tr, DH), lambda bi, i: (i, 0)),
        ],
        out_specs=[pl.BlockSpec((1, tr, w), seq)] * 10 + [
            pl.BlockSpec((1, HEADS, tr, LANES), lambda bi, i: (bi, 0, i, 0)),
            pl.BlockSpec((1, HEADS, 8, tr), lambda bi, i: (bi, 0, 0, i)),
        ],
        out_shape=[out_bf] * 10 + [
            jax.ShapeDtypeStruct((b, HEADS, t, LANES), F32),
            jax.ShapeDtypeStruct((b, HEADS, 8, t), F32),
        ],
        scratch_shapes=[pltpu.VMEM((tr + 2 * HALO, 2 * w), F32)],
        compiler_params=_cparams(("parallel", "arbitrary")),
        name="in_proj",
    )(h_pad, h_pad, h_pad, norm_g.reshape(1, d), wr, wm, wg, wgt, bg, bgt,
      conv_w, conv_b.reshape(1, 2 * w), cosf, sinf)


def _retention_kernel(lg_ref, q_ref, k_ref, v_ref, o_ref, *, n_chunks):
    hd = pl.program_id(1)
    lgf = lg_ref[0, hd]
    lgb = lg_ref[1, hd]
    li = lax.broadcasted_iota(jnp.int32, (CHUNK, CHUNK), 0)
    mi = lax.broadcasted_iota(jnp.int32, (CHUNK, CHUNK), 1)
    diff = (li - mi).astype(F32)
    decay = (jnp.where(diff >= 0, jnp.exp(lgf * jnp.maximum(diff, 0.0)), 0.0)
             + jnp.where(diff <= 0, jnp.exp(lgb * jnp.maximum(-diff, 0.0)), 0.0))
    lcol = lax.broadcasted_iota(jnp.int32, (CHUNK, 1), 0).astype(F32)
    one = jnp.ones((1, 1), F32)
    zeta_f = jnp.exp(lgf * (CHUNK - 1.0 - lcol))
    xi_f = jnp.exp(lgf * (lcol + 1.0))
    g_f = jnp.exp(lgf * CHUNK * one)
    zeta_b = jnp.exp(lgb * lcol)
    xi_b = jnp.exp(lgb * (CHUNK - lcol))
    g_b = jnp.exp(lgb * CHUNK * one)

    def fwd(c, state):
        rows = pl.ds(pl.multiple_of(c * CHUNK, CHUNK), CHUNK)
        q = q_ref[0, rows, :]
        k = k_ref[0, rows, :]
        v = v_ref[0, rows, :]
        s = _dot_nt(q, k) * decay
        intra = _dot(s.astype(BF16), v)
        cross = _dot(q, state.astype(BF16)) * xi_f
        o_ref[0, rows, :] = intra + cross
        kz = (k.astype(F32) * zeta_f).astype(BF16)
        return g_f * state + _dot_tn(kz, v)

    unroll = _chunk_unroll(n_chunks)
    lax.fori_loop(0, n_chunks, fwd, jnp.zeros((DH, DH), F32), unroll=unroll)

    def bwd(j, state):
        c = n_chunks - 1 - j
        rows = pl.ds(pl.multiple_of(c * CHUNK, CHUNK), CHUNK)
        q = q_ref[0, rows, :]
        k = k_ref[0, rows, :]
        v = v_ref[0, rows, :]
        cross = _dot(q, state.astype(BF16)) * xi_b
        o_ref[0, rows, :] = o_ref[0, rows, :] + cross
        kz = (k.astype(F32) * zeta_b).astype(BF16)
        return g_b * state + _dot_tn(kz, v)

    lax.fori_loop(0, n_chunks, bwd, jnp.zeros((DH, DH), F32), unroll=unroll)


def _retention(lg, rq, rk, rv):
    b, t, _ = rq.shape
    spec = pl.BlockSpec((1, t, DH), lambda bi, hi, lg_ref: (bi, 0, hi))
    return pl.pallas_call(
        functools.partial(_retention_kernel, n_chunks=t // CHUNK),
        grid_spec=pltpu.PrefetchScalarGridSpec(
            num_scalar_prefetch=1,
            grid=(b, HEADS),
            in_specs=[spec, spec, spec],
            out_specs=spec,
        ),
        out_shape=jax.ShapeDtypeStruct((b, t, HEADS * DH), F32),
        compiler_params=_cparams(("parallel", "arbitrary")),
        name="retention",
    )(lg, rq, rk, rv)


def _mlstm_kernel(q_ref, k_ref, v_ref, kwf_ref, kwb_ref, gc_ref, gr_ref, o_ref, *, n_chunks):
    li = lax.broadcasted_iota(jnp.int32, (CHUNK, CHUNK), 0)
    mi = lax.broadcasted_iota(jnp.int32, (CHUNK, CHUNK), 1)
    causal = li >= mi
    anti = li <= mi

    ones = jnp.ones((CHUNK, DH), BF16)

    def step(c, carry, backward):
        st, m_st = carry
        rows = pl.ds(pl.multiple_of(c * CHUNK, CHUNK), CHUNK)
        q = q_ref[0, rows, :]
        k = k_ref[0, rows, :]
        v_aug = jnp.concatenate([v_ref[0, rows, :], ones], axis=1)
        gcol = gc_ref[0, 0, rows, :]
        grow = gr_ref[0, 0, :, rows]
        if backward:
            kw = kwb_ref[0, rows, :]
            r_max, bc = gcol[:, 2:3], gcol[:, 3:4]
            r_row, a, b_last = grow[1:2, :], grow[4:5, 0:1], grow[5:6, 0:1]
            mask = anti
        else:
            kw = kwf_ref[0, rows, :]
            r_max, bc = gcol[:, 0:1], gcol[:, 1:2]
            r_row, a, b_last = grow[0:1, :], grow[2:3, 0:1], grow[3:4, 0:1]
            mask = causal
        mm = jnp.maximum(jnp.broadcast_to(r_max, (CHUNK, DH)), m_st)
        s = _dot_nt(q, k) * jnp.where(mask, jnp.exp(r_row - mm), 0.0)
        w_inter = jnp.exp(m_st - mm)
        w2 = jnp.concatenate([w_inter, w_inter], axis=1)
        tot = _dot(s.astype(BF16), v_aug) + _dot(q, st.astype(BF16)) * w2
        floor = jnp.exp(-(jnp.broadcast_to(bc, (CHUNK, DH)) + mm))
        hval = tot[:, 0:DH] / jnp.maximum(jnp.abs(tot[:, DH:2 * DH]), floor)
        if backward:
            o_ref[0, rows, :] = o_ref[0, rows, :] + hval
        else:
            o_ref[0, rows, :] = hval
        upd = _dot_tn(kw, v_aug)
        m_new = jnp.maximum(b_last + m_st, a)
        f = jnp.exp(b_last + m_st - m_new)
        g = jnp.exp(a - m_new)
        return f * st + g * upd, m_new

    init = (jnp.zeros((DH, 2 * DH), F32), jnp.zeros((1, 1), F32))
    unroll = _chunk_unroll(n_chunks)
    lax.fori_loop(0, n_chunks, lambda c, s: step(c, s, False), init, unroll=unroll)
    lax.fori_loop(0, n_chunks, lambda j, s: step(n_chunks - 1 - j, s, True), init, unroll=unroll)


def _mlstm(mq, mk, mv, kwf, kwb, gcol, grow):
    b, t, _ = mq.shape
    spec = pl.BlockSpec((1, t, DH), lambda bi, hi: (bi, 0, hi))
    return pl.pallas_call(
        functools.partial(_mlstm_kernel, n_chunks=t // CHUNK),
        grid=(b, HEADS),
        in_specs=[spec, spec, spec, spec, spec,
                  pl.BlockSpec((1, 1, t, LANES), lambda bi, hi: (bi, hi, 0, 0)),
                  pl.BlockSpec((1, 1, 8, t), lambda bi, hi: (bi, hi, 0, 0))],
        out_specs=spec,
        out_shape=jax.ShapeDtypeStruct((b, t, HEADS * DH), F32),
        compiler_params=_cparams(("parallel", "arbitrary")),
        name="mlstm",
    )(mq, mk, mv, kwf, kwb, gcol, grow)


def _head_norm(y, g):
    outs = []
    for hd in range(HEADS):
        yh = y[:, hd * DH:(hd + 1) * DH]
        mu = jnp.mean(yh, axis=1, keepdims=True)
        yc = yh - mu
        var = jnp.mean(yc * yc, axis=1, keepdims=True)
        outs.append(yc * lax.rsqrt(var + EPS))
    return jnp.concatenate(outs, axis=1) * g


def _out_proj_kernel(h_ref, ret_ref, hm_ref, rg_ref, mo_ref, rgn_ref, mgn_ref, wo_ref,
                     fg_ref, wrt_ref, h1_ref, u2_ref, aff_ref, *, tr):
    i = pl.program_id(1)
    row = lax.broadcasted_iota(jnp.int32, (tr, 1), 0) + i * tr
    real = row >= PAD
    w = HEADS * DH
    y_ret = _head_norm(ret_ref[0], rgn_ref[...]) * rg_ref[0].astype(F32)
    y_m = _head_norm(mo_ref[0].astype(F32) * hm_ref[0], mgn_ref[...])
    y_ret = jnp.where(real, y_ret, 0.0).astype(BF16)
    y_m = jnp.where(real, y_m, 0.0).astype(BF16)
    h1 = h_ref[0] + _dot(y_ret, wo_ref[0:w, :]) + _dot(y_m, wo_ref[w:2 * w, :])
    h1_ref[0] = h1
    u2 = _rms(h1, fg_ref[...])
    _to_segments(u2_ref, _pack_pairs(u2))
    logits = _dot_nt(wrt_ref[...], u2.astype(BF16))
    mx = jnp.max(logits, axis=0, keepdims=True)
    ex = jnp.exp(logits - mx)
    aff = ex / jnp.sum(ex, axis=0, keepdims=True)
    coln = lax.broadcasted_iota(jnp.int32, (1, tr), 1) + i * tr
    aff_ref[0] = jnp.where(coln >= PAD, aff, -1.0)


def _out_proj(h_pad, ret, hm, rgs, mos, ret_gn_g, mlstm_gn_g, w_out, ffn_norm_g, w_router):
    b, t, d = h_pad.shape
    tr = _row_tile(t)
    n_tiles = t // tr
    nseg = d // 2 // LANES
    w = HEADS * DH
    seq = lambda bi, i: (bi, i, 0)
    const2 = lambda bi, i: (0, 0)
    return pl.pallas_call(
        functools.partial(_out_proj_kernel, tr=tr),
        grid=(b, n_tiles),
        in_specs=[
            pl.BlockSpec((1, tr, d), seq),
            pl.BlockSpec((1, tr, w), seq),
            pl.BlockSpec((1, tr, w), seq),
            pl.BlockSpec((1, tr, w), seq),
            pl.BlockSpec((1, tr, w), seq),
            pl.BlockSpec((1, w), const2),
            pl.BlockSpec((1, w), const2),
            pl.BlockSpec((2 * w, d), const2),
            pl.BlockSpec((1, d), const2),
            pl.BlockSpec((N_EXPERTS, d), const2),
        ],
        out_specs=[
            pl.BlockSpec((1, tr, d), seq),
            pl.BlockSpec((tr // SUB, nseg, SUB, LANES), lambda bi, i: (bi * n_tiles + i, 0, 0, 0)),
            pl.BlockSpec((1, N_EXPERTS, tr), lambda bi, i: (bi, 0, i)),
        ],
        out_shape=[
            jax.ShapeDtypeStruct((b, t, d), F32),
            jax.ShapeDtypeStruct((b * t // SUB, nseg, SUB, LANES), jnp.int32),
            jax.ShapeDtypeStruct((b, N_EXPERTS, t), F32),
        ],
        compiler_params=_cparams(("parallel", "arbitrary")),
        name="out_proj",
    )(h_pad, ret, hm, rgs, mos, ret_gn_g.reshape(1, w), mlstm_gn_g.reshape(1, w),
      w_out.astype(BF16), ffn_norm_g.reshape(1, d), w_router.T.astype(BF16))


def _select_kernel(aff_ref, pm_ref, dv_ref, cnt_ref, bt_ref, mc_ref, sel_ref, *, t, cap, tr):
    e_n = N_EXPERTS
    nblk = t // LANES
    bits = pltpu.bitcast(aff_ref[0], jnp.int32)
    capf = float(cap)

    def search(_, lohi):
        lo, hi = lohi
        mid = lo + ((hi - lo + 1) >> 1)
        cnt = jnp.sum((bits >= mid).astype(F32), axis=1, keepdims=True)
        ok = cnt >= capf
        return jnp.where(ok, mid, lo), jnp.where(ok, hi, mid - 1)

    lo0 = jnp.zeros((e_n, 1), jnp.int32)
    hi0 = jnp.full((e_n, 1), 0x3F800000, jnp.int32)
    thr, _ = lax.fori_loop(0, 31, search, (lo0, hi0))
    need = capf - jnp.sum((bits > thr).astype(F32), axis=1, keepdims=True)

    ci = lax.broadcasted_iota(jnp.int32, (LANES, LANES), 0)
    cj = lax.broadcasted_iota(jnp.int32, (LANES, LANES), 1)
    upp = (ci <= cj).astype(BF16)
    ei = lax.broadcasted_iota(jnp.int32, (e_n, e_n), 0)
    ej = lax.broadcasted_iota(jnp.int32, (e_n, e_n), 1)
    strict = (ej < ei).astype(BF16)

    carry = jnp.zeros((e_n, 1), F32)
    for j in range(nblk):
        sl = slice(j * LANES, (j + 1) * LANES)
        bb = bits[:, sl]
        eqf = (bb == thr).astype(F32)
        ceq = _dot(eqf.astype(BF16), upp) + carry
        carry = ceq[:, LANES - 1:LANES]
        sel_ref[:, sl] = jnp.where(bb > thr, 1.0, jnp.where(ceq <= need, eqf, 0.0))

    carry = jnp.zeros((e_n, 1), F32)
    tok = lax.broadcasted_iota(jnp.int32, (1, LANES), 1).astype(F32)
    lane_b = lax.broadcasted_iota(jnp.int32, (e_n, LANES), 1)
    btot = jnp.zeros((e_n, LANES), F32)
    tile_max = jnp.zeros((SUB, LANES), F32)
    lane_t = lax.broadcasted_iota(jnp.int32, (SUB, LANES), 1)
    for j in range(nblk):
        sl = slice(j * LANES, (j + 1) * LANES)
        selb = sel_ref[:, sl]
        selbf = selb.astype(BF16)
        pin = _dot(selbf, upp) + carry
        carry = pin[:, LANES - 1:LANES]
        btot = jnp.where(lane_b == j, carry, btot)
        pm_ref[0, :, sl] = jnp.where(selb > 0, pin, 0.0)
        rank = _dot(strict, selbf)
        dv_ref[0, :, sl] = (tok + float(j * LANES)) * float(N_EXPERTS) + rank
        cnt_row = jnp.sum(selb, axis=0, keepdims=True)
        cnt_ref[0, sl, :] = jnp.broadcast_to(cnt_row, (LANES, LANES)).T
        tile_max = jnp.where(lane_t == (j * LANES) // tr,
                             jnp.maximum(tile_max, jnp.max(cnt_row, axis=1, keepdims=True)), tile_max)
    bt_ref[0] = jnp.where(lane_b < nblk, btot, carry).astype(jnp.int32)
    mc_ref[0] = tile_max.astype(jnp.int32)


def _select(aff, cap):
    b, e_n, t = aff.shape
    tr = _row_tile(t)
    assert t // tr <= LANES
    spec = pl.BlockSpec((1, e_n, t), lambda bi: (bi, 0, 0))
    return pl.pallas_call(
        functools.partial(_select_kernel, t=t, cap=cap, tr=tr),
        grid=(b,),
        in_specs=[spec],
        out_specs=[
            spec,
            spec,
            pl.BlockSpec((1, t, LANES), lambda bi: (bi, 0, 0)),
            pl.BlockSpec((1, e_n, LANES), lambda bi: (bi, 0, 0)),
            pl.BlockSpec((1, SUB, LANES), lambda bi: (bi, 0, 0)),
        ],
        out_shape=[
            jax.ShapeDtypeStruct((b, e_n, t), F32),
            jax.ShapeDtypeStruct((b, e_n, t), F32),
            jax.ShapeDtypeStruct((b, t, LANES), F32),
            jax.ShapeDtypeStruct((b, e_n, LANES), jnp.int32),
            jax.ShapeDtypeStruct((b, SUB, LANES), jnp.int32),
        ],
        scratch_shapes=[pltpu.VMEM((e_n, t), F32)],
        compiler_params=_cparams(("parallel",)),
        name="select",
    )(aff)


def _compact_kernel(bt_ref, pm_ref, dv_ref, aff_ref, src_ref, dst_ref, gate_ref,
                    *, t, cap, c_pad, nseg, batch):
    bi = batch
    e_n = N_EXPERTS
    nblk = t // LANES
    nct = c_pad // LANES
    gate_ref[...] = jnp.zeros(gate_ref.shape, F32)
    lane_e = lax.broadcasted_iota(jnp.int32, (LANES, LANES), 1)
    slot_l = lax.broadcasted_iota(jnp.int32, (1, LANES), 1)

    spr = LANES // nseg
    sh_spr = spr.bit_length() - 1
    sh_grp = (SUB * nseg).bit_length() - 1
    sub_i = lax.broadcasted_iota(jnp.int32, (SUB, LANES), 0)
    lane_i = lax.broadcasted_iota(jnp.int32, (SUB, LANES), 1)
    pick = (lane_i >> sh_spr) == sub_i
    pl_i = lax.broadcasted_iota(jnp.int32, (LANES, LANES), 0)
    pl_j = lax.broadcasted_iota(jnp.int32, (LANES, LANES), 1)
    perm = ((pl_i & (spr - 1)) == ((pl_j >> sh_grp) << 3) + (pl_j & (SUB - 1))).astype(BF16)
    seg_j = (lane_i & (SUB * nseg - 1)) >> 3

    def segment_tile(rows):
        spread = _dot3_right(jnp.where(pick, rows.astype(F32), 0.0), perm).astype(jnp.int32)
        return (((spread >> 3) * nseg + seg_j) << 3) + (spread & (SUB - 1))

    def expert(e):
        base = e * LANES

        def ctile(ct, j0):
            c0 = pl.multiple_of(ct * LANES, LANES)
            j_lo = lax.while_loop(
                lambda j: (j < nblk) & (bt_ref[base + jnp.minimum(j, nblk - 1)] <= c0),
                lambda j: j + 1, j0)
            j_hi = lax.while_loop(
                lambda j: (j < nblk) & (bt_ref[base + jnp.minimum(j, nblk - 1)] < c0 + LANES),
                lambda j: j + 1, j_lo)
            j_end = jnp.minimum(j_hi + 1, nblk)
            want = (lax.broadcasted_iota(jnp.int32, (LANES, 1), 0) + (c0 + 1)).astype(F32)

            def blk(j, acc):
                acc_d, acc_g = acc
                cols = pl.ds(pl.multiple_of(j * LANES, LANES), LANES)
                hit = pm_ref[0, e:e + 1, cols] == want
                acc_d = acc_d + jnp.where(hit, dv_ref[0, e:e + 1, cols], 0.0)
                acc_g = acc_g + jnp.where(hit, aff_ref[0, e:e + 1, cols], 0.0)
                return acc_d, acc_g

            zero = jnp.zeros((LANES, LANES), F32)
            acc_d, acc_g = lax.fori_loop(j_lo, j_end, blk, (zero, zero))
            v = jnp.sum(acc_d.T, axis=0, keepdims=True).astype(jnp.int32)
            tok = v >> 4
            rank = v & (N_EXPERTS - 1)
            p = slot_l + (c0 - cap)
            is_pad = p >= 0
            src = bi * t + jnp.where(is_pad, 0, tok)
            dst = jnp.where(is_pad, N_EXPERTS * t + e * (c_pad - cap) + p, rank * t + tok)
            out_rows = pl.ds(pl.multiple_of((e * nct + ct) * SUB, SUB), SUB)
            src_ref[0, out_rows, :] = segment_tile(src)
            dst_ref[0, out_rows, :] = segment_tile(dst)
            gcol = jnp.sum(acc_g, axis=1, keepdims=True)
            rows = pl.ds(c0, LANES)
            gate_ref[0, rows, :] = jnp.where(lane_e == e, gcol, gate_ref[0, rows, :])
            return j_lo

        lax.fori_loop(0, c_pad // LANES, ctile, 0)

    for e in range(e_n):
        expert(e)


def _compact(btot, pm, dv, aff, cap, c_pad, nseg, batch):
    _, e_n, t = aff.shape
    assert nseg in (1, 2, 4, 8)
    tile_rows = e_n * (c_pad // LANES) * SUB
    spec = pl.BlockSpec((1, e_n, t), lambda i, bt: (0, 0, 0))
    ospec = pl.BlockSpec((1, tile_rows, LANES), lambda i, bt: (0, 0, 0))
    src, dst, gate = pl.pallas_call(
        functools.partial(_compact_kernel, t=t, cap=cap, c_pad=c_pad, nseg=nseg, batch=batch),
        grid_spec=pltpu.PrefetchScalarGridSpec(
            num_scalar_prefetch=1,
            grid=(1,),
            in_specs=[spec, spec, spec],
            out_specs=[ospec, ospec, pl.BlockSpec((1, c_pad, LANES), lambda i, bt: (0, 0, 0))],
        ),
        out_shape=[
            jax.ShapeDtypeStruct((1, tile_rows, LANES), jnp.int32),
            jax.ShapeDtypeStruct((1, tile_rows, LANES), jnp.int32),
            jax.ShapeDtypeStruct((1, c_pad, LANES), F32),
        ],
        compiler_params=_cparams(("arbitrary",)),
        name="compact",
    )(btot.reshape(e_n * LANES), pm, dv, aff)
    ids = lambda a: a.reshape(-1, SUB, LANES)[:, :nseg].reshape(-1)
    return ids(src), ids(dst), gate


def _sc_mesh():
    return plsc.VectorSubcoreMesh(core_axis_name="core", subcore_axis_name="subcore")


def _row_gather(table, idx):
    m = idx.shape[0]
    d = table.shape[1]

    @functools.partial(pl.kernel, out_type=jax.ShapeDtypeStruct((m, d), table.dtype),
                       mesh=_sc_mesh(), scratch_types=[])
    def gather_kernel(x_hbm, i_hbm, o_hbm):
        def body(i_vmem, o_vmem):
            pltpu.sync_copy(x_hbm.at[i_vmem.at[0]], o_vmem)

        pltpu.emit_pipeline(
            body,
            grid=(m // SC_WINDOW,),
            in_specs=[pl.BlockSpec((1, SC_WINDOW), lambda i: (0, i))],
            out_specs=[pl.BlockSpec((SC_WINDOW, d), lambda i: (i, 0))],
            core_axis_name=("core", "subcore"),
            dimension_semantics=(pltpu.PARALLEL,),
        )(i_hbm, o_hbm)

    return gather_kernel(table, idx.reshape(1, m))


def _row_scatter(rows, idx, n_out):
    m, d = rows.shape

    @functools.partial(pl.kernel, out_type=jax.ShapeDtypeStruct((n_out, d), rows.dtype),
                       mesh=_sc_mesh(), scratch_types=[])
    def scatter_kernel(x_hbm, i_hbm, o_hbm):
        def body(x_vmem, i_vmem):
            pltpu.sync_copy(x_vmem, o_hbm.at[i_vmem.at[0]])

        pltpu.emit_pipeline(
            body,
            grid=(m // SC_WINDOW,),
            in_specs=[pl.BlockSpec((SC_WINDOW, d), lambda i: (i, 0)),
                      pl.BlockSpec((1, SC_WINDOW), lambda i: (0, i))],
            out_specs=[],
            core_axis_name=("core", "subcore"),
            dimension_semantics=(pltpu.PARALLEL,),
        )(x_hbm, i_hbm)

    return scatter_kernel(rows, idx.reshape(1, m))


def _expert_kernel(gate_ref, x_ref, wg_ref, wu_ref, wd_ref, y_ref, xb_ref, acc_ref, *, c_pad, n_f):
    e = pl.program_id(0)
    f = pl.program_id(1)

    @pl.when(f == 0)
    def _():
        w = _from_segments(x_ref)
        xb_ref[...] = jnp.concatenate([_unpack_hi(w), _unpack_lo(w)], axis=1).astype(BF16)

    x = xb_ref[...]
    hid = _silu(_dot(x, wg_ref[0].astype(BF16))) * _dot(x, wu_ref[0].astype(BF16))
    part = _dot(hid.astype(BF16), wd_ref[0].astype(BF16))

    @pl.when(f == 0)
    def _():
        acc_ref[...] = part

    @pl.when(f > 0)
    def _():
        acc_ref[...] = acc_ref[...] + part

    @pl.when(f == n_f - 1)
    def _():
        lane = lax.broadcasted_iota(jnp.int32, (c_pad, LANES), 1)
        gcol = jnp.sum(jnp.where(lane == e, gate_ref[0], 0.0), axis=1, keepdims=True)
        _to_segments(y_ref, _pack_pairs(acc_ref[...] * gcol))


def _experts(gate, xs, w_gate, w_up, w_down):
    _, c_pad, _ = gate.shape
    e_n, d, d_ff = w_gate.shape
    nseg = d // 2 // LANES
    fc = 256
    n_f = d_ff // fc
    blk = pl.BlockSpec((c_pad // SUB, nseg, SUB, LANES), lambda e, f: (e, 0, 0, 0))
    return pl.pallas_call(
        functools.partial(_expert_kernel, c_pad=c_pad, n_f=n_f),
        grid=(e_n, n_f),
        in_specs=[
            pl.BlockSpec((1, c_pad, LANES), lambda e, f: (0, 0, 0)),
            blk,
            pl.BlockSpec((1, d, fc), lambda e, f: (e, 0, f)),
            pl.BlockSpec((1, d, fc), lambda e, f: (e, 0, f)),
            pl.BlockSpec((1, fc, d), lambda e, f: (e, f, 0)),
        ],
        out_specs=blk,
        out_shape=jax.ShapeDtypeStruct(xs.shape, jnp.int32),
        scratch_shapes=[pltpu.VMEM((c_pad, d), BF16), pltpu.VMEM((c_pad, d), F32)],
        compiler_params=_cparams(("parallel", "arbitrary")),
        name="experts",
    )(gate, xs, w_gate, w_up, w_down)


def _combine_kernel(mc_ref, h1_ref, cnt_ref, fg_ref, *rest, n_layers, nb):
    z_refs, o_ref, acc_ref = rest[:nb], rest[nb], rest[nb + 1]
    bi = pl.program_id(0)
    i = pl.program_id(1)
    j = pl.program_id(2)

    @pl.when(j == 0)
    def _():
        _to_segments(acc_ref, h1_ref[0])

    for k in range(nb):
        @pl.when((bi == k) & (j < mc_ref[bi * LANES + i]))
        def _():
            g, nz = z_refs[k].shape[0], z_refs[k].shape[1]
            take = cnt_ref[0].reshape(g, 1, SUB, LANES) > j.astype(F32)
            w = z_refs[k][...]
            acc_ref[:, 0:nz] = acc_ref[:, 0:nz] + jnp.where(take, _unpack_hi(w), 0.0)
            acc_ref[:, nz:2 * nz] = acc_ref[:, nz:2 * nz] + jnp.where(take, _unpack_lo(w), 0.0)

    @pl.when(j == n_layers - 1)
    def _():
        o_ref[0] = _rms(_from_segments(acc_ref), fg_ref[...])


def _combine(maxcnt, h1, zs, cnt, final_norm_g):
    b, t, d = h1.shape
    n_layers = N_EXPERTS
    tr = _row_tile(t)
    n_tiles = t // tr

    def z_map(k):
        def index(bi, i, j, mc):
            layer = jnp.maximum(jnp.minimum(j, mc[bi * LANES + i] - 1), 0)
            return (jnp.where(bi == k, layer * n_tiles + i, 0), 0, 0, 0)
        return index

    seg_block = (tr // SUB, d // LANES, SUB, LANES)
    z_block = (tr // SUB, d // 2 // LANES, SUB, LANES)
    return pl.pallas_call(
        functools.partial(_combine_kernel, n_layers=n_layers, nb=b),
        grid_spec=pltpu.PrefetchScalarGridSpec(
            num_scalar_prefetch=1,
            grid=(b, n_tiles, n_layers),
            in_specs=[
                pl.BlockSpec((1, tr, d), lambda bi, i, j, mc: (bi, i, 0)),
                pl.BlockSpec((1, tr, LANES), lambda bi, i, j, mc: (bi, i, 0)),
                pl.BlockSpec((1, d), lambda bi, i, j, mc: (0, 0)),
            ] + [pl.BlockSpec(z_block, z_map(k)) for k in range(b)],
            out_specs=pl.BlockSpec((1, tr, d), lambda bi, i, j, mc: (bi, i, 0)),
            scratch_shapes=[pltpu.VMEM(seg_block, F32)],
        ),
        out_shape=jax.ShapeDtypeStruct((b, t, d), F32),
        compiler_params=_cparams(("parallel", "arbitrary", "arbitrary")),
        name="combine",
    )(maxcnt, h1, cnt, final_norm_g.reshape(1, d), *zs)


def kernel(x, meta_tokens, mix_norm_g, w_in, b_gates, conv_w, conv_b, ret_decay_logit, ret_gn_g,
           mlstm_gn_g, w_out, ffn_norm_g, w_router, w_gate, w_up, w_down, final_norm_g):
    b, s, d = x.shape
    t = CHUNK + s
    n = N_META + s
    cap = 2 * n // N_EXPERTS
    c_pad = -(-cap // LANES) * LANES

    meta = jnp.broadcast_to(meta_tokens.astype(x.dtype)[None], (b, N_META, d))
    h_pad = jnp.concatenate([jnp.zeros((b, PAD, d), x.dtype), meta, x], axis=1)

    half = DH // 2
    pos = jnp.arange(t, dtype=F32) - PAD
    inv = ROPE_BASE ** (-jnp.arange(half, dtype=F32) / half)
    ang = pos[:, None] * inv[None, :]
    cosf = jnp.concatenate([jnp.cos(ang), jnp.cos(ang)], axis=1)
    sinf = jnp.concatenate([-jnp.sin(ang), jnp.sin(ang)], axis=1)

    assert mix_norm_g.shape[0] == 1, "single-layer block only"
    rq, rk, rv, rgs, mq, mk, mv, mos, kwf, kwb, gcol, grow = _in_proj(
        h_pad, mix_norm_g[0], w_in[0], b_gates[0], conv_w[0], conv_b[0], cosf, sinf)
    lg = jax.nn.log_sigmoid(ret_decay_logit[0].astype(F32))
    ret = _retention(lg, rq, rk, rv)
    hm = _mlstm(mq, mk, mv, kwf, kwb, gcol, grow)
    h1, u2, aff = _out_proj(h_pad, ret, hm, rgs, mos, ret_gn_g[0], mlstm_gn_g[0], w_out[0],
                            ffn_norm_g[0], w_router[0])
    nseg = d // 2 // LANES
    tiled = lambda a: a.reshape(-1, nseg, SUB, LANES)
    flat = lambda a: a.reshape(-1, LANES)
    z_rows = -(-(N_EXPERTS * t + N_EXPERTS * (c_pad - cap)) // SUB) * SUB
    routed = []
    for bi in range(b):
        aff_b = aff[bi:bi + 1]
        pm, dv, cnt, btot, maxcnt = _select(aff_b, cap)
        src, dst, gate = _compact(btot, pm, dv, aff_b, cap, c_pad, nseg, bi)
        routed.append((_row_gather(flat(u2), src), dst, gate, cnt, maxcnt))
    zs = []
    for xs, dst, gate, _, _ in routed:
        ys = _experts(gate, tiled(xs), w_gate[0], w_up[0], w_down[0])
        zs.append(tiled(_row_scatter(flat(ys), dst, z_rows * nseg)))
    cnt = jnp.concatenate([r[3] for r in routed], axis=0)
    maxcnt = jnp.concatenate([r[4][:, 0, :] for r in routed], axis=0).reshape(-1)
    out = _combine(maxcnt, h1, zs, cnt, final_norm_g)
    return out[:, CHUNK:]
```

```python
import functools

import jax
import jax.numpy as jnp
from jax import lax
from jax.experimental import pallas as pl
from jax.experimental.pallas import tpu as pltpu
from jax.experimental.pallas import tpu_sc as plsc

F32 = jnp.float32
BF16 = jnp.bfloat16

LANES = 128
CHUNK = 128
N_META = 16
PAD = CHUNK - N_META
HEADS = 4
DH = 128
N_EXPERTS = 16
CONV_K = 5
HALO = 8
SUB = 8
SC_WINDOW = 128
EPS = 1e-6
NEG = -1e30
ROPE_BASE = 10000.0
VMEM_LIMIT = 56 * 1024 * 1024


def _cparams(sem, vmem=VMEM_LIMIT, **kw):
    return pltpu.CompilerParams(dimension_semantics=sem, vmem_limit_bytes=vmem, **kw)


def _dot(a, b):
    return jnp.dot(a, b, preferred_element_type=F32)


def _dot_nt(a, b):
    return lax.dot_general(a, b, (((1,), (1,)), ((), ())), preferred_element_type=F32)


def _dot_tn(a, b):
    return lax.dot_general(a, b, (((0,), (0,)), ((), ())), preferred_element_type=F32)


def _split3(x):
    hi = x.astype(BF16)
    r = x - hi.astype(F32)
    mid = r.astype(BF16)
    lo = (r - mid.astype(F32)).astype(BF16)
    return hi, mid, lo


def _dot3_left(m_bf, x):
    hi, mid, lo = _split3(x)
    return _dot(m_bf, hi) + _dot(m_bf, mid) + _dot(m_bf, lo)


def _dot3_right(x, m_bf):
    hi, mid, lo = _split3(x)
    return _dot(hi, m_bf) + _dot(mid, m_bf) + _dot(lo, m_bf)


def _rms(x, g):
    return x * lax.rsqrt(jnp.mean(x * x, axis=-1, keepdims=True) + EPS) * g


def _log_sigmoid(x):
    return jnp.minimum(x, 0.0) - jnp.log1p(jnp.exp(-jnp.abs(x)))


def _silu(x):
    return x * (1.0 / (1.0 + jnp.exp(-x)))


def _sigmoid(x):
    return 1.0 / (1.0 + jnp.exp(-x))


def _row_tile(t):
    return 640 if t % 640 == 0 else CHUNK


def _chunk_unroll(n_chunks):
    return 13 if n_chunks % 13 == 0 else 1


def _to_segments(dst_ref, x):
    r, w = x.shape
    for j in range(w // LANES):
        dst_ref[:, j] = x[:, j * LANES:(j + 1) * LANES].reshape(r // SUB, SUB, LANES)


def _from_segments(src_ref):
    g, nseg = src_ref.shape[0], src_ref.shape[1]
    return jnp.concatenate([src_ref[:, j].reshape(g * SUB, LANES) for j in range(nseg)], axis=1)


def _bf16_bits(x):
    return pltpu.bitcast(x.astype(BF16).astype(F32), jnp.int32)


def _pack_pairs(x):
    half = x.shape[1] // 2
    return _bf16_bits(x[:, :half]) | lax.shift_right_logical(_bf16_bits(x[:, half:]), 16)


def _unpack_hi(w):
    return pltpu.bitcast(w & -65536, F32)


def _unpack_lo(w):
    return pltpu.bitcast(w << 16, F32)


def _in_proj_kernel(h_ref, hp_ref, hn_ref, ng_ref, wr_ref, wm_ref, wg_ref, wgt_ref,
                    bg_ref, bgt_ref, cw_ref, cb_ref, cos_ref, sin_ref,
                    rq_ref, rk_ref, rv_ref, rg_ref, mq_ref, mk_ref, mv_ref, mo_ref,
                    kwf_ref, kwb_ref, gc_ref, gr_ref, ext_ref, u_ref, gts_ref, *, tr, n_tiles):
    i = pl.program_id(1)
    ng = ng_ref[...]
    w = HEADS * DH
    nd = 2 * HEADS

    up = _rms(hp_ref[0], ng)
    un = _rms(hn_ref[0], ng)
    up = jnp.where(i == 0, 0.0, up)
    un = jnp.where(i == n_tiles - 1, 0.0, un)
    uh = jnp.concatenate([up, un], axis=0).astype(BF16)
    halo = _dot(uh, wm_ref[:, 0:2 * w])
    ext_ref[0:HALO, :] = halo[0:HALO]
    ext_ref[HALO + tr:2 * HALO + tr, :] = halo[HALO:2 * HALO]

    sub = tr // 2
    lane_s = lax.broadcasted_iota(jnp.int32, (sub, LANES), 1)
    for r0 in (0, sub):
        rs = slice(r0, r0 + sub)
        u = _rms(h_ref[0, rs, :], ng).astype(BF16)
        u_ref[rs, :] = u
        real = lax.broadcasted_iota(jnp.int32, (sub, 1), 0) + (i * tr + r0) >= PAD
        cosf = cos_ref[rs, :]
        sinf = sin_ref[rs, :]
        pq = _dot(u, wr_ref[:, 0:w])
        pk = _dot(u, wr_ref[:, w:2 * w])
        for hd in range(HEADS):
            sl = slice(hd * DH, (hd + 1) * DH)
            xq = pq[:, sl]
            xk = pk[:, sl]
            rq_ref[0, rs, sl] = ((xq * cosf + pltpu.roll(xq, DH // 2, 1) * sinf) * (DH ** -0.5)).astype(BF16)
            rk_ref[0, rs, sl] = (xk * cosf + pltpu.roll(xk, DH // 2, 1) * sinf).astype(BF16)
        rv_ref[0, rs, :] = _dot(u, wr_ref[:, 2 * w:3 * w]).astype(BF16)
        rg_ref[0, rs, :] = _silu(_dot(u, wr_ref[:, 3 * w:4 * w])).astype(BF16)
        ext_ref[HALO + r0:HALO + r0 + sub, :] = _dot(u, wm_ref[:, 0:2 * w])
        mv_ref[0, rs, :] = _dot(u, wm_ref[:, 2 * w:3 * w]).astype(BF16)
        mo_ref[0, rs, :] = _sigmoid(_dot(u, wm_ref[:, 3 * w:4 * w])).astype(BF16)
        gts = _dot(u, wg_ref[...]) + bg_ref[...]
        gts_ref[rs, :] = jnp.where(lane_s < nd, jnp.where(real, gts, NEG),
                                   jnp.where(real & (lane_s < 2 * nd), _log_sigmoid(gts), 0.0))

    rown = lax.broadcasted_iota(jnp.int32, (2 * nd, tr), 0)
    coln = lax.broadcasted_iota(jnp.int32, (1, tr), 1) + i * tr
    realr = coln >= PAD
    gtr = _dot_nt(wgt_ref[...], u_ref[...]) + bgt_ref[...]
    gtr = jnp.where(rown < nd, jnp.where(realr, gtr, NEG),
                    jnp.where(realr, _log_sigmoid(gtr), 0.0))
    ci = lax.broadcasted_iota(jnp.int32, (CHUNK, CHUNK), 0)
    cj = lax.broadcasted_iota(jnp.int32, (CHUNK, CHUNK), 1)
    causal = ci >= cj
    anti = ci <= cj
    low = causal.astype(BF16)
    upp = anti.astype(BF16)
    lane_c = lax.broadcasted_iota(jnp.int32, (CHUNK, LANES), 1)
    lane_1 = lax.broadcasted_iota(jnp.int32, (1, LANES), 1)
    rown_c = lax.broadcasted_iota(jnp.int32, (2 * nd, CHUNK), 0)
    for c in range(tr // CHUNK):
        rs = slice(c * CHUNK, (c + 1) * CHUNK)
        realc = lax.broadcasted_iota(jnp.int32, (CHUNK, 1), 0) + (i * tr + c * CHUNK) >= PAD
        conv = cb_ref[...] + jnp.zeros((CHUNK, 2 * w), F32)
        for k in range(CONV_K):
            off = HALO + c * CHUNK + k - CONV_K // 2
            conv = conv + cw_ref[k:k + 1, :] * ext_ref[off:off + CHUNK, :]
        qk = jnp.where(realc, _silu(conv), 0.0)
        mq_ref[0, rs, :] = (qk[:, 0:w] * (DH ** -0.5)).astype(BF16)
        mk_ref[0, rs, :] = qk[:, w:2 * w].astype(BF16)
        kconv = qk[:, w:2 * w]
        g = gts_ref[rs, :]
        lf = jnp.where(lane_c >= nd, g, 0.0)
        pre = _dot3_left(low, lf)
        suf = _dot3_left(upp, lf)
        bc = pltpu.roll(jnp.where(lane_c < nd + HEADS, pre, suf), LANES - nd, 1)
        blast = jnp.where(lane_1 < HEADS, bc[CHUNK - 1:CHUNK], bc[0:1])
        log_u = blast - bc + g
        a = jnp.max(log_u, axis=0, keepdims=True)
        wu = jnp.exp(log_u - a)
        gl = gtr[:, rs]
        lfr = jnp.where(rown_c >= nd, gl, 0.0)
        prer = _dot3_right(lfr, upp)
        sufr = _dot3_right(lfr, low)
        rr = gl[0:nd] - jnp.concatenate([prer[nd:nd + HEADS], sufr[nd + HEADS:2 * nd]], axis=0)
        for hd in range(HEADS):
            hs = slice(hd * DH, (hd + 1) * DH)
            cols = []
            for dr, (msk, kw_ref) in enumerate(((causal, kwf_ref), (anti, kwb_ref))):
                i8 = dr * HEADS + hd
                rmax = jnp.max(jnp.where(msk, rr[i8:i8 + 1, :], NEG), axis=1, keepdims=True)
                cols += [rmax, bc[:, i8:i8 + 1]]
                kw_ref[0, rs, hs] = (kconv[:, hs] * wu[:, i8:i8 + 1]).astype(BF16)
            gc_ref[0, hd, rs, :] = jnp.where(
                lane_c == 0, cols[0], jnp.where(lane_c == 1, cols[1], jnp.where(lane_c == 2, cols[2], cols[3])))
            scal = [jnp.broadcast_to(x[:, j:j + 1], (1, CHUNK))
                    for j in (hd, HEADS + hd) for x in (a, blast)]
            gr_ref[0, hd, :, rs] = jnp.concatenate(
                [rr[hd:hd + 1], rr[HEADS + hd:HEADS + hd + 1]] + scal + [jnp.zeros((2, CHUNK), F32)], axis=0)


def _in_proj(h_pad, norm_g, w_in, b_gates, conv_w, conv_b, cosf, sinf):
    b, t, d = h_pad.shape
    tr = _row_tile(t)
    n_tiles = t // tr
    w = HEADS * DH
    wr = w_in[:, 0:4 * w].astype(BF16)
    wm = w_in[:, 4 * w:8 * w].astype(BF16)
    wg = jnp.pad(w_in[:, 8 * w:], ((0, 0), (0, LANES - 4 * HEADS))).astype(BF16)
    wgt = w_in[:, 8 * w:].T.astype(BF16)
    bg = jnp.pad(b_gates, (0, LANES - 4 * HEADS)).reshape(1, LANES)
    bgt = b_gates.reshape(4 * HEADS, 1)
    hb = tr // HALO
    last_h = t // HALO - 1
    seq = lambda bi, i: (bi, i, 0)
    const2 = lambda bi, i: (0, 0)
    out_bf = jax.ShapeDtypeStruct((b, t, w), BF16)
    kern = functools.partial(_in_proj_kernel, tr=tr, n_tiles=n_tiles)
    return pl.pallas_call(
        kern,
        grid=(b, n_tiles),
        in_specs=[
            pl.BlockSpec((1, tr, d), seq),
            pl.BlockSpec((1, HALO, d), lambda bi, i: (bi, jnp.maximum(i * hb - 1, 0), 0)),
            pl.BlockSpec((1, HALO, d), lambda bi, i: (bi, jnp.minimum((i + 1) * hb, last_h), 0)),
            pl.BlockSpec((1, d), const2),
            pl.BlockSpec((d, 4 * w), const2),
            pl.BlockSpec((d, 4 * w), const2),
            pl.BlockSpec((d, LANES), const2),
            pl.BlockSpec((4 * HEADS, d), const2),
            pl.BlockSpec((1, LANES), const2),
            pl.BlockSpec((4 * HEADS, 1), const2),
            pl.BlockSpec((CONV_K, 2 * w), const2),
            pl.BlockSpec((1, 2 * w), const2),
            pl.BlockSpec((tr, DH), lambda bi, i: (i, 0)),
            pl.BlockSpec((tr, DH), lambda bi, i: (i, 0)),
        ],
        out_specs=[pl.BlockSpec((1, tr, w), seq)] * 10 + [
            pl.BlockSpec((1, HEADS, tr, LANES), lambda bi, i: (bi, 0, i, 0)),
            pl.BlockSpec((1, HEADS, 8, tr), lambda bi, i: (bi, 0, 0, i)),
        ],
        out_shape=[out_bf] * 10 + [
            jax.ShapeDtypeStruct((b, HEADS, t, LANES), F32),
            jax.ShapeDtypeStruct((b, HEADS, 8, t), F32),
        ],
        scratch_shapes=[pltpu.VMEM((tr + 2 * HALO, 2 * w), F32), pltpu.VMEM((tr, d), BF16),
                        pltpu.VMEM((tr, LANES), F32)],
        compiler_params=_cparams(("parallel", "arbitrary")),
        name="in_proj",
    )(h_pad, h_pad, h_pad, norm_g.reshape(1, d), wr, wm, wg, wgt, bg, bgt,
      conv_w, conv_b.reshape(1, 2 * w), cosf, sinf)


def _retention_kernel(lg_ref, q_ref, k_ref, v_ref, o_ref, *, n_chunks):
    hd = pl.program_id(1)
    lgf = lg_ref[0, hd]
    lgb = lg_ref[1, hd]
    li = lax.broadcasted_iota(jnp.int32, (CHUNK, CHUNK), 0)
    mi = lax.broadcasted_iota(jnp.int32, (CHUNK, CHUNK), 1)
    diff = (li - mi).astype(F32)
    decay = (jnp.where(diff >= 0, jnp.exp(lgf * jnp.maximum(diff, 0.0)), 0.0)
             + jnp.where(diff <= 0, jnp.exp(lgb * jnp.maximum(-diff, 0.0)), 0.0))
    lcol = lax.broadcasted_iota(jnp.int32, (CHUNK, 1), 0).astype(F32)
    one = jnp.ones((1, 1), F32)
    zeta_f = jnp.exp(lgf * (CHUNK - 1.0 - lcol))
    xi_f = jnp.exp(lgf * (lcol + 1.0))
    g_f = jnp.exp(lgf * CHUNK * one)
    zeta_b = jnp.exp(lgb * lcol)
    xi_b = jnp.exp(lgb * (CHUNK - lcol))
    g_b = jnp.exp(lgb * CHUNK * one)

    def fwd(c, state):
        rows = pl.ds(pl.multiple_of(c * CHUNK, CHUNK), CHUNK)
        q = q_ref[0, rows, :]
        k = k_ref[0, rows, :]
        v = v_ref[0, rows, :]
        s = _dot_nt(q, k) * decay
        intra = _dot(s.astype(BF16), v)
        cross = _dot(q, state.astype(BF16)) * xi_f
        o_ref[0, rows, :] = intra + cross
        kz = (k.astype(F32) * zeta_f).astype(BF16)
        return g_f * state + _dot_tn(kz, v)

    unroll = _chunk_unroll(n_chunks)
    lax.fori_loop(0, n_chunks, fwd, jnp.zeros((DH, DH), F32), unroll=unroll)

    def bwd(j, state):
        c = n_chunks - 1 - j
        rows = pl.ds(pl.multiple_of(c * CHUNK, CHUNK), CHUNK)
        q = q_ref[0, rows, :]
        k = k_ref[0, rows, :]
        v = v_ref[0, rows, :]
        cross = _dot(q, state.astype(BF16)) * xi_b
        o_ref[0, rows, :] = o_ref[0, rows, :] + cross
        kz = (k.astype(F32) * zeta_b).astype(BF16)
        return g_b * state + _dot_tn(kz, v)

    lax.fori_loop(0, n_chunks, bwd, jnp.zeros((DH, DH), F32), unroll=unroll)


def _retention(lg, rq, rk, rv):
    b, t, _ = rq.shape
    spec = pl.BlockSpec((1, t, DH), lambda bi, hi, lg_ref: (bi, 0, hi))
    return pl.pallas_call(
        functools.partial(_retention_kernel, n_chunks=t // CHUNK),
        grid_spec=pltpu.PrefetchScalarGridSpec(
            num_scalar_prefetch=1,
            grid=(b, HEADS),
            in_specs=[spec, spec, spec],
            out_specs=spec,
        ),
        out_shape=jax.ShapeDtypeStruct((b, t, HEADS * DH), F32),
        compiler_params=_cparams(("parallel", "arbitrary")),
        name="retention",
    )(lg, rq, rk, rv)


def _mlstm_kernel(q_ref, k_ref, v_ref, kwf_ref, kwb_ref, gc_ref, gr_ref, o_ref, *, n_chunks):
    li = lax.broadcasted_iota(jnp.int32, (CHUNK, CHUNK), 0)
    mi = lax.broadcasted_iota(jnp.int32, (CHUNK, CHUNK), 1)
    causal = li >= mi
    anti = li <= mi

    ones = jnp.ones((CHUNK, DH), BF16)

    def step(c, carry, backward):
        st, m_st = carry
        rows = pl.ds(pl.multiple_of(c * CHUNK, CHUNK), CHUNK)
        q = q_ref[0, rows, :]
        k = k_ref[0, rows, :]
        v_aug = jnp.concatenate([v_ref[0, rows, :], ones], axis=1)
        gcol = gc_ref[0, 0, rows, :]
        grow = gr_ref[0, 0, :, rows]
        if backward:
            kw = kwb_ref[0, rows, :]
            r_max, bc = gcol[:, 2:3], gcol[:, 3:4]
            r_row, a, b_last = grow[1:2, :], grow[4:5, 0:1], grow[5:6, 0:1]
            mask = anti
        else:
            kw = kwf_ref[0, rows, :]
            r_max, bc = gcol[:, 0:1], gcol[:, 1:2]
            r_row, a, b_last = grow[0:1, :], grow[2:3, 0:1], grow[3:4, 0:1]
            mask = causal
        mm = jnp.maximum(jnp.broadcast_to(r_max, (CHUNK, DH)), m_st)
        s = _dot_nt(q, k) * jnp.where(mask, jnp.exp(r_row - mm), 0.0)
        w_inter = jnp.exp(m_st - mm)
        w2 = jnp.concatenate([w_inter, w_inter], axis=1)
        tot = _dot(s.astype(BF16), v_aug) + _dot(q, st.astype(BF16)) * w2
        floor = jnp.exp(-(jnp.broadcast_to(bc, (CHUNK, DH)) + mm))
        hval = tot[:, 0:DH] / jnp.maximum(jnp.abs(tot[:, DH:2 * DH]), floor)
        if backward:
            o_ref[0, rows, :] = o_ref[0, rows, :] + hval
        else:
            o_ref[0, rows, :] = hval
        upd = _dot_tn(kw, v_aug)
        m_new = jnp.maximum(b_last + m_st, a)
        f = jnp.exp(b_last + m_st - m_new)
        g = jnp.exp(a - m_new)
        return f * st + g * upd, m_new

    init = (jnp.zeros((DH, 2 * DH), F32), jnp.zeros((1, 1), F32))
    unroll = _chunk_unroll(n_chunks)
    lax.fori_loop(0, n_chunks, lambda c, s: step(c, s, False), init, unroll=unroll)
    lax.fori_loop(0, n_chunks, lambda j, s: step(n_chunks - 1 - j, s, True), init, unroll=unroll)


def _mlstm(mq, mk, mv, kwf, kwb, gcol, grow):
    b, t, _ = mq.shape
    spec = pl.BlockSpec((1, t, DH), lambda bi, hi: (bi, 0, hi))
    return pl.pallas_call(
        functools.partial(_mlstm_kernel, n_chunks=t // CHUNK),
        grid=(b, HEADS),
        in_specs=[spec, spec, spec, spec, spec,
                  pl.BlockSpec((1, 1, t, LANES), lambda bi, hi: (bi, hi, 0, 0)),
                  pl.BlockSpec((1, 1, 8, t), lambda bi, hi: (bi, hi, 0, 0))],
        out_specs=spec,
        out_shape=jax.ShapeDtypeStruct((b, t, HEADS * DH), F32),
        compiler_params=_cparams(("parallel", "arbitrary")),
        name="mlstm",
    )(mq, mk, mv, kwf, kwb, gcol, grow)


def _head_norm(y, g):
    outs = []
    for hd in range(HEADS):
        yh = y[:, hd * DH:(hd + 1) * DH]
        mu = jnp.mean(yh, axis=1, keepdims=True)
        yc = yh - mu
        var = jnp.mean(yc * yc, axis=1, keepdims=True)
        outs.append(yc * lax.rsqrt(var + EPS))
    return jnp.concatenate(outs, axis=1) * g


def _out_proj_kernel(h_ref, ret_ref, hm_ref, rg_ref, mo_ref, rgn_ref, mgn_ref, wo_ref,
                     fg_ref, wrt_ref, h1_ref, u2_ref, aff_ref, *, tr):
    i = pl.program_id(1)
    row = lax.broadcasted_iota(jnp.int32, (tr, 1), 0) + i * tr
    real = row >= PAD
    w = HEADS * DH
    y_ret = _head_norm(ret_ref[0], rgn_ref[...]) * rg_ref[0].astype(F32)
    y_m = _head_norm(mo_ref[0].astype(F32) * hm_ref[0], mgn_ref[...])
    y_ret = jnp.where(real, y_ret, 0.0).astype(BF16)
    y_m = jnp.where(real, y_m, 0.0).astype(BF16)
    h1 = h_ref[0] + _dot(y_ret, wo_ref[0:w, :]) + _dot(y_m, wo_ref[w:2 * w, :])
    h1_ref[0] = h1
    u2 = _rms(h1, fg_ref[...])
    _to_segments(u2_ref, _pack_pairs(u2))
    logits = _dot_nt(wrt_ref[...], u2.astype(BF16))
    mx = jnp.max(logits, axis=0, keepdims=True)
    ex = jnp.exp(logits - mx)
    aff = ex / jnp.sum(ex, axis=0, keepdims=True)
    coln = lax.broadcasted_iota(jnp.int32, (1, tr), 1) + i * tr
    aff_ref[0] = jnp.where(coln >= PAD, aff, -1.0)


def _out_proj(h_pad, ret, hm, rgs, mos, ret_gn_g, mlstm_gn_g, w_out, ffn_norm_g, w_router):
    b, t, d = h_pad.shape
    tr = _row_tile(t)
    n_tiles = t // tr
    nseg = d // 2 // LANES
    w = HEADS * DH
    seq = lambda bi, i: (bi, i, 0)
    const2 = lambda bi, i: (0, 0)
    return pl.pallas_call(
        functools.partial(_out_proj_kernel, tr=tr),
        grid=(b, n_tiles),
        in_specs=[
            pl.BlockSpec((1, tr, d), seq),
            pl.BlockSpec((1, tr, w), seq),
            pl.BlockSpec((1, tr, w), seq),
            pl.BlockSpec((1, tr, w), seq),
            pl.BlockSpec((1, tr, w), seq),
            pl.BlockSpec((1, w), const2),
            pl.BlockSpec((1, w), const2),
            pl.BlockSpec((2 * w, d), const2),
            pl.BlockSpec((1, d), const2),
            pl.BlockSpec((N_EXPERTS, d), const2),
        ],
        out_specs=[
            pl.BlockSpec((1, tr, d), seq),
            pl.BlockSpec((tr // SUB, nseg, SUB, LANES), lambda bi, i: (bi * n_tiles + i, 0, 0, 0)),
            pl.BlockSpec((1, N_EXPERTS, tr), lambda bi, i: (bi, 0, i)),
        ],
        out_shape=[
            jax.ShapeDtypeStruct((b, t, d), F32),
            jax.ShapeDtypeStruct((b * t // SUB, nseg, SUB, LANES), jnp.int32),
            jax.ShapeDtypeStruct((b, N_EXPERTS, t), F32),
        ],
        compiler_params=_cparams(("parallel", "arbitrary")),
        name="out_proj",
    )(h_pad, ret, hm, rgs, mos, ret_gn_g.reshape(1, w), mlstm_gn_g.reshape(1, w),
      w_out.astype(BF16), ffn_norm_g.reshape(1, d), w_router.T.astype(BF16))


def _select_kernel(aff_ref, pm_ref, dv_ref, cnt_ref, bt_ref, mc_ref, sel_ref, *, t, cap, tr):
    e_n = N_EXPERTS
    nblk = t // LANES
    bits = pltpu.bitcast(aff_ref[0], jnp.int32)
    capf = float(cap)

    def search(_, lohi):
        lo, hi = lohi
        mid = lo + ((hi - lo + 1) >> 1)
        cnt = jnp.sum((bits >= mid).astype(F32), axis=1, keepdims=True)
        ok = cnt >= capf
        return jnp.where(ok, mid, lo), jnp.where(ok, hi, mid - 1)

    lo0 = jnp.zeros((e_n, 1), jnp.int32)
    hi0 = jnp.full((e_n, 1), 0x3F800000, jnp.int32)
    thr, _ = lax.fori_loop(0, 31, search, (lo0, hi0))
    need = capf - jnp.sum((bits > thr).astype(F32), axis=1, keepdims=True)

    ci = lax.broadcasted_iota(jnp.int32, (LANES, LANES), 0)
    cj = lax.broadcasted_iota(jnp.int32, (LANES, LANES), 1)
    upp = (ci <= cj).astype(BF16)
    ei = lax.broadcasted_iota(jnp.int32, (e_n, e_n), 0)
    ej = lax.broadcasted_iota(jnp.int32, (e_n, e_n), 1)
    strict = (ej < ei).astype(BF16)

    carry = jnp.zeros((e_n, 1), F32)
    for j in range(nblk):
        sl = slice(j * LANES, (j + 1) * LANES)
        bb = bits[:, sl]
        eqf = (bb == thr).astype(F32)
        ceq = _dot(eqf.astype(BF16), upp) + carry
        carry = ceq[:, LANES - 1:LANES]
        sel_ref[:, sl] = jnp.where(bb > thr, 1.0, jnp.where(ceq <= need, eqf, 0.0))

    carry = jnp.zeros((e_n, 1), F32)
    tok = lax.broadcasted_iota(jnp.int32, (1, LANES), 1).astype(F32)
    lane_b = lax.broadcasted_iota(jnp.int32, (e_n, LANES), 1)
    btot = jnp.zeros((e_n, LANES), F32)
    tile_max = jnp.zeros((SUB, LANES), F32)
    lane_t = lax.broadcasted_iota(jnp.int32, (SUB, LANES), 1)
    for j in range(nblk):
        sl = slice(j * LANES, (j + 1) * LANES)
        selb = sel_ref[:, sl]
        selbf = selb.astype(BF16)
        pin = _dot(selbf, upp) + carry
        carry = pin[:, LANES - 1:LANES]
        btot = jnp.where(lane_b == j, carry, btot)
        pm_ref[0, :, sl] = jnp.where(selb > 0, pin, 0.0)
        rank = _dot(strict, selbf)
        dv_ref[0, :, sl] = (tok + float(j * LANES)) * float(N_EXPERTS) + rank
        cnt_row = jnp.sum(selb, axis=0, keepdims=True)
        cnt_ref[0, sl, :] = jnp.broadcast_to(cnt_row, (LANES, LANES)).T
        tile_max = jnp.where(lane_t == (j * LANES) // tr,
                             jnp.maximum(tile_max, jnp.max(cnt_row, axis=1, keepdims=True)), tile_max)
    bt_ref[0] = jnp.where(lane_b < nblk, btot, carry).astype(jnp.int32)
    mc_ref[0] = tile_max.astype(jnp.int32)


def _select(aff, cap):
    b, e_n, t = aff.shape
    tr = _row_tile(t)
    assert t // tr <= LANES
    spec = pl.BlockSpec((1, e_n, t), lambda bi: (bi, 0, 0))
    return pl.pallas_call(
        functools.partial(_select_kernel, t=t, cap=cap, tr=tr),
        grid=(b,),
        in_specs=[spec],
        out_specs=[
            spec,
            spec,
            pl.BlockSpec((1, t, LANES), lambda bi: (bi, 0, 0)),
            pl.BlockSpec((1, e_n, LANES), lambda bi: (bi, 0, 0)),
            pl.BlockSpec((1, SUB, LANES), lambda bi: (bi, 0, 0)),
        ],
        out_shape=[
            jax.ShapeDtypeStruct((b, e_n, t), F32),
            jax.ShapeDtypeStruct((b, e_n, t), F32),
            jax.ShapeDtypeStruct((b, t, LANES), F32),
            jax.ShapeDtypeStruct((b, e_n, LANES), jnp.int32),
            jax.ShapeDtypeStruct((b, SUB, LANES), jnp.int32),
        ],
        scratch_shapes=[pltpu.VMEM((e_n, t), F32)],
        compiler_params=_cparams(("parallel",)),
        name="select",
    )(aff)


def _compact_kernel(bt_ref, pm_ref, dv_ref, aff_ref, src_ref, dst_ref, gate_ref,
                    *, t, cap, c_pad, nseg, batch):
    bi = batch
    e_n = N_EXPERTS
    nblk = t // LANES
    nct = c_pad // LANES
    gate_ref[...] = jnp.zeros(gate_ref.shape, F32)
    lane_e = lax.broadcasted_iota(jnp.int32, (LANES, LANES), 1)
    slot_l = lax.broadcasted_iota(jnp.int32, (1, LANES), 1)

    spr = LANES // nseg
    sh_spr = spr.bit_length() - 1
    sh_grp = (SUB * nseg).bit_length() - 1
    sub_i = lax.broadcasted_iota(jnp.int32, (SUB, LANES), 0)
    lane_i = lax.broadcasted_iota(jnp.int32, (SUB, LANES), 1)
    pick = (lane_i >> sh_spr) == sub_i
    pl_i = lax.broadcasted_iota(jnp.int32, (LANES, LANES), 0)
    pl_j = lax.broadcasted_iota(jnp.int32, (LANES, LANES), 1)
    perm = ((pl_i & (spr - 1)) == ((pl_j >> sh_grp) << 3) + (pl_j & (SUB - 1))).astype(BF16)
    seg_j = (lane_i & (SUB * nseg - 1)) >> 3

    def segment_tile(rows):
        spread = _dot3_right(jnp.where(pick, rows.astype(F32), 0.0), perm).astype(jnp.int32)
        return (((spread >> 3) * nseg + seg_j) << 3) + (spread & (SUB - 1))

    def expert(e):
        base = e * LANES

        def ctile(ct, j0):
            c0 = pl.multiple_of(ct * LANES, LANES)
            j_lo = lax.while_loop(
                lambda j: (j < nblk) & (bt_ref[base + jnp.minimum(j, nblk - 1)] <= c0),
                lambda j: j + 1, j0)
            j_hi = lax.while_loop(
                lambda j: (j < nblk) & (bt_ref[base + jnp.minimum(j, nblk - 1)] < c0 + LANES),
                lambda j: j + 1, j_lo)
            j_end = jnp.minimum(j_hi + 1, nblk)
            want = (lax.broadcasted_iota(jnp.int32, (LANES, 1), 0) + (c0 + 1)).astype(F32)

            def blk(j, acc):
                acc_d, acc_g = acc
                cols = pl.ds(pl.multiple_of(j * LANES, LANES), LANES)
                hit = pm_ref[0, e:e + 1, cols] == want
                acc_d = acc_d + jnp.where(hit, dv_ref[0, e:e + 1, cols], 0.0)
                acc_g = acc_g + jnp.where(hit, aff_ref[0, e:e + 1, cols], 0.0)
                return acc_d, acc_g

            zero = jnp.zeros((LANES, LANES), F32)
            acc_d, acc_g = lax.fori_loop(j_lo, j_end, blk, (zero, zero))
            v = jnp.sum(acc_d.T, axis=0, keepdims=True).astype(jnp.int32)
            tok = v >> 4
            rank = v & (N_EXPERTS - 1)
            p = slot_l + (c0 - cap)
            is_pad = p >= 0
            src = bi * t + jnp.where(is_pad, 0, tok)
            dst = jnp.where(is_pad, N_EXPERTS * t + e * (c_pad - cap) + p, rank * t + tok)
            out_rows = pl.ds(pl.multiple_of((e * nct + ct) * SUB, SUB), SUB)
            src_ref[0, out_rows, :] = segment_tile(src)
            dst_ref[0, out_rows, :] = segment_tile(dst)
            gcol = jnp.sum(acc_g, axis=1, keepdims=True)
            rows = pl.ds(c0, LANES)
            gate_ref[0, rows, :] = jnp.where(lane_e == e, gcol, gate_ref[0, rows, :])
            return j_lo

        lax.fori_loop(0, c_pad // LANES, ctile, 0)

    for e in range(e_n):
        expert(e)


def _compact(btot, pm, dv, aff, cap, c_pad, nseg, batch):
    _, e_n, t = aff.shape
    assert nseg in (1, 2, 4, 8)
    tile_rows = e_n * (c_pad // LANES) * SUB
    spec = pl.BlockSpec((1, e_n, t), lambda i, bt: (0, 0, 0))
    ospec = pl.BlockSpec((1, tile_rows, LANES), lambda i, bt: (0, 0, 0))
    src, dst, gate = pl.pallas_call(
        functools.partial(_compact_kernel, t=t, cap=cap, c_pad=c_pad, nseg=nseg, batch=batch),
        grid_spec=pltpu.PrefetchScalarGridSpec(
            num_scalar_prefetch=1,
            grid=(1,),
            in_specs=[spec, spec, spec],
            out_specs=[ospec, ospec, pl.BlockSpec((1, c_pad, LANES), lambda i, bt: (0, 0, 0))],
        ),
        out_shape=[
            jax.ShapeDtypeStruct((1, tile_rows, LANES), jnp.int32),
            jax.ShapeDtypeStruct((1, tile_rows, LANES), jnp.int32),
            jax.ShapeDtypeStruct((1, c_pad, LANES), F32),
        ],
        compiler_params=_cparams(("arbitrary",)),
        name="compact",
    )(btot.reshape(e_n * LANES), pm, dv, aff)
    ids = lambda a: a.reshape(-1, SUB, LANES)[:, :nseg].reshape(-1)
    return ids(src), ids(dst), gate


def _sc_mesh():
    return plsc.VectorSubcoreMesh(core_axis_name="core", subcore_axis_name="subcore")


def _row_gather(table, idx):
    m = idx.shape[0]
    d = table.shape[1]

    @functools.partial(pl.kernel, out_type=jax.ShapeDtypeStruct((m, d), table.dtype),
                       mesh=_sc_mesh(), scratch_types=[])
    def gather_kernel(x_hbm, i_hbm, o_hbm):
        def body(i_vmem, o_vmem):
            pltpu.sync_copy(x_hbm.at[i_vmem.at[0]], o_vmem)

        pltpu.emit_pipeline(
            body,
            grid=(m // SC_WINDOW,),
            in_specs=[pl.BlockSpec((1, SC_WINDOW), lambda i: (0, i))],
            out_specs=[pl.BlockSpec((SC_WINDOW, d), lambda i: (i, 0))],
            core_axis_name=("core", "subcore"),
            dimension_semantics=(pltpu.PARALLEL,),
        )(i_hbm, o_hbm)

    return gather_kernel(table, idx.reshape(1, m))


def _row_scatter(rows, idx, n_out):
    m, d = rows.shape

    @functools.partial(pl.kernel, out_type=jax.ShapeDtypeStruct((n_out, d), rows.dtype),
                       mesh=_sc_mesh(), scratch_types=[])
    def scatter_kernel(x_hbm, i_hbm, o_hbm):
        def body(x_vmem, i_vmem):
            pltpu.sync_copy(x_vmem, o_hbm.at[i_vmem.at[0]])

        pltpu.emit_pipeline(
            body,
            grid=(m // SC_WINDOW,),
            in_specs=[pl.BlockSpec((SC_WINDOW, d), lambda i: (i, 0)),
                      pl.BlockSpec((1, SC_WINDOW), lambda i: (0, i))],
            out_specs=[],
            core_axis_name=("core", "subcore"),
            dimension_semantics=(pltpu.PARALLEL,),
        )(x_hbm, i_hbm)

    return scatter_kernel(rows, idx.reshape(1, m))


def _expert_kernel(gate_ref, x_ref, wg_ref, wu_ref, wd_ref, y_ref, xb_ref, acc_ref, *, c_pad, n_f):
    e = pl.program_id(0)
    f = pl.program_id(1)

    @pl.when(f == 0)
    def _():
        w = _from_segments(x_ref)
        xb_ref[...] = jnp.concatenate([_unpack_hi(w), _unpack_lo(w)], axis=1).astype(BF16)
        acc_ref[...] = jnp.zeros(acc_ref.shape, F32)

    wg = wg_ref[0].astype(BF16)
    wu = wu_ref[0].astype(BF16)
    wd = wd_ref[0].astype(BF16)
    half = c_pad // 2
    for r0 in (0, half):
        rows = slice(r0, r0 + half)
        x = xb_ref[rows, :]
        hid = _silu(_dot(x, wg)) * _dot(x, wu)
        part = _dot(hid.astype(BF16), wd)
        acc_ref[rows, :] = acc_ref[rows, :] + part

    @pl.when(f == n_f - 1)
    def _():
        lane = lax.broadcasted_iota(jnp.int32, (c_pad, LANES), 1)
        gcol = jnp.sum(jnp.where(lane == e, gate_ref[0], 0.0), axis=1, keepdims=True)
        _to_segments(y_ref, _pack_pairs(acc_ref[...] * gcol))


def _experts(gate, xs, w_gate, w_up, w_down):
    _, c_pad, _ = gate.shape
    e_n, d, d_ff = w_gate.shape
    nseg = d // 2 // LANES
    fc = 256
    n_f = d_ff // fc
    blk = pl.BlockSpec((c_pad // SUB, nseg, SUB, LANES), lambda e, f: (e, 0, 0, 0))
    return pl.pallas_call(
        functools.partial(_expert_kernel, c_pad=c_pad, n_f=n_f),
        grid=(e_n, n_f),
        in_specs=[
            pl.BlockSpec((1, c_pad, LANES), lambda e, f: (0, 0, 0)),
            blk,
            pl.BlockSpec((1, d, fc), lambda e, f: (e, 0, f)),
            pl.BlockSpec((1, d, fc), lambda e, f: (e, 0, f)),
            pl.BlockSpec((1, fc, d), lambda e, f: (e, f, 0)),
        ],
        out_specs=blk,
        out_shape=jax.ShapeDtypeStruct(xs.shape, jnp.int32),
        scratch_shapes=[pltpu.VMEM((c_pad, d), BF16), pltpu.VMEM((c_pad, d), F32)],
        compiler_params=_cparams(("parallel", "arbitrary")),
        name="experts",
    )(gate, xs, w_gate, w_up, w_down)


def _combine_kernel(mc_ref, h1_ref, cnt_ref, fg_ref, *rest, n_layers, nb):
    z_refs, o_ref, acc_ref = rest[:nb], rest[nb], rest[nb + 1]
    bi = pl.program_id(0)
    i = pl.program_id(1)
    j = pl.program_id(2)

    @pl.when(j == 0)
    def _():
        _to_segments(acc_ref, h1_ref[0])

    for k in range(nb):
        @pl.when((bi == k) & (j < mc_ref[bi * LANES + i]))
        def _():
            g, nz = z_refs[k].shape[0], z_refs[k].shape[1]
            take = cnt_ref[0].reshape(g, 1, SUB, LANES) > j.astype(F32)
            w = z_refs[k][...]
            acc_ref[:, 0:nz] = acc_ref[:, 0:nz] + jnp.where(take, _unpack_hi(w), 0.0)
            acc_ref[:, nz:2 * nz] = acc_ref[:, nz:2 * nz] + jnp.where(take, _unpack_lo(w), 0.0)

    @pl.when(j == n_layers - 1)
    def _():
        o_ref[0] = _rms(_from_segments(acc_ref), fg_ref[...])


def _combine(maxcnt, h1, zs, cnt, final_norm_g):
    b, t, d = h1.shape
    n_layers = N_EXPERTS
    tr = _row_tile(t)
    n_tiles = t // tr

    def z_map(k):
        def index(bi, i, j, mc):
            layer = jnp.maximum(jnp.minimum(j, mc[bi * LANES + i] - 1), 0)
            return (jnp.where(bi == k, layer * n_tiles + i, 0), 0, 0, 0)
        return index

    seg_block = (tr // SUB, d // LANES, SUB, LANES)
    z_block = (tr // SUB, d // 2 // LANES, SUB, LANES)
    return pl.pallas_call(
        functools.partial(_combine_kernel, n_layers=n_layers, nb=b),
        grid_spec=pltpu.PrefetchScalarGridSpec(
            num_scalar_prefetch=1,
            grid=(b, n_tiles, n_layers),
            in_specs=[
                pl.BlockSpec((1, tr, d), lambda bi, i, j, mc: (bi, i, 0)),
                pl.BlockSpec((1, tr, LANES), lambda bi, i, j, mc: (bi, i, 0)),
                pl.BlockSpec((1, d), lambda bi, i, j, mc: (0, 0)),
            ] + [pl.BlockSpec(z_block, z_map(k)) for k in range(b)],
            out_specs=pl.BlockSpec((1, tr, d), lambda bi, i, j, mc: (bi, i, 0)),
            scratch_shapes=[pltpu.VMEM(seg_block, F32)],
        ),
        out_shape=jax.ShapeDtypeStruct((b, t, d), F32),
        compiler_params=_cparams(("parallel", "arbitrary", "arbitrary")),
        name="combine",
    )(maxcnt, h1, cnt, final_norm_g.reshape(1, d), *zs)


def kernel(x, meta_tokens, mix_norm_g, w_in, b_gates, conv_w, conv_b, ret_decay_logit, ret_gn_g,
           mlstm_gn_g, w_out, ffn_norm_g, w_router, w_gate, w_up, w_down, final_norm_g):
    b, s, d = x.shape
    t = CHUNK + s
    n = N_META + s
    cap = 2 * n // N_EXPERTS
    c_pad = -(-cap // LANES) * LANES

    meta = jnp.broadcast_to(meta_tokens.astype(x.dtype)[None], (b, N_META, d))
    h_pad = jnp.concatenate([jnp.zeros((b, PAD, d), x.dtype), meta, x], axis=1)

    half = DH // 2
    pos = jnp.arange(t, dtype=F32) - PAD
    inv = ROPE_BASE ** (-jnp.arange(half, dtype=F32) / half)
    ang = pos[:, None] * inv[None, :]
    cosf = jnp.concatenate([jnp.cos(ang), jnp.cos(ang)], axis=1)
    sinf = jnp.concatenate([-jnp.sin(ang), jnp.sin(ang)], axis=1)

    assert mix_norm_g.shape[0] == 1, "single-layer block only"
    rq, rk, rv, rgs, mq, mk, mv, mos, kwf, kwb, gcol, grow = _in_proj(
        h_pad, mix_norm_g[0], w_in[0], b_gates[0], conv_w[0], conv_b[0], cosf, sinf)
    lg = jax.nn.log_sigmoid(ret_decay_logit[0].astype(F32))
    ret = _retention(lg, rq, rk, rv)
    hm = _mlstm(mq, mk, mv, kwf, kwb, gcol, grow)
    h1, u2, aff = _out_proj(h_pad, ret, hm, rgs, mos, ret_gn_g[0], mlstm_gn_g[0], w_out[0],
                            ffn_norm_g[0], w_router[0])
    nseg = d // 2 // LANES
    tiled = lambda a: a.reshape(-1, nseg, SUB, LANES)
    flat = lambda a: a.reshape(-1, LANES)
    z_rows = -(-(N_EXPERTS * t + N_EXPERTS * (c_pad - cap)) // SUB) * SUB
    routed = []
    for bi in range(b):
        aff_b = aff[bi:bi + 1]
        pm, dv, cnt, btot, maxcnt = _select(aff_b, cap)
        src, dst, gate = _compact(btot, pm, dv, aff_b, cap, c_pad, nseg, bi)
        routed.append((_row_gather(flat(u2), src), dst, gate, cnt, maxcnt))
    zs = []
    for xs, dst, gate, _, _ in routed:
        ys = _experts(gate, tiled(xs), w_gate[0], w_up[0], w_down[0])
        zs.append(tiled(_row_scatter(flat(ys), dst, z_rows * nseg)))
    cnt = jnp.concatenate([r[3] for r in routed], axis=0)
    maxcnt = jnp.concatenate([r[4][:, 0, :] for r in routed], axis=0).reshape(-1)
    out = _combine(maxcnt, h1, zs, cnt, final_norm_g)
    return out[:, CHUNK:]
```

```python
import functools

import jax
import jax.numpy as jnp
from jax import lax
from jax.experimental import pallas as pl
from jax.experimental.pallas import tpu as pltpu
from jax.experimental.pallas import tpu_sc as plsc

F32 = jnp.float32
BF16 = jnp.bfloat16

LANES = 128
CHUNK = 128
N_META = 16
PAD = CHUNK - N_META
HEADS = 4
DH = 128
N_EXPERTS = 16
CONV_K = 5
HALO = 8
SUB = 8
SC_WINDOW = 128
SC_SUBCORES = 32
EPS = 1e-6
NEG = -1e30
ROPE_BASE = 10000.0
VMEM_LIMIT = 56 * 1024 * 1024


def _cparams(sem, vmem=VMEM_LIMIT, **kw):
    return pltpu.CompilerParams(dimension_semantics=sem, vmem_limit_bytes=vmem, **kw)


def _dot(a, b):
    return jnp.dot(a, b, preferred_element_type=F32)


def _dot_nt(a, b):
    return lax.dot_general(a, b, (((1,), (1,)), ((), ())), preferred_element_type=F32)


def _dot_tn(a, b):
    return lax.dot_general(a, b, (((0,), (0,)), ((), ())), preferred_element_type=F32)


def _split3(x):
    hi = x.astype(BF16)
    r = x - hi.astype(F32)
    mid = r.astype(BF16)
    lo = (r - mid.astype(F32)).astype(BF16)
    return hi, mid, lo


def _dot3_left(m_bf, x):
    hi, mid, lo = _split3(x)
    return _dot(m_bf, hi) + _dot(m_bf, mid) + _dot(m_bf, lo)


def _dot3_right(x, m_bf):
    hi, mid, lo = _split3(x)
    return _dot(hi, m_bf) + _dot(mid, m_bf) + _dot(lo, m_bf)


def _rms(x, g):
    return x * lax.rsqrt(jnp.mean(x * x, axis=-1, keepdims=True) + EPS) * g


def _log_sigmoid(x):
    return jnp.minimum(x, 0.0) - jnp.log1p(jnp.exp(-jnp.abs(x)))


def _silu(x):
    return x * (1.0 / (1.0 + jnp.exp(-x)))


def _sigmoid(x):
    return 1.0 / (1.0 + jnp.exp(-x))


def _row_tile(t):
    return 640 if t % 640 == 0 else CHUNK


def _chunk_unroll(n_chunks):
    return 13 if n_chunks % 13 == 0 else 1


def _to_segments(dst_ref, x):
    r, w = x.shape
    for j in range(w // LANES):
        dst_ref[:, j] = x[:, j * LANES:(j + 1) * LANES].reshape(r // SUB, SUB, LANES)


def _from_segments(src_ref):
    g, nseg = src_ref.shape[0], src_ref.shape[1]
    return jnp.concatenate([src_ref[:, j].reshape(g * SUB, LANES) for j in range(nseg)], axis=1)


def _bf16_bits(x):
    return pltpu.bitcast(x.astype(BF16).astype(F32), jnp.int32)


def _pack_pairs(x):
    half = x.shape[1] // 2
    return _bf16_bits(x[:, :half]) | lax.shift_right_logical(_bf16_bits(x[:, half:]), 16)


def _unpack_hi(w):
    return pltpu.bitcast(w & -65536, F32)


def _unpack_lo(w):
    return pltpu.bitcast(w << 16, F32)


def _in_proj_kernel(h_ref, hp_ref, hn_ref, ng_ref, wr_ref, wm_ref, wg_ref, wgt_ref,
                    bg_ref, bgt_ref, cw_ref, cb_ref, cos_ref, sin_ref,
                    rq_ref, rk_ref, rv_ref, rg_ref, mq_ref, mk_ref, mv_ref, mo_ref,
                    kwf_ref, kwb_ref, gc_ref, gr_ref, ext_ref, u_ref, gts_ref, *, tr, n_tiles):
    i = pl.program_id(1)
    ng = ng_ref[...]
    w = HEADS * DH
    nd = 2 * HEADS

    up = _rms(hp_ref[0], ng)
    un = _rms(hn_ref[0], ng)
    up = jnp.where(i == 0, 0.0, up)
    un = jnp.where(i == n_tiles - 1, 0.0, un)
    uh = jnp.concatenate([up, un], axis=0).astype(BF16)
    halo = _dot(uh, wm_ref[:, 0:2 * w])
    ext_ref[0:HALO, :] = halo[0:HALO]
    ext_ref[HALO + tr:2 * HALO + tr, :] = halo[HALO:2 * HALO]

    sub = tr // 2
    lane_s = lax.broadcasted_iota(jnp.int32, (sub, LANES), 1)
    for r0 in (0, sub):
        rs = slice(r0, r0 + sub)
        u = _rms(h_ref[0, rs, :], ng).astype(BF16)
        u_ref[rs, :] = u
        real = lax.broadcasted_iota(jnp.int32, (sub, 1), 0) + (i * tr + r0) >= PAD
        cosf = cos_ref[rs, :]
        sinf = sin_ref[rs, :]
        pq = _dot(u, wr_ref[:, 0:w])
        pk = _dot(u, wr_ref[:, w:2 * w])
        for hd in range(HEADS):
            sl = slice(hd * DH, (hd + 1) * DH)
            xq = pq[:, sl]
            xk = pk[:, sl]
            rq_ref[0, rs, sl] = ((xq * cosf + pltpu.roll(xq, DH // 2, 1) * sinf) * (DH ** -0.5)).astype(BF16)
            rk_ref[0, rs, sl] = (xk * cosf + pltpu.roll(xk, DH // 2, 1) * sinf).astype(BF16)
        rv_ref[0, rs, :] = _dot(u, wr_ref[:, 2 * w:3 * w]).astype(BF16)
        rg_ref[0, rs, :] = _silu(_dot(u, wr_ref[:, 3 * w:4 * w])).astype(BF16)
        ext_ref[HALO + r0:HALO + r0 + sub, :] = _dot(u, wm_ref[:, 0:2 * w])
        mv_ref[0, rs, :] = _dot(u, wm_ref[:, 2 * w:3 * w]).astype(BF16)
        mo_ref[0, rs, :] = _sigmoid(_dot(u, wm_ref[:, 3 * w:4 * w])).astype(BF16)
        gts = _dot(u, wg_ref[...]) + bg_ref[...]
        gts_ref[rs, :] = jnp.where(lane_s < nd, jnp.where(real, gts, NEG),
                                   jnp.where(real & (lane_s < 2 * nd), _log_sigmoid(gts), 0.0))

    rown = lax.broadcasted_iota(jnp.int32, (2 * nd, tr), 0)
    coln = lax.broadcasted_iota(jnp.int32, (1, tr), 1) + i * tr
    realr = coln >= PAD
    gtr = _dot_nt(wgt_ref[...], u_ref[...]) + bgt_ref[...]
    gtr = jnp.where(rown < nd, jnp.where(realr, gtr, NEG),
                    jnp.where(realr, _log_sigmoid(gtr), 0.0))
    ci = lax.broadcasted_iota(jnp.int32, (CHUNK, CHUNK), 0)
    cj = lax.broadcasted_iota(jnp.int32, (CHUNK, CHUNK), 1)
    causal = ci >= cj
    anti = ci <= cj
    low = causal.astype(BF16)
    upp = anti.astype(BF16)
    lane_c = lax.broadcasted_iota(jnp.int32, (CHUNK, LANES), 1)
    lane_1 = lax.broadcasted_iota(jnp.int32, (1, LANES), 1)
    rown_c = lax.broadcasted_iota(jnp.int32, (2 * nd, CHUNK), 0)
    for c in range(tr // CHUNK):
        rs = slice(c * CHUNK, (c + 1) * CHUNK)
        realc = lax.broadcasted_iota(jnp.int32, (CHUNK, 1), 0) + (i * tr + c * CHUNK) >= PAD
        conv = cb_ref[...] + jnp.zeros((CHUNK, 2 * w), F32)
        for k in range(CONV_K):
            off = HALO + c * CHUNK + k - CONV_K // 2
            conv = conv + cw_ref[k:k + 1, :] * ext_ref[off:off + CHUNK, :]
        qk = jnp.where(realc, _silu(conv), 0.0)
        mq_ref[0, rs, :] = (qk[:, 0:w] * (DH ** -0.5)).astype(BF16)
        mk_ref[0, rs, :] = qk[:, w:2 * w].astype(BF16)
        kconv = qk[:, w:2 * w]
        g = gts_ref[rs, :]
        lf = jnp.where(lane_c >= nd, g, 0.0)
        pre = _dot3_left(low, lf)
        suf = _dot3_left(upp, lf)
        bc = pltpu.roll(jnp.where(lane_c < nd + HEADS, pre, suf), LANES - nd, 1)
        blast = jnp.where(lane_1 < HEADS, bc[CHUNK - 1:CHUNK], bc[0:1])
        log_u = blast - bc + g
        a = jnp.max(log_u, axis=0, keepdims=True)
        wu = jnp.exp(log_u - a)
        gl = gtr[:, rs]
        lfr = jnp.where(rown_c >= nd, gl, 0.0)
        prer = _dot3_right(lfr, upp)
        sufr = _dot3_right(lfr, low)
        rr = gl[0:nd] - jnp.concatenate([prer[nd:nd + HEADS], sufr[nd + HEADS:2 * nd]], axis=0)
        for hd in range(HEADS):
            hs = slice(hd * DH, (hd + 1) * DH)
            cols = []
            for dr, (msk, kw_ref) in enumerate(((causal, kwf_ref), (anti, kwb_ref))):
                i8 = dr * HEADS + hd
                rmax = jnp.max(jnp.where(msk, rr[i8:i8 + 1, :], NEG), axis=1, keepdims=True)
                cols += [rmax, bc[:, i8:i8 + 1]]
                kw_ref[0, rs, hs] = (kconv[:, hs] * wu[:, i8:i8 + 1]).astype(BF16)
            gc_ref[0, hd, rs, :] = jnp.where(
                lane_c == 0, cols[0], jnp.where(lane_c == 1, cols[1], jnp.where(lane_c == 2, cols[2], cols[3])))
            scal = [jnp.broadcast_to(x[:, j:j + 1], (1, CHUNK))
                    for j in (hd, HEADS + hd) for x in (a, blast)]
            gr_ref[0, hd, :, rs] = jnp.concatenate(
                [rr[hd:hd + 1], rr[HEADS + hd:HEADS + hd + 1]] + scal + [jnp.zeros((2, CHUNK), F32)], axis=0)


def _in_proj(h_pad, norm_g, w_in, b_gates, conv_w, conv_b, cosf, sinf):
    b, t, d = h_pad.shape
    tr = _row_tile(t)
    n_tiles = t // tr
    w = HEADS * DH
    wr = w_in[:, 0:4 * w].astype(BF16)
    wm = w_in[:, 4 * w:8 * w].astype(BF16)
    wg = jnp.pad(w_in[:, 8 * w:], ((0, 0), (0, LANES - 4 * HEADS))).astype(BF16)
    wgt = w_in[:, 8 * w:].T.astype(BF16)
    bg = jnp.pad(b_gates, (0, LANES - 4 * HEADS)).reshape(1, LANES)
    bgt = b_gates.reshape(4 * HEADS, 1)
    hb = tr // HALO
    last_h = t // HALO - 1
    seq = lambda bi, i: (bi, i, 0)
    const2 = lambda bi, i: (0, 0)
    out_bf = jax.ShapeDtypeStruct((b, t, w), BF16)
    kern = functools.partial(_in_proj_kernel, tr=tr, n_tiles=n_tiles)
    return pl.pallas_call(
        kern,
        grid=(b, n_tiles),
        in_specs=[
            pl.BlockSpec((1, tr, d), seq),
            pl.BlockSpec((1, HALO, d), lambda bi, i: (bi, jnp.maximum(i * hb - 1, 0), 0)),
            pl.BlockSpec((1, HALO, d), lambda bi, i: (bi, jnp.minimum((i + 1) * hb, last_h), 0)),
            pl.BlockSpec((1, d), const2),
            pl.BlockSpec((d, 4 * w), const2),
            pl.BlockSpec((d, 4 * w), const2),
            pl.BlockSpec((d, LANES), const2),
            pl.BlockSpec((4 * HEADS, d), const2),
            pl.BlockSpec((1, LANES), const2),
            pl.BlockSpec((4 * HEADS, 1), const2),
            pl.BlockSpec((CONV_K, 2 * w), const2),
            pl.BlockSpec((1, 2 * w), const2),
            pl.BlockSpec((tr, DH), lambda bi, i: (i, 0)),
            pl.BlockSpec((tr, DH), lambda bi, i: (i, 0)),
        ],
        out_specs=[pl.BlockSpec((1, tr, w), seq)] * 10 + [
            pl.BlockSpec((1, HEADS, tr, LANES), lambda bi, i: (bi, 0, i, 0)),
            pl.BlockSpec((1, HEADS, 8, tr), lambda bi, i: (bi, 0, 0, i)),
        ],
        out_shape=[out_bf] * 10 + [
            jax.ShapeDtypeStruct((b, HEADS, t, LANES), F32),
            jax.ShapeDtypeStruct((b, HEADS, 8, t), F32),
        ],
        scratch_shapes=[pltpu.VMEM((tr + 2 * HALO, 2 * w), F32), pltpu.VMEM((tr, d), BF16),
                        pltpu.VMEM((tr, LANES), F32)],
        compiler_params=_cparams(("parallel", "arbitrary")),
        name="in_proj",
    )(h_pad, h_pad, h_pad, norm_g.reshape(1, d), wr, wm, wg, wgt, bg, bgt,
      conv_w, conv_b.reshape(1, 2 * w), cosf, sinf)


def _retention_kernel(lg_ref, q_ref, k_ref, v_ref, o_ref, *, n_chunks):
    hd = pl.program_id(1)
    lgf = lg_ref[0, hd]
    lgb = lg_ref[1, hd]
    li = lax.broadcasted_iota(jnp.int32, (CHUNK, CHUNK), 0)
    mi = lax.broadcasted_iota(jnp.int32, (CHUNK, CHUNK), 1)
    diff = (li - mi).astype(F32)
    decay = (jnp.where(diff >= 0, jnp.exp(lgf * jnp.maximum(diff, 0.0)), 0.0)
             + jnp.where(diff <= 0, jnp.exp(lgb * jnp.maximum(-diff, 0.0)), 0.0))
    lcol = lax.broadcasted_iota(jnp.int32, (CHUNK, 1), 0).astype(F32)
    one = jnp.ones((1, 1), F32)
    zeta_f = jnp.exp(lgf * (CHUNK - 1.0 - lcol))
    xi_f = jnp.exp(lgf * (lcol + 1.0))
    g_f = jnp.exp(lgf * CHUNK * one)
    zeta_b = jnp.exp(lgb * lcol)
    xi_b = jnp.exp(lgb * (CHUNK - lcol))
    g_b = jnp.exp(lgb * CHUNK * one)

    def fwd(c, state):
        rows = pl.ds(pl.multiple_of(c * CHUNK, CHUNK), CHUNK)
        q = q_ref[0, rows, :]
        k = k_ref[0, rows, :]
        v = v_ref[0, rows, :]
        s = _dot_nt(q, k) * decay
        intra = _dot(s.astype(BF16), v)
        cross = _dot(q, state.astype(BF16)) * xi_f
        o_ref[0, rows, :] = intra + cross
        kz = (k.astype(F32) * zeta_f).astype(BF16)
        return g_f * state + _dot_tn(kz, v)

    unroll = _chunk_unroll(n_chunks)
    lax.fori_loop(0, n_chunks, fwd, jnp.zeros((DH, DH), F32), unroll=unroll)

    def bwd(j, state):
        c = n_chunks - 1 - j
        rows = pl.ds(pl.multiple_of(c * CHUNK, CHUNK), CHUNK)
        q = q_ref[0, rows, :]
        k = k_ref[0, rows, :]
        v = v_ref[0, rows, :]
        cross = _dot(q, state.astype(BF16)) * xi_b
        o_ref[0, rows, :] = o_ref[0, rows, :] + cross
        kz = (k.astype(F32) * zeta_b).astype(BF16)
        return g_b * state + _dot_tn(kz, v)

    lax.fori_loop(0, n_chunks, bwd, jnp.zeros((DH, DH), F32), unroll=unroll)


def _retention(lg, rq, rk, rv):
    b, t, _ = rq.shape
    spec = pl.BlockSpec((1, t, DH), lambda bi, hi, lg_ref: (bi, 0, hi))
    return pl.pallas_call(
        functools.partial(_retention_kernel, n_chunks=t // CHUNK),
        grid_spec=pltpu.PrefetchScalarGridSpec(
            num_scalar_prefetch=1,
            grid=(b, HEADS),
            in_specs=[spec, spec, spec],
            out_specs=spec,
        ),
        out_shape=jax.ShapeDtypeStruct((b, t, HEADS * DH), F32),
        compiler_params=_cparams(("parallel", "arbitrary")),
        name="retention",
    )(lg, rq, rk, rv)


def _mlstm_kernel(q_ref, k_ref, v_ref, kwf_ref, kwb_ref, gc_ref, gr_ref, o_ref, *, n_chunks):
    li = lax.broadcasted_iota(jnp.int32, (CHUNK, CHUNK), 0)
    mi = lax.broadcasted_iota(jnp.int32, (CHUNK, CHUNK), 1)
    causal = li >= mi
    anti = li <= mi

    ones = jnp.ones((CHUNK, DH), BF16)

    def step(c, carry, backward):
        st, m_st = carry
        rows = pl.ds(pl.multiple_of(c * CHUNK, CHUNK), CHUNK)
        q = q_ref[0, rows, :]
        k = k_ref[0, rows, :]
        v_aug = jnp.concatenate([v_ref[0, rows, :], ones], axis=1)
        gcol = gc_ref[0, 0, rows, :]
        grow = gr_ref[0, 0, :, rows]
        if backward:
            kw = kwb_ref[0, rows, :]
            r_max, bc = gcol[:, 2:3], gcol[:, 3:4]
            r_row, a, b_last = grow[1:2, :], grow[4:5, 0:1], grow[5:6, 0:1]
            mask = anti
        else:
            kw = kwf_ref[0, rows, :]
            r_max, bc = gcol[:, 0:1], gcol[:, 1:2]
            r_row, a, b_last = grow[0:1, :], grow[2:3, 0:1], grow[3:4, 0:1]
            mask = causal
        mm = jnp.maximum(jnp.broadcast_to(r_max, (CHUNK, DH)), m_st)
        s = _dot_nt(q, k) * jnp.where(mask, jnp.exp(r_row - mm), 0.0)
        w_inter = jnp.exp(m_st - mm)
        w2 = jnp.concatenate([w_inter, w_inter], axis=1)
        tot = _dot(s.astype(BF16), v_aug) + _dot(q, st.astype(BF16)) * w2
        floor = jnp.exp(-(jnp.broadcast_to(bc, (CHUNK, DH)) + mm))
        hval = tot[:, 0:DH] / jnp.maximum(jnp.abs(tot[:, DH:2 * DH]), floor)
        if backward:
            o_ref[0, rows, :] = o_ref[0, rows, :] + hval
        else:
            o_ref[0, rows, :] = hval
        upd = _dot_tn(kw, v_aug)
        m_new = jnp.maximum(b_last + m_st, a)
        f = jnp.exp(b_last + m_st - m_new)
        g = jnp.exp(a - m_new)
        return f * st + g * upd, m_new

    init = (jnp.zeros((DH, 2 * DH), F32), jnp.zeros((1, 1), F32))
    unroll = _chunk_unroll(n_chunks)
    lax.fori_loop(0, n_chunks, lambda c, s: step(c, s, False), init, unroll=unroll)
    lax.fori_loop(0, n_chunks, lambda j, s: step(n_chunks - 1 - j, s, True), init, unroll=unroll)


def _mlstm(mq, mk, mv, kwf, kwb, gcol, grow):
    b, t, _ = mq.shape
    spec = pl.BlockSpec((1, t, DH), lambda bi, hi: (bi, 0, hi))
    return pl.pallas_call(
        functools.partial(_mlstm_kernel, n_chunks=t // CHUNK),
        grid=(b, HEADS),
        in_specs=[spec, spec, spec, spec, spec,
                  pl.BlockSpec((1, 1, t, LANES), lambda bi, hi: (bi, hi, 0, 0)),
                  pl.BlockSpec((1, 1, 8, t), lambda bi, hi: (bi, hi, 0, 0))],
        out_specs=spec,
        out_shape=jax.ShapeDtypeStruct((b, t, HEADS * DH), F32),
        compiler_params=_cparams(("parallel", "arbitrary")),
        name="mlstm",
    )(mq, mk, mv, kwf, kwb, gcol, grow)


def _head_norm(y, g):
    outs = []
    for hd in range(HEADS):
        yh = y[:, hd * DH:(hd + 1) * DH]
        mu = jnp.mean(yh, axis=1, keepdims=True)
        yc = yh - mu
        var = jnp.mean(yc * yc, axis=1, keepdims=True)
        outs.append(yc * lax.rsqrt(var + EPS))
    return jnp.concatenate(outs, axis=1) * g


def _out_proj_kernel(h_ref, ret_ref, hm_ref, rg_ref, mo_ref, rgn_ref, mgn_ref, wo_ref,
                     fg_ref, wrt_ref, h1_ref, u2_ref, aff_ref, *, tr):
    i = pl.program_id(1)
    row = lax.broadcasted_iota(jnp.int32, (tr, 1), 0) + i * tr
    real = row >= PAD
    w = HEADS * DH
    y_ret = _head_norm(ret_ref[0], rgn_ref[...]) * rg_ref[0].astype(F32)
    y_m = _head_norm(mo_ref[0].astype(F32) * hm_ref[0], mgn_ref[...])
    y_ret = jnp.where(real, y_ret, 0.0).astype(BF16)
    y_m = jnp.where(real, y_m, 0.0).astype(BF16)
    h1 = h_ref[0] + _dot(y_ret, wo_ref[0:w, :]) + _dot(y_m, wo_ref[w:2 * w, :])
    h1_ref[0] = h1
    u2 = _rms(h1, fg_ref[...])
    _to_segments(u2_ref, _pack_pairs(u2))
    logits = _dot_nt(wrt_ref[...], u2.astype(BF16))
    mx = jnp.max(logits, axis=0, keepdims=True)
    ex = jnp.exp(logits - mx)
    aff = ex / jnp.sum(ex, axis=0, keepdims=True)
    coln = lax.broadcasted_iota(jnp.int32, (1, tr), 1) + i * tr
    aff_ref[0] = jnp.where(coln >= PAD, aff, -1.0)


def _out_proj(h_pad, ret, hm, rgs, mos, ret_gn_g, mlstm_gn_g, w_out, ffn_norm_g, w_router):
    b, t, d = h_pad.shape
    tr = _row_tile(t)
    n_tiles = t // tr
    nseg = d // 2 // LANES
    w = HEADS * DH
    seq = lambda bi, i: (bi, i, 0)
    const2 = lambda bi, i: (0, 0)
    return pl.pallas_call(
        functools.partial(_out_proj_kernel, tr=tr),
        grid=(b, n_tiles),
        in_specs=[
            pl.BlockSpec((1, tr, d), seq),
            pl.BlockSpec((1, tr, w), seq),
            pl.BlockSpec((1, tr, w), seq),
            pl.BlockSpec((1, tr, w), seq),
            pl.BlockSpec((1, tr, w), seq),
            pl.BlockSpec((1, w), const2),
            pl.BlockSpec((1, w), const2),
            pl.BlockSpec((2 * w, d), const2),
            pl.BlockSpec((1, d), const2),
            pl.BlockSpec((N_EXPERTS, d), const2),
        ],
        out_specs=[
            pl.BlockSpec((1, tr, d), seq),
            pl.BlockSpec((tr // SUB, nseg, SUB, LANES), lambda bi, i: (bi * n_tiles + i, 0, 0, 0)),
            pl.BlockSpec((1, N_EXPERTS, tr), lambda bi, i: (bi, 0, i)),
        ],
        out_shape=[
            jax.ShapeDtypeStruct((b, t, d), F32),
            jax.ShapeDtypeStruct((b * t // SUB, nseg, SUB, LANES), jnp.int32),
            jax.ShapeDtypeStruct((b, N_EXPERTS, t), F32),
        ],
        compiler_params=_cparams(("parallel", "arbitrary")),
        name="out_proj",
    )(h_pad, ret, hm, rgs, mos, ret_gn_g.reshape(1, w), mlstm_gn_g.reshape(1, w),
      w_out.astype(BF16), ffn_norm_g.reshape(1, d), w_router.T.astype(BF16))


def _select_kernel(aff_ref, pm_ref, dv_ref, cnt_ref, bt_ref, mc_ref, sel_ref, *, t, cap, tr):
    e_n = N_EXPERTS
    nblk = t // LANES
    bits = pltpu.bitcast(aff_ref[0], jnp.int32)
    capf = float(cap)

    def search(_, lohi):
        lo, hi = lohi
        mid = lo + ((hi - lo + 1) >> 1)
        cnt = jnp.sum((bits >= mid).astype(F32), axis=1, keepdims=True)
        ok = cnt >= capf
        return jnp.where(ok, mid, lo), jnp.where(ok, hi, mid - 1)

    lo0 = jnp.zeros((e_n, 1), jnp.int32)
    hi0 = jnp.full((e_n, 1), 0x3F800000, jnp.int32)
    thr, _ = lax.fori_loop(0, 31, search, (lo0, hi0))
    need = capf - jnp.sum((bits > thr).astype(F32), axis=1, keepdims=True)

    ci = lax.broadcasted_iota(jnp.int32, (LANES, LANES), 0)
    cj = lax.broadcasted_iota(jnp.int32, (LANES, LANES), 1)
    upp = (ci <= cj).astype(BF16)
    ei = lax.broadcasted_iota(jnp.int32, (e_n, e_n), 0)
    ej = lax.broadcasted_iota(jnp.int32, (e_n, e_n), 1)
    strict = (ej < ei).astype(BF16)

    carry = jnp.zeros((e_n, 1), F32)
    for j in range(nblk):
        sl = slice(j * LANES, (j + 1) * LANES)
        bb = bits[:, sl]
        eqf = (bb == thr).astype(F32)
        ceq = _dot(eqf.astype(BF16), upp) + carry
        carry = ceq[:, LANES - 1:LANES]
        sel_ref[:, sl] = jnp.where(bb > thr, 1.0, jnp.where(ceq <= need, eqf, 0.0))

    carry = jnp.zeros((e_n, 1), F32)
    tok = lax.broadcasted_iota(jnp.int32, (1, LANES), 1).astype(F32)
    lane_b = lax.broadcasted_iota(jnp.int32, (e_n, LANES), 1)
    btot = jnp.zeros((e_n, LANES), F32)
    tile_max = jnp.zeros((SUB, LANES), F32)
    lane_t = lax.broadcasted_iota(jnp.int32, (SUB, LANES), 1)
    for j in range(nblk):
        sl = slice(j * LANES, (j + 1) * LANES)
        selb = sel_ref[:, sl]
        selbf = selb.astype(BF16)
        pin = _dot(selbf, upp) + carry
        carry = pin[:, LANES - 1:LANES]
        btot = jnp.where(lane_b == j, carry, btot)
        pm_ref[0, :, sl] = jnp.where(selb > 0, pin, 0.0)
        rank = _dot(strict, selbf)
        dv_ref[0, :, sl] = (tok + float(j * LANES)) * float(N_EXPERTS) + rank
        cnt_row = jnp.sum(selb, axis=0, keepdims=True)
        cnt_ref[0, sl, :] = jnp.broadcast_to(cnt_row, (LANES, LANES)).T
        tile_max = jnp.where(lane_t == (j * LANES) // tr,
                             jnp.maximum(tile_max, jnp.max(cnt_row, axis=1, keepdims=True)), tile_max)
    bt_ref[0] = jnp.where(lane_b < nblk, btot, carry).astype(jnp.int32)
    mc_ref[0] = tile_max.astype(jnp.int32)


def _select(aff, cap):
    b, e_n, t = aff.shape
    tr = _row_tile(t)
    assert t // tr <= LANES
    spec = pl.BlockSpec((1, e_n, t), lambda bi: (bi, 0, 0))
    return pl.pallas_call(
        functools.partial(_select_kernel, t=t, cap=cap, tr=tr),
        grid=(b,),
        in_specs=[spec],
        out_specs=[
            spec,
            spec,
            pl.BlockSpec((1, t, LANES), lambda bi: (bi, 0, 0)),
            pl.BlockSpec((1, e_n, LANES), lambda bi: (bi, 0, 0)),
            pl.BlockSpec((1, SUB, LANES), lambda bi: (bi, 0, 0)),
        ],
        out_shape=[
            jax.ShapeDtypeStruct((b, e_n, t), F32),
            jax.ShapeDtypeStruct((b, e_n, t), F32),
            jax.ShapeDtypeStruct((b, t, LANES), F32),
            jax.ShapeDtypeStruct((b, e_n, LANES), jnp.int32),
            jax.ShapeDtypeStruct((b, SUB, LANES), jnp.int32),
        ],
        scratch_shapes=[pltpu.VMEM((e_n, t), F32)],
        compiler_params=_cparams(("parallel",)),
        name="select",
    )(aff)


def _compact_kernel(bt_ref, pm_ref, dv_ref, aff_ref, src_ref, dst_ref, gate_ref,
                    *, t, cap, c_pad, nct, nseg, batch):
    bi = batch
    e_n = N_EXPERTS
    nblk = t // LANES
    gate_ref[...] = jnp.zeros(gate_ref.shape, F32)
    lane_e = lax.broadcasted_iota(jnp.int32, (LANES, LANES), 1)
    slot_l = lax.broadcasted_iota(jnp.int32, (1, LANES), 1)

    spr = LANES // nseg
    sh_spr = spr.bit_length() - 1
    sh_grp = (SUB * nseg).bit_length() - 1
    sub_i = lax.broadcasted_iota(jnp.int32, (SUB, LANES), 0)
    lane_i = lax.broadcasted_iota(jnp.int32, (SUB, LANES), 1)
    pick = (lane_i >> sh_spr) == sub_i
    pl_i = lax.broadcasted_iota(jnp.int32, (LANES, LANES), 0)
    pl_j = lax.broadcasted_iota(jnp.int32, (LANES, LANES), 1)
    perm = ((pl_i & (spr - 1)) == ((pl_j >> sh_grp) << 3) + (pl_j & (SUB - 1))).astype(BF16)
    seg_j = (lane_i & (SUB * nseg - 1)) >> 3

    def segment_tile(rows):
        spread = _dot3_right(jnp.where(pick, rows.astype(F32), 0.0), perm).astype(jnp.int32)
        return (((spread >> 3) * nseg + seg_j) << 3) + (spread & (SUB - 1))

    def expert(e):
        base = e * LANES

        def ctile(ct, j0):
            c0 = pl.multiple_of(ct * LANES, LANES)
            j_lo = lax.while_loop(
                lambda j: (j < nblk) & (bt_ref[base + jnp.minimum(j, nblk - 1)] <= c0),
                lambda j: j + 1, j0)
            j_hi = lax.while_loop(
                lambda j: (j < nblk) & (bt_ref[base + jnp.minimum(j, nblk - 1)] < c0 + LANES),
                lambda j: j + 1, j_lo)
            j_end = jnp.minimum(j_hi + 1, nblk)
            want = (lax.broadcasted_iota(jnp.int32, (LANES, 1), 0) + (c0 + 1)).astype(F32)

            def blk(j, acc):
                acc_d, acc_g = acc
                cols = pl.ds(pl.multiple_of(j * LANES, LANES), LANES)
                hit = pm_ref[0, e:e + 1, cols] == want
                acc_d = acc_d + jnp.where(hit, dv_ref[0, e:e + 1, cols], 0.0)
                acc_g = acc_g + jnp.where(hit, aff_ref[0, e:e + 1, cols], 0.0)
                return acc_d, acc_g

            zero = jnp.zeros((LANES, LANES), F32)
            acc_d, acc_g = lax.fori_loop(j_lo, j_end, blk, (zero, zero))
            v = jnp.sum(acc_d.T, axis=0, keepdims=True).astype(jnp.int32)
            tok = v >> 4
            rank = v & (N_EXPERTS - 1)
            p = slot_l + (c0 - cap)
            is_pad = p >= 0
            src = bi * t + jnp.where(is_pad, 0, tok)
            dst = jnp.where(is_pad, N_EXPERTS * t + e * (c_pad - cap) + p, rank * t + tok)
            out_rows = pl.ds(pl.multiple_of((e * nct + ct) * SUB, SUB), SUB)
            src_ref[0, out_rows, :] = segment_tile(src)
            dst_ref[0, out_rows, :] = segment_tile(dst)
            gcol = jnp.sum(acc_g, axis=1, keepdims=True)
            rows = pl.ds(c0, LANES)
            gate_ref[0, rows, :] = jnp.where(lane_e == e, gcol, gate_ref[0, rows, :])
            return j_lo

        lax.fori_loop(0, nct, ctile, 0)

    for e in range(e_n):
        expert(e)


def _compact(btot, pm, dv, aff, cap, c_pad, nseg, batch):
    _, e_n, t = aff.shape
    assert nseg in (1, 2, 4, 8)
    nct = -(-c_pad // LANES)
    tile_rows = e_n * nct * SUB
    spec = pl.BlockSpec((1, e_n, t), lambda i, bt: (0, 0, 0))
    ospec = pl.BlockSpec((1, tile_rows, LANES), lambda i, bt: (0, 0, 0))
    src, dst, gate = pl.pallas_call(
        functools.partial(_compact_kernel, t=t, cap=cap, c_pad=c_pad, nct=nct, nseg=nseg, batch=batch),
        grid_spec=pltpu.PrefetchScalarGridSpec(
            num_scalar_prefetch=1,
            grid=(1,),
            in_specs=[spec, spec, spec],
            out_specs=[ospec, ospec, pl.BlockSpec((1, nct * LANES, LANES), lambda i, bt: (0, 0, 0))],
        ),
        out_shape=[
            jax.ShapeDtypeStruct((1, tile_rows, LANES), jnp.int32),
            jax.ShapeDtypeStruct((1, tile_rows, LANES), jnp.int32),
            jax.ShapeDtypeStruct((1, nct * LANES, LANES), F32),
        ],
        compiler_params=_cparams(("arbitrary",)),
        name="compact",
    )(btot.reshape(e_n * LANES), pm, dv, aff)
    ids = lambda a: a.reshape(e_n, nct, SUB, LANES)[:, :, :nseg].reshape(e_n, -1)[:, :c_pad * nseg].reshape(-1)
    return ids(src), ids(dst), gate


def _sc_mesh():
    return plsc.VectorSubcoreMesh(core_axis_name="core", subcore_axis_name="subcore")


def _row_gather(table, idx):
    m = idx.shape[0]
    d = table.shape[1]

    @functools.partial(pl.kernel, out_type=jax.ShapeDtypeStruct((m, d), table.dtype),
                       mesh=_sc_mesh(), scratch_types=[])
    def gather_kernel(x_hbm, i_hbm, o_hbm):
        def body(i_vmem, o_vmem):
            pltpu.sync_copy(x_hbm.at[i_vmem.at[0]], o_vmem)

        pltpu.emit_pipeline(
            body,
            grid=(m // SC_WINDOW,),
            in_specs=[pl.BlockSpec((1, SC_WINDOW), lambda i: (0, i))],
            out_specs=[pl.BlockSpec((SC_WINDOW, d), lambda i: (i, 0))],
            core_axis_name=("core", "subcore"),
            dimension_semantics=(pltpu.PARALLEL,),
        )(i_hbm, o_hbm)

    return gather_kernel(table, idx.reshape(1, m))


def _row_scatter(rows, idx, n_out):
    m, d = rows.shape

    @functools.partial(pl.kernel, out_type=jax.ShapeDtypeStruct((n_out, d), rows.dtype),
                       mesh=_sc_mesh(), scratch_types=[])
    def scatter_kernel(x_hbm, i_hbm, o_hbm):
        def body(x_vmem, i_vmem):
            pltpu.sync_copy(x_vmem, o_hbm.at[i_vmem.at[0]])

        pltpu.emit_pipeline(
            body,
            grid=(m // SC_WINDOW,),
            in_specs=[pl.BlockSpec((SC_WINDOW, d), lambda i: (i, 0)),
                      pl.BlockSpec((1, SC_WINDOW), lambda i: (0, i))],
            out_specs=[],
            core_axis_name=("core", "subcore"),
            dimension_semantics=(pltpu.PARALLEL,),
        )(x_hbm, i_hbm)

    return scatter_kernel(rows, idx.reshape(1, m))


def _expert_kernel(gate_ref, x_ref, wg_ref, wu_ref, wd_ref, y_ref, xb_ref, acc_ref, *, c_pad, n_f):
    e = pl.program_id(0)
    f = pl.program_id(1)

    @pl.when(f == 0)
    def _():
        w = _from_segments(x_ref)
        xb_ref[...] = jnp.concatenate([_unpack_hi(w), _unpack_lo(w)], axis=1).astype(BF16)
        acc_ref[...] = jnp.zeros(acc_ref.shape, F32)

    wg = wg_ref[0].astype(BF16)
    wu = wu_ref[0].astype(BF16)
    wd = wd_ref[0].astype(BF16)
    half = c_pad // 2
    for r0 in (0, half):
        rows = slice(r0, r0 + half)
        x = xb_ref[rows, :]
        hid = _silu(_dot(x, wg)) * _dot(x, wu)
        part = _dot(hid.astype(BF16), wd)
        acc_ref[rows, :] = acc_ref[rows, :] + part

    @pl.when(f == n_f - 1)
    def _():
        lane = lax.broadcasted_iota(jnp.int32, (c_pad, LANES), 1)
        gcol = jnp.sum(jnp.where(lane == e, gate_ref[0], 0.0), axis=1, keepdims=True)
        _to_segments(y_ref, _pack_pairs(acc_ref[...] * gcol))


def _experts(gate, xs, w_gate, w_up, w_down):
    e_n, d, d_ff = w_gate.shape
    nseg = d // 2 // LANES
    c_pad = xs.shape[0] * SUB // e_n
    fc = 256
    n_f = d_ff // fc
    blk = pl.BlockSpec((c_pad // SUB, nseg, SUB, LANES), lambda e, f: (e, 0, 0, 0))
    return pl.pallas_call(
        functools.partial(_expert_kernel, c_pad=c_pad, n_f=n_f),
        grid=(e_n, n_f),
        in_specs=[
            pl.BlockSpec((1, c_pad, LANES), lambda e, f: (0, 0, 0)),
            blk,
            pl.BlockSpec((1, d, fc), lambda e, f: (e, 0, f)),
            pl.BlockSpec((1, d, fc), lambda e, f: (e, 0, f)),
            pl.BlockSpec((1, fc, d), lambda e, f: (e, f, 0)),
        ],
        out_specs=blk,
        out_shape=jax.ShapeDtypeStruct(xs.shape, jnp.int32),
        scratch_shapes=[pltpu.VMEM((c_pad, d), BF16), pltpu.VMEM((c_pad, d), F32)],
        compiler_params=_cparams(("parallel", "arbitrary")),
        name="experts",
    )(gate, xs, w_gate, w_up, w_down)


LAYERS_PER_STEP = 2


def _combine_kernel(mc_ref, h1_ref, fg_ref, *rest, n_steps, nb):
    cnt_refs = rest[:nb]
    z_refs = rest[nb:nb + nb * LAYERS_PER_STEP]
    o_ref, acc_ref = rest[-2], rest[-1]
    bi = pl.program_id(0)
    i = pl.program_id(1)
    j = pl.program_id(2)

    @pl.when(j == 0)
    def _():
        _to_segments(acc_ref, h1_ref[0])

    for k in range(nb):
        for l in range(LAYERS_PER_STEP):
            layer = j * LAYERS_PER_STEP + l

            @pl.when((bi == k) & (layer < mc_ref[bi * LANES + i]))
            def _():
                z_ref = z_refs[k * LAYERS_PER_STEP + l]
                g, nz = z_ref.shape[0], z_ref.shape[1]
                take = cnt_refs[k][0].reshape(g, 1, SUB, LANES) > layer.astype(F32)
                w = z_ref[...]
                acc_ref[:, 0:nz] = acc_ref[:, 0:nz] + jnp.where(take, _unpack_hi(w), 0.0)
                acc_ref[:, nz:2 * nz] = acc_ref[:, nz:2 * nz] + jnp.where(take, _unpack_lo(w), 0.0)

    @pl.when(j == n_steps - 1)
    def _():
        o_ref[0] = _rms(_from_segments(acc_ref), fg_ref[...])


def _combine(maxcnt, h1, zs, cnts, final_norm_g):
    b, t, d = h1.shape
    n_steps = N_EXPERTS // LAYERS_PER_STEP
    tr = _row_tile(t)
    n_tiles = t // tr

    def z_map(k, l):
        def index(bi, i, j, mc):
            layer = jnp.maximum(jnp.minimum(j * LAYERS_PER_STEP + l, mc[bi * LANES + i] - 1), 0)
            return (jnp.where(bi == k, layer * n_tiles + i, 0), 0, 0, 0)
        return index

    def cnt_map(k):
        return lambda bi, i, j, mc: (0, jnp.where(bi == k, i, 0), 0)

    seg_block = (tr // SUB, d // LANES, SUB, LANES)
    z_block = (tr // SUB, d // 2 // LANES, SUB, LANES)
    z_specs = [pl.BlockSpec(z_block, z_map(k, l)) for k in range(b) for l in range(LAYERS_PER_STEP)]
    z_args = [zs[k] for k in range(b) for _ in range(LAYERS_PER_STEP)]
    return pl.pallas_call(
        functools.partial(_combine_kernel, n_steps=n_steps, nb=b),
        grid_spec=pltpu.PrefetchScalarGridSpec(
            num_scalar_prefetch=1,
            grid=(b, n_tiles, n_steps),
            in_specs=[
                pl.BlockSpec((1, tr, d), lambda bi, i, j, mc: (bi, i, 0)),
                pl.BlockSpec((1, d), lambda bi, i, j, mc: (0, 0)),
            ] + [pl.BlockSpec((1, tr, LANES), cnt_map(k)) for k in range(b)] + z_specs,
            out_specs=pl.BlockSpec((1, tr, d), lambda bi, i, j, mc: (bi, i, 0)),
            scratch_shapes=[pltpu.VMEM(seg_block, F32)],
        ),
        out_shape=jax.ShapeDtypeStruct((b, t, d), F32),
        compiler_params=_cparams(("parallel", "arbitrary", "arbitrary")),
        name="combine",
    )(maxcnt, h1, final_norm_g.reshape(1, d), *cnts, *z_args)


def kernel(x, meta_tokens, mix_norm_g, w_in, b_gates, conv_w, conv_b, ret_decay_logit, ret_gn_g,
           mlstm_gn_g, w_out, ffn_norm_g, w_router, w_gate, w_up, w_down, final_norm_g):
    b, s, d = x.shape
    t = CHUNK + s
    n = N_META + s
    cap = 2 * n // N_EXPERTS
    nseg = d // 2 // LANES
    c_pad = next(c for c in range(-(-cap // 16) * 16, cap + 4096, 16)
                 if (N_EXPERTS * c * nseg) % (SC_WINDOW * SC_SUBCORES) == 0)

    meta = jnp.broadcast_to(meta_tokens.astype(x.dtype)[None], (b, N_META, d))
    h_pad = jnp.concatenate([jnp.zeros((b, PAD, d), x.dtype), meta, x], axis=1)

    half = DH // 2
    pos = jnp.arange(t, dtype=F32) - PAD
    inv = ROPE_BASE ** (-jnp.arange(half, dtype=F32) / half)
    ang = pos[:, None] * inv[None, :]
    cosf = jnp.concatenate([jnp.cos(ang), jnp.cos(ang)], axis=1)
    sinf = jnp.concatenate([-jnp.sin(ang), jnp.sin(ang)], axis=1)

    assert mix_norm_g.shape[0] == 1, "single-layer block only"
    rq, rk, rv, rgs, mq, mk, mv, mos, kwf, kwb, gcol, grow = _in_proj(
        h_pad, mix_norm_g[0], w_in[0], b_gates[0], conv_w[0], conv_b[0], cosf, sinf)
    lg = jax.nn.log_sigmoid(ret_decay_logit[0].astype(F32))
    ret = _retention(lg, rq, rk, rv)
    hm = _mlstm(mq, mk, mv, kwf, kwb, gcol, grow)
    h1, u2, aff = _out_proj(h_pad, ret, hm, rgs, mos, ret_gn_g[0], mlstm_gn_g[0], w_out[0],
                            ffn_norm_g[0], w_router[0])
    tiled = lambda a: a.reshape(-1, nseg, SUB, LANES)
    flat = lambda a: a.reshape(-1, LANES)
    z_rows = -(-(N_EXPERTS * t + N_EXPERTS * (c_pad - cap)) // SUB) * SUB
    routed = []
    for bi in range(b):
        aff_b = aff[bi:bi + 1]
        pm, dv, cnt, btot, maxcnt = _select(aff_b, cap)
        src, dst, gate = _compact(btot, pm, dv, aff_b, cap, c_pad, nseg, bi)
        routed.append((_row_gather(flat(u2), src), dst, gate, cnt, maxcnt))
    zs = []
    for xs, dst, gate, _, _ in routed:
        ys = _experts(gate, tiled(xs), w_gate[0], w_up[0], w_down[0])
        zs.append(tiled(_row_scatter(flat(ys), dst, z_rows * nseg)))
    maxcnt = jnp.concatenate([r[4][:, 0, :] for r in routed], axis=0).reshape(-1)
    out = _combine(maxcnt, h1, zs, [r[3] for r in routed], final_norm_g)
    return out[:, CHUNK:]
```

```python
import functools

import jax
import jax.numpy as jnp
from jax import lax
from jax.experimental import pallas as pl
from jax.experimental.pallas import tpu as pltpu
from jax.experimental.pallas import tpu_sc as plsc

F32 = jnp.float32
BF16 = jnp.bfloat16

LANES = 128
CHUNK = 128
N_META = 16
PAD = CHUNK - N_META
HEADS = 4
DH = 128
N_EXPERTS = 16
CONV_K = 5
HALO = 8
SUB = 8
SC_WINDOW = 128
SC_SUBCORES = 32
EPS = 1e-6
NEG = -1e30
ROPE_BASE = 10000.0
VMEM_LIMIT = 56 * 1024 * 1024


def _cparams(sem, vmem=VMEM_LIMIT, **kw):
    return pltpu.CompilerParams(dimension_semantics=sem, vmem_limit_bytes=vmem, **kw)


def _dot(a, b):
    return jnp.dot(a, b, preferred_element_type=F32)


def _dot_nt(a, b):
    return lax.dot_general(a, b, (((1,), (1,)), ((), ())), preferred_element_type=F32)


def _dot_tn(a, b):
    return lax.dot_general(a, b, (((0,), (0,)), ((), ())), preferred_element_type=F32)


def _split3(x):
    hi = x.astype(BF16)
    r = x - hi.astype(F32)
    mid = r.astype(BF16)
    lo = (r - mid.astype(F32)).astype(BF16)
    return hi, mid, lo


def _dot3_left(m_bf, x):
    hi, mid, lo = _split3(x)
    return _dot(m_bf, hi) + _dot(m_bf, mid) + _dot(m_bf, lo)


def _dot3_right(x, m_bf):
    hi, mid, lo = _split3(x)
    return _dot(hi, m_bf) + _dot(mid, m_bf) + _dot(lo, m_bf)


def _rms(x, g):
    return x * lax.rsqrt(jnp.mean(x * x, axis=-1, keepdims=True) + EPS) * g


def _log_sigmoid(x):
    return jnp.minimum(x, 0.0) - jnp.log1p(jnp.exp(-jnp.abs(x)))


def _silu(x):
    return x * (1.0 / (1.0 + jnp.exp(-x)))


def _sigmoid(x):
    return 1.0 / (1.0 + jnp.exp(-x))


def _row_tile(t):
    return 640 if t % 640 == 0 else CHUNK


def _chunk_unroll(n_chunks):
    return 13 if n_chunks % 13 == 0 else 1


def _to_segments(dst_ref, x):
    r, w = x.shape
    for j in range(w // LANES):
        dst_ref[:, j] = x[:, j * LANES:(j + 1) * LANES].reshape(r // SUB, SUB, LANES)


def _from_segments(src_ref):
    g, nseg = src_ref.shape[0], src_ref.shape[1]
    return jnp.concatenate([src_ref[:, j].reshape(g * SUB, LANES) for j in range(nseg)], axis=1)


def _bf16_bits(x):
    return pltpu.bitcast(x.astype(BF16).astype(F32), jnp.int32)


def _pack_pairs(x):
    half = x.shape[1] // 2
    return _bf16_bits(x[:, :half]) | lax.shift_right_logical(_bf16_bits(x[:, half:]), 16)


def _unpack_hi(w):
    return pltpu.bitcast(w & -65536, F32)


def _unpack_lo(w):
    return pltpu.bitcast(w << 16, F32)


def _load_rows(x_refs, head_ref, i):
    blocks = [r[0] for r in x_refs]
    blocks[0] = jnp.where(i == 0, head_ref[...], blocks[0])
    return blocks


def _x_specs(tr, d):
    nb = tr // CHUNK
    return [pl.BlockSpec((1, CHUNK, d), functools.partial(
        lambda bi, i, k: (bi, jnp.maximum(i * nb - 1 + k, 0), 0), k=k)) for k in range(nb)]


def _in_proj_kernel(*refs, tr, n_tiles):
    nb = tr // CHUNK
    x_refs, refs = refs[:nb], refs[nb:]
    (head_ref, hp_ref, hn_ref, ng_ref, wr_ref, wm_ref, wg_ref, wgt_ref,
     bg_ref, bgt_ref, cw_ref, cb_ref, cos_ref, sin_ref,
     rq_ref, rk_ref, rv_ref, rg_ref, mq_ref, mk_ref, mv_ref, mo_ref,
     kwf_ref, kwb_ref, gc_ref, gr_ref, ext_ref, u_ref, gts_ref, h_ref) = refs
    i = pl.program_id(1)
    ng = ng_ref[...]
    w = HEADS * DH
    nd = 2 * HEADS
    for k, blk in enumerate(_load_rows(x_refs, head_ref, i)):
        h_ref[k * CHUNK:(k + 1) * CHUNK, :] = blk

    hp = hp_ref[0]
    if tr == CHUNK:
        hp = jnp.where(i == 1, head_ref[CHUNK - HALO:CHUNK, :], hp)
    up = _rms(hp, ng)
    un = _rms(hn_ref[0], ng)
    up = jnp.where(i == 0, 0.0, up)
    un = jnp.where(i == n_tiles - 1, 0.0, un)
    uh = jnp.concatenate([up, un], axis=0).astype(BF16)
    halo = _dot(uh, wm_ref[:, 0:2 * w])
    ext_ref[0:HALO, :] = halo[0:HALO]
    ext_ref[HALO + tr:2 * HALO + tr, :] = halo[HALO:2 * HALO]

    sub = tr // 2
    lane_s = lax.broadcasted_iota(jnp.int32, (sub, LANES), 1)
    for r0 in (0, sub):
        rs = slice(r0, r0 + sub)
        u = _rms(h_ref[rs, :], ng).astype(BF16)
        u_ref[rs, :] = u
        real = lax.broadcasted_iota(jnp.int32, (sub, 1), 0) + (i * tr + r0) >= PAD
        cosf = cos_ref[rs, :]
        sinf = sin_ref[rs, :]
        pq = _dot(u, wr_ref[:, 0:w])
        pk = _dot(u, wr_ref[:, w:2 * w])
        for hd in range(HEADS):
            sl = slice(hd * DH, (hd + 1) * DH)
            xq = pq[:, sl]
            xk = pk[:, sl]
            rq_ref[0, rs, sl] = ((xq * cosf + pltpu.roll(xq, DH // 2, 1) * sinf) * (DH ** -0.5)).astype(BF16)
            rk_ref[0, rs, sl] = (xk * cosf + pltpu.roll(xk, DH // 2, 1) * sinf).astype(BF16)
        rv_ref[0, rs, :] = _dot(u, wr_ref[:, 2 * w:3 * w]).astype(BF16)
        rg_ref[0, rs, :] = _silu(_dot(u, wr_ref[:, 3 * w:4 * w])).astype(BF16)
        ext_ref[HALO + r0:HALO + r0 + sub, :] = _dot(u, wm_ref[:, 0:2 * w])
        mv_ref[0, rs, :] = _dot(u, wm_ref[:, 2 * w:3 * w]).astype(BF16)
        mo_ref[0, rs, :] = _sigmoid(_dot(u, wm_ref[:, 3 * w:4 * w])).astype(BF16)
        gts = _dot(u, wg_ref[...]) + bg_ref[...]
        gts_ref[rs, :] = jnp.where(lane_s < nd, jnp.where(real, gts, NEG),
                                   jnp.where(real & (lane_s < 2 * nd), _log_sigmoid(gts), 0.0))

    rown = lax.broadcasted_iota(jnp.int32, (2 * nd, tr), 0)
    coln = lax.broadcasted_iota(jnp.int32, (1, tr), 1) + i * tr
    realr = coln >= PAD
    gtr = _dot_nt(wgt_ref[...], u_ref[...]) + bgt_ref[...]
    gtr = jnp.where(rown < nd, jnp.where(realr, gtr, NEG),
                    jnp.where(realr, _log_sigmoid(gtr), 0.0))
    ci = lax.broadcasted_iota(jnp.int32, (CHUNK, CHUNK), 0)
    cj = lax.broadcasted_iota(jnp.int32, (CHUNK, CHUNK), 1)
    causal = ci >= cj
    anti = ci <= cj
    low = causal.astype(BF16)
    upp = anti.astype(BF16)
    lane_c = lax.broadcasted_iota(jnp.int32, (CHUNK, LANES), 1)
    lane_1 = lax.broadcasted_iota(jnp.int32, (1, LANES), 1)
    rown_c = lax.broadcasted_iota(jnp.int32, (2 * nd, CHUNK), 0)
    for c in range(tr // CHUNK):
        rs = slice(c * CHUNK, (c + 1) * CHUNK)
        realc = lax.broadcasted_iota(jnp.int32, (CHUNK, 1), 0) + (i * tr + c * CHUNK) >= PAD
        conv = cb_ref[...] + jnp.zeros((CHUNK, 2 * w), F32)
        for k in range(CONV_K):
            off = HALO + c * CHUNK + k - CONV_K // 2
            conv = conv + cw_ref[k:k + 1, :] * ext_ref[off:off + CHUNK, :]
        qk = jnp.where(realc, _silu(conv), 0.0)
        mq_ref[0, rs, :] = (qk[:, 0:w] * (DH ** -0.5)).astype(BF16)
        mk_ref[0, rs, :] = qk[:, w:2 * w].astype(BF16)
        kconv = qk[:, w:2 * w]
        g = gts_ref[rs, :]
        lf = jnp.where(lane_c >= nd, g, 0.0)
        pre = _dot3_left(low, lf)
        suf = _dot3_left(upp, lf)
        bc = pltpu.roll(jnp.where(lane_c < nd + HEADS, pre, suf), LANES - nd, 1)
        blast = jnp.where(lane_1 < HEADS, bc[CHUNK - 1:CHUNK], bc[0:1])
        log_u = blast - bc + g
        a = jnp.max(log_u, axis=0, keepdims=True)
        wu = jnp.exp(log_u - a)
        gl = gtr[:, rs]
        lfr = jnp.where(rown_c >= nd, gl, 0.0)
        prer = _dot3_right(lfr, upp)
        sufr = _dot3_right(lfr, low)
        rr = gl[0:nd] - jnp.concatenate([prer[nd:nd + HEADS], sufr[nd + HEADS:2 * nd]], axis=0)
        for hd in range(HEADS):
            hs = slice(hd * DH, (hd + 1) * DH)
            cols = []
            for dr, (msk, kw_ref) in enumerate(((causal, kwf_ref), (anti, kwb_ref))):
                i8 = dr * HEADS + hd
                rmax = jnp.max(jnp.where(msk, rr[i8:i8 + 1, :], NEG), axis=1, keepdims=True)
                cols += [rmax, bc[:, i8:i8 + 1]]
                kw_ref[0, rs, hs] = (kconv[:, hs] * wu[:, i8:i8 + 1]).astype(BF16)
            gc_ref[0, hd, rs, :] = jnp.where(
                lane_c == 0, cols[0], jnp.where(lane_c == 1, cols[1], jnp.where(lane_c == 2, cols[2], cols[3])))
            scal = [jnp.broadcast_to(x[:, j:j + 1], (1, CHUNK))
                    for j in (hd, HEADS + hd) for x in (a, blast)]
            gr_ref[0, hd, :, rs] = jnp.concatenate(
                [rr[hd:hd + 1], rr[HEADS + hd:HEADS + hd + 1]] + scal + [jnp.zeros((2, CHUNK), F32)], axis=0)


def _in_proj(x, head, norm_g, w_in, b_gates, conv_w, conv_b, cosf, sinf):
    b, s, d = x.shape
    t = CHUNK + s
    tr = _row_tile(t)
    n_tiles = t // tr
    w = HEADS * DH
    wr = w_in[:, 0:4 * w].astype(BF16)
    wm = w_in[:, 4 * w:8 * w].astype(BF16)
    wg = jnp.pad(w_in[:, 8 * w:], ((0, 0), (0, LANES - 4 * HEADS))).astype(BF16)
    wgt = w_in[:, 8 * w:].T.astype(BF16)
    bg = jnp.pad(b_gates, (0, LANES - 4 * HEADS)).reshape(1, LANES)
    bgt = b_gates.reshape(4 * HEADS, 1)
    hb = tr // HALO
    pad_h = CHUNK // HALO
    last_h = s // HALO - 1
    seq = lambda bi, i: (bi, i, 0)
    const2 = lambda bi, i: (0, 0)
    out_bf = jax.ShapeDtypeStruct((b, t, w), BF16)
    kern = functools.partial(_in_proj_kernel, tr=tr, n_tiles=n_tiles)
    return pl.pallas_call(
        kern,
        grid=(b, n_tiles),
        in_specs=_x_specs(tr, d) + [
            pl.BlockSpec((CHUNK, d), const2),
            pl.BlockSpec((1, HALO, d), lambda bi, i: (bi, jnp.maximum(i * hb - pad_h - 1, 0), 0)),
            pl.BlockSpec((1, HALO, d), lambda bi, i: (bi, jnp.minimum((i + 1) * hb - pad_h, last_h), 0)),
            pl.BlockSpec((1, d), const2),
            pl.BlockSpec((d, 4 * w), const2),
            pl.BlockSpec((d, 4 * w), const2),
            pl.BlockSpec((d, LANES), const2),
            pl.BlockSpec((4 * HEADS, d), const2),
            pl.BlockSpec((1, LANES), const2),
            pl.BlockSpec((4 * HEADS, 1), const2),
            pl.BlockSpec((CONV_K, 2 * w), const2),
            pl.BlockSpec((1, 2 * w), const2),
            pl.BlockSpec((tr, DH), lambda bi, i: (i, 0)),
            pl.BlockSpec((tr, DH), lambda bi, i: (i, 0)),
        ],
        out_specs=[pl.BlockSpec((1, tr, w), seq)] * 10 + [
            pl.BlockSpec((1, HEADS, tr, LANES), lambda bi, i: (bi, 0, i, 0)),
            pl.BlockSpec((1, HEADS, 8, tr), lambda bi, i: (bi, 0, 0, i)),
        ],
        out_shape=[out_bf] * 10 + [
            jax.ShapeDtypeStruct((b, HEADS, t, LANES), F32),
            jax.ShapeDtypeStruct((b, HEADS, 8, t), F32),
        ],
        scratch_shapes=[pltpu.VMEM((tr + 2 * HALO, 2 * w), F32), pltpu.VMEM((tr, d), BF16),
                        pltpu.VMEM((tr, LANES), F32), pltpu.VMEM((tr, d), F32)],
        compiler_params=_cparams(("parallel", "arbitrary")),
        name="in_proj",
    )(*([x] * (tr // CHUNK)), head, x, x, norm_g.reshape(1, d), wr, wm, wg, wgt, bg, bgt,
      conv_w, conv_b.reshape(1, 2 * w), cosf, sinf)


def _retention_kernel(lg_ref, q_ref, k_ref, v_ref, o_ref, *, n_chunks):
    hd = pl.program_id(1)
    lgf = lg_ref[0, hd]
    lgb = lg_ref[1, hd]
    li = lax.broadcasted_iota(jnp.int32, (CHUNK, CHUNK), 0)
    mi = lax.broadcasted_iota(jnp.int32, (CHUNK, CHUNK), 1)
    diff = (li - mi).astype(F32)
    decay = (jnp.where(diff >= 0, jnp.exp(lgf * jnp.maximum(diff, 0.0)), 0.0)
             + jnp.where(diff <= 0, jnp.exp(lgb * jnp.maximum(-diff, 0.0)), 0.0))
    lcol = lax.broadcasted_iota(jnp.int32, (CHUNK, 1), 0).astype(F32)
    one = jnp.ones((1, 1), F32)
    zeta_f = jnp.exp(lgf * (CHUNK - 1.0 - lcol))
    xi_f = jnp.exp(lgf * (lcol + 1.0))
    g_f = jnp.exp(lgf * CHUNK * one)
    zeta_b = jnp.exp(lgb * lcol)
    xi_b = jnp.exp(lgb * (CHUNK - lcol))
    g_b = jnp.exp(lgb * CHUNK * one)

    def fwd(c, state):
        rows = pl.ds(pl.multiple_of(c * CHUNK, CHUNK), CHUNK)
        q = q_ref[0, rows, :]
        k = k_ref[0, rows, :]
        v = v_ref[0, rows, :]
        s = _dot_nt(q, k) * decay
        intra = _dot(s.astype(BF16), v)
        cross = _dot(q, state.astype(BF16)) * xi_f
        o_ref[0, rows, :] = intra + cross
        kz = (k.astype(F32) * zeta_f).astype(BF16)
        return g_f * state + _dot_tn(kz, v)

    unroll = _chunk_unroll(n_chunks)
    lax.fori_loop(0, n_chunks, fwd, jnp.zeros((DH, DH), F32), unroll=unroll)

    def bwd(j, state):
        c = n_chunks - 1 - j
        rows = pl.ds(pl.multiple_of(c * CHUNK, CHUNK), CHUNK)
        q = q_ref[0, rows, :]
        k = k_ref[0, rows, :]
        v = v_ref[0, rows, :]
        cross = _dot(q, state.astype(BF16)) * xi_b
        o_ref[0, rows, :] = o_ref[0, rows, :] + cross
        kz = (k.astype(F32) * zeta_b).astype(BF16)
        return g_b * state + _dot_tn(kz, v)

    lax.fori_loop(0, n_chunks, bwd, jnp.zeros((DH, DH), F32), unroll=unroll)


def _retention(lg, rq, rk, rv):
    b, t, _ = rq.shape
    spec = pl.BlockSpec((1, t, DH), lambda bi, hi, lg_ref: (bi, 0, hi))
    return pl.pallas_call(
        functools.partial(_retention_kernel, n_chunks=t // CHUNK),
        grid_spec=pltpu.PrefetchScalarGridSpec(
            num_scalar_prefetch=1,
            grid=(b, HEADS),
            in_specs=[spec, spec, spec],
            out_specs=spec,
        ),
        out_shape=jax.ShapeDtypeStruct((b, t, HEADS * DH), F32),
        compiler_params=_cparams(("parallel", "arbitrary")),
        name="retention",
    )(lg, rq, rk, rv)


def _mlstm_kernel(q_ref, k_ref, v_ref, kwf_ref, kwb_ref, gc_ref, gr_ref, o_ref, *, n_chunks):
    li = lax.broadcasted_iota(jnp.int32, (CHUNK, CHUNK), 0)
    mi = lax.broadcasted_iota(jnp.int32, (CHUNK, CHUNK), 1)
    causal = li >= mi
    anti = li <= mi

    ones = jnp.ones((CHUNK, DH), BF16)

    def step(c, carry, backward):
        st, m_st = carry
        rows = pl.ds(pl.multiple_of(c * CHUNK, CHUNK), CHUNK)
        q = q_ref[0, rows, :]
        k = k_ref[0, rows, :]
        v_aug = jnp.concatenate([v_ref[0, rows, :], ones], axis=1)
        gcol = gc_ref[0, 0, rows, :]
        grow = gr_ref[0, 0, :, rows]
        if backward:
            kw = kwb_ref[0, rows, :]
            r_max, bc = gcol[:, 2:3], gcol[:, 3:4]
            r_row, a, b_last = grow[1:2, :], grow[4:5, 0:1], grow[5:6, 0:1]
            mask = anti
        else:
            kw = kwf_ref[0, rows, :]
            r_max, bc = gcol[:, 0:1], gcol[:, 1:2]
            r_row, a, b_last = grow[0:1, :], grow[2:3, 0:1], grow[3:4, 0:1]
            mask = causal
        mm = jnp.maximum(jnp.broadcast_to(r_max, (CHUNK, DH)), m_st)
        s = _dot_nt(q, k) * jnp.where(mask, jnp.exp(r_row - mm), 0.0)
        w_inter = jnp.exp(m_st - mm)
        w2 = jnp.concatenate([w_inter, w_inter], axis=1)
        tot = _dot(s.astype(BF16), v_aug) + _dot(q, st.astype(BF16)) * w2
        floor = jnp.exp(-(jnp.broadcast_to(bc, (CHUNK, DH)) + mm))
        hval = tot[:, 0:DH] / jnp.maximum(jnp.abs(tot[:, DH:2 * DH]), floor)
        if backward:
            o_ref[0, rows, :] = o_ref[0, rows, :] + hval
        else:
            o_ref[0, rows, :] = hval
        upd = _dot_tn(kw, v_aug)
        m_new = jnp.maximum(b_last + m_st, a)
        f = jnp.exp(b_last + m_st - m_new)
        g = jnp.exp(a - m_new)
        return f * st + g * upd, m_new

    init = (jnp.zeros((DH, 2 * DH), F32), jnp.zeros((1, 1), F32))
    unroll = _chunk_unroll(n_chunks)
    lax.fori_loop(0, n_chunks, lambda c, s: step(c, s, False), init, unroll=unroll)
    lax.fori_loop(0, n_chunks, lambda j, s: step(n_chunks - 1 - j, s, True), init, unroll=unroll)


def _mlstm(mq, mk, mv, kwf, kwb, gcol, grow):
    b, t, _ = mq.shape
    spec = pl.BlockSpec((1, t, DH), lambda bi, hi: (bi, 0, hi))
    return pl.pallas_call(
        functools.partial(_mlstm_kernel, n_chunks=t // CHUNK),
        grid=(b, HEADS),
        in_specs=[spec, spec, spec, spec, spec,
                  pl.BlockSpec((1, 1, t, LANES), lambda bi, hi: (bi, hi, 0, 0)),
                  pl.BlockSpec((1, 1, 8, t), lambda bi, hi: (bi, hi, 0, 0))],
        out_specs=spec,
        out_shape=jax.ShapeDtypeStruct((b, t, HEADS * DH), F32),
        compiler_params=_cparams(("parallel", "arbitrary")),
        name="mlstm",
    )(mq, mk, mv, kwf, kwb, gcol, grow)


def _head_norm(y, g):
    outs = []
    for hd in range(HEADS):
        yh = y[:, hd * DH:(hd + 1) * DH]
        mu = jnp.mean(yh, axis=1, keepdims=True)
        yc = yh - mu
        var = jnp.mean(yc * yc, axis=1, keepdims=True)
        outs.append(yc * lax.rsqrt(var + EPS))
    return jnp.concatenate(outs, axis=1) * g


def _out_proj_kernel(*refs, tr):
    nb = tr // CHUNK
    x_refs, refs = refs[:nb], refs[nb:]
    (head_ref, ret_ref, hm_ref, rg_ref, mo_ref, rgn_ref, mgn_ref, wo_ref,
     fg_ref, wrt_ref, h1_ref, u2_ref, aff_ref) = refs
    i = pl.program_id(1)
    row = lax.broadcasted_iota(jnp.int32, (tr, 1), 0) + i * tr
    real = row >= PAD
    w = HEADS * DH
    y_ret = _head_norm(ret_ref[0], rgn_ref[...]) * rg_ref[0].astype(F32)
    y_m = _head_norm(mo_ref[0].astype(F32) * hm_ref[0], mgn_ref[...])
    y_ret = jnp.where(real, y_ret, 0.0).astype(BF16)
    y_m = jnp.where(real, y_m, 0.0).astype(BF16)
    h = jnp.concatenate(_load_rows(x_refs, head_ref, i), axis=0)
    h1 = h + _dot(y_ret, wo_ref[0:w, :]) + _dot(y_m, wo_ref[w:2 * w, :])
    h1_ref[0] = h1
    u2 = _rms(h1, fg_ref[...])
    _to_segments(u2_ref, _pack_pairs(u2))
    logits = _dot_nt(wrt_ref[...], u2.astype(BF16))
    mx = jnp.max(logits, axis=0, keepdims=True)
    ex = jnp.exp(logits - mx)
    aff = ex / jnp.sum(ex, axis=0, keepdims=True)
    coln = lax.broadcasted_iota(jnp.int32, (1, tr), 1) + i * tr
    aff_ref[0] = jnp.where(coln >= PAD, aff, -1.0)


def _out_proj(x, head, ret, hm, rgs, mos, ret_gn_g, mlstm_gn_g, w_out, ffn_norm_g, w_router):
    b, s, d = x.shape
    t = CHUNK + s
    tr = _row_tile(t)
    n_tiles = t // tr
    nseg = d // 2 // LANES
    w = HEADS * DH
    seq = lambda bi, i: (bi, i, 0)
    const2 = lambda bi, i: (0, 0)
    return pl.pallas_call(
        functools.partial(_out_proj_kernel, tr=tr),
        grid=(b, n_tiles),
        in_specs=_x_specs(tr, d) + [
            pl.BlockSpec((CHUNK, d), const2),
            pl.BlockSpec((1, tr, w), seq),
            pl.BlockSpec((1, tr, w), seq),
            pl.BlockSpec((1, tr, w), seq),
            pl.BlockSpec((1, tr, w), seq),
            pl.BlockSpec((1, w), const2),
            pl.BlockSpec((1, w), const2),
            pl.BlockSpec((2 * w, d), const2),
            pl.BlockSpec((1, d), const2),
            pl.BlockSpec((N_EXPERTS, d), const2),
        ],
        out_specs=[
            pl.BlockSpec((1, tr, d), seq),
            pl.BlockSpec((tr // SUB, nseg, SUB, LANES), lambda bi, i: (bi * n_tiles + i, 0, 0, 0)),
            pl.BlockSpec((1, N_EXPERTS, tr), lambda bi, i: (bi, 0, i)),
        ],
        out_shape=[
            jax.ShapeDtypeStruct((b, t, d), F32),
            jax.ShapeDtypeStruct((b * t // SUB, nseg, SUB, LANES), jnp.int32),
            jax.ShapeDtypeStruct((b, N_EXPERTS, t), F32),
        ],
        compiler_params=_cparams(("parallel", "arbitrary")),
        name="out_proj",
    )(*([x] * (tr // CHUNK)), head, ret, hm, rgs, mos, ret_gn_g.reshape(1, w), mlstm_gn_g.reshape(1, w),
      w_out.astype(BF16), ffn_norm_g.reshape(1, d), w_router.T.astype(BF16))


def _select_kernel(aff_ref, pm_ref, dv_ref, cnt_ref, bt_ref, mc_ref, sel_ref, *, t, cap, tr):
    e_n = N_EXPERTS
    nblk = t // LANES
    bits = pltpu.bitcast(aff_ref[0], jnp.int32)
    capf = float(cap)

    def search(_, lohi):
        lo, hi = lohi
        mid = lo + ((hi - lo + 1) >> 1)
        cnt = jnp.sum((bits >= mid).astype(F32), axis=1, keepdims=True)
        ok = cnt >= capf
        return jnp.where(ok, mid, lo), jnp.where(ok, hi, mid - 1)

    lo0 = jnp.zeros((e_n, 1), jnp.int32)
    hi0 = jnp.full((e_n, 1), 0x3F800000, jnp.int32)
    thr, _ = lax.fori_loop(0, 31, search, (lo0, hi0))
    need = capf - jnp.sum((bits > thr).astype(F32), axis=1, keepdims=True)

    ci = lax.broadcasted_iota(jnp.int32, (LANES, LANES), 0)
    cj = lax.broadcasted_iota(jnp.int32, (LANES, LANES), 1)
    upp = (ci <= cj).astype(BF16)
    ei = lax.broadcasted_iota(jnp.int32, (e_n, e_n), 0)
    ej = lax.broadcasted_iota(jnp.int32, (e_n, e_n), 1)
    strict = (ej < ei).astype(BF16)

    carry = jnp.zeros((e_n, 1), F32)
    for j in range(nblk):
        sl = slice(j * LANES, (j + 1) * LANES)
        bb = bits[:, sl]
        eqf = (bb == thr).astype(F32)
        ceq = _dot(eqf.astype(BF16), upp) + carry
        carry = ceq[:, LANES - 1:LANES]
        sel_ref[:, sl] = jnp.where(bb > thr, 1.0, jnp.where(ceq <= need, eqf, 0.0))

    carry = jnp.zeros((e_n, 1), F32)
    tok = lax.broadcasted_iota(jnp.int32, (1, LANES), 1).astype(F32)
    lane_b = lax.broadcasted_iota(jnp.int32, (e_n, LANES), 1)
    btot = jnp.zeros((e_n, LANES), F32)
    tile_max = jnp.zeros((SUB, LANES), F32)
    lane_t = lax.broadcasted_iota(jnp.int32, (SUB, LANES), 1)
    for j in range(nblk):
        sl = slice(j * LANES, (j + 1) * LANES)
        selb = sel_ref[:, sl]
        selbf = selb.astype(BF16)
        pin = _dot(selbf, upp) + carry
        carry = pin[:, LANES - 1:LANES]
        btot = jnp.where(lane_b == j, carry, btot)
        pm_ref[0, :, sl] = jnp.where(selb > 0, pin, 0.0)
        rank = _dot(strict, selbf)
        dv_ref[0, :, sl] = (tok + float(j * LANES)) * float(N_EXPERTS) + rank
        cnt_row = jnp.sum(selb, axis=0, keepdims=True)
        cnt_ref[0, sl, :] = jnp.broadcast_to(cnt_row, (LANES, LANES)).T
        tile_max = jnp.where(lane_t == (j * LANES) // tr,
                             jnp.maximum(tile_max, jnp.max(cnt_row, axis=1, keepdims=True)), tile_max)
    bt_ref[0] = jnp.where(lane_b < nblk, btot, carry).astype(jnp.int32)
    mc_ref[0] = tile_max.astype(jnp.int32)


def _select(aff, cap):
    b, e_n, t = aff.shape
    tr = _row_tile(t)
    assert t // tr <= LANES
    spec = pl.BlockSpec((1, e_n, t), lambda bi: (bi, 0, 0))
    return pl.pallas_call(
        functools.partial(_select_kernel, t=t, cap=cap, tr=tr),
        grid=(b,),
        in_specs=[spec],
        out_specs=[
            spec,
            spec,
            pl.BlockSpec((1, t, LANES), lambda bi: (bi, 0, 0)),
            pl.BlockSpec((1, e_n, LANES), lambda bi: (bi, 0, 0)),
            pl.BlockSpec((1, SUB, LANES), lambda bi: (bi, 0, 0)),
        ],
        out_shape=[
            jax.ShapeDtypeStruct((b, e_n, t), F32),
            jax.ShapeDtypeStruct((b, e_n, t), F32),
            jax.ShapeDtypeStruct((b, t, LANES), F32),
            jax.ShapeDtypeStruct((b, e_n, LANES), jnp.int32),
            jax.ShapeDtypeStruct((b, SUB, LANES), jnp.int32),
        ],
        scratch_shapes=[pltpu.VMEM((e_n, t), F32)],
        compiler_params=_cparams(("parallel",)),
        name="select",
    )(aff)


def _compact_kernel(bt_ref, pm_ref, dv_ref, aff_ref, src_ref, dst_ref, gate_ref,
                    *, t, cap, c_pad, nct, nseg, batch):
    bi = batch
    e_n = N_EXPERTS
    nblk = t // LANES
    gate_ref[...] = jnp.zeros(gate_ref.shape, F32)
    lane_e = lax.broadcasted_iota(jnp.int32, (LANES, LANES), 1)
    slot_l = lax.broadcasted_iota(jnp.int32, (1, LANES), 1)

    spr = LANES // nseg
    sh_spr = spr.bit_length() - 1
    sh_grp = (SUB * nseg).bit_length() - 1
    sub_i = lax.broadcasted_iota(jnp.int32, (SUB, LANES), 0)
    lane_i = lax.broadcasted_iota(jnp.int32, (SUB, LANES), 1)
    pick = (lane_i >> sh_spr) == sub_i
    pl_i = lax.broadcasted_iota(jnp.int32, (LANES, LANES), 0)
    pl_j = lax.broadcasted_iota(jnp.int32, (LANES, LANES), 1)
    perm = ((pl_i & (spr - 1)) == ((pl_j >> sh_grp) << 3) + (pl_j & (SUB - 1))).astype(BF16)
    seg_j = (lane_i & (SUB * nseg - 1)) >> 3

    def segment_tile(rows):
        spread = _dot3_right(jnp.where(pick, rows.astype(F32), 0.0), perm).astype(jnp.int32)
        return (((spread >> 3) * nseg + seg_j) << 3) + (spread & (SUB - 1))

    def expert(e):
        base = e * LANES

        def ctile(ct, j0):
            c0 = pl.multiple_of(ct * LANES, LANES)
            j_lo = lax.while_loop(
                lambda j: (j < nblk) & (bt_ref[base + jnp.minimum(j, nblk - 1)] <= c0),
                lambda j: j + 1, j0)
            j_hi = lax.while_loop(
                lambda j: (j < nblk) & (bt_ref[base + jnp.minimum(j, nblk - 1)] < c0 + LANES),
                lambda j: j + 1, j_lo)
            j_end = jnp.minimum(j_hi + 1, nblk)
            want = (lax.broadcasted_iota(jnp.int32, (LANES, 1), 0) + (c0 + 1)).astype(F32)

            def blk(j, acc):
                acc_d, acc_g = acc
                cols = pl.ds(pl.multiple_of(j * LANES, LANES), LANES)
                hit = pm_ref[0, e:e + 1, cols] == want
                acc_d = acc_d + jnp.where(hit, dv_ref[0, e:e + 1, cols], 0.0)
                acc_g = acc_g + jnp.where(hit, aff_ref[0, e:e + 1, cols], 0.0)
                return acc_d, acc_g

            zero = jnp.zeros((LANES, LANES), F32)
            acc_d, acc_g = lax.fori_loop(j_lo, j_end, blk, (zero, zero))
            v = jnp.sum(acc_d.T, axis=0, keepdims=True).astype(jnp.int32)
            tok = v >> 4
            rank = v & (N_EXPERTS - 1)
            p = slot_l + (c0 - cap)
            is_pad = p >= 0
            src = bi * t + jnp.where(is_pad, 0, tok)
            dst = jnp.where(is_pad, N_EXPERTS * t + e * (c_pad - cap) + p, rank * t + tok)
            out_rows = pl.ds(pl.multiple_of((e * nct + ct) * SUB, SUB), SUB)
            src_ref[0, out_rows, :] = segment_tile(src)
            dst_ref[0, out_rows, :] = segment_tile(dst)
            gcol = jnp.sum(acc_g, axis=1, keepdims=True)
            rows = pl.ds(c0, LANES)
            gate_ref[0, rows, :] = jnp.where(lane_e == e, gcol, gate_ref[0, rows, :])
            return j_lo

        lax.fori_loop(0, nct, ctile, 0)

    for e in range(e_n):
        expert(e)


def _compact(btot, pm, dv, aff, cap, c_pad, nseg, batch):
    _, e_n, t = aff.shape
    assert nseg in (1, 2, 4, 8)
    nct = -(-c_pad // LANES)
    tile_rows = e_n * nct * SUB
    spec = pl.BlockSpec((1, e_n, t), lambda i, bt: (0, 0, 0))
    ospec = pl.BlockSpec((1, tile_rows, LANES), lambda i, bt: (0, 0, 0))
    src, dst, gate = pl.pallas_call(
        functools.partial(_compact_kernel, t=t, cap=cap, c_pad=c_pad, nct=nct, nseg=nseg, batch=batch),
        grid_spec=pltpu.PrefetchScalarGridSpec(
            num_scalar_prefetch=1,
            grid=(1,),
            in_specs=[spec, spec, spec],
            out_specs=[ospec, ospec, pl.BlockSpec((1, nct * LANES, LANES), lambda i, bt: (0, 0, 0))],
        ),
        out_shape=[
            jax.ShapeDtypeStruct((1, tile_rows, LANES), jnp.int32),
            jax.ShapeDtypeStruct((1, tile_rows, LANES), jnp.int32),
            jax.ShapeDtypeStruct((1, nct * LANES, LANES), F32),
        ],
        compiler_params=_cparams(("arbitrary",)),
        name="compact",
    )(btot.reshape(e_n * LANES), pm, dv, aff)
    ids = lambda a: a.reshape(e_n, nct, SUB, LANES)[:, :, :nseg].reshape(e_n, -1)[:, :c_pad * nseg].reshape(-1)
    return ids(src), ids(dst), gate


def _sc_mesh():
    return plsc.VectorSubcoreMesh(core_axis_name="core", subcore_axis_name="subcore")


def _row_gather(table, idx):
    m = idx.shape[0]
    d = table.shape[1]

    @functools.partial(pl.kernel, out_type=jax.ShapeDtypeStruct((m, d), table.dtype),
                       mesh=_sc_mesh(), scratch_types=[])
    def gather_kernel(x_hbm, i_hbm, o_hbm):
        def body(i_vmem, o_vmem):
            pltpu.sync_copy(x_hbm.at[i_vmem.at[0]], o_vmem)

        pltpu.emit_pipeline(
            body,
            grid=(m // SC_WINDOW,),
            in_specs=[pl.BlockSpec((1, SC_WINDOW), lambda i: (0, i))],
            out_specs=[pl.BlockSpec((SC_WINDOW, d), lambda i: (i, 0))],
            core_axis_name=("core", "subcore"),
            dimension_semantics=(pltpu.PARALLEL,),
        )(i_hbm, o_hbm)

    return gather_kernel(table, idx.reshape(1, m))


def _row_scatter(rows, idx, n_out):
    m, d = rows.shape

    @functools.partial(pl.kernel, out_type=jax.ShapeDtypeStruct((n_out, d), rows.dtype),
                       mesh=_sc_mesh(), scratch_types=[])
    def scatter_kernel(x_hbm, i_hbm, o_hbm):
        def body(x_vmem, i_vmem):
            pltpu.sync_copy(x_vmem, o_hbm.at[i_vmem.at[0]])

        pltpu.emit_pipeline(
            body,
            grid=(m // SC_WINDOW,),
            in_specs=[pl.BlockSpec((SC_WINDOW, d), lambda i: (i, 0)),
                      pl.BlockSpec((1, SC_WINDOW), lambda i: (0, i))],
            out_specs=[],
            core_axis_name=("core", "subcore"),
            dimension_semantics=(pltpu.PARALLEL,),
        )(x_hbm, i_hbm)

    return scatter_kernel(rows, idx.reshape(1, m))


def _expert_kernel(gate_ref, x_ref, wg_ref, wu_ref, wd_ref, y_ref, xb_ref, acc_ref, *, c_pad, n_f):
    e = pl.program_id(0)
    f = pl.program_id(1)

    @pl.when(f == 0)
    def _():
        w = _from_segments(x_ref)
        xb_ref[...] = jnp.concatenate([_unpack_hi(w), _unpack_lo(w)], axis=1).astype(BF16)
        acc_ref[...] = jnp.zeros(acc_ref.shape, F32)

    wg = wg_ref[0].astype(BF16)
    wu = wu_ref[0].astype(BF16)
    wd = wd_ref[0].astype(BF16)
    half = c_pad // 2
    for r0 in (0, half):
        rows = slice(r0, r0 + half)
        x = xb_ref[rows, :]
        hid = _silu(_dot(x, wg)) * _dot(x, wu)
        part = _dot(hid.astype(BF16), wd)
        acc_ref[rows, :] = acc_ref[rows, :] + part

    @pl.when(f == n_f - 1)
    def _():
        lane = lax.broadcasted_iota(jnp.int32, (c_pad, LANES), 1)
        gcol = jnp.sum(jnp.where(lane == e, gate_ref[0], 0.0), axis=1, keepdims=True)
        _to_segments(y_ref, _pack_pairs(acc_ref[...] * gcol))


def _experts(gate, xs, w_gate, w_up, w_down):
    e_n, d, d_ff = w_gate.shape
    nseg = d // 2 // LANES
    c_pad = xs.shape[0] * SUB // e_n
    fc = 256
    n_f = d_ff // fc
    blk = pl.BlockSpec((c_pad // SUB, nseg, SUB, LANES), lambda e, f: (e, 0, 0, 0))
    return pl.pallas_call(
        functools.partial(_expert_kernel, c_pad=c_pad, n_f=n_f),
        grid=(e_n, n_f),
        in_specs=[
            pl.BlockSpec((1, c_pad, LANES), lambda e, f: (0, 0, 0)),
            blk,
            pl.BlockSpec((1, d, fc), lambda e, f: (e, 0, f)),
            pl.BlockSpec((1, d, fc), lambda e, f: (e, 0, f)),
            pl.BlockSpec((1, fc, d), lambda e, f: (e, f, 0)),
        ],
        out_specs=blk,
        out_shape=jax.ShapeDtypeStruct(xs.shape, jnp.int32),
        scratch_shapes=[pltpu.VMEM((c_pad, d), BF16), pltpu.VMEM((c_pad, d), F32)],
        compiler_params=_cparams(("parallel", "arbitrary")),
        name="experts",
    )(gate, xs, w_gate, w_up, w_down)


LAYERS_PER_STEP = 2


def _combine_kernel(mc_ref, h1_ref, fg_ref, *rest, n_steps, nb):
    cnt_refs = rest[:nb]
    z_refs = rest[nb:nb + nb * LAYERS_PER_STEP]
    o_ref, acc_ref = rest[-2], rest[-1]
    bi = pl.program_id(0)
    i = pl.program_id(1)
    j = pl.program_id(2)

    @pl.when(j == 0)
    def _():
        _to_segments(acc_ref, h1_ref[0])

    for k in range(nb):
        for l in range(LAYERS_PER_STEP):
            layer = j * LAYERS_PER_STEP + l

            @pl.when((bi == k) & (layer < mc_ref[bi * LANES + i]))
            def _():
                z_ref = z_refs[k * LAYERS_PER_STEP + l]
                g, nz = z_ref.shape[0], z_ref.shape[1]
                take = cnt_refs[k][0].reshape(g, 1, SUB, LANES) > layer.astype(F32)
                w = z_ref[...]
                acc_ref[:, 0:nz] = acc_ref[:, 0:nz] + jnp.where(take, _unpack_hi(w), 0.0)
                acc_ref[:, nz:2 * nz] = acc_ref[:, nz:2 * nz] + jnp.where(take, _unpack_lo(w), 0.0)

    @pl.when(j == n_steps - 1)
    def _():
        o_ref[0] = _rms(_from_segments(acc_ref), fg_ref[...])


def _combine(maxcnt, h1, zs, cnts, final_norm_g):
    b, t, d = h1.shape
    n_steps = N_EXPERTS // LAYERS_PER_STEP
    tr = _row_tile(t)
    n_tiles = t // tr

    def z_map(k, l):
        def index(bi, i, j, mc):
            layer = jnp.maximum(jnp.minimum(j * LAYERS_PER_STEP + l, mc[bi * LANES + i] - 1), 0)
            return (jnp.where(bi == k, layer * n_tiles + i, 0), 0, 0, 0)
        return index

    def cnt_map(k):
        return lambda bi, i, j, mc: (0, jnp.where(bi == k, i, 0), 0)

    seg_block = (tr // SUB, d // LANES, SUB, LANES)
    z_block = (tr // SUB, d // 2 // LANES, SUB, LANES)
    z_specs = [pl.BlockSpec(z_block, z_map(k, l)) for k in range(b) for l in range(LAYERS_PER_STEP)]
    z_args = [zs[k] for k in range(b) for _ in range(LAYERS_PER_STEP)]
    return pl.pallas_call(
        functools.partial(_combine_kernel, n_steps=n_steps, nb=b),
        grid_spec=pltpu.PrefetchScalarGridSpec(
            num_scalar_prefetch=1,
            grid=(b, n_tiles, n_steps),
            in_specs=[
                pl.BlockSpec((1, tr, d), lambda bi, i, j, mc: (bi, i, 0)),
                pl.BlockSpec((1, d), lambda bi, i, j, mc: (0, 0)),
            ] + [pl.BlockSpec((1, tr, LANES), cnt_map(k)) for k in range(b)] + z_specs,
            out_specs=pl.BlockSpec((1, tr, d), lambda bi, i, j, mc: (bi, i, 0)),
            scratch_shapes=[pltpu.VMEM(seg_block, F32)],
        ),
        out_shape=jax.ShapeDtypeStruct((b, t, d), F32),
        compiler_params=_cparams(("parallel", "arbitrary", "arbitrary")),
        name="combine",
    )(maxcnt, h1, final_norm_g.reshape(1, d), *cnts, *z_args)


def kernel(x, meta_tokens, mix_norm_g, w_in, b_gates, conv_w, conv_b, ret_decay_logit, ret_gn_g,
           mlstm_gn_g, w_out, ffn_norm_g, w_router, w_gate, w_up, w_down, final_norm_g):
    b, s, d = x.shape
    t = CHUNK + s
    n = N_META + s
    cap = 2 * n // N_EXPERTS
    nseg = d // 2 // LANES
    c_pad = next(c for c in range(-(-cap // 16) * 16, cap + 4096, 16)
                 if (N_EXPERTS * c * nseg) % (SC_WINDOW * SC_SUBCORES) == 0)

    head = jnp.concatenate([jnp.zeros((PAD, d), x.dtype), meta_tokens.astype(x.dtype)], axis=0)

    half = DH // 2
    pos = jnp.arange(t, dtype=F32) - PAD
    inv = ROPE_BASE ** (-jnp.arange(half, dtype=F32) / half)
    ang = pos[:, None] * inv[None, :]
    cosf = jnp.concatenate([jnp.cos(ang), jnp.cos(ang)], axis=1)
    sinf = jnp.concatenate([-jnp.sin(ang), jnp.sin(ang)], axis=1)

    assert mix_norm_g.shape[0] == 1, "single-layer block only"
    rq, rk, rv, rgs, mq, mk, mv, mos, kwf, kwb, gcol, grow = _in_proj(
        x, head, mix_norm_g[0], w_in[0], b_gates[0], conv_w[0], conv_b[0], cosf, sinf)
    lg = jax.nn.log_sigmoid(ret_decay_logit[0].astype(F32))
    ret = _retention(lg, rq, rk, rv)
    hm = _mlstm(mq, mk, mv, kwf, kwb, gcol, grow)
    h1, u2, aff = _out_proj(x, head, ret, hm, rgs, mos, ret_gn_g[0], mlstm_gn_g[0], w_out[0],
                            ffn_norm_g[0], w_router[0])
    tiled = lambda a: a.reshape(-1, nseg, SUB, LANES)
    flat = lambda a: a.reshape(-1, LANES)
    z_rows = -(-(N_EXPERTS * t + N_EXPERTS * (c_pad - cap)) // SUB) * SUB
    routed = []
    for bi in range(b):
        aff_b = aff[bi:bi + 1]
        pm, dv, cnt, btot, maxcnt = _select(aff_b, cap)
        src, dst, gate = _compact(btot, pm, dv, aff_b, cap, c_pad, nseg, bi)
        routed.append((_row_gather(flat(u2), src), dst, gate, cnt, maxcnt))
    zs = []
    for xs, dst, gate, _, _ in routed:
        ys = _experts(gate, tiled(xs), w_gate[0], w_up[0], w_down[0])
        zs.append(tiled(_row_scatter(flat(ys), dst, z_rows * nseg)))
    maxcnt = jnp.concatenate([r[4][:, 0, :] for r in routed], axis=0).reshape(-1)
    out = _combine(maxcnt, h1, zs, [r[3] for r in routed], final_norm_g)
    return out[:, CHUNK:]
```

```python
import functools

import jax
import jax.numpy as jnp
from jax import lax
from jax.experimental import pallas as pl
from jax.experimental.pallas import tpu as pltpu
from jax.experimental.pallas import tpu_sc as plsc

F32 = jnp.float32
BF16 = jnp.bfloat16

LANES = 128
CHUNK = 128
N_META = 16
PAD = CHUNK - N_META
HEADS = 4
DH = 128
N_EXPERTS = 16
CONV_K = 5
HALO = 8
SUB = 8
SC_WINDOW = 128
SC_SUBCORES = 32
EPS = 1e-6
NEG = -1e30
ROPE_BASE = 10000.0
VMEM_LIMIT = 56 * 1024 * 1024


def _cparams(sem, vmem=VMEM_LIMIT, **kw):
    return pltpu.CompilerParams(dimension_semantics=sem, vmem_limit_bytes=vmem, **kw)


def _dot(a, b):
    return jnp.dot(a, b, preferred_element_type=F32)


def _dot_nt(a, b):
    return lax.dot_general(a, b, (((1,), (1,)), ((), ())), preferred_element_type=F32)


def _dot_tn(a, b):
    return lax.dot_general(a, b, (((0,), (0,)), ((), ())), preferred_element_type=F32)


def _split3(x):
    hi = x.astype(BF16)
    r = x - hi.astype(F32)
    mid = r.astype(BF16)
    lo = (r - mid.astype(F32)).astype(BF16)
    return hi, mid, lo


def _dot3_left(m_bf, x):
    hi, mid, lo = _split3(x)
    return _dot(m_bf, hi) + _dot(m_bf, mid) + _dot(m_bf, lo)


def _dot3_right(x, m_bf):
    hi, mid, lo = _split3(x)
    return _dot(hi, m_bf) + _dot(mid, m_bf) + _dot(lo, m_bf)


def _rms(x, g):
    return x * lax.rsqrt(jnp.mean(x * x, axis=-1, keepdims=True) + EPS) * g


def _log_sigmoid(x):
    return jnp.minimum(x, 0.0) - jnp.log1p(jnp.exp(-jnp.abs(x)))


def _silu(x):
    return x * (1.0 / (1.0 + jnp.exp(-x)))


def _sigmoid(x):
    return 1.0 / (1.0 + jnp.exp(-x))


def _row_tile(t):
    return 640 if t % 640 == 0 else CHUNK


def _chunk_unroll(n_chunks):
    return 13 if n_chunks % 13 == 0 else 1


def _to_segments(dst_ref, x):
    r, w = x.shape
    for j in range(w // LANES):
        dst_ref[:, j] = x[:, j * LANES:(j + 1) * LANES].reshape(r // SUB, SUB, LANES)


def _from_segments(src_ref):
    g, nseg = src_ref.shape[0], src_ref.shape[1]
    return jnp.concatenate([src_ref[:, j].reshape(g * SUB, LANES) for j in range(nseg)], axis=1)


def _bf16_bits(x):
    return pltpu.bitcast(x.astype(BF16).astype(F32), jnp.int32)


def _pack_pairs(x):
    half = x.shape[1] // 2
    return _bf16_bits(x[:, :half]) | lax.shift_right_logical(_bf16_bits(x[:, half:]), 16)


def _unpack_hi(w):
    return pltpu.bitcast(w & -65536, F32)


def _unpack_lo(w):
    return pltpu.bitcast(w << 16, F32)


def _load_rows(x_refs, head_ref, i):
    blocks = [r[0] for r in x_refs]
    blocks[0] = jnp.where(i == 0, head_ref[...], blocks[0])
    return blocks


def _x_specs(tr, d):
    nb = tr // CHUNK
    return [pl.BlockSpec((1, CHUNK, d), functools.partial(
        lambda bi, i, k: (bi, jnp.maximum(i * nb - 1 + k, 0), 0), k=k)) for k in range(nb)]


def _in_proj_kernel(*refs, tr, n_tiles):
    nb = tr // CHUNK
    x_refs, refs = refs[:nb], refs[nb:]
    (head_ref, hp_ref, hn_ref, ng_ref, wr_ref, wm_ref, wg_ref, wgt_ref,
     bg_ref, bgt_ref, cw_ref, cb_ref, cos_ref, sin_ref,
     rq_ref, rk_ref, rv_ref, rg_ref, mq_ref, mk_ref, mv_ref, mo_ref,
     kwf_ref, kwb_ref, gc_ref, gr_ref, ext_ref, u_ref, gts_ref, h_ref) = refs
    i = pl.program_id(1)
    ng = ng_ref[...]
    w = HEADS * DH
    nd = 2 * HEADS
    for k, blk in enumerate(_load_rows(x_refs, head_ref, i)):
        h_ref[k * CHUNK:(k + 1) * CHUNK, :] = blk

    hp = hp_ref[0]
    if tr == CHUNK:
        hp = jnp.where(i == 1, head_ref[CHUNK - HALO:CHUNK, :], hp)
    up = _rms(hp, ng)
    un = _rms(hn_ref[0], ng)
    up = jnp.where(i == 0, 0.0, up)
    un = jnp.where(i == n_tiles - 1, 0.0, un)
    uh = jnp.concatenate([up, un], axis=0).astype(BF16)
    halo = _dot(uh, wm_ref[:, 0:2 * w])
    ext_ref[0:HALO, :] = halo[0:HALO]
    ext_ref[HALO + tr:2 * HALO + tr, :] = halo[HALO:2 * HALO]

    sub = tr // 2
    lane_s = lax.broadcasted_iota(jnp.int32, (sub, LANES), 1)
    for r0 in (0, sub):
        rs = slice(r0, r0 + sub)
        u = _rms(h_ref[rs, :], ng).astype(BF16)
        u_ref[rs, :] = u
        real = lax.broadcasted_iota(jnp.int32, (sub, 1), 0) + (i * tr + r0) >= PAD
        cosf = cos_ref[rs, :]
        sinf = sin_ref[rs, :]
        pq = _dot(u, wr_ref[:, 0:w])
        pk = _dot(u, wr_ref[:, w:2 * w])
        for hd in range(HEADS):
            sl = slice(hd * DH, (hd + 1) * DH)
            xq = pq[:, sl]
            xk = pk[:, sl]
            rq_ref[0, rs, sl] = ((xq * cosf + pltpu.roll(xq, DH // 2, 1) * sinf) * (DH ** -0.5)).astype(BF16)
            rk_ref[0, rs, sl] = (xk * cosf + pltpu.roll(xk, DH // 2, 1) * sinf).astype(BF16)
        rv_ref[0, rs, :] = _dot(u, wr_ref[:, 2 * w:3 * w]).astype(BF16)
        rg_ref[0, rs, :] = _silu(_dot(u, wr_ref[:, 3 * w:4 * w])).astype(BF16)
        ext_ref[HALO + r0:HALO + r0 + sub, :] = _dot(u, wm_ref[:, 0:2 * w])
        mv_ref[0, rs, :] = _dot(u, wm_ref[:, 2 * w:3 * w]).astype(BF16)
        mo_ref[0, rs, :] = _sigmoid(_dot(u, wm_ref[:, 3 * w:4 * w])).astype(BF16)
        gts = _dot(u, wg_ref[...]) + bg_ref[...]
        gts_ref[rs, :] = jnp.where(lane_s < nd, jnp.where(real, gts, NEG),
                                   jnp.where(real & (lane_s < 2 * nd), _log_sigmoid(gts), 0.0))

    rown = lax.broadcasted_iota(jnp.int32, (2 * nd, tr), 0)
    coln = lax.broadcasted_iota(jnp.int32, (1, tr), 1) + i * tr
    realr = coln >= PAD
    gtr = _dot_nt(wgt_ref[...], u_ref[...]) + bgt_ref[...]
    gtr = jnp.where(rown < nd, jnp.where(realr, gtr, NEG),
                    jnp.where(realr, _log_sigmoid(gtr), 0.0))
    ci = lax.broadcasted_iota(jnp.int32, (CHUNK, CHUNK), 0)
    cj = lax.broadcasted_iota(jnp.int32, (CHUNK, CHUNK), 1)
    causal = ci >= cj
    anti = ci <= cj
    low = causal.astype(BF16)
    upp = anti.astype(BF16)
    lane_c = lax.broadcasted_iota(jnp.int32, (CHUNK, LANES), 1)
    lane_1 = lax.broadcasted_iota(jnp.int32, (1, LANES), 1)
    rown_c = lax.broadcasted_iota(jnp.int32, (2 * nd, CHUNK), 0)
    for c in range(tr // CHUNK):
        rs = slice(c * CHUNK, (c + 1) * CHUNK)
        realc = lax.broadcasted_iota(jnp.int32, (CHUNK, 1), 0) + (i * tr + c * CHUNK) >= PAD
        conv = cb_ref[...] + jnp.zeros((CHUNK, 2 * w), F32)
        for k in range(CONV_K):
            off = HALO + c * CHUNK + k - CONV_K // 2
            conv = conv + cw_ref[k:k + 1, :] * ext_ref[off:off + CHUNK, :]
        qk = jnp.where(realc, _silu(conv), 0.0)
        mq_ref[0, rs, :] = (qk[:, 0:w] * (DH ** -0.5)).astype(BF16)
        mk_ref[0, rs, :] = qk[:, w:2 * w].astype(BF16)
        kconv = qk[:, w:2 * w]
        g = gts_ref[rs, :]
        lf = jnp.where(lane_c >= nd, g, 0.0)
        pre = _dot3_left(low, lf)
        suf = _dot3_left(upp, lf)
        bc = pltpu.roll(jnp.where(lane_c < nd + HEADS, pre, suf), LANES - nd, 1)
        blast = jnp.where(lane_1 < HEADS, bc[CHUNK - 1:CHUNK], bc[0:1])
        log_u = blast - bc + g
        a = jnp.max(log_u, axis=0, keepdims=True)
        wu = jnp.exp(log_u - a)
        gl = gtr[:, rs]
        lfr = jnp.where(rown_c >= nd, gl, 0.0)
        prer = _dot3_right(lfr, upp)
        sufr = _dot3_right(lfr, low)
        rr = gl[0:nd] - jnp.concatenate([prer[nd:nd + HEADS], sufr[nd + HEADS:2 * nd]], axis=0)
        for hd in range(HEADS):
            hs = slice(hd * DH, (hd + 1) * DH)
            cols = []
            for dr, (msk, kw_ref) in enumerate(((causal, kwf_ref), (anti, kwb_ref))):
                i8 = dr * HEADS + hd
                rmax = jnp.max(jnp.where(msk, rr[i8:i8 + 1, :], NEG), axis=1, keepdims=True)
                cols += [rmax, bc[:, i8:i8 + 1]]
                kw_ref[0, rs, hs] = (kconv[:, hs] * wu[:, i8:i8 + 1]).astype(BF16)
            gc_ref[0, hd, rs, :] = jnp.where(
                lane_c == 0, cols[0], jnp.where(lane_c == 1, cols[1], jnp.where(lane_c == 2, cols[2], cols[3])))
            scal = [jnp.broadcast_to(x[:, j:j + 1], (1, CHUNK))
                    for j in (hd, HEADS + hd) for x in (a, blast)]
            gr_ref[0, hd, :, rs] = jnp.concatenate(
                [rr[hd:hd + 1], rr[HEADS + hd:HEADS + hd + 1]] + scal + [jnp.zeros((2, CHUNK), F32)], axis=0)


def _in_proj(x, head, norm_g, w_in, b_gates, conv_w, conv_b, cosf, sinf):
    b, s, d = x.shape
    t = CHUNK + s
    tr = _row_tile(t)
    n_tiles = t // tr
    w = HEADS * DH
    wr = w_in[:, 0:4 * w].astype(BF16)
    wm = w_in[:, 4 * w:8 * w].astype(BF16)
    wg = jnp.pad(w_in[:, 8 * w:], ((0, 0), (0, LANES - 4 * HEADS))).astype(BF16)
    wgt = w_in[:, 8 * w:].T.astype(BF16)
    bg = jnp.pad(b_gates, (0, LANES - 4 * HEADS)).reshape(1, LANES)
    bgt = b_gates.reshape(4 * HEADS, 1)
    hb = tr // HALO
    pad_h = CHUNK // HALO
    last_h = s // HALO - 1
    seq = lambda bi, i: (bi, i, 0)
    const2 = lambda bi, i: (0, 0)
    out_bf = jax.ShapeDtypeStruct((b, t, w), BF16)
    kern = functools.partial(_in_proj_kernel, tr=tr, n_tiles=n_tiles)
    return pl.pallas_call(
        kern,
        grid=(b, n_tiles),
        in_specs=_x_specs(tr, d) + [
            pl.BlockSpec((CHUNK, d), const2),
            pl.BlockSpec((1, HALO, d), lambda bi, i: (bi, jnp.maximum(i * hb - pad_h - 1, 0), 0)),
            pl.BlockSpec((1, HALO, d), lambda bi, i: (bi, jnp.minimum((i + 1) * hb - pad_h, last_h), 0)),
            pl.BlockSpec((1, d), const2),
            pl.BlockSpec((d, 4 * w), const2),
            pl.BlockSpec((d, 4 * w), const2),
            pl.BlockSpec((d, LANES), const2),
            pl.BlockSpec((4 * HEADS, d), const2),
            pl.BlockSpec((1, LANES), const2),
            pl.BlockSpec((4 * HEADS, 1), const2),
            pl.BlockSpec((CONV_K, 2 * w), const2),
            pl.BlockSpec((1, 2 * w), const2),
            pl.BlockSpec((tr, DH), lambda bi, i: (i, 0)),
            pl.BlockSpec((tr, DH), lambda bi, i: (i, 0)),
        ],
        out_specs=[pl.BlockSpec((1, tr, w), seq)] * 10 + [
            pl.BlockSpec((1, HEADS, tr, LANES), lambda bi, i: (bi, 0, i, 0)),
            pl.BlockSpec((1, HEADS, 8, tr), lambda bi, i: (bi, 0, 0, i)),
        ],
        out_shape=[out_bf] * 10 + [
            jax.ShapeDtypeStruct((b, HEADS, t, LANES), F32),
            jax.ShapeDtypeStruct((b, HEADS, 8, t), F32),
        ],
        scratch_shapes=[pltpu.VMEM((tr + 2 * HALO, 2 * w), F32), pltpu.VMEM((tr, d), BF16),
                        pltpu.VMEM((tr, LANES), F32), pltpu.VMEM((tr, d), F32)],
        compiler_params=_cparams(("parallel", "arbitrary")),
        name="in_proj",
    )(*([x] * (tr // CHUNK)), head, x, x, norm_g.reshape(1, d), wr, wm, wg, wgt, bg, bgt,
      conv_w, conv_b.reshape(1, 2 * w), cosf, sinf)


def _retention_kernel(lg_ref, q_ref, k_ref, v_ref, o_ref, *, n_chunks):
    hd = pl.program_id(1)
    lgf = lg_ref[0, hd]
    lgb = lg_ref[1, hd]
    li = lax.broadcasted_iota(jnp.int32, (CHUNK, CHUNK), 0)
    mi = lax.broadcasted_iota(jnp.int32, (CHUNK, CHUNK), 1)
    diff = (li - mi).astype(F32)
    decay = (jnp.where(diff >= 0, jnp.exp(lgf * jnp.maximum(diff, 0.0)), 0.0)
             + jnp.where(diff <= 0, jnp.exp(lgb * jnp.maximum(-diff, 0.0)), 0.0))
    lcol = lax.broadcasted_iota(jnp.int32, (CHUNK, 1), 0).astype(F32)
    one = jnp.ones((1, 1), F32)
    zeta_f = jnp.exp(lgf * (CHUNK - 1.0 - lcol))
    xi_f = jnp.exp(lgf * (lcol + 1.0))
    g_f = jnp.exp(lgf * CHUNK * one)
    zeta_b = jnp.exp(lgb * lcol)
    xi_b = jnp.exp(lgb * (CHUNK - lcol))
    g_b = jnp.exp(lgb * CHUNK * one)

    def fwd(c, state):
        rows = pl.ds(pl.multiple_of(c * CHUNK, CHUNK), CHUNK)
        q = q_ref[0, rows, :]
        k = k_ref[0, rows, :]
        v = v_ref[0, rows, :]
        s = _dot_nt(q, k) * decay
        intra = _dot(s.astype(BF16), v)
        cross = _dot(q, state.astype(BF16)) * xi_f
        o_ref[0, rows, :] = intra + cross
        kz = (k.astype(F32) * zeta_f).astype(BF16)
        return g_f * state + _dot_tn(kz, v)

    unroll = _chunk_unroll(n_chunks)
    lax.fori_loop(0, n_chunks, fwd, jnp.zeros((DH, DH), F32), unroll=unroll)

    def bwd(j, state):
        c = n_chunks - 1 - j
        rows = pl.ds(pl.multiple_of(c * CHUNK, CHUNK), CHUNK)
        q = q_ref[0, rows, :]
        k = k_ref[0, rows, :]
        v = v_ref[0, rows, :]
        cross = _dot(q, state.astype(BF16)) * xi_b
        o_ref[0, rows, :] = o_ref[0, rows, :] + cross
        kz = (k.astype(F32) * zeta_b).astype(BF16)
        return g_b * state + _dot_tn(kz, v)

    lax.fori_loop(0, n_chunks, bwd, jnp.zeros((DH, DH), F32), unroll=unroll)


def _retention(lg, rq, rk, rv):
    b, t, _ = rq.shape
    spec = pl.BlockSpec((1, t, DH), lambda bi, hi, lg_ref: (bi, 0, hi))
    return pl.pallas_call(
        functools.partial(_retention_kernel, n_chunks=t // CHUNK),
        grid_spec=pltpu.PrefetchScalarGridSpec(
            num_scalar_prefetch=1,
            grid=(b, HEADS),
            in_specs=[spec, spec, spec],
            out_specs=spec,
        ),
        out_shape=jax.ShapeDtypeStruct((b, t, HEADS * DH), F32),
        compiler_params=_cparams(("parallel", "arbitrary")),
        name="retention",
    )(lg, rq, rk, rv)


def _mlstm_kernel(q_ref, k_ref, v_ref, kwf_ref, kwb_ref, gc_ref, gr_ref, o_ref, *, n_chunks):
    li = lax.broadcasted_iota(jnp.int32, (CHUNK, CHUNK), 0)
    mi = lax.broadcasted_iota(jnp.int32, (CHUNK, CHUNK), 1)
    causal = li >= mi
    anti = li <= mi

    ones = jnp.ones((CHUNK, DH), BF16)

    def step(c, carry, backward):
        st, m_st = carry
        rows = pl.ds(pl.multiple_of(c * CHUNK, CHUNK), CHUNK)
        q = q_ref[0, rows, :]
        k = k_ref[0, rows, :]
        v_aug = jnp.concatenate([v_ref[0, rows, :], ones], axis=1)
        gcol = gc_ref[0, 0, rows, :]
        grow = gr_ref[0, 0, :, rows]
        if backward:
            kw = kwb_ref[0, rows, :]
            r_max, bc = gcol[:, 2:3], gcol[:, 3:4]
            r_row, a, b_last = grow[1:2, :], grow[4:5, 0:1], grow[5:6, 0:1]
            mask = anti
        else:
            kw = kwf_ref[0, rows, :]
            r_max, bc = gcol[:, 0:1], gcol[:, 1:2]
            r_row, a, b_last = grow[0:1, :], grow[2:3, 0:1], grow[3:4, 0:1]
            mask = causal
        mm = jnp.maximum(jnp.broadcast_to(r_max, (CHUNK, DH)), m_st)
        s = _dot_nt(q, k) * jnp.where(mask, jnp.exp(r_row - mm), 0.0)
        w_inter = jnp.exp(m_st - mm)
        w2 = jnp.concatenate([w_inter, w_inter], axis=1)
        tot = _dot(s.astype(BF16), v_aug) + _dot(q, st.astype(BF16)) * w2
        floor = jnp.exp(-(jnp.broadcast_to(bc, (CHUNK, DH)) + mm))
        hval = tot[:, 0:DH] / jnp.maximum(jnp.abs(tot[:, DH:2 * DH]), floor)
        if backward:
            o_ref[0, rows, :] = o_ref[0, rows, :] + hval
        else:
            o_ref[0, rows, :] = hval
        upd = _dot_tn(kw, v_aug)
        m_new = jnp.maximum(b_last + m_st, a)
        f = jnp.exp(b_last + m_st - m_new)
        g = jnp.exp(a - m_new)
        return f * st + g * upd, m_new

    init = (jnp.zeros((DH, 2 * DH), F32), jnp.zeros((1, 1), F32))
    unroll = _chunk_unroll(n_chunks)
    lax.fori_loop(0, n_chunks, lambda c, s: step(c, s, False), init, unroll=unroll)
    lax.fori_loop(0, n_chunks, lambda j, s: step(n_chunks - 1 - j, s, True), init, unroll=unroll)


def _mlstm(mq, mk, mv, kwf, kwb, gcol, grow):
    b, t, _ = mq.shape
    spec = pl.BlockSpec((1, t, DH), lambda bi, hi: (bi, 0, hi))
    return pl.pallas_call(
        functools.partial(_mlstm_kernel, n_chunks=t // CHUNK),
        grid=(b, HEADS),
        in_specs=[spec, spec, spec, spec, spec,
                  pl.BlockSpec((1, 1, t, LANES), lambda bi, hi: (bi, hi, 0, 0)),
                  pl.BlockSpec((1, 1, 8, t), lambda bi, hi: (bi, hi, 0, 0))],
        out_specs=spec,
        out_shape=jax.ShapeDtypeStruct((b, t, HEADS * DH), F32),
        compiler_params=_cparams(("parallel", "arbitrary")),
        name="mlstm",
    )(mq, mk, mv, kwf, kwb, gcol, grow)


def _head_norm(y, g):
    outs = []
    for hd in range(HEADS):
        yh = y[:, hd * DH:(hd + 1) * DH]
        mu = jnp.mean(yh, axis=1, keepdims=True)
        yc = yh - mu
        var = jnp.mean(yc * yc, axis=1, keepdims=True)
        outs.append(yc * lax.rsqrt(var + EPS))
    return jnp.concatenate(outs, axis=1) * g


def _out_proj_kernel(*refs, tr):
    nb = tr // CHUNK
    x_refs, refs = refs[:nb], refs[nb:]
    (head_ref, ret_ref, hm_ref, rg_ref, mo_ref, rgn_ref, mgn_ref, wo_ref,
     fg_ref, wrt_ref, h1_ref, u2_ref, aff_ref) = refs
    i = pl.program_id(1)
    row = lax.broadcasted_iota(jnp.int32, (tr, 1), 0) + i * tr
    real = row >= PAD
    w = HEADS * DH
    y_ret = _head_norm(ret_ref[0], rgn_ref[...]) * rg_ref[0].astype(F32)
    y_m = _head_norm(mo_ref[0].astype(F32) * hm_ref[0], mgn_ref[...])
    y_ret = jnp.where(real, y_ret, 0.0).astype(BF16)
    y_m = jnp.where(real, y_m, 0.0).astype(BF16)
    h = jnp.concatenate(_load_rows(x_refs, head_ref, i), axis=0)
    h1 = h + _dot(y_ret, wo_ref[0:w, :]) + _dot(y_m, wo_ref[w:2 * w, :])
    h1_ref[0] = h1
    u2 = _rms(h1, fg_ref[...])
    _to_segments(u2_ref, _pack_pairs(u2))
    logits = _dot_nt(wrt_ref[...], u2.astype(BF16))
    mx = jnp.max(logits, axis=0, keepdims=True)
    ex = jnp.exp(logits - mx)
    aff = ex / jnp.sum(ex, axis=0, keepdims=True)
    coln = lax.broadcasted_iota(jnp.int32, (1, tr), 1) + i * tr
    aff_ref[0] = jnp.where(coln >= PAD, aff, -1.0)


def _out_proj(x, head, ret, hm, rgs, mos, ret_gn_g, mlstm_gn_g, w_out, ffn_norm_g, w_router):
    b, s, d = x.shape
    t = CHUNK + s
    tr = _row_tile(t)
    n_tiles = t // tr
    nseg = d // 2 // LANES
    w = HEADS * DH
    seq = lambda bi, i: (bi, i, 0)
    const2 = lambda bi, i: (0, 0)
    return pl.pallas_call(
        functools.partial(_out_proj_kernel, tr=tr),
        grid=(b, n_tiles),
        in_specs=_x_specs(tr, d) + [
            pl.BlockSpec((CHUNK, d), const2),
            pl.BlockSpec((1, tr, w), seq),
            pl.BlockSpec((1, tr, w), seq),
            pl.BlockSpec((1, tr, w), seq),
            pl.BlockSpec((1, tr, w), seq),
            pl.BlockSpec((1, w), const2),
            pl.BlockSpec((1, w), const2),
            pl.BlockSpec((2 * w, d), const2),
            pl.BlockSpec((1, d), const2),
            pl.BlockSpec((N_EXPERTS, d), const2),
        ],
        out_specs=[
            pl.BlockSpec((1, tr, d), seq),
            pl.BlockSpec((tr // SUB, nseg, SUB, LANES), lambda bi, i: (bi * n_tiles + i, 0, 0, 0)),
            pl.BlockSpec((1, N_EXPERTS, tr), lambda bi, i: (bi, 0, i)),
        ],
        out_shape=[
            jax.ShapeDtypeStruct((b, t, d), F32),
            jax.ShapeDtypeStruct((b * t // SUB, nseg, SUB, LANES), jnp.int32),
            jax.ShapeDtypeStruct((b, N_EXPERTS, t), F32),
        ],
        compiler_params=_cparams(("parallel", "arbitrary")),
        name="out_proj",
    )(*([x] * (tr // CHUNK)), head, ret, hm, rgs, mos, ret_gn_g.reshape(1, w), mlstm_gn_g.reshape(1, w),
      w_out.astype(BF16), ffn_norm_g.reshape(1, d), w_router.T.astype(BF16))


def _select_kernel(aff_ref, pm_ref, dv_ref, cnt_ref, bt_ref, mc_ref, sel_ref, *, t, cap, tr):
    e_n = N_EXPERTS
    nblk = t // LANES
    bits = pltpu.bitcast(aff_ref[0], jnp.int32)
    capf = float(cap)

    def search(_, lohi):
        lo, hi = lohi
        mid = lo + ((hi - lo + 1) >> 1)
        cnt = jnp.sum((bits >= mid).astype(F32), axis=1, keepdims=True)
        ok = cnt >= capf
        return jnp.where(ok, mid, lo), jnp.where(ok, hi, mid - 1)

    lo0 = jnp.zeros((e_n, 1), jnp.int32)
    hi0 = jnp.full((e_n, 1), 0x3F800000, jnp.int32)
    thr, _ = lax.fori_loop(0, 31, search, (lo0, hi0))
    need = capf - jnp.sum((bits > thr).astype(F32), axis=1, keepdims=True)

    ci = lax.broadcasted_iota(jnp.int32, (LANES, LANES), 0)
    cj = lax.broadcasted_iota(jnp.int32, (LANES, LANES), 1)
    upp = (ci <= cj).astype(BF16)
    ei = lax.broadcasted_iota(jnp.int32, (e_n, e_n), 0)
    ej = lax.broadcasted_iota(jnp.int32, (e_n, e_n), 1)
    strict = (ej < ei).astype(BF16)

    carry = jnp.zeros((e_n, 1), F32)
    for j in range(nblk):
        sl = slice(j * LANES, (j + 1) * LANES)
        bb = bits[:, sl]
        eqf = (bb == thr).astype(F32)
        ceq = _dot(eqf.astype(BF16), upp) + carry
        carry = ceq[:, LANES - 1:LANES]
        sel_ref[:, sl] = jnp.where(bb > thr, 1.0, jnp.where(ceq <= need, eqf, 0.0))

    carry = jnp.zeros((e_n, 1), F32)
    tok = lax.broadcasted_iota(jnp.int32, (1, LANES), 1).astype(F32)
    lane_b = lax.broadcasted_iota(jnp.int32, (e_n, LANES), 1)
    btot = jnp.zeros((e_n, LANES), F32)
    tile_max = jnp.zeros((SUB, LANES), F32)
    lane_t = lax.broadcasted_iota(jnp.int32, (SUB, LANES), 1)
    for j in range(nblk):
        sl = slice(j * LANES, (j + 1) * LANES)
        selb = sel_ref[:, sl]
        selbf = selb.astype(BF16)
        pin = _dot(selbf, upp) + carry
        carry = pin[:, LANES - 1:LANES]
        btot = jnp.where(lane_b == j, carry, btot)
        pm_ref[0, :, sl] = jnp.where(selb > 0, pin, 0.0)
        rank = _dot(strict, selbf)
        dv_ref[0, :, sl] = (tok + float(j * LANES)) * float(N_EXPERTS) + rank
        cnt_row = jnp.sum(selb, axis=0, keepdims=True)
        cnt_ref[0, sl, :] = jnp.broadcast_to(cnt_row, (LANES, LANES)).T
        tile_max = jnp.where(lane_t == (j * LANES) // tr,
                             jnp.maximum(tile_max, jnp.max(cnt_row, axis=1, keepdims=True)), tile_max)
    bt_ref[0] = jnp.where(lane_b < nblk, btot, carry).astype(jnp.int32)
    mc_ref[0] = tile_max.astype(jnp.int32)


def _select(aff, cap):
    b, e_n, t = aff.shape
    tr = _row_tile(t)
    assert t // tr <= LANES
    spec = pl.BlockSpec((1, e_n, t), lambda bi: (bi, 0, 0))
    return pl.pallas_call(
        functools.partial(_select_kernel, t=t, cap=cap, tr=tr),
        grid=(b,),
        in_specs=[spec],
        out_specs=[
            spec,
            spec,
            pl.BlockSpec((1, t, LANES), lambda bi: (bi, 0, 0)),
            pl.BlockSpec((1, e_n, LANES), lambda bi: (bi, 0, 0)),
            pl.BlockSpec((1, SUB, LANES), lambda bi: (bi, 0, 0)),
        ],
        out_shape=[
            jax.ShapeDtypeStruct((b, e_n, t), F32),
            jax.ShapeDtypeStruct((b, e_n, t), F32),
            jax.ShapeDtypeStruct((b, t, LANES), F32),
            jax.ShapeDtypeStruct((b, e_n, LANES), jnp.int32),
            jax.ShapeDtypeStruct((b, SUB, LANES), jnp.int32),
        ],
        scratch_shapes=[pltpu.VMEM((e_n, t), F32)],
        compiler_params=_cparams(("parallel",)),
        name="select",
    )(aff)


def _compact_kernel(bt_ref, pm_ref, dv_ref, aff_ref, src_ref, dst_ref, gate_ref, accd_ref, accg_ref,
                    *, t, cap, c_pad, nct, nseg, batch):
    bi = batch
    e_n = N_EXPERTS
    nblk = t // LANES
    lane_e = lax.broadcasted_iota(jnp.int32, (LANES, LANES), 1)
    slot_l = lax.broadcasted_iota(jnp.int32, (1, LANES), 1)

    spr = LANES // nseg
    sh_spr = spr.bit_length() - 1
    sh_grp = (SUB * nseg).bit_length() - 1
    sub_i = lax.broadcasted_iota(jnp.int32, (SUB, LANES), 0)
    lane_i = lax.broadcasted_iota(jnp.int32, (SUB, LANES), 1)
    pick = (lane_i >> sh_spr) == sub_i
    pl_i = lax.broadcasted_iota(jnp.int32, (LANES, LANES), 0)
    pl_j = lax.broadcasted_iota(jnp.int32, (LANES, LANES), 1)
    perm = ((pl_i & (spr - 1)) == ((pl_j >> sh_grp) << 3) + (pl_j & (SUB - 1))).astype(BF16)
    seg_j = (lane_i & (SUB * nseg - 1)) >> 3

    def segment_tile(rows):
        spread = _dot3_right(jnp.where(pick, rows.astype(F32), 0.0), perm).astype(jnp.int32)
        return (((spread >> 3) * nseg + seg_j) << 3) + (spread & (SUB - 1))

    def expert(e):
        base = e * LANES

        def ctile(ct, j0):
            c0 = pl.multiple_of(ct * LANES, LANES)
            j_lo = lax.while_loop(
                lambda j: (j < nblk) & (bt_ref[base + jnp.minimum(j, nblk - 1)] <= c0),
                lambda j: j + 1, j0)
            j_hi = lax.while_loop(
                lambda j: (j < nblk) & (bt_ref[base + jnp.minimum(j, nblk - 1)] < c0 + LANES),
                lambda j: j + 1, j_lo)
            j_end = jnp.minimum(j_hi + 1, nblk)
            want = (lax.broadcasted_iota(jnp.int32, (LANES, 1), 0) + (c0 + 1)).astype(F32)

            def blk(j, acc):
                acc_d, acc_g = acc
                cols = pl.ds(pl.multiple_of(j * LANES, LANES), LANES)
                hit = pm_ref[0, e:e + 1, cols] == want
                acc_d = acc_d + jnp.where(hit, dv_ref[0, e:e + 1, cols], 0.0)
                acc_g = acc_g + jnp.where(hit, aff_ref[0, e:e + 1, cols], 0.0)
                return acc_d, acc_g

            zero = jnp.zeros((LANES, LANES), F32)
            acc_d, acc_g = lax.fori_loop(j_lo, j_end, blk, (zero, zero))
            accd_ref[e * nct + ct] = acc_d
            accg_ref[e * nct + ct] = acc_g
            return j_lo

        lax.fori_loop(0, nct, ctile, 0)

    for e in range(e_n):
        expert(e)

    def finish(ct, _):
        c0 = pl.multiple_of(ct * LANES, LANES)
        p = slot_l + (c0 - cap)
        is_pad = p >= 0
        gate_tile = jnp.zeros((LANES, LANES), F32)
        for e in range(e_n):
            v = jnp.sum(accd_ref[e * nct + ct].T, axis=0, keepdims=True).astype(jnp.int32)
            tok = v >> 4
            rank = v & (N_EXPERTS - 1)
            src = bi * t + jnp.where(is_pad, 0, tok)
            dst = jnp.where(is_pad, N_EXPERTS * t + e * (c_pad - cap) + p, rank * t + tok)
            out_rows = pl.ds(pl.multiple_of((e * nct + ct) * SUB, SUB), SUB)
            src_ref[0, out_rows, :] = segment_tile(src)
            dst_ref[0, out_rows, :] = segment_tile(dst)
            gcol = jnp.sum(accg_ref[e * nct + ct], axis=1, keepdims=True)
            gate_tile = jnp.where(lane_e == e, gcol, gate_tile)
        gate_ref[0, pl.ds(c0, LANES), :] = gate_tile
        return 0

    lax.fori_loop(0, nct, finish, 0)


def _compact(btot, pm, dv, aff, cap, c_pad, nseg, batch):
    _, e_n, t = aff.shape
    assert nseg in (1, 2, 4, 8)
    nct = -(-c_pad // LANES)
    tile_rows = e_n * nct * SUB
    spec = pl.BlockSpec((1, e_n, t), lambda i, bt: (0, 0, 0))
    ospec = pl.BlockSpec((1, tile_rows, LANES), lambda i, bt: (0, 0, 0))
    src, dst, gate = pl.pallas_call(
        functools.partial(_compact_kernel, t=t, cap=cap, c_pad=c_pad, nct=nct, nseg=nseg, batch=batch),
        grid_spec=pltpu.PrefetchScalarGridSpec(
            num_scalar_prefetch=1,
            grid=(1,),
            in_specs=[spec, spec, spec],
            out_specs=[ospec, ospec, pl.BlockSpec((1, nct * LANES, LANES), lambda i, bt: (0, 0, 0))],
            scratch_shapes=[pltpu.VMEM((e_n * nct, LANES, LANES), F32)] * 2,
        ),
        out_shape=[
            jax.ShapeDtypeStruct((1, tile_rows, LANES), jnp.int32),
            jax.ShapeDtypeStruct((1, tile_rows, LANES), jnp.int32),
            jax.ShapeDtypeStruct((1, nct * LANES, LANES), F32),
        ],
        compiler_params=_cparams(("arbitrary",)),
        name="compact",
    )(btot.reshape(e_n * LANES), pm, dv, aff)
    ids = lambda a: a.reshape(e_n, nct, SUB, LANES)[:, :, :nseg].reshape(e_n, -1)[:, :c_pad * nseg].reshape(-1)
    return ids(src), ids(dst), gate


def _sc_mesh():
    return plsc.VectorSubcoreMesh(core_axis_name="core", subcore_axis_name="subcore")


def _row_gather(table, idx):
    m = idx.shape[0]
    d = table.shape[1]

    @functools.partial(pl.kernel, out_type=jax.ShapeDtypeStruct((m, d), table.dtype),
                       mesh=_sc_mesh(), scratch_types=[])
    def gather_kernel(x_hbm, i_hbm, o_hbm):
        def body(i_vmem, o_vmem):
            pltpu.sync_copy(x_hbm.at[i_vmem.at[0]], o_vmem)

        pltpu.emit_pipeline(
            body,
            grid=(m // SC_WINDOW,),
            in_specs=[pl.BlockSpec((1, SC_WINDOW), lambda i: (0, i))],
            out_specs=[pl.BlockSpec((SC_WINDOW, d), lambda i: (i, 0))],
            core_axis_name=("core", "subcore"),
            dimension_semantics=(pltpu.PARALLEL,),
        )(i_hbm, o_hbm)

    return gather_kernel(table, idx.reshape(1, m))


def _row_scatter(rows, idx, n_out):
    m, d = rows.shape

    @functools.partial(pl.kernel, out_type=jax.ShapeDtypeStruct((n_out, d), rows.dtype),
                       mesh=_sc_mesh(), scratch_types=[])
    def scatter_kernel(x_hbm, i_hbm, o_hbm):
        def body(x_vmem, i_vmem):
            pltpu.sync_copy(x_vmem, o_hbm.at[i_vmem.at[0]])

        pltpu.emit_pipeline(
            body,
            grid=(m // SC_WINDOW,),
            in_specs=[pl.BlockSpec((SC_WINDOW, d), lambda i: (i, 0)),
                      pl.BlockSpec((1, SC_WINDOW), lambda i: (0, i))],
            out_specs=[],
            core_axis_name=("core", "subcore"),
            dimension_semantics=(pltpu.PARALLEL,),
        )(x_hbm, i_hbm)

    return scatter_kernel(rows, idx.reshape(1, m))


def _expert_kernel(gate_ref, x_ref, wg_ref, wu_ref, wd_ref, y_ref, xb_ref, acc_ref, *, c_pad, n_f):
    e = pl.program_id(0)
    f = pl.program_id(1)

    @pl.when(f == 0)
    def _():
        w = _from_segments(x_ref)
        xb_ref[...] = jnp.concatenate([_unpack_hi(w), _unpack_lo(w)], axis=1).astype(BF16)
        acc_ref[...] = jnp.zeros(acc_ref.shape, F32)

    wg = wg_ref[0].astype(BF16)
    wu = wu_ref[0].astype(BF16)
    wd = wd_ref[0].astype(BF16)
    half = c_pad // 2
    for r0 in (0, half):
        rows = slice(r0, r0 + half)
        x = xb_ref[rows, :]
        hid = _silu(_dot(x, wg)) * _dot(x, wu)
        part = _dot(hid.astype(BF16), wd)
        acc_ref[rows, :] = acc_ref[rows, :] + part

    @pl.when(f == n_f - 1)
    def _():
        lane = lax.broadcasted_iota(jnp.int32, (c_pad, LANES), 1)
        gcol = jnp.sum(jnp.where(lane == e, gate_ref[0], 0.0), axis=1, keepdims=True)
        _to_segments(y_ref, _pack_pairs(acc_ref[...] * gcol))


def _experts(gate, xs, w_gate, w_up, w_down):
    e_n, d, d_ff = w_gate.shape
    nseg = d // 2 // LANES
    c_pad = xs.shape[0] * SUB // e_n
    fc = 256
    n_f = d_ff // fc
    blk = pl.BlockSpec((c_pad // SUB, nseg, SUB, LANES), lambda e, f: (e, 0, 0, 0))
    return pl.pallas_call(
        functools.partial(_expert_kernel, c_pad=c_pad, n_f=n_f),
        grid=(e_n, n_f),
        in_specs=[
            pl.BlockSpec((1, c_pad, LANES), lambda e, f: (0, 0, 0)),
            blk,
            pl.BlockSpec((1, d, fc), lambda e, f: (e, 0, f)),
            pl.BlockSpec((1, d, fc), lambda e, f: (e, 0, f)),
            pl.BlockSpec((1, fc, d), lambda e, f: (e, f, 0)),
        ],
        out_specs=blk,
        out_shape=jax.ShapeDtypeStruct(xs.shape, jnp.int32),
        scratch_shapes=[pltpu.VMEM((c_pad, d), BF16), pltpu.VMEM((c_pad, d), F32)],
        compiler_params=_cparams(("parallel", "arbitrary")),
        name="experts",
    )(gate, xs, w_gate, w_up, w_down)


LAYERS_PER_STEP = 2


def _combine_kernel(mc_ref, h1_ref, fg_ref, *rest, n_steps, nb):
    cnt_refs = rest[:nb]
    z_refs = rest[nb:nb + nb * LAYERS_PER_STEP]
    o_ref, acc_ref = rest[-2], rest[-1]
    bi = pl.program_id(0)
    i = pl.program_id(1)
    j = pl.program_id(2)

    @pl.when(j == 0)
    def _():
        _to_segments(acc_ref, h1_ref[0])

    for k in range(nb):
        for l in range(LAYERS_PER_STEP):
            layer = j * LAYERS_PER_STEP + l

            @pl.when((bi == k) & (layer < mc_ref[bi * LANES + i]))
            def _():
                z_ref = z_refs[k * LAYERS_PER_STEP + l]
                g, nz = z_ref.shape[0], z_ref.shape[1]
                take = cnt_refs[k][0].reshape(g, 1, SUB, LANES) > layer.astype(F32)
                w = z_ref[...]
                acc_ref[:, 0:nz] = acc_ref[:, 0:nz] + jnp.where(take, _unpack_hi(w), 0.0)
                acc_ref[:, nz:2 * nz] = acc_ref[:, nz:2 * nz] + jnp.where(take, _unpack_lo(w), 0.0)

    @pl.when(j == n_steps - 1)
    def _():
        o_ref[0] = _rms(_from_segments(acc_ref), fg_ref[...])


def _combine(maxcnt, h1, zs, cnts, final_norm_g):
    b, t, d = h1.shape
    n_steps = N_EXPERTS // LAYERS_PER_STEP
    tr = _row_tile(t)
    n_tiles = t // tr

    def z_map(k, l):
        def index(bi, i, j, mc):
            layer = jnp.maximum(jnp.minimum(j * LAYERS_PER_STEP + l, mc[bi * LANES + i] - 1), 0)
            return (jnp.where(bi == k, layer * n_tiles + i, 0), 0, 0, 0)
        return index

    def cnt_map(k):
        return lambda bi, i, j, mc: (0, jnp.where(bi == k, i, 0), 0)

    seg_block = (tr // SUB, d // LANES, SUB, LANES)
    z_block = (tr // SUB, d // 2 // LANES, SUB, LANES)
    z_specs = [pl.BlockSpec(z_block, z_map(k, l)) for k in range(b) for l in range(LAYERS_PER_STEP)]
    z_args = [zs[k] for k in range(b) for _ in range(LAYERS_PER_STEP)]
    return pl.pallas_call(
        functools.partial(_combine_kernel, n_steps=n_steps, nb=b),
        grid_spec=pltpu.PrefetchScalarGridSpec(
            num_scalar_prefetch=1,
            grid=(b, n_tiles, n_steps),
            in_specs=[
                pl.BlockSpec((1, tr, d), lambda bi, i, j, mc: (bi, i, 0)),
                pl.BlockSpec((1, d), lambda bi, i, j, mc: (0, 0)),
            ] + [pl.BlockSpec((1, tr, LANES), cnt_map(k)) for k in range(b)] + z_specs,
            out_specs=pl.BlockSpec((1, tr, d), lambda bi, i, j, mc: (bi, i, 0)),
            scratch_shapes=[pltpu.VMEM(seg_block, F32)],
        ),
        out_shape=jax.ShapeDtypeStruct((b, t, d), F32),
        compiler_params=_cparams(("parallel", "arbitrary", "arbitrary")),
        name="combine",
    )(maxcnt, h1, final_norm_g.reshape(1, d), *cnts, *z_args)


def kernel(x, meta_tokens, mix_norm_g, w_in, b_gates, conv_w, conv_b, ret_decay_logit, ret_gn_g,
           mlstm_gn_g, w_out, ffn_norm_g, w_router, w_gate, w_up, w_down, final_norm_g):
    b, s, d = x.shape
    t = CHUNK + s
    n = N_META + s
    cap = 2 * n // N_EXPERTS
    nseg = d // 2 // LANES
    c_pad = next(c for c in range(-(-cap // 16) * 16, cap + 4096, 16)
                 if (N_EXPERTS * c * nseg) % (SC_WINDOW * SC_SUBCORES) == 0)

    head = jnp.concatenate([jnp.zeros((PAD, d), x.dtype), meta_tokens.astype(x.dtype)], axis=0)

    half = DH // 2
    pos = jnp.arange(t, dtype=F32) - PAD
    inv = ROPE_BASE ** (-jnp.arange(half, dtype=F32) / half)
    ang = pos[:, None] * inv[None, :]
    cosf = jnp.concatenate([jnp.cos(ang), jnp.cos(ang)], axis=1)
    sinf = jnp.concatenate([-jnp.sin(ang), jnp.sin(ang)], axis=1)

    assert mix_norm_g.shape[0] == 1, "single-layer block only"
    rq, rk, rv, rgs, mq, mk, mv, mos, kwf, kwb, gcol, grow = _in_proj(
        x, head, mix_norm_g[0], w_in[0], b_gates[0], conv_w[0], conv_b[0], cosf, sinf)
    lg = jax.nn.log_sigmoid(ret_decay_logit[0].astype(F32))
    ret = _retention(lg, rq, rk, rv)
    hm = _mlstm(mq, mk, mv, kwf, kwb, gcol, grow)
    h1, u2, aff = _out_proj(x, head, ret, hm, rgs, mos, ret_gn_g[0], mlstm_gn_g[0], w_out[0],
                            ffn_norm_g[0], w_router[0])
    tiled = lambda a: a.reshape(-1, nseg, SUB, LANES)
    flat = lambda a: a.reshape(-1, LANES)
    z_rows = -(-(N_EXPERTS * t + N_EXPERTS * (c_pad - cap)) // SUB) * SUB
    routed = []
    for bi in range(b):
        aff_b = aff[bi:bi + 1]
        pm, dv, cnt, btot, maxcnt = _select(aff_b, cap)
        src, dst, gate = _compact(btot, pm, dv, aff_b, cap, c_pad, nseg, bi)
        routed.append((_row_gather(flat(u2), src), dst, gate, cnt, maxcnt))
    zs = []
    for xs, dst, gate, _, _ in routed:
        ys = _experts(gate, tiled(xs), w_gate[0], w_up[0], w_down[0])
        zs.append(tiled(_row_scatter(flat(ys), dst, z_rows * nseg)))
    maxcnt = jnp.concatenate([r[4][:, 0, :] for r in routed], axis=0).reshape(-1)
    out = _combine(maxcnt, h1, zs, [r[3] for r in routed], final_norm_g)
    return out[:, CHUNK:]
```

```python
import functools

import jax
import jax.numpy as jnp
from jax import lax
from jax.experimental import pallas as pl
from jax.experimental.pallas import tpu as pltpu
from jax.experimental.pallas import tpu_sc as plsc

F32 = jnp.float32
BF16 = jnp.bfloat16

LANES = 128
CHUNK = 128
N_META = 16
PAD = CHUNK - N_META
HEADS = 4
DH = 128
N_EXPERTS = 16
CONV_K = 5
HALO = 8
SUB = 8
SC_WINDOW = 128
SC_SUBCORES = 32
EPS = 1e-6
NEG = -1e30
ROPE_BASE = 10000.0
VMEM_LIMIT = 56 * 1024 * 1024


def _cparams(sem, vmem=VMEM_LIMIT, **kw):
    return pltpu.CompilerParams(dimension_semantics=sem, vmem_limit_bytes=vmem, **kw)


def _dot(a, b):
    return jnp.dot(a, b, preferred_element_type=F32)


def _dot_nt(a, b):
    return lax.dot_general(a, b, (((1,), (1,)), ((), ())), preferred_element_type=F32)


def _dot_tn(a, b):
    return lax.dot_general(a, b, (((0,), (0,)), ((), ())), preferred_element_type=F32)


def _split3(x):
    hi = x.astype(BF16)
    r = x - hi.astype(F32)
    mid = r.astype(BF16)
    lo = (r - mid.astype(F32)).astype(BF16)
    return hi, mid, lo


def _dot3_left(m_bf, x):
    hi, mid, lo = _split3(x)
    return _dot(m_bf, hi) + _dot(m_bf, mid) + _dot(m_bf, lo)


def _dot3_right(x, m_bf):
    hi, mid, lo = _split3(x)
    return _dot(hi, m_bf) + _dot(mid, m_bf) + _dot(lo, m_bf)


def _rms(x, g):
    return x * lax.rsqrt(jnp.mean(x * x, axis=-1, keepdims=True) + EPS) * g


def _log_sigmoid(x):
    return jnp.minimum(x, 0.0) - jnp.log1p(jnp.exp(-jnp.abs(x)))


def _silu(x):
    return x * (1.0 / (1.0 + jnp.exp(-x)))


def _sigmoid(x):
    return 1.0 / (1.0 + jnp.exp(-x))


def _row_tile(t):
    return 640 if t % 640 == 0 else CHUNK


def _chunk_unroll(n_chunks):
    return 13 if n_chunks % 13 == 0 else 1


def _to_segments(dst_ref, x):
    r, w = x.shape
    for j in range(w // LANES):
        dst_ref[:, j] = x[:, j * LANES:(j + 1) * LANES].reshape(r // SUB, SUB, LANES)


def _from_segments(src_ref):
    g, nseg = src_ref.shape[0], src_ref.shape[1]
    return jnp.concatenate([src_ref[:, j].reshape(g * SUB, LANES) for j in range(nseg)], axis=1)


def _bf16_bits(x):
    return pltpu.bitcast(x.astype(BF16).astype(F32), jnp.int32)


def _pack_pairs(x):
    half = x.shape[1] // 2
    return _bf16_bits(x[:, :half]) | lax.shift_right_logical(_bf16_bits(x[:, half:]), 16)


def _unpack_hi(w):
    return pltpu.bitcast(w & -65536, F32)


def _unpack_lo(w):
    return pltpu.bitcast(w << 16, F32)


def _load_rows(x_refs, head_ref, i):
    blocks = [r[0] for r in x_refs]
    blocks[0] = jnp.where(i == 0, head_ref[...], blocks[0])
    return blocks


def _x_specs(tr, d):
    nb = tr // CHUNK
    return [pl.BlockSpec((1, CHUNK, d), functools.partial(
        lambda bi, i, k: (bi, jnp.maximum(i * nb - 1 + k, 0), 0), k=k)) for k in range(nb)]


def _in_proj_kernel(*refs, tr, n_tiles):
    nb = tr // CHUNK
    x_refs, refs = refs[:nb], refs[nb:]
    (head_ref, hp_ref, hn_ref, ng_ref, wr_ref, wm_ref, wg_ref, wgt_ref,
     bg_ref, bgt_ref, cw_ref, cb_ref, cos_ref, sin_ref,
     rq_ref, rk_ref, rv_ref, rg_ref, mq_ref, mk_ref, mv_ref, mo_ref,
     kwf_ref, kwb_ref, gc_ref, gr_ref, ext_ref, u_ref, gts_ref, h_ref) = refs
    i = pl.program_id(1)
    ng = ng_ref[...]
    w = HEADS * DH
    nd = 2 * HEADS
    for k, blk in enumerate(_load_rows(x_refs, head_ref, i)):
        h_ref[k * CHUNK:(k + 1) * CHUNK, :] = blk

    hp = hp_ref[0]
    if tr == CHUNK:
        hp = jnp.where(i == 1, head_ref[CHUNK - HALO:CHUNK, :], hp)
    up = _rms(hp, ng)
    un = _rms(hn_ref[0], ng)
    up = jnp.where(i == 0, 0.0, up)
    un = jnp.where(i == n_tiles - 1, 0.0, un)
    uh = jnp.concatenate([up, un], axis=0).astype(BF16)
    halo = _dot(uh, wm_ref[:, 0:2 * w])
    ext_ref[0:HALO, :] = halo[0:HALO]
    ext_ref[HALO + tr:2 * HALO + tr, :] = halo[HALO:2 * HALO]

    sub = tr // 2
    lane_s = lax.broadcasted_iota(jnp.int32, (sub, LANES), 1)
    for r0 in (0, sub):
        rs = slice(r0, r0 + sub)
        u = _rms(h_ref[rs, :], ng).astype(BF16)
        u_ref[rs, :] = u
        real = lax.broadcasted_iota(jnp.int32, (sub, 1), 0) + (i * tr + r0) >= PAD
        cosf = cos_ref[rs, :]
        sinf = sin_ref[rs, :]
        pq = _dot(u, wr_ref[:, 0:w])
        pk = _dot(u, wr_ref[:, w:2 * w])
        for hd in range(HEADS):
            sl = slice(hd * DH, (hd + 1) * DH)
            xq = pq[:, sl]
            xk = pk[:, sl]
            rq_ref[0, rs, sl] = ((xq * cosf + pltpu.roll(xq, DH // 2, 1) * sinf) * (DH ** -0.5)).astype(BF16)
            rk_ref[0, rs, sl] = (xk * cosf + pltpu.roll(xk, DH // 2, 1) * sinf).astype(BF16)
        rv_ref[0, rs, :] = _dot(u, wr_ref[:, 2 * w:3 * w]).astype(BF16)
        rg_ref[0, rs, :] = _silu(_dot(u, wr_ref[:, 3 * w:4 * w])).astype(BF16)
        ext_ref[HALO + r0:HALO + r0 + sub, :] = _dot(u, wm_ref[:, 0:2 * w])
        mv_ref[0, rs, :] = _dot(u, wm_ref[:, 2 * w:3 * w]).astype(BF16)
        mo_ref[0, rs, :] = _sigmoid(_dot(u, wm_ref[:, 3 * w:4 * w])).astype(BF16)
        gts = _dot(u, wg_ref[...]) + bg_ref[...]
        gts_ref[rs, :] = jnp.where(lane_s < nd, jnp.where(real, gts, NEG),
                                   jnp.where(real & (lane_s < 2 * nd), _log_sigmoid(gts), 0.0))

    rown = lax.broadcasted_iota(jnp.int32, (2 * nd, tr), 0)
    coln = lax.broadcasted_iota(jnp.int32, (1, tr), 1) + i * tr
    realr = coln >= PAD
    gtr = _dot_nt(wgt_ref[...], u_ref[...]) + bgt_ref[...]
    gtr = jnp.where(rown < nd, jnp.where(realr, gtr, NEG),
                    jnp.where(realr, _log_sigmoid(gtr), 0.0))
    ci = lax.broadcasted_iota(jnp.int32, (CHUNK, CHUNK), 0)
    cj = lax.broadcasted_iota(jnp.int32, (CHUNK, CHUNK), 1)
    causal = ci >= cj
    anti = ci <= cj
    low = causal.astype(BF16)
    upp = anti.astype(BF16)
    lane_c = lax.broadcasted_iota(jnp.int32, (CHUNK, LANES), 1)
    lane_1 = lax.broadcasted_iota(jnp.int32, (1, LANES), 1)
    rown_c = lax.broadcasted_iota(jnp.int32, (2 * nd, CHUNK), 0)
    for c in range(tr // CHUNK):
        rs = slice(c * CHUNK, (c + 1) * CHUNK)
        realc = lax.broadcasted_iota(jnp.int32, (CHUNK, 1), 0) + (i * tr + c * CHUNK) >= PAD
        conv = cb_ref[...] + jnp.zeros((CHUNK, 2 * w), F32)
        for k in range(CONV_K):
            off = HALO + c * CHUNK + k - CONV_K // 2
            conv = conv + cw_ref[k:k + 1, :] * ext_ref[off:off + CHUNK, :]
        qk = jnp.where(realc, _silu(conv), 0.0)
        mq_ref[0, rs, :] = (qk[:, 0:w] * (DH ** -0.5)).astype(BF16)
        mk_ref[0, rs, :] = qk[:, w:2 * w].astype(BF16)
        kconv = qk[:, w:2 * w]
        g = gts_ref[rs, :]
        lf = jnp.where(lane_c >= nd, g, 0.0)
        pre = _dot3_left(low, lf)
        suf = _dot3_left(upp, lf)
        bc = pltpu.roll(jnp.where(lane_c < nd + HEADS, pre, suf), LANES - nd, 1)
        blast = jnp.where(lane_1 < HEADS, bc[CHUNK - 1:CHUNK], bc[0:1])
        log_u = blast - bc + g
        a = jnp.max(log_u, axis=0, keepdims=True)
        wu = jnp.exp(log_u - a)
        gl = gtr[:, rs]
        lfr = jnp.where(rown_c >= nd, gl, 0.0)
        prer = _dot3_right(lfr, upp)
        sufr = _dot3_right(lfr, low)
        rr = gl[0:nd] - jnp.concatenate([prer[nd:nd + HEADS], sufr[nd + HEADS:2 * nd]], axis=0)
        for hd in range(HEADS):
            hs = slice(hd * DH, (hd + 1) * DH)
            cols = []
            for dr, (msk, kw_ref) in enumerate(((causal, kwf_ref), (anti, kwb_ref))):
                i8 = dr * HEADS + hd
                rmax = jnp.max(jnp.where(msk, rr[i8:i8 + 1, :], NEG), axis=1, keepdims=True)
                cols += [rmax, bc[:, i8:i8 + 1]]
                kw_ref[0, rs, hs] = (kconv[:, hs] * wu[:, i8:i8 + 1]).astype(BF16)
            gc_ref[0, hd, rs, :] = jnp.where(
                lane_c == 0, cols[0], jnp.where(lane_c == 1, cols[1], jnp.where(lane_c == 2, cols[2], cols[3])))
            scal = [jnp.broadcast_to(x[:, j:j + 1], (1, CHUNK))
                    for j in (hd, HEADS + hd) for x in (a, blast)]
            gr_ref[0, hd, :, rs] = jnp.concatenate(
                [rr[hd:hd + 1], rr[HEADS + hd:HEADS + hd + 1]] + scal + [jnp.zeros((2, CHUNK), F32)], axis=0)


def _in_proj(x, head, norm_g, w_in, b_gates, conv_w, conv_b, cosf, sinf):
    b, s, d = x.shape
    t = CHUNK + s
    tr = _row_tile(t)
    n_tiles = t // tr
    w = HEADS * DH
    wr = w_in[:, 0:4 * w].astype(BF16)
    wm = w_in[:, 4 * w:8 * w].astype(BF16)
    wg = jnp.pad(w_in[:, 8 * w:], ((0, 0), (0, LANES - 4 * HEADS))).astype(BF16)
    wgt = w_in[:, 8 * w:].T.astype(BF16)
    bg = jnp.pad(b_gates, (0, LANES - 4 * HEADS)).reshape(1, LANES)
    bgt = b_gates.reshape(4 * HEADS, 1)
    hb = tr // HALO
    pad_h = CHUNK // HALO
    last_h = s // HALO - 1
    seq = lambda bi, i: (bi, i, 0)
    const2 = lambda bi, i: (0, 0)
    out_bf = jax.ShapeDtypeStruct((b, t, w), BF16)
    kern = functools.partial(_in_proj_kernel, tr=tr, n_tiles=n_tiles)
    return pl.pallas_call(
        kern,
        grid=(b, n_tiles),
        in_specs=_x_specs(tr, d) + [
            pl.BlockSpec((CHUNK, d), const2),
            pl.BlockSpec((1, HALO, d), lambda bi, i: (bi, jnp.maximum(i * hb - pad_h - 1, 0), 0)),
            pl.BlockSpec((1, HALO, d), lambda bi, i: (bi, jnp.minimum((i + 1) * hb - pad_h, last_h), 0)),
            pl.BlockSpec((1, d), const2),
            pl.BlockSpec((d, 4 * w), const2),
            pl.BlockSpec((d, 4 * w), const2),
            pl.BlockSpec((d, LANES), const2),
            pl.BlockSpec((4 * HEADS, d), const2),
            pl.BlockSpec((1, LANES), const2),
            pl.BlockSpec((4 * HEADS, 1), const2),
            pl.BlockSpec((CONV_K, 2 * w), const2),
            pl.BlockSpec((1, 2 * w), const2),
            pl.BlockSpec((tr, DH), lambda bi, i: (i, 0)),
            pl.BlockSpec((tr, DH), lambda bi, i: (i, 0)),
        ],
        out_specs=[pl.BlockSpec((1, tr, w), seq)] * 10 + [
            pl.BlockSpec((1, HEADS, tr, LANES), lambda bi, i: (bi, 0, i, 0)),
            pl.BlockSpec((1, HEADS, 8, tr), lambda bi, i: (bi, 0, 0, i)),
        ],
        out_shape=[out_bf] * 10 + [
            jax.ShapeDtypeStruct((b, HEADS, t, LANES), F32),
            jax.ShapeDtypeStruct((b, HEADS, 8, t), F32),
        ],
        scratch_shapes=[pltpu.VMEM((tr + 2 * HALO, 2 * w), F32), pltpu.VMEM((tr, d), BF16),
                        pltpu.VMEM((tr, LANES), F32), pltpu.VMEM((tr, d), F32)],
        compiler_params=_cparams(("parallel", "arbitrary")),
        name="in_proj",
    )(*([x] * (tr // CHUNK)), head, x, x, norm_g.reshape(1, d), wr, wm, wg, wgt, bg, bgt,
      conv_w, conv_b.reshape(1, 2 * w), cosf, sinf)


def _retention_kernel(lg_ref, q_ref, k_ref, v_ref, o_ref, ob_ref, *, n_chunks):
    hd = pl.program_id(1)
    lgf = lg_ref[0, hd]
    lgb = lg_ref[1, hd]
    li = lax.broadcasted_iota(jnp.int32, (CHUNK, CHUNK), 0)
    mi = lax.broadcasted_iota(jnp.int32, (CHUNK, CHUNK), 1)
    diff = (li - mi).astype(F32)
    decay = (jnp.where(diff >= 0, jnp.exp(lgf * jnp.maximum(diff, 0.0)), 0.0)
             + jnp.where(diff <= 0, jnp.exp(lgb * jnp.maximum(-diff, 0.0)), 0.0))
    lcol = lax.broadcasted_iota(jnp.int32, (CHUNK, 1), 0).astype(F32)
    one = jnp.ones((1, 1), F32)
    zeta_f = jnp.exp(lgf * (CHUNK - 1.0 - lcol))
    xi_f = jnp.exp(lgf * (lcol + 1.0))
    g_f = jnp.exp(lgf * CHUNK * one)
    zeta_b = jnp.exp(lgb * lcol)
    xi_b = jnp.exp(lgb * (CHUNK - lcol))
    g_b = jnp.exp(lgb * CHUNK * one)

    def fwd(c, state):
        rows = pl.ds(pl.multiple_of(c * CHUNK, CHUNK), CHUNK)
        q = q_ref[0, rows, :]
        k = k_ref[0, rows, :]
        v = v_ref[0, rows, :]
        s = _dot_nt(q, k) * decay
        intra = _dot(s.astype(BF16), v)
        cross = _dot(q, state.astype(BF16)) * xi_f
        o_ref[0, rows, :] = intra + cross
        kz = (k.astype(F32) * zeta_f).astype(BF16)
        return g_f * state + _dot_tn(kz, v)

    def bwd(j, state):
        c = n_chunks - 1 - j
        rows = pl.ds(pl.multiple_of(c * CHUNK, CHUNK), CHUNK)
        q = q_ref[0, rows, :]
        k = k_ref[0, rows, :]
        v = v_ref[0, rows, :]
        ob_ref[rows, :] = _dot(q, state.astype(BF16)) * xi_b
        kz = (k.astype(F32) * zeta_b).astype(BF16)
        return g_b * state + _dot_tn(kz, v)

    zero = jnp.zeros((DH, DH), F32)
    lax.fori_loop(0, n_chunks, lambda j, st: (fwd(j, st[0]), bwd(j, st[1])), (zero, zero),
                  unroll=_chunk_unroll(n_chunks))
    o_ref[0] = o_ref[0] + ob_ref[...]


def _retention(lg, rq, rk, rv):
    b, t, _ = rq.shape
    spec = pl.BlockSpec((1, t, DH), lambda bi, hi, lg_ref: (bi, 0, hi))
    return pl.pallas_call(
        functools.partial(_retention_kernel, n_chunks=t // CHUNK),
        grid_spec=pltpu.PrefetchScalarGridSpec(
            num_scalar_prefetch=1,
            grid=(b, HEADS),
            in_specs=[spec, spec, spec],
            out_specs=spec,
            scratch_shapes=[pltpu.VMEM((t, DH), F32)],
        ),
        out_shape=jax.ShapeDtypeStruct((b, t, HEADS * DH), F32),
        compiler_params=_cparams(("parallel", "arbitrary")),
        name="retention",
    )(lg, rq, rk, rv)


def _mlstm_kernel(q_ref, k_ref, v_ref, kwf_ref, kwb_ref, gc_ref, gr_ref, o_ref, ob_ref, *, n_chunks):
    li = lax.broadcasted_iota(jnp.int32, (CHUNK, CHUNK), 0)
    mi = lax.broadcasted_iota(jnp.int32, (CHUNK, CHUNK), 1)
    causal = li >= mi
    anti = li <= mi

    ones = jnp.ones((CHUNK, DH), BF16)

    def step(c, carry, backward):
        st, m_st = carry
        rows = pl.ds(pl.multiple_of(c * CHUNK, CHUNK), CHUNK)
        q = q_ref[0, rows, :]
        k = k_ref[0, rows, :]
        v_aug = jnp.concatenate([v_ref[0, rows, :], ones], axis=1)
        gcol = gc_ref[0, 0, rows, :]
        grow = gr_ref[0, 0, :, rows]
        if backward:
            kw = kwb_ref[0, rows, :]
            r_max, bc = gcol[:, 2:3], gcol[:, 3:4]
            r_row, a, b_last = grow[1:2, :], grow[4:5, 0:1], grow[5:6, 0:1]
            mask = anti
        else:
            kw = kwf_ref[0, rows, :]
            r_max, bc = gcol[:, 0:1], gcol[:, 1:2]
            r_row, a, b_last = grow[0:1, :], grow[2:3, 0:1], grow[3:4, 0:1]
            mask = causal
        mm = jnp.maximum(jnp.broadcast_to(r_max, (CHUNK, DH)), m_st)
        s = _dot_nt(q, k) * jnp.where(mask, jnp.exp(r_row - mm), 0.0)
        w_inter = jnp.exp(m_st - mm)
        w2 = jnp.concatenate([w_inter, w_inter], axis=1)
        tot = _dot(s.astype(BF16), v_aug) + _dot(q, st.astype(BF16)) * w2
        floor = jnp.exp(-(jnp.broadcast_to(bc, (CHUNK, DH)) + mm))
        hval = tot[:, 0:DH] / jnp.maximum(jnp.abs(tot[:, DH:2 * DH]), floor)
        if backward:
            ob_ref[rows, :] = hval
        else:
            o_ref[0, rows, :] = hval
        upd = _dot_tn(kw, v_aug)
        m_new = jnp.maximum(b_last + m_st, a)
        f = jnp.exp(b_last + m_st - m_new)
        g = jnp.exp(a - m_new)
        return f * st + g * upd, m_new

    def both(j, carry):
        return step(j, carry[0], False), step(n_chunks - 1 - j, carry[1], True)

    init = (jnp.zeros((DH, 2 * DH), F32), jnp.zeros((1, 1), F32))
    lax.fori_loop(0, n_chunks, both, (init, init), unroll=_chunk_unroll(n_chunks))
    o_ref[0] = o_ref[0] + ob_ref[...]


def _mlstm(mq, mk, mv, kwf, kwb, gcol, grow):
    b, t, _ = mq.shape
    spec = pl.BlockSpec((1, t, DH), lambda bi, hi: (bi, 0, hi))
    return pl.pallas_call(
        functools.partial(_mlstm_kernel, n_chunks=t // CHUNK),
        grid=(b, HEADS),
        in_specs=[spec, spec, spec, spec, spec,
                  pl.BlockSpec((1, 1, t, LANES), lambda bi, hi: (bi, hi, 0, 0)),
                  pl.BlockSpec((1, 1, 8, t), lambda bi, hi: (bi, hi, 0, 0))],
        out_specs=spec,
        out_shape=jax.ShapeDtypeStruct((b, t, HEADS * DH), F32),
        scratch_shapes=[pltpu.VMEM((t, DH), F32)],
        compiler_params=_cparams(("parallel", "arbitrary")),
        name="mlstm",
    )(mq, mk, mv, kwf, kwb, gcol, grow)


def _head_norm(y, g):
    outs = []
    for hd in range(HEADS):
        yh = y[:, hd * DH:(hd + 1) * DH]
        mu = jnp.mean(yh, axis=1, keepdims=True)
        yc = yh - mu
        var = jnp.mean(yc * yc, axis=1, keepdims=True)
        outs.append(yc * lax.rsqrt(var + EPS))
    return jnp.concatenate(outs, axis=1) * g


def _out_proj_kernel(*refs, tr):
    nb = tr // CHUNK
    x_refs, refs = refs[:nb], refs[nb:]
    (head_ref, ret_ref, hm_ref, rg_ref, mo_ref, rgn_ref, mgn_ref, wo_ref,
     fg_ref, wrt_ref, h1_ref, u2_ref, aff_ref) = refs
    i = pl.program_id(1)
    row = lax.broadcasted_iota(jnp.int32, (tr, 1), 0) + i * tr
    real = row >= PAD
    w = HEADS * DH
    y_ret = _head_norm(ret_ref[0], rgn_ref[...]) * rg_ref[0].astype(F32)
    y_m = _head_norm(mo_ref[0].astype(F32) * hm_ref[0], mgn_ref[...])
    y_ret = jnp.where(real, y_ret, 0.0).astype(BF16)
    y_m = jnp.where(real, y_m, 0.0).astype(BF16)
    h = jnp.concatenate(_load_rows(x_refs, head_ref, i), axis=0)
    h1 = h + _dot(y_ret, wo_ref[0:w, :]) + _dot(y_m, wo_ref[w:2 * w, :])
    h1_ref[0] = h1
    u2 = _rms(h1, fg_ref[...])
    _to_segments(u2_ref, _pack_pairs(u2))
    logits = _dot_nt(wrt_ref[...], u2.astype(BF16))
    mx = jnp.max(logits, axis=0, keepdims=True)
    ex = jnp.exp(logits - mx)
    aff = ex / jnp.sum(ex, axis=0, keepdims=True)
    coln = lax.broadcasted_iota(jnp.int32, (1, tr), 1) + i * tr
    aff_ref[0] = jnp.where(coln >= PAD, aff, -1.0)


def _out_proj(x, head, ret, hm, rgs, mos, ret_gn_g, mlstm_gn_g, w_out, ffn_norm_g, w_router):
    b, s, d = x.shape
    t = CHUNK + s
    tr = _row_tile(t)
    n_tiles = t // tr
    nseg = d // 2 // LANES
    w = HEADS * DH
    seq = lambda bi, i: (bi, i, 0)
    const2 = lambda bi, i: (0, 0)
    return pl.pallas_call(
        functools.partial(_out_proj_kernel, tr=tr),
        grid=(b, n_tiles),
        in_specs=_x_specs(tr, d) + [
            pl.BlockSpec((CHUNK, d), const2),
            pl.BlockSpec((1, tr, w), seq),
            pl.BlockSpec((1, tr, w), seq),
            pl.BlockSpec((1, tr, w), seq),
            pl.BlockSpec((1, tr, w), seq),
            pl.BlockSpec((1, w), const2),
            pl.BlockSpec((1, w), const2),
            pl.BlockSpec((2 * w, d), const2),
            pl.BlockSpec((1, d), const2),
            pl.BlockSpec((N_EXPERTS, d), const2),
        ],
        out_specs=[
            pl.BlockSpec((1, tr, d), seq),
            pl.BlockSpec((tr // SUB, nseg, SUB, LANES), lambda bi, i: (bi * n_tiles + i, 0, 0, 0)),
            pl.BlockSpec((1, N_EXPERTS, tr), lambda bi, i: (bi, 0, i)),
        ],
        out_shape=[
            jax.ShapeDtypeStruct((b, t, d), F32),
            jax.ShapeDtypeStruct((b * t // SUB, nseg, SUB, LANES), jnp.int32),
            jax.ShapeDtypeStruct((b, N_EXPERTS, t), F32),
        ],
        compiler_params=_cparams(("parallel", "arbitrary")),
        name="out_proj",
    )(*([x] * (tr // CHUNK)), head, ret, hm, rgs, mos, ret_gn_g.reshape(1, w), mlstm_gn_g.reshape(1, w),
      w_out.astype(BF16), ffn_norm_g.reshape(1, d), w_router.T.astype(BF16))


def _select_kernel(aff_ref, pm_ref, dv_ref, cnt_ref, bt_ref, mc_ref, sel_ref, *, t, cap, tr):
    e_n = N_EXPERTS
    nblk = t // LANES
    bits = pltpu.bitcast(aff_ref[0], jnp.int32)
    capf = float(cap)

    def search(_, lohi):
        lo, hi = lohi
        mid = lo + ((hi - lo + 1) >> 1)
        cnt = jnp.sum((bits >= mid).astype(F32), axis=1, keepdims=True)
        ok = cnt >= capf
        return jnp.where(ok, mid, lo), jnp.where(ok, hi, mid - 1)

    lo0 = jnp.zeros((e_n, 1), jnp.int32)
    hi0 = jnp.full((e_n, 1), 0x3F800000, jnp.int32)
    thr, _ = lax.fori_loop(0, 31, search, (lo0, hi0))
    need = capf - jnp.sum((bits > thr).astype(F32), axis=1, keepdims=True)

    ci = lax.broadcasted_iota(jnp.int32, (LANES, LANES), 0)
    cj = lax.broadcasted_iota(jnp.int32, (LANES, LANES), 1)
    upp = (ci <= cj).astype(BF16)
    ei = lax.broadcasted_iota(jnp.int32, (e_n, e_n), 0)
    ej = lax.broadcasted_iota(jnp.int32, (e_n, e_n), 1)
    strict = (ej < ei).astype(BF16)

    carry = jnp.zeros((e_n, 1), F32)
    for j in range(nblk):
        sl = slice(j * LANES, (j + 1) * LANES)
        bb = bits[:, sl]
        eqf = (bb == thr).astype(F32)
        ceq = _dot(eqf.astype(BF16), upp) + carry
        carry = ceq[:, LANES - 1:LANES]
        sel_ref[:, sl] = jnp.where(bb > thr, 1.0, jnp.where(ceq <= need, eqf, 0.0))

    carry = jnp.zeros((e_n, 1), F32)
    tok = lax.broadcasted_iota(jnp.int32, (1, LANES), 1).astype(F32)
    lane_b = lax.broadcasted_iota(jnp.int32, (e_n, LANES), 1)
    btot = jnp.zeros((e_n, LANES), F32)
    tile_max = jnp.zeros((SUB, LANES), F32)
    lane_t = lax.broadcasted_iota(jnp.int32, (SUB, LANES), 1)
    for j in range(nblk):
        sl = slice(j * LANES, (j + 1) * LANES)
        selb = sel_ref[:, sl]
        selbf = selb.astype(BF16)
        pin = _dot(selbf, upp) + carry
        carry = pin[:, LANES - 1:LANES]
        btot = jnp.where(lane_b == j, carry, btot)
        pm_ref[0, :, sl] = jnp.where(selb > 0, pin, 0.0)
        rank = _dot(strict, selbf)
        dv_ref[0, :, sl] = (tok + float(j * LANES)) * float(N_EXPERTS) + rank
        cnt_row = jnp.sum(selb, axis=0, keepdims=True)
        cnt_ref[0, sl, :] = jnp.broadcast_to(cnt_row, (LANES, LANES)).T
        tile_max = jnp.where(lane_t == (j * LANES) // tr,
                             jnp.maximum(tile_max, jnp.max(cnt_row, axis=1, keepdims=True)), tile_max)
    bt_ref[0] = jnp.where(lane_b < nblk, btot, carry).astype(jnp.int32)
    mc_ref[0] = tile_max.astype(jnp.int32)


def _select(aff, cap):
    b, e_n, t = aff.shape
    tr = _row_tile(t)
    assert t // tr <= LANES
    spec = pl.BlockSpec((1, e_n, t), lambda bi: (bi, 0, 0))
    return pl.pallas_call(
        functools.partial(_select_kernel, t=t, cap=cap, tr=tr),
        grid=(b,),
        in_specs=[spec],
        out_specs=[
            spec,
            spec,
            pl.BlockSpec((1, t, LANES), lambda bi: (bi, 0, 0)),
            pl.BlockSpec((1, e_n, LANES), lambda bi: (bi, 0, 0)),
            pl.BlockSpec((1, SUB, LANES), lambda bi: (bi, 0, 0)),
        ],
        out_shape=[
            jax.ShapeDtypeStruct((b, e_n, t), F32),
            jax.ShapeDtypeStruct((b, e_n, t), F32),
            jax.ShapeDtypeStruct((b, t, LANES), F32),
            jax.ShapeDtypeStruct((b, e_n, LANES), jnp.int32),
            jax.ShapeDtypeStruct((b, SUB, LANES), jnp.int32),
        ],
        scratch_shapes=[pltpu.VMEM((e_n, t), F32)],
        compiler_params=_cparams(("parallel",)),
        name="select",
    )(aff)


def _compact_kernel(bt_ref, pm_ref, dv_ref, aff_ref, src_ref, dst_ref, gate_ref, accd_ref, accg_ref,
                    *, t, cap, c_pad, nct, nseg, batch):
    bi = batch
    e_n = N_EXPERTS
    nblk = t // LANES
    lane_e = lax.broadcasted_iota(jnp.int32, (LANES, LANES), 1)
    slot_l = lax.broadcasted_iota(jnp.int32, (1, LANES), 1)

    spr = LANES // nseg
    sh_spr = spr.bit_length() - 1
    sh_grp = (SUB * nseg).bit_length() - 1
    sub_i = lax.broadcasted_iota(jnp.int32, (SUB, LANES), 0)
    lane_i = lax.broadcasted_iota(jnp.int32, (SUB, LANES), 1)
    pick = (lane_i >> sh_spr) == sub_i
    pl_i = lax.broadcasted_iota(jnp.int32, (LANES, LANES), 0)
    pl_j = lax.broadcasted_iota(jnp.int32, (LANES, LANES), 1)
    perm = ((pl_i & (spr - 1)) == ((pl_j >> sh_grp) << 3) + (pl_j & (SUB - 1))).astype(BF16)
    seg_j = (lane_i & (SUB * nseg - 1)) >> 3

    def segment_tile(rows):
        spread = _dot3_right(jnp.where(pick, rows.astype(F32), 0.0), perm).astype(jnp.int32)
        return (((spread >> 3) * nseg + seg_j) << 3) + (spread & (SUB - 1))

    def expert(e):
        base = e * LANES

        def ctile(ct, j0):
            c0 = pl.multiple_of(ct * LANES, LANES)
            j_lo = lax.while_loop(
                lambda j: (j < nblk) & (bt_ref[base + jnp.minimum(j, nblk - 1)] <= c0),
                lambda j: j + 1, j0)
            j_hi = lax.while_loop(
                lambda j: (j < nblk) & (bt_ref[base + jnp.minimum(j, nblk - 1)] < c0 + LANES),
                lambda j: j + 1, j_lo)
            j_end = jnp.minimum(j_hi + 1, nblk)
            want = (lax.broadcasted_iota(jnp.int32, (LANES, 1), 0) + (c0 + 1)).astype(F32)

            def blk(j, acc):
                acc_d, acc_g = acc
                cols = pl.ds(pl.multiple_of(j * LANES, LANES), LANES)
                hit = pm_ref[0, e:e + 1, cols] == want
                acc_d = acc_d + jnp.where(hit, dv_ref[0, e:e + 1, cols], 0.0)
                acc_g = acc_g + jnp.where(hit, aff_ref[0, e:e + 1, cols], 0.0)
                return acc_d, acc_g

            zero = jnp.zeros((LANES, LANES), F32)
            acc_d, acc_g = lax.fori_loop(j_lo, j_end, blk, (zero, zero))
            accd_ref[e * nct + ct] = acc_d
            accg_ref[e * nct + ct] = acc_g
            return j_lo

        lax.fori_loop(0, nct, ctile, 0)

    for e in range(e_n):
        expert(e)

    def finish(ct, _):
        c0 = pl.multiple_of(ct * LANES, LANES)
        p = slot_l + (c0 - cap)
        is_pad = p >= 0
        gate_tile = jnp.zeros((LANES, LANES), F32)
        for e in range(e_n):
            v = jnp.sum(accd_ref[e * nct + ct].T, axis=0, keepdims=True).astype(jnp.int32)
            tok = v >> 4
            rank = v & (N_EXPERTS - 1)
            src = bi * t + jnp.where(is_pad, 0, tok)
            dst = jnp.where(is_pad, N_EXPERTS * t + e * (c_pad - cap) + p, rank * t + tok)
            out_rows = pl.ds(pl.multiple_of((e * nct + ct) * SUB, SUB), SUB)
            src_ref[0, out_rows, :] = segment_tile(src)
            dst_ref[0, out_rows, :] = segment_tile(dst)
            gcol = jnp.sum(accg_ref[e * nct + ct], axis=1, keepdims=True)
            gate_tile = jnp.where(lane_e == e, gcol, gate_tile)
        gate_ref[0, pl.ds(c0, LANES), :] = gate_tile
        return 0

    lax.fori_loop(0, nct, finish, 0)


def _compact(btot, pm, dv, aff, cap, c_pad, nseg, batch):
    _, e_n, t = aff.shape
    assert nseg in (1, 2, 4, 8)
    nct = -(-c_pad // LANES)
    tile_rows = e_n * nct * SUB
    spec = pl.BlockSpec((1, e_n, t), lambda i, bt: (0, 0, 0))
    ospec = pl.BlockSpec((1, tile_rows, LANES), lambda i, bt: (0, 0, 0))
    src, dst, gate = pl.pallas_call(
        functools.partial(_compact_kernel, t=t, cap=cap, c_pad=c_pad, nct=nct, nseg=nseg, batch=batch),
        grid_spec=pltpu.PrefetchScalarGridSpec(
            num_scalar_prefetch=1,
            grid=(1,),
            in_specs=[spec, spec, spec],
            out_specs=[ospec, ospec, pl.BlockSpec((1, nct * LANES, LANES), lambda i, bt: (0, 0, 0))],
            scratch_shapes=[pltpu.VMEM((e_n * nct, LANES, LANES), F32)] * 2,
        ),
        out_shape=[
            jax.ShapeDtypeStruct((1, tile_rows, LANES), jnp.int32),
            jax.ShapeDtypeStruct((1, tile_rows, LANES), jnp.int32),
            jax.ShapeDtypeStruct((1, nct * LANES, LANES), F32),
        ],
        compiler_params=_cparams(("arbitrary",)),
        name="compact",
    )(btot.reshape(e_n * LANES), pm, dv, aff)
    ids = lambda a: a.reshape(e_n, nct, SUB, LANES)[:, :, :nseg].reshape(e_n, -1)[:, :c_pad * nseg].reshape(-1)
    return ids(src), ids(dst), gate


def _sc_mesh():
    return plsc.VectorSubcoreMesh(core_axis_name="core", subcore_axis_name="subcore")


def _row_gather(table, idx):
    m = idx.shape[0]
    d = table.shape[1]

    @functools.partial(pl.kernel, out_type=jax.ShapeDtypeStruct((m, d), table.dtype),
                       mesh=_sc_mesh(), scratch_types=[])
    def gather_kernel(x_hbm, i_hbm, o_hbm):
        def body(i_vmem, o_vmem):
            pltpu.sync_copy(x_hbm.at[i_vmem.at[0]], o_vmem)

        pltpu.emit_pipeline(
            body,
            grid=(m // SC_WINDOW,),
            in_specs=[pl.BlockSpec((1, SC_WINDOW), lambda i: (0, i))],
            out_specs=[pl.BlockSpec((SC_WINDOW, d), lambda i: (i, 0))],
            core_axis_name=("core", "subcore"),
            dimension_semantics=(pltpu.PARALLEL,),
        )(i_hbm, o_hbm)

    return gather_kernel(table, idx.reshape(1, m))


def _row_scatter(rows, idx, n_out):
    m, d = rows.shape

    @functools.partial(pl.kernel, out_type=jax.ShapeDtypeStruct((n_out, d), rows.dtype),
                       mesh=_sc_mesh(), scratch_types=[])
    def scatter_kernel(x_hbm, i_hbm, o_hbm):
        def body(x_vmem, i_vmem):
            pltpu.sync_copy(x_vmem, o_hbm.at[i_vmem.at[0]])

        pltpu.emit_pipeline(
            body,
            grid=(m // SC_WINDOW,),
            in_specs=[pl.BlockSpec((SC_WINDOW, d), lambda i: (i, 0)),
                      pl.BlockSpec((1, SC_WINDOW), lambda i: (0, i))],
            out_specs=[],
            core_axis_name=("core", "subcore"),
            dimension_semantics=(pltpu.PARALLEL,),
        )(x_hbm, i_hbm)

    return scatter_kernel(rows, idx.reshape(1, m))


def _expert_kernel(gate_ref, x_ref, wg_ref, wu_ref, wd_ref, y_ref, xb_ref, acc_ref, *, c_pad, n_f):
    e = pl.program_id(0)
    f = pl.program_id(1)

    @pl.when(f == 0)
    def _():
        w = _from_segments(x_ref)
        xb_ref[...] = jnp.concatenate([_unpack_hi(w), _unpack_lo(w)], axis=1).astype(BF16)
        acc_ref[...] = jnp.zeros(acc_ref.shape, F32)

    wg = wg_ref[0].astype(BF16)
    wu = wu_ref[0].astype(BF16)
    wd = wd_ref[0].astype(BF16)
    half = c_pad // 2
    for r0 in (0, half):
        rows = slice(r0, r0 + half)
        x = xb_ref[rows, :]
        hid = _silu(_dot(x, wg)) * _dot(x, wu)
        part = _dot(hid.astype(BF16), wd)
        acc_ref[rows, :] = acc_ref[rows, :] + part

    @pl.when(f == n_f - 1)
    def _():
        lane = lax.broadcasted_iota(jnp.int32, (c_pad, LANES), 1)
        gcol = jnp.sum(jnp.where(lane == e, gate_ref[0], 0.0), axis=1, keepdims=True)
        _to_segments(y_ref, _pack_pairs(acc_ref[...] * gcol))


def _experts(gate, xs, w_gate, w_up, w_down):
    e_n, d, d_ff = w_gate.shape
    nseg = d // 2 // LANES
    c_pad = xs.shape[0] * SUB // e_n
    fc = 256
    n_f = d_ff // fc
    blk = pl.BlockSpec((c_pad // SUB, nseg, SUB, LANES), lambda e, f: (e, 0, 0, 0))
    return pl.pallas_call(
        functools.partial(_expert_kernel, c_pad=c_pad, n_f=n_f),
        grid=(e_n, n_f),
        in_specs=[
            pl.BlockSpec((1, c_pad, LANES), lambda e, f: (0, 0, 0)),
            blk,
            pl.BlockSpec((1, d, fc), lambda e, f: (e, 0, f)),
            pl.BlockSpec((1, d, fc), lambda e, f: (e, 0, f)),
            pl.BlockSpec((1, fc, d), lambda e, f: (e, f, 0)),
        ],
        out_specs=blk,
        out_shape=jax.ShapeDtypeStruct(xs.shape, jnp.int32),
        scratch_shapes=[pltpu.VMEM((c_pad, d), BF16), pltpu.VMEM((c_pad, d), F32)],
        compiler_params=_cparams(("parallel", "arbitrary")),
        name="experts",
    )(gate, xs, w_gate, w_up, w_down)


LAYERS_PER_STEP = 2


def _combine_kernel(mc_ref, h1_ref, fg_ref, *rest, n_steps, nb):
    cnt_refs = rest[:nb]
    z_refs = rest[nb:nb + nb * LAYERS_PER_STEP]
    o_ref, acc_ref = rest[-2], rest[-1]
    bi = pl.program_id(0)
    i = pl.program_id(1)
    j = pl.program_id(2)

    @pl.when(j == 0)
    def _():
        _to_segments(acc_ref, h1_ref[0])

    def add_layers(k, n_take):
        g, nz = z_refs[0].shape[0], z_refs[0].shape[1]
        cnt = cnt_refs[k][0].reshape(g, 1, SUB, LANES)
        hi = acc_ref[:, 0:nz]
        lo = acc_ref[:, nz:2 * nz]
        for l in range(n_take):
            take = cnt > (j * LAYERS_PER_STEP + l).astype(F32)
            w = z_refs[k * LAYERS_PER_STEP + l][...]
            hi = hi + jnp.where(take, _unpack_hi(w), 0.0)
            lo = lo + jnp.where(take, _unpack_lo(w), 0.0)
        acc_ref[:, 0:nz] = hi
        acc_ref[:, nz:2 * nz] = lo

    live = mc_ref[bi * LANES + i] - j * LAYERS_PER_STEP
    for k in range(nb):
        for n_take in range(1, LAYERS_PER_STEP + 1):
            last = n_take == LAYERS_PER_STEP
            pl.when((bi == k) & ((live >= n_take) if last else (live == n_take)))(
                functools.partial(add_layers, k, n_take))

    @pl.when(j == n_steps - 1)
    def _():
        o_ref[0] = _rms(_from_segments(acc_ref), fg_ref[...])


def _combine(maxcnt, h1, zs, cnts, final_norm_g):
    b, t, d = h1.shape
    n_steps = N_EXPERTS // LAYERS_PER_STEP
    tr = _row_tile(t)
    n_tiles = t // tr

    def z_map(k, l):
        def index(bi, i, j, mc):
            layer = jnp.maximum(jnp.minimum(j * LAYERS_PER_STEP + l, mc[bi * LANES + i] - 1), 0)
            return (jnp.where(bi == k, layer * n_tiles + i, 0), 0, 0, 0)
        return index

    def cnt_map(k):
        return lambda bi, i, j, mc: (0, jnp.where(bi == k, i, 0), 0)

    seg_block = (tr // SUB, d // LANES, SUB, LANES)
    z_block = (tr // SUB, d // 2 // LANES, SUB, LANES)
    z_specs = [pl.BlockSpec(z_block, z_map(k, l)) for k in range(b) for l in range(LAYERS_PER_STEP)]
    z_args = [zs[k] for k in range(b) for _ in range(LAYERS_PER_STEP)]
    return pl.pallas_call(
        functools.partial(_combine_kernel, n_steps=n_steps, nb=b),
        grid_spec=pltpu.PrefetchScalarGridSpec(
            num_scalar_prefetch=1,
            grid=(b, n_tiles, n_steps),
            in_specs=[
                pl.BlockSpec((1, tr, d), lambda bi, i, j, mc: (bi, i, 0)),
                pl.BlockSpec((1, d), lambda bi, i, j, mc: (0, 0)),
            ] + [pl.BlockSpec((1, tr, LANES), cnt_map(k)) for k in range(b)] + z_specs,
            out_specs=pl.BlockSpec((1, tr, d), lambda bi, i, j, mc: (bi, i, 0)),
            scratch_shapes=[pltpu.VMEM(seg_block, F32)],
        ),
        out_shape=jax.ShapeDtypeStruct((b, t, d), F32),
        compiler_params=_cparams(("parallel", "arbitrary", "arbitrary")),
        name="combine",
    )(maxcnt, h1, final_norm_g.reshape(1, d), *cnts, *z_args)


def kernel(x, meta_tokens, mix_norm_g, w_in, b_gates, conv_w, conv_b, ret_decay_logit, ret_gn_g,
           mlstm_gn_g, w_out, ffn_norm_g, w_router, w_gate, w_up, w_down, final_norm_g):
    b, s, d = x.shape
    t = CHUNK + s
    n = N_META + s
    cap = 2 * n // N_EXPERTS
    nseg = d // 2 // LANES
    c_pad = next(c for c in range(-(-cap // 16) * 16, cap + 4096, 16)
                 if (N_EXPERTS * c * nseg) % (SC_WINDOW * SC_SUBCORES) == 0)

    head = jnp.concatenate([jnp.zeros((PAD, d), x.dtype), meta_tokens.astype(x.dtype)], axis=0)

    half = DH // 2
    pos = jnp.arange(t, dtype=F32) - PAD
    inv = ROPE_BASE ** (-jnp.arange(half, dtype=F32) / half)
    ang = pos[:, None] * inv[None, :]
    cosf = jnp.concatenate([jnp.cos(ang), jnp.cos(ang)], axis=1)
    sinf = jnp.concatenate([-jnp.sin(ang), jnp.sin(ang)], axis=1)

    assert mix_norm_g.shape[0] == 1, "single-layer block only"
    rq, rk, rv, rgs, mq, mk, mv, mos, kwf, kwb, gcol, grow = _in_proj(
        x, head, mix_norm_g[0], w_in[0], b_gates[0], conv_w[0], conv_b[0], cosf, sinf)
    lg = jax.nn.log_sigmoid(ret_decay_logit[0].astype(F32))
    ret = _retention(lg, rq, rk, rv)
    hm = _mlstm(mq, mk, mv, kwf, kwb, gcol, grow)
    h1, u2, aff = _out_proj(x, head, ret, hm, rgs, mos, ret_gn_g[0], mlstm_gn_g[0], w_out[0],
                            ffn_norm_g[0], w_router[0])
    tiled = lambda a: a.reshape(-1, nseg, SUB, LANES)
    flat = lambda a: a.reshape(-1, LANES)
    z_rows = -(-(N_EXPERTS * t + N_EXPERTS * (c_pad - cap)) // SUB) * SUB
    routed = []
    for bi in range(b):
        aff_b = aff[bi:bi + 1]
        pm, dv, cnt, btot, maxcnt = _select(aff_b, cap)
        src, dst, gate = _compact(btot, pm, dv, aff_b, cap, c_pad, nseg, bi)
        routed.append((_row_gather(flat(u2), src), dst, gate, cnt, maxcnt))
    zs = []
    for xs, dst, gate, _, _ in routed:
        ys = _experts(gate, tiled(xs), w_gate[0], w_up[0], w_down[0])
        zs.append(tiled(_row_scatter(flat(ys), dst, z_rows * nseg)))
    maxcnt = jnp.concatenate([r[4][:, 0, :] for r in routed], axis=0).reshape(-1)
    out = _combine(maxcnt, h1, zs, [r[3] for r in routed], final_norm_g)
    return out[:, CHUNK:]
```

```python
import functools

import jax
import jax.numpy as jnp
from jax import lax
from jax.experimental import pallas as pl
from jax.experimental.pallas import tpu as pltpu
from jax.experimental.pallas import tpu_sc as plsc

F32 = jnp.float32
BF16 = jnp.bfloat16

LANES = 128
CHUNK = 128
N_META = 16
PAD = CHUNK - N_META
HEADS = 4
DH = 128
N_EXPERTS = 16
CONV_K = 5
HALO = 8
SUB = 8
SC_WINDOW = 128
SC_SUBCORES = 32
EPS = 1e-6
NEG = -1e30
ROPE_BASE = 10000.0
VMEM_LIMIT = 56 * 1024 * 1024


def _cparams(sem, vmem=VMEM_LIMIT, **kw):
    return pltpu.CompilerParams(dimension_semantics=sem, vmem_limit_bytes=vmem, **kw)


def _dot(a, b):
    return jnp.dot(a, b, preferred_element_type=F32)


def _dot_nt(a, b):
    return lax.dot_general(a, b, (((1,), (1,)), ((), ())), preferred_element_type=F32)


def _dot_tn(a, b):
    return lax.dot_general(a, b, (((0,), (0,)), ((), ())), preferred_element_type=F32)


def _split3(x):
    hi = x.astype(BF16)
    r = x - hi.astype(F32)
    mid = r.astype(BF16)
    lo = (r - mid.astype(F32)).astype(BF16)
    return hi, mid, lo


def _dot3_left(m_bf, x):
    hi, mid, lo = _split3(x)
    return _dot(m_bf, hi) + _dot(m_bf, mid) + _dot(m_bf, lo)


def _dot3_right(x, m_bf):
    hi, mid, lo = _split3(x)
    return _dot(hi, m_bf) + _dot(mid, m_bf) + _dot(lo, m_bf)


def _rms(x, g):
    return x * lax.rsqrt(jnp.mean(x * x, axis=-1, keepdims=True) + EPS) * g


def _log_sigmoid(x):
    return jnp.minimum(x, 0.0) - jnp.log1p(jnp.exp(-jnp.abs(x)))


def _silu(x):
    return x * (1.0 / (1.0 + jnp.exp(-x)))


def _sigmoid(x):
    return 1.0 / (1.0 + jnp.exp(-x))


def _row_tile(t):
    return 640 if t % 640 == 0 else CHUNK


def _chunk_unroll(n_chunks):
    return 13 if n_chunks % 13 == 0 else 1


def _to_segments(dst_ref, x):
    r, w = x.shape
    for j in range(w // LANES):
        dst_ref[:, j] = x[:, j * LANES:(j + 1) * LANES].reshape(r // SUB, SUB, LANES)


def _from_segments(src_ref):
    g, nseg = src_ref.shape[0], src_ref.shape[1]
    return jnp.concatenate([src_ref[:, j].reshape(g * SUB, LANES) for j in range(nseg)], axis=1)


def _bf16_bits(x):
    return pltpu.bitcast(x.astype(BF16).astype(F32), jnp.int32)


def _pack_pairs(x):
    half = x.shape[1] // 2
    return _bf16_bits(x[:, :half]) | lax.shift_right_logical(_bf16_bits(x[:, half:]), 16)


def _unpack_hi(w):
    return pltpu.bitcast(w & -65536, F32)


def _unpack_lo(w):
    return pltpu.bitcast(w << 16, F32)


def _load_rows(x_refs, head_ref, i):
    blocks = [r[0] for r in x_refs]
    blocks[0] = jnp.where(i == 0, head_ref[...], blocks[0])
    return blocks


def _x_specs(tr, d):
    nb = tr // CHUNK
    return [pl.BlockSpec((1, CHUNK, d), functools.partial(
        lambda bi, i, k: (bi, jnp.maximum(i * nb - 1 + k, 0), 0), k=k)) for k in range(nb)]


def _in_proj_kernel(*refs, tr, n_tiles):
    nb = tr // CHUNK
    x_refs, refs = refs[:nb], refs[nb:]
    (head_ref, hp_ref, hn_ref, ng_ref, wr_ref, wm_ref, wg_ref, wgt_ref,
     bg_ref, bgt_ref, cw_ref, cb_ref, cos_ref, sin_ref,
     rq_ref, rk_ref, rv_ref, rg_ref, mq_ref, mk_ref, mv_ref, mo_ref,
     kwf_ref, kwb_ref, gc_ref, gr_ref, ext_ref, u_ref, gts_ref, h_ref) = refs
    i = pl.program_id(1)
    ng = ng_ref[...]
    w = HEADS * DH
    nd = 2 * HEADS
    for k, blk in enumerate(_load_rows(x_refs, head_ref, i)):
        h_ref[k * CHUNK:(k + 1) * CHUNK, :] = blk

    hp = hp_ref[0]
    if tr == CHUNK:
        hp = jnp.where(i == 1, head_ref[CHUNK - HALO:CHUNK, :], hp)
    up = _rms(hp, ng)
    un = _rms(hn_ref[0], ng)
    up = jnp.where(i == 0, 0.0, up)
    un = jnp.where(i == n_tiles - 1, 0.0, un)
    uh = jnp.concatenate([up, un], axis=0).astype(BF16)
    halo = _dot(uh, wm_ref[:, 0:2 * w])
    ext_ref[0:HALO, :] = halo[0:HALO]
    ext_ref[HALO + tr:2 * HALO + tr, :] = halo[HALO:2 * HALO]

    sub = tr // 2
    lane_s = lax.broadcasted_iota(jnp.int32, (sub, LANES), 1)
    for r0 in (0, sub):
        rs = slice(r0, r0 + sub)
        u = _rms(h_ref[rs, :], ng).astype(BF16)
        u_ref[rs, :] = u
        real = lax.broadcasted_iota(jnp.int32, (sub, 1), 0) + (i * tr + r0) >= PAD
        cosf = cos_ref[rs, :]
        sinf = sin_ref[rs, :]
        pq = _dot(u, wr_ref[:, 0:w])
        pk = _dot(u, wr_ref[:, w:2 * w])
        for hd in range(HEADS):
            sl = slice(hd * DH, (hd + 1) * DH)
            xq = pq[:, sl]
            xk = pk[:, sl]
            rq_ref[0, rs, sl] = ((xq * cosf + pltpu.roll(xq, DH // 2, 1) * sinf) * (DH ** -0.5)).astype(BF16)
            rk_ref[0, rs, sl] = (xk * cosf + pltpu.roll(xk, DH // 2, 1) * sinf).astype(BF16)
        rv_ref[0, rs, :] = _dot(u, wr_ref[:, 2 * w:3 * w]).astype(BF16)
        rg_ref[0, rs, :] = _silu(_dot(u, wr_ref[:, 3 * w:4 * w])).astype(BF16)
        ext_ref[HALO + r0:HALO + r0 + sub, :] = _dot(u, wm_ref[:, 0:2 * w])
        mv_ref[0, rs, :] = _dot(u, wm_ref[:, 2 * w:3 * w]).astype(BF16)
        mo_ref[0, rs, :] = _sigmoid(_dot(u, wm_ref[:, 3 * w:4 * w])).astype(BF16)
        gts = _dot(u, wg_ref[...]) + bg_ref[...]
        gts_ref[rs, :] = jnp.where(lane_s < nd, jnp.where(real, gts, NEG),
                                   jnp.where(real & (lane_s < 2 * nd), _log_sigmoid(gts), 0.0))

    rown = lax.broadcasted_iota(jnp.int32, (2 * nd, tr), 0)
    coln = lax.broadcasted_iota(jnp.int32, (1, tr), 1) + i * tr
    realr = coln >= PAD
    gtr = _dot_nt(wgt_ref[...], u_ref[...]) + bgt_ref[...]
    gtr = jnp.where(rown < nd, jnp.where(realr, gtr, NEG),
                    jnp.where(realr, _log_sigmoid(gtr), 0.0))
    ci = lax.broadcasted_iota(jnp.int32, (CHUNK, CHUNK), 0)
    cj = lax.broadcasted_iota(jnp.int32, (CHUNK, CHUNK), 1)
    causal = ci >= cj
    anti = ci <= cj
    low = causal.astype(BF16)
    upp = anti.astype(BF16)
    lane_c = lax.broadcasted_iota(jnp.int32, (CHUNK, LANES), 1)
    lane_1 = lax.broadcasted_iota(jnp.int32, (1, LANES), 1)
    rown_c = lax.broadcasted_iota(jnp.int32, (2 * nd, CHUNK), 0)
    for c in range(tr // CHUNK):
        rs = slice(c * CHUNK, (c + 1) * CHUNK)
        realc = lax.broadcasted_iota(jnp.int32, (CHUNK, 1), 0) + (i * tr + c * CHUNK) >= PAD
        conv = cb_ref[...] + jnp.zeros((CHUNK, 2 * w), F32)
        for k in range(CONV_K):
            off = HALO + c * CHUNK + k - CONV_K // 2
            conv = conv + cw_ref[k:k + 1, :] * ext_ref[off:off + CHUNK, :]
        qk = jnp.where(realc, _silu(conv), 0.0)
        mq_ref[0, rs, :] = (qk[:, 0:w] * (DH ** -0.5)).astype(BF16)
        mk_ref[0, rs, :] = qk[:, w:2 * w].astype(BF16)
        kconv = qk[:, w:2 * w]
        g = gts_ref[rs, :]
        lf = jnp.where(lane_c >= nd, g, 0.0)
        pre = _dot3_left(low, lf)
        suf = _dot3_left(upp, lf)
        bc = pltpu.roll(jnp.where(lane_c < nd + HEADS, pre, suf), LANES - nd, 1)
        blast = jnp.where(lane_1 < HEADS, bc[CHUNK - 1:CHUNK], bc[0:1])
        log_u = blast - bc + g
        a = jnp.max(log_u, axis=0, keepdims=True)
        wu = jnp.exp(log_u - a)
        gl = gtr[:, rs]
        lfr = jnp.where(rown_c >= nd, gl, 0.0)
        prer = _dot3_right(lfr, upp)
        sufr = _dot3_right(lfr, low)
        rr = gl[0:nd] - jnp.concatenate([prer[nd:nd + HEADS], sufr[nd + HEADS:2 * nd]], axis=0)
        for hd in range(HEADS):
            hs = slice(hd * DH, (hd + 1) * DH)
            cols = []
            for dr, (msk, kw_ref) in enumerate(((causal, kwf_ref), (anti, kwb_ref))):
                i8 = dr * HEADS + hd
                rmax = jnp.max(jnp.where(msk, rr[i8:i8 + 1, :], NEG), axis=1, keepdims=True)
                cols += [rmax, bc[:, i8:i8 + 1]]
                kw_ref[0, rs, hs] = (kconv[:, hs] * wu[:, i8:i8 + 1]).astype(BF16)
            gc_ref[0, hd, rs, :] = jnp.where(
                lane_c == 0, cols[0], jnp.where(lane_c == 1, cols[1], jnp.where(lane_c == 2, cols[2], cols[3])))
            scal = [jnp.broadcast_to(x[:, j:j + 1], (1, CHUNK))
                    for j in (hd, HEADS + hd) for x in (a, blast)]
            gr_ref[0, hd, :, rs] = jnp.concatenate(
                [rr[hd:hd + 1], rr[HEADS + hd:HEADS + hd + 1]] + scal + [jnp.zeros((2, CHUNK), F32)], axis=0)


def _in_proj(x, head, norm_g, w_in, b_gates, conv_w, conv_b, cosf, sinf):
    b, s, d = x.shape
    t = CHUNK + s
    tr = _row_tile(t)
    n_tiles = t // tr
    w = HEADS * DH
    wr = w_in[:, 0:4 * w].astype(BF16)
    wm = w_in[:, 4 * w:8 * w].astype(BF16)
    wg = jnp.pad(w_in[:, 8 * w:], ((0, 0), (0, LANES - 4 * HEADS))).astype(BF16)
    wgt = w_in[:, 8 * w:].T.astype(BF16)
    bg = jnp.pad(b_gates, (0, LANES - 4 * HEADS)).reshape(1, LANES)
    bgt = b_gates.reshape(4 * HEADS, 1)
    hb = tr // HALO
    pad_h = CHUNK // HALO
    last_h = s // HALO - 1
    seq = lambda bi, i: (bi, i, 0)
    const2 = lambda bi, i: (0, 0)
    out_bf = jax.ShapeDtypeStruct((b, t, w), BF16)
    kern = functools.partial(_in_proj_kernel, tr=tr, n_tiles=n_tiles)
    return pl.pallas_call(
        kern,
        grid=(b, n_tiles),
        in_specs=_x_specs(tr, d) + [
            pl.BlockSpec((CHUNK, d), const2),
            pl.BlockSpec((1, HALO, d), lambda bi, i: (bi, jnp.maximum(i * hb - pad_h - 1, 0), 0)),
            pl.BlockSpec((1, HALO, d), lambda bi, i: (bi, jnp.minimum((i + 1) * hb - pad_h, last_h), 0)),
            pl.BlockSpec((1, d), const2),
            pl.BlockSpec((d, 4 * w), const2),
            pl.BlockSpec((d, 4 * w), const2),
            pl.BlockSpec((d, LANES), const2),
            pl.BlockSpec((4 * HEADS, d), const2),
            pl.BlockSpec((1, LANES), const2),
            pl.BlockSpec((4 * HEADS, 1), const2),
            pl.BlockSpec((CONV_K, 2 * w), const2),
            pl.BlockSpec((1, 2 * w), const2),
            pl.BlockSpec((tr, DH), lambda bi, i: (i, 0)),
            pl.BlockSpec((tr, DH), lambda bi, i: (i, 0)),
        ],
        out_specs=[pl.BlockSpec((1, tr, w), seq)] * 10 + [
            pl.BlockSpec((1, HEADS, tr, LANES), lambda bi, i: (bi, 0, i, 0)),
            pl.BlockSpec((1, HEADS, 8, tr), lambda bi, i: (bi, 0, 0, i)),
        ],
        out_shape=[out_bf] * 10 + [
            jax.ShapeDtypeStruct((b, HEADS, t, LANES), F32),
            jax.ShapeDtypeStruct((b, HEADS, 8, t), F32),
        ],
        scratch_shapes=[pltpu.VMEM((tr + 2 * HALO, 2 * w), F32), pltpu.VMEM((tr, d), BF16),
                        pltpu.VMEM((tr, LANES), F32), pltpu.VMEM((tr, d), F32)],
        compiler_params=_cparams(("parallel", "arbitrary")),
        name="in_proj",
    )(*([x] * (tr // CHUNK)), head, x, x, norm_g.reshape(1, d), wr, wm, wg, wgt, bg, bgt,
      conv_w, conv_b.reshape(1, 2 * w), cosf, sinf)


def _retention_kernel(lg_ref, q_ref, k_ref, v_ref, o_ref, ob_ref, *, n_chunks):
    hd = pl.program_id(1)
    lgf = lg_ref[0, hd]
    lgb = lg_ref[1, hd]
    li = lax.broadcasted_iota(jnp.int32, (CHUNK, CHUNK), 0)
    mi = lax.broadcasted_iota(jnp.int32, (CHUNK, CHUNK), 1)
    diff = (li - mi).astype(F32)
    decay = (jnp.where(diff >= 0, jnp.exp(lgf * jnp.maximum(diff, 0.0)), 0.0)
             + jnp.where(diff <= 0, jnp.exp(lgb * jnp.maximum(-diff, 0.0)), 0.0))
    lcol = lax.broadcasted_iota(jnp.int32, (CHUNK, 1), 0).astype(F32)
    one = jnp.ones((1, 1), F32)
    zeta_f = jnp.exp(lgf * (CHUNK - 1.0 - lcol))
    xi_f = jnp.exp(lgf * (lcol + 1.0))
    g_f = jnp.exp(lgf * CHUNK * one)
    zeta_b = jnp.exp(lgb * lcol)
    xi_b = jnp.exp(lgb * (CHUNK - lcol))
    g_b = jnp.exp(lgb * CHUNK * one)

    def fwd(c, state):
        rows = pl.ds(pl.multiple_of(c * CHUNK, CHUNK), CHUNK)
        q = q_ref[0, rows, :]
        k = k_ref[0, rows, :]
        v = v_ref[0, rows, :]
        s = _dot_nt(q, k) * decay
        intra = _dot(s.astype(BF16), v)
        cross = _dot(q, state.astype(BF16)) * xi_f
        o_ref[0, rows, :] = intra + cross
        kz = (k.astype(F32) * zeta_f).astype(BF16)
        return g_f * state + _dot_tn(kz, v)

    def bwd(j, state):
        c = n_chunks - 1 - j
        rows = pl.ds(pl.multiple_of(c * CHUNK, CHUNK), CHUNK)
        q = q_ref[0, rows, :]
        k = k_ref[0, rows, :]
        v = v_ref[0, rows, :]
        ob_ref[rows, :] = _dot(q, state.astype(BF16)) * xi_b
        kz = (k.astype(F32) * zeta_b).astype(BF16)
        return g_b * state + _dot_tn(kz, v)

    zero = jnp.zeros((DH, DH), F32)
    lax.fori_loop(0, n_chunks, lambda j, st: (fwd(j, st[0]), bwd(j, st[1])), (zero, zero),
                  unroll=_chunk_unroll(n_chunks))
    o_ref[0] = o_ref[0] + ob_ref[...]


def _retention(lg, rq, rk, rv):
    b, t, _ = rq.shape
    spec = pl.BlockSpec((1, t, DH), lambda bi, hi, lg_ref: (bi, 0, hi))
    return pl.pallas_call(
        functools.partial(_retention_kernel, n_chunks=t // CHUNK),
        grid_spec=pltpu.PrefetchScalarGridSpec(
            num_scalar_prefetch=1,
            grid=(b, HEADS),
            in_specs=[spec, spec, spec],
            out_specs=spec,
            scratch_shapes=[pltpu.VMEM((t, DH), F32)],
        ),
        out_shape=jax.ShapeDtypeStruct((b, t, HEADS * DH), F32),
        compiler_params=_cparams(("parallel", "arbitrary")),
        name="retention",
    )(lg, rq, rk, rv)


def _mlstm_kernel(q_ref, k_ref, v_ref, kwf_ref, kwb_ref, gc_ref, gr_ref, o_ref, ob_ref, *, n_chunks):
    li = lax.broadcasted_iota(jnp.int32, (CHUNK, CHUNK), 0)
    mi = lax.broadcasted_iota(jnp.int32, (CHUNK, CHUNK), 1)
    causal = li >= mi
    anti = li <= mi

    ones = jnp.ones((CHUNK, DH), BF16)

    def step(c, carry, backward):
        st, m_st = carry
        rows = pl.ds(pl.multiple_of(c * CHUNK, CHUNK), CHUNK)
        q = q_ref[0, rows, :]
        k = k_ref[0, rows, :]
        v_aug = jnp.concatenate([v_ref[0, rows, :], ones], axis=1)
        gcol = gc_ref[0, 0, rows, :]
        grow = gr_ref[0, 0, :, rows]
        if backward:
            kw = kwb_ref[0, rows, :]
            r_max, bc = gcol[:, 2:3], gcol[:, 3:4]
            r_row, a, b_last = grow[1:2, :], grow[4:5, 0:1], grow[5:6, 0:1]
            mask = anti
        else:
            kw = kwf_ref[0, rows, :]
            r_max, bc = gcol[:, 0:1], gcol[:, 1:2]
            r_row, a, b_last = grow[0:1, :], grow[2:3, 0:1], grow[3:4, 0:1]
            mask = causal
        mm = jnp.maximum(jnp.broadcast_to(r_max, (CHUNK, DH)), m_st)
        s = _dot_nt(q, k) * jnp.where(mask, jnp.exp(r_row - mm), 0.0)
        w_inter = jnp.exp(m_st - mm)
        w2 = jnp.concatenate([w_inter, w_inter], axis=1)
        tot = _dot(s.astype(BF16), v_aug) + _dot(q, st.astype(BF16)) * w2
        floor = jnp.exp(-(jnp.broadcast_to(bc, (CHUNK, DH)) + mm))
        hval = tot[:, 0:DH] / jnp.maximum(jnp.abs(tot[:, DH:2 * DH]), floor)
        if backward:
            ob_ref[rows, :] = hval
        else:
            o_ref[0, rows, :] = hval
        upd = _dot_tn(kw, v_aug)
        m_new = jnp.maximum(b_last + m_st, a)
        f = jnp.exp(b_last + m_st - m_new)
        g = jnp.exp(a - m_new)
        return f * st + g * upd, m_new

    def both(j, carry):
        return step(j, carry[0], False), step(n_chunks - 1 - j, carry[1], True)

    init = (jnp.zeros((DH, 2 * DH), F32), jnp.zeros((1, 1), F32))
    lax.fori_loop(0, n_chunks, both, (init, init), unroll=_chunk_unroll(n_chunks))
    o_ref[0] = o_ref[0] + ob_ref[...]


def _mlstm(mq, mk, mv, kwf, kwb, gcol, grow):
    b, t, _ = mq.shape
    spec = pl.BlockSpec((1, t, DH), lambda bi, hi: (bi, 0, hi))
    return pl.pallas_call(
        functools.partial(_mlstm_kernel, n_chunks=t // CHUNK),
        grid=(b, HEADS),
        in_specs=[spec, spec, spec, spec, spec,
                  pl.BlockSpec((1, 1, t, LANES), lambda bi, hi: (bi, hi, 0, 0)),
                  pl.BlockSpec((1, 1, 8, t), lambda bi, hi: (bi, hi, 0, 0))],
        out_specs=spec,
        out_shape=jax.ShapeDtypeStruct((b, t, HEADS * DH), F32),
        scratch_shapes=[pltpu.VMEM((t, DH), F32)],
        compiler_params=_cparams(("parallel", "arbitrary")),
        name="mlstm",
    )(mq, mk, mv, kwf, kwb, gcol, grow)


def _head_norm(y, g):
    outs = []
    for hd in range(HEADS):
        yh = y[:, hd * DH:(hd + 1) * DH]
        mu = jnp.mean(yh, axis=1, keepdims=True)
        yc = yh - mu
        var = jnp.mean(yc * yc, axis=1, keepdims=True)
        outs.append(yc * lax.rsqrt(var + EPS))
    return jnp.concatenate(outs, axis=1) * g


def _out_proj_kernel(*refs, tr):
    nb = tr // CHUNK
    x_refs, refs = refs[:nb], refs[nb:]
    (head_ref, ret_ref, hm_ref, rg_ref, mo_ref, rgn_ref, mgn_ref, wo_ref,
     fg_ref, wrt_ref, h1_ref, u2_ref, aff_ref) = refs
    i = pl.program_id(1)
    row = lax.broadcasted_iota(jnp.int32, (tr, 1), 0) + i * tr
    real = row >= PAD
    w = HEADS * DH
    y_ret = _head_norm(ret_ref[0], rgn_ref[...]) * rg_ref[0].astype(F32)
    y_m = _head_norm(mo_ref[0].astype(F32) * hm_ref[0], mgn_ref[...])
    y_ret = jnp.where(real, y_ret, 0.0).astype(BF16)
    y_m = jnp.where(real, y_m, 0.0).astype(BF16)
    h = jnp.concatenate(_load_rows(x_refs, head_ref, i), axis=0)
    h1 = h + _dot(y_ret, wo_ref[0:w, :]) + _dot(y_m, wo_ref[w:2 * w, :])
    h1_ref[0] = h1
    u2 = _rms(h1, fg_ref[...])
    _to_segments(u2_ref, _pack_pairs(u2))
    logits = _dot_nt(wrt_ref[...], u2.astype(BF16))
    mx = jnp.max(logits, axis=0, keepdims=True)
    ex = jnp.exp(logits - mx)
    aff = ex / jnp.sum(ex, axis=0, keepdims=True)
    coln = lax.broadcasted_iota(jnp.int32, (1, tr), 1) + i * tr
    aff_ref[0] = jnp.where(coln >= PAD, aff, -1.0)


def _out_proj(x, head, ret, hm, rgs, mos, ret_gn_g, mlstm_gn_g, w_out, ffn_norm_g, w_router):
    b, s, d = x.shape
    t = CHUNK + s
    tr = _row_tile(t)
    n_tiles = t // tr
    nseg = d // 2 // LANES
    w = HEADS * DH
    seq = lambda bi, i: (bi, i, 0)
    const2 = lambda bi, i: (0, 0)
    return pl.pallas_call(
        functools.partial(_out_proj_kernel, tr=tr),
        grid=(b, n_tiles),
        in_specs=_x_specs(tr, d) + [
            pl.BlockSpec((CHUNK, d), const2),
            pl.BlockSpec((1, tr, w), seq),
            pl.BlockSpec((1, tr, w), seq),
            pl.BlockSpec((1, tr, w), seq),
            pl.BlockSpec((1, tr, w), seq),
            pl.BlockSpec((1, w), const2),
            pl.BlockSpec((1, w), const2),
            pl.BlockSpec((2 * w, d), const2),
            pl.BlockSpec((1, d), const2),
            pl.BlockSpec((N_EXPERTS, d), const2),
        ],
        out_specs=[
            pl.BlockSpec((1, tr, d), seq),
            pl.BlockSpec((tr // SUB, nseg, SUB, LANES), lambda bi, i: (bi * n_tiles + i, 0, 0, 0)),
            pl.BlockSpec((1, N_EXPERTS, tr), lambda bi, i: (bi, 0, i)),
        ],
        out_shape=[
            jax.ShapeDtypeStruct((b, t, d), F32),
            jax.ShapeDtypeStruct((b * t // SUB, nseg, SUB, LANES), jnp.int32),
            jax.ShapeDtypeStruct((b, N_EXPERTS, t), F32),
        ],
        compiler_params=_cparams(("parallel", "arbitrary")),
        name="out_proj",
    )(*([x] * (tr // CHUNK)), head, ret, hm, rgs, mos, ret_gn_g.reshape(1, w), mlstm_gn_g.reshape(1, w),
      w_out.astype(BF16), ffn_norm_g.reshape(1, d), w_router.T.astype(BF16))


def _select_kernel(aff_ref, pm_ref, dv_ref, cnt_ref, bt_ref, mc_ref, sel_ref, *, t, cap, tr):
    e_n = N_EXPERTS
    nblk = t // LANES
    bits = pltpu.bitcast(aff_ref[0], jnp.int32)
    capf = float(cap)

    def search(_, lohi):
        lo, hi = lohi
        mid = lo + ((hi - lo + 1) >> 1)
        cnt = jnp.sum((bits >= mid).astype(F32), axis=1, keepdims=True)
        ok = cnt >= capf
        return jnp.where(ok, mid, lo), jnp.where(ok, hi, mid - 1)

    lo0 = jnp.zeros((e_n, 1), jnp.int32)
    hi0 = jnp.full((e_n, 1), 0x3F800000, jnp.int32)
    thr, _ = lax.fori_loop(0, 31, search, (lo0, hi0))
    need = capf - jnp.sum((bits > thr).astype(F32), axis=1, keepdims=True)

    ci = lax.broadcasted_iota(jnp.int32, (LANES, LANES), 0)
    cj = lax.broadcasted_iota(jnp.int32, (LANES, LANES), 1)
    upp = (ci <= cj).astype(BF16)
    ei = lax.broadcasted_iota(jnp.int32, (e_n, e_n), 0)
    ej = lax.broadcasted_iota(jnp.int32, (e_n, e_n), 1)
    strict = (ej < ei).astype(BF16)

    carry = jnp.zeros((e_n, 1), F32)
    for j in range(nblk):
        sl = slice(j * LANES, (j + 1) * LANES)
        bb = bits[:, sl]
        eqf = (bb == thr).astype(F32)
        ceq = _dot(eqf.astype(BF16), upp) + carry
        carry = ceq[:, LANES - 1:LANES]
        sel_ref[:, sl] = jnp.where(bb > thr, 1.0, jnp.where(ceq <= need, eqf, 0.0))

    carry = jnp.zeros((e_n, 1), F32)
    tok = lax.broadcasted_iota(jnp.int32, (1, LANES), 1).astype(F32)
    lane_b = lax.broadcasted_iota(jnp.int32, (e_n, LANES), 1)
    btot = jnp.zeros((e_n, LANES), F32)
    tile_max = jnp.zeros((SUB, LANES), F32)
    lane_t = lax.broadcasted_iota(jnp.int32, (SUB, LANES), 1)
    for j in range(nblk):
        sl = slice(j * LANES, (j + 1) * LANES)
        selb = sel_ref[:, sl]
        selbf = selb.astype(BF16)
        pin = _dot(selbf, upp) + carry
        carry = pin[:, LANES - 1:LANES]
        btot = jnp.where(lane_b == j, carry, btot)
        pm_ref[0, :, sl] = jnp.where(selb > 0, pin, 0.0)
        rank = _dot(strict, selbf)
        dv_ref[0, :, sl] = (tok + float(j * LANES)) * float(N_EXPERTS) + rank
        cnt_row = jnp.sum(selb, axis=0, keepdims=True)
        cnt_ref[0, sl, :] = jnp.broadcast_to(cnt_row, (LANES, LANES)).T
        tile_max = jnp.where(lane_t == (j * LANES) // tr,
                             jnp.maximum(tile_max, jnp.max(cnt_row, axis=1, keepdims=True)), tile_max)
    bt_ref[0] = jnp.where(lane_b < nblk, btot, carry).astype(jnp.int32)
    mc_ref[0] = tile_max.astype(jnp.int32)


def _select(aff, cap):
    b, e_n, t = aff.shape
    tr = _row_tile(t)
    assert t // tr <= LANES
    spec = pl.BlockSpec((1, e_n, t), lambda bi: (bi, 0, 0))
    return pl.pallas_call(
        functools.partial(_select_kernel, t=t, cap=cap, tr=tr),
        grid=(b,),
        in_specs=[spec],
        out_specs=[
            spec,
            spec,
            pl.BlockSpec((1, t, LANES), lambda bi: (bi, 0, 0)),
            pl.BlockSpec((1, e_n, LANES), lambda bi: (bi, 0, 0)),
            pl.BlockSpec((1, SUB, LANES), lambda bi: (bi, 0, 0)),
        ],
        out_shape=[
            jax.ShapeDtypeStruct((b, e_n, t), F32),
            jax.ShapeDtypeStruct((b, e_n, t), F32),
            jax.ShapeDtypeStruct((b, t, LANES), F32),
            jax.ShapeDtypeStruct((b, e_n, LANES), jnp.int32),
            jax.ShapeDtypeStruct((b, SUB, LANES), jnp.int32),
        ],
        scratch_shapes=[pltpu.VMEM((e_n, t), F32)],
        compiler_params=_cparams(("parallel",)),
        name="select",
    )(aff)


def _compact_kernel(bt_ref, pm_ref, dv_ref, aff_ref, src_ref, dst_ref, gate_ref, accd_ref, accg_ref,
                    *, t, cap, c_pad, nct, nseg, batch):
    bi = batch
    e_n = N_EXPERTS
    nblk = t // LANES
    lane_e = lax.broadcasted_iota(jnp.int32, (LANES, LANES), 1)
    slot_l = lax.broadcasted_iota(jnp.int32, (1, LANES), 1)

    spr = LANES // nseg
    sh_spr = spr.bit_length() - 1
    sh_grp = (SUB * nseg).bit_length() - 1
    sub_i = lax.broadcasted_iota(jnp.int32, (SUB, LANES), 0)
    lane_i = lax.broadcasted_iota(jnp.int32, (SUB, LANES), 1)
    pick = (lane_i >> sh_spr) == sub_i
    pl_i = lax.broadcasted_iota(jnp.int32, (LANES, LANES), 0)
    pl_j = lax.broadcasted_iota(jnp.int32, (LANES, LANES), 1)
    perm = ((pl_i & (spr - 1)) == ((pl_j >> sh_grp) << 3) + (pl_j & (SUB - 1))).astype(BF16)
    seg_j = (lane_i & (SUB * nseg - 1)) >> 3

    def segment_tile(rows):
        spread = _dot3_right(jnp.where(pick, rows.astype(F32), 0.0), perm).astype(jnp.int32)
        return (((spread >> 3) * nseg + seg_j) << 3) + (spread & (SUB - 1))

    def expert(e):
        base = e * LANES

        def ctile(ct, j0):
            c0 = pl.multiple_of(ct * LANES, LANES)
            j_lo = lax.while_loop(
                lambda j: (j < nblk) & (bt_ref[base + jnp.minimum(j, nblk - 1)] <= c0),
                lambda j: j + 1, j0)
            j_hi = lax.while_loop(
                lambda j: (j < nblk) & (bt_ref[base + jnp.minimum(j, nblk - 1)] < c0 + LANES),
                lambda j: j + 1, j_lo)
            j_end = jnp.minimum(j_hi + 1, nblk)
            want = (lax.broadcasted_iota(jnp.int32, (LANES, 1), 0) + (c0 + 1)).astype(F32)

            def blk(j, acc):
                acc_d, acc_g = acc
                cols = pl.ds(pl.multiple_of(j * LANES, LANES), LANES)
                hit = pm_ref[0, e:e + 1, cols] == want
                acc_d = acc_d + jnp.where(hit, dv_ref[0, e:e + 1, cols], 0.0)
                acc_g = acc_g + jnp.where(hit, aff_ref[0, e:e + 1, cols], 0.0)
                return acc_d, acc_g

            zero = jnp.zeros((LANES, LANES), F32)
            acc_d, acc_g = lax.fori_loop(j_lo, j_end, blk, (zero, zero))
            accd_ref[e * nct + ct] = acc_d
            accg_ref[e * nct + ct] = acc_g
            return j_lo

        lax.fori_loop(0, nct, ctile, 0)

    for e in range(e_n):
        expert(e)

    def finish(ct, _):
        c0 = pl.multiple_of(ct * LANES, LANES)
        p = slot_l + (c0 - cap)
        is_pad = p >= 0
        gate_tile = jnp.zeros((LANES, LANES), F32)
        for e in range(e_n):
            v = jnp.sum(accd_ref[e * nct + ct].T, axis=0, keepdims=True).astype(jnp.int32)
            tok = v >> 4
            rank = v & (N_EXPERTS - 1)
            src = bi * t + jnp.where(is_pad, 0, tok)
            dst = jnp.where(is_pad, N_EXPERTS * t + e * (c_pad - cap) + p, rank * t + tok)
            out_rows = pl.ds(pl.multiple_of((e * nct + ct) * SUB, SUB), SUB)
            src_ref[0, out_rows, :] = segment_tile(src)
            dst_ref[0, out_rows, :] = segment_tile(dst)
            gcol = jnp.sum(accg_ref[e * nct + ct], axis=1, keepdims=True)
            gate_tile = jnp.where(lane_e == e, gcol, gate_tile)
        gate_ref[0, pl.ds(c0, LANES), :] = gate_tile
        return 0

    lax.fori_loop(0, nct, finish, 0)


def _compact(btot, pm, dv, aff, cap, c_pad, nseg, batch):
    _, e_n, t = aff.shape
    assert nseg in (1, 2, 4, 8)
    nct = -(-c_pad // LANES)
    tile_rows = e_n * nct * SUB
    spec = pl.BlockSpec((1, e_n, t), lambda i, bt: (0, 0, 0))
    ospec = pl.BlockSpec((1, tile_rows, LANES), lambda i, bt: (0, 0, 0))
    src, dst, gate = pl.pallas_call(
        functools.partial(_compact_kernel, t=t, cap=cap, c_pad=c_pad, nct=nct, nseg=nseg, batch=batch),
        grid_spec=pltpu.PrefetchScalarGridSpec(
            num_scalar_prefetch=1,
            grid=(1,),
            in_specs=[spec, spec, spec],
            out_specs=[ospec, ospec, pl.BlockSpec((1, nct * LANES, LANES), lambda i, bt: (0, 0, 0))],
            scratch_shapes=[pltpu.VMEM((e_n * nct, LANES, LANES), F32)] * 2,
        ),
        out_shape=[
            jax.ShapeDtypeStruct((1, tile_rows, LANES), jnp.int32),
            jax.ShapeDtypeStruct((1, tile_rows, LANES), jnp.int32),
            jax.ShapeDtypeStruct((1, nct * LANES, LANES), F32),
        ],
        compiler_params=_cparams(("arbitrary",)),
        name="compact",
    )(btot.reshape(e_n * LANES), pm, dv, aff)
    ids = lambda a: a.reshape(e_n, nct, SUB, LANES)[:, :, :nseg].reshape(e_n, -1)[:, :c_pad * nseg].reshape(-1)
    return ids(src), ids(dst), gate


def _sc_mesh():
    return plsc.VectorSubcoreMesh(core_axis_name="core", subcore_axis_name="subcore")


def _row_gather(table, idx):
    m = idx.shape[0]
    d = table.shape[1]

    @functools.partial(pl.kernel, out_type=jax.ShapeDtypeStruct((m, d), table.dtype),
                       mesh=_sc_mesh(), scratch_types=[])
    def gather_kernel(x_hbm, i_hbm, o_hbm):
        def body(i_vmem, o_vmem):
            pltpu.sync_copy(x_hbm.at[i_vmem.at[0]], o_vmem)

        pltpu.emit_pipeline(
            body,
            grid=(m // SC_WINDOW,),
            in_specs=[pl.BlockSpec((1, SC_WINDOW), lambda i: (0, i))],
            out_specs=[pl.BlockSpec((SC_WINDOW, d), lambda i: (i, 0))],
            core_axis_name=("core", "subcore"),
            dimension_semantics=(pltpu.PARALLEL,),
        )(i_hbm, o_hbm)

    return gather_kernel(table, idx.reshape(1, m))


def _row_scatter(rows, idx, n_out):
    m, d = rows.shape

    @functools.partial(pl.kernel, out_type=jax.ShapeDtypeStruct((n_out, d), rows.dtype),
                       mesh=_sc_mesh(), scratch_types=[])
    def scatter_kernel(x_hbm, i_hbm, o_hbm):
        def body(x_vmem, i_vmem):
            pltpu.sync_copy(x_vmem, o_hbm.at[i_vmem.at[0]])

        pltpu.emit_pipeline(
            body,
            grid=(m // SC_WINDOW,),
            in_specs=[pl.BlockSpec((SC_WINDOW, d), lambda i: (i, 0)),
                      pl.BlockSpec((1, SC_WINDOW), lambda i: (0, i))],
            out_specs=[],
            core_axis_name=("core", "subcore"),
            dimension_semantics=(pltpu.PARALLEL,),
        )(x_hbm, i_hbm)

    return scatter_kernel(rows, idx.reshape(1, m))


def _expert_kernel(gate_ref, x_ref, wg_ref, wu_ref, wd_ref, y_ref, xb_ref, acc_ref, *, c_pad, n_f):
    e = pl.program_id(0)
    f = pl.program_id(1)

    @pl.when(f == 0)
    def _():
        w = _from_segments(x_ref)
        xb_ref[...] = jnp.concatenate([_unpack_hi(w), _unpack_lo(w)], axis=1).astype(BF16)
        acc_ref[...] = jnp.zeros(acc_ref.shape, F32)

    wg = wg_ref[0].astype(BF16)
    wu = wu_ref[0].astype(BF16)
    wd = wd_ref[0].astype(BF16)
    half = c_pad // 2
    for r0 in (0, half):
        rows = slice(r0, r0 + half)
        x = xb_ref[rows, :]
        hid = _silu(_dot(x, wg)) * _dot(x, wu)
        part = _dot(hid.astype(BF16), wd)
        acc_ref[rows, :] = acc_ref[rows, :] + part

    @pl.when(f == n_f - 1)
    def _():
        lane = lax.broadcasted_iota(jnp.int32, (c_pad, LANES), 1)
        gcol = jnp.sum(jnp.where(lane == e, gate_ref[0], 0.0), axis=1, keepdims=True)
        _to_segments(y_ref, _pack_pairs(acc_ref[...] * gcol))


def _experts(gate, xs, w_gate, w_up, w_down):
    e_n, d, d_ff = w_gate.shape
    nseg = d // 2 // LANES
    c_pad = xs.shape[0] * SUB // e_n
    fc = 256
    n_f = d_ff // fc
    blk = pl.BlockSpec((c_pad // SUB, nseg, SUB, LANES), lambda e, f: (e, 0, 0, 0))
    return pl.pallas_call(
        functools.partial(_expert_kernel, c_pad=c_pad, n_f=n_f),
        grid=(e_n, n_f),
        in_specs=[
            pl.BlockSpec((1, c_pad, LANES), lambda e, f: (0, 0, 0)),
            blk,
            pl.BlockSpec((1, d, fc), lambda e, f: (e, 0, f)),
            pl.BlockSpec((1, d, fc), lambda e, f: (e, 0, f)),
            pl.BlockSpec((1, fc, d), lambda e, f: (e, f, 0)),
        ],
        out_specs=blk,
        out_shape=jax.ShapeDtypeStruct(xs.shape, jnp.int32),
        scratch_shapes=[pltpu.VMEM((c_pad, d), BF16), pltpu.VMEM((c_pad, d), F32)],
        compiler_params=_cparams(("parallel", "arbitrary")),
        name="experts",
    )(gate, xs, w_gate, w_up, w_down)


LAYERS_PER_STEP = 2


def _combine_kernel(mc_ref, h1_ref, fg_ref, *rest, n_steps, nb, n_tiles):
    cnt_refs = rest[:nb]
    z_refs = rest[nb:nb + nb * LAYERS_PER_STEP]
    o_hbm, acc_ref, obuf_ref, sem = rest[-4:]
    bi = pl.program_id(0)
    i = pl.program_id(1)
    j = pl.program_id(2)
    blocks = obuf_ref.shape[1] // CHUNK

    def tile_copies(b_, i_, slot, fn):
        for k in range(blocks):
            dst = pl.multiple_of((i_ * blocks - 1 + k) * CHUNK, CHUNK)
            copy = pltpu.make_async_copy(obuf_ref.at[slot, pl.ds(k * CHUNK, CHUNK), :],
                                         o_hbm.at[b_, pl.ds(dst, CHUNK), :], sem.at[slot])
            if k == 0:
                pl.when(i_ > 0)(functools.partial(fn, copy))
            else:
                fn(copy)

    def tile_before(steps):
        wrap = i < steps
        return bi - wrap.astype(jnp.int32), i - steps + jnp.where(wrap, n_tiles, 0)

    @pl.when(j == 0)
    def _():
        _to_segments(acc_ref, h1_ref[0])

    def add_layers(k, n_take):
        g, nz = z_refs[0].shape[0], z_refs[0].shape[1]
        cnt = cnt_refs[k][0].reshape(g, 1, SUB, LANES)
        hi = acc_ref[:, 0:nz]
        lo = acc_ref[:, nz:2 * nz]
        for l in range(n_take):
            take = cnt > (j * LAYERS_PER_STEP + l).astype(F32)
            w = z_refs[k * LAYERS_PER_STEP + l][...]
            hi = hi + jnp.where(take, _unpack_hi(w), 0.0)
            lo = lo + jnp.where(take, _unpack_lo(w), 0.0)
        acc_ref[:, 0:nz] = hi
        acc_ref[:, nz:2 * nz] = lo

    live = mc_ref[bi * LANES + i] - j * LAYERS_PER_STEP
    for k in range(nb):
        for n_take in range(1, LAYERS_PER_STEP + 1):
            last = n_take == LAYERS_PER_STEP
            pl.when((bi == k) & ((live >= n_take) if last else (live == n_take)))(
                functools.partial(add_layers, k, n_take))

    @pl.when(j == n_steps - 1)
    def _():
        idx = bi * n_tiles + i
        slot = idx % 2

        @pl.when(idx >= 2)
        def _():
            b2, i2 = tile_before(2)
            tile_copies(b2, i2, slot, lambda c: c.wait())

        obuf_ref[slot] = _rms(_from_segments(acc_ref), fg_ref[...])
        tile_copies(bi, i, slot, lambda c: c.start())

        @pl.when(idx == nb * n_tiles - 1)
        def _():
            if nb * n_tiles >= 2:
                b1, i1 = tile_before(1)
                tile_copies(b1, i1, 1 - slot, lambda c: c.wait())
            tile_copies(bi, i, slot, lambda c: c.wait())


def _combine(maxcnt, h1, zs, cnts, final_norm_g):
    b, t, d = h1.shape
    n_steps = N_EXPERTS // LAYERS_PER_STEP
    tr = _row_tile(t)
    n_tiles = t // tr

    def z_map(k, l):
        def index(bi, i, j, mc):
            layer = jnp.maximum(jnp.minimum(j * LAYERS_PER_STEP + l, mc[bi * LANES + i] - 1), 0)
            return (jnp.where(bi == k, layer * n_tiles + i, 0), 0, 0, 0)
        return index

    def cnt_map(k):
        return lambda bi, i, j, mc: (0, jnp.where(bi == k, i, 0), 0)

    seg_block = (tr // SUB, d // LANES, SUB, LANES)
    z_block = (tr // SUB, d // 2 // LANES, SUB, LANES)
    z_specs = [pl.BlockSpec(z_block, z_map(k, l)) for k in range(b) for l in range(LAYERS_PER_STEP)]
    z_args = [zs[k] for k in range(b) for _ in range(LAYERS_PER_STEP)]
    return pl.pallas_call(
        functools.partial(_combine_kernel, n_steps=n_steps, nb=b, n_tiles=n_tiles),
        grid_spec=pltpu.PrefetchScalarGridSpec(
            num_scalar_prefetch=1,
            grid=(b, n_tiles, n_steps),
            in_specs=[
                pl.BlockSpec((1, tr, d), lambda bi, i, j, mc: (bi, i, 0)),
                pl.BlockSpec((1, d), lambda bi, i, j, mc: (0, 0)),
            ] + [pl.BlockSpec((1, tr, LANES), cnt_map(k)) for k in range(b)] + z_specs,
            out_specs=pl.BlockSpec(memory_space=pl.ANY),
            scratch_shapes=[pltpu.VMEM(seg_block, F32), pltpu.VMEM((2, tr, d), F32),
                            pltpu.SemaphoreType.DMA((2,))],
        ),
        out_shape=jax.ShapeDtypeStruct((b, t - CHUNK, d), F32),
        compiler_params=_cparams(("arbitrary", "arbitrary", "arbitrary")),
        name="combine",
    )(maxcnt, h1, final_norm_g.reshape(1, d), *cnts, *z_args)


def kernel(x, meta_tokens, mix_norm_g, w_in, b_gates, conv_w, conv_b, ret_decay_logit, ret_gn_g,
           mlstm_gn_g, w_out, ffn_norm_g, w_router, w_gate, w_up, w_down, final_norm_g):
    b, s, d = x.shape
    t = CHUNK + s
    n = N_META + s
    cap = 2 * n // N_EXPERTS
    nseg = d // 2 // LANES
    c_pad = next(c for c in range(-(-cap // 16) * 16, cap + 4096, 16)
                 if (N_EXPERTS * c * nseg) % (SC_WINDOW * SC_SUBCORES) == 0)

    head = jnp.concatenate([jnp.zeros((PAD, d), x.dtype), meta_tokens.astype(x.dtype)], axis=0)

    half = DH // 2
    pos = jnp.arange(t, dtype=F32) - PAD
    inv = ROPE_BASE ** (-jnp.arange(half, dtype=F32) / half)
    ang = pos[:, None] * inv[None, :]
    cosf = jnp.concatenate([jnp.cos(ang), jnp.cos(ang)], axis=1)
    sinf = jnp.concatenate([-jnp.sin(ang), jnp.sin(ang)], axis=1)

    assert mix_norm_g.shape[0] == 1, "single-layer block only"
    rq, rk, rv, rgs, mq, mk, mv, mos, kwf, kwb, gcol, grow = _in_proj(
        x, head, mix_norm_g[0], w_in[0], b_gates[0], conv_w[0], conv_b[0], cosf, sinf)
    lg = jax.nn.log_sigmoid(ret_decay_logit[0].astype(F32))
    ret = _retention(lg, rq, rk, rv)
    hm = _mlstm(mq, mk, mv, kwf, kwb, gcol, grow)
    h1, u2, aff = _out_proj(x, head, ret, hm, rgs, mos, ret_gn_g[0], mlstm_gn_g[0], w_out[0],
                            ffn_norm_g[0], w_router[0])
    tiled = lambda a: a.reshape(-1, nseg, SUB, LANES)
    flat = lambda a: a.reshape(-1, LANES)
    z_rows = -(-(N_EXPERTS * t + N_EXPERTS * (c_pad - cap)) // SUB) * SUB
    routed = []
    for bi in range(b):
        aff_b = aff[bi:bi + 1]
        pm, dv, cnt, btot, maxcnt = _select(aff_b, cap)
        src, dst, gate = _compact(btot, pm, dv, aff_b, cap, c_pad, nseg, bi)
        routed.append((_row_gather(flat(u2), src), dst, gate, cnt, maxcnt))
    zs = []
    for xs, dst, gate, _, _ in routed:
        ys = _experts(gate, tiled(xs), w_gate[0], w_up[0], w_down[0])
        zs.append(tiled(_row_scatter(flat(ys), dst, z_rows * nseg)))
    maxcnt = jnp.concatenate([r[4][:, 0, :] for r in routed], axis=0).reshape(-1)
    return _combine(maxcnt, h1, zs, [r[3] for r in routed], final_norm_g)
```

```python
import functools

import jax
import jax.numpy as jnp
from jax import lax
from jax.experimental import pallas as pl
from jax.experimental.pallas import tpu as pltpu
from jax.experimental.pallas import tpu_sc as plsc

F32 = jnp.float32
BF16 = jnp.bfloat16

LANES = 128
CHUNK = 128
N_META = 16
PAD = CHUNK - N_META
HEADS = 4
DH = 128
N_EXPERTS = 16
CONV_K = 5
HALO = 8
SUB = 8
SC_WINDOW = 128
SC_SUBCORES = 32
EPS = 1e-6
NEG = -1e30
ROPE_BASE = 10000.0
VMEM_LIMIT = 56 * 1024 * 1024


def _cparams(sem, vmem=VMEM_LIMIT, **kw):
    return pltpu.CompilerParams(dimension_semantics=sem, vmem_limit_bytes=vmem, **kw)


def _dot(a, b):
    return jnp.dot(a, b, preferred_element_type=F32)


def _dot_nt(a, b):
    return lax.dot_general(a, b, (((1,), (1,)), ((), ())), preferred_element_type=F32)


def _dot_tn(a, b):
    return lax.dot_general(a, b, (((0,), (0,)), ((), ())), preferred_element_type=F32)


def _split3(x):
    hi = x.astype(BF16)
    r = x - hi.astype(F32)
    mid = r.astype(BF16)
    lo = (r - mid.astype(F32)).astype(BF16)
    return hi, mid, lo


def _dot3_left(m_bf, x):
    hi, mid, lo = _split3(x)
    return _dot(m_bf, hi) + _dot(m_bf, mid) + _dot(m_bf, lo)


def _dot3_right(x, m_bf):
    hi, mid, lo = _split3(x)
    return _dot(hi, m_bf) + _dot(mid, m_bf) + _dot(lo, m_bf)


def _rms(x, g):
    return x * lax.rsqrt(jnp.mean(x * x, axis=-1, keepdims=True) + EPS) * g


def _log_sigmoid(x):
    return jnp.minimum(x, 0.0) - jnp.log1p(jnp.exp(-jnp.abs(x)))


def _silu(x):
    return x * (1.0 / (1.0 + jnp.exp(-x)))


def _sigmoid(x):
    return 1.0 / (1.0 + jnp.exp(-x))


def _row_tile(t):
    return 640 if t % 640 == 0 else CHUNK


def _chunk_unroll(n_chunks):
    return 13 if n_chunks % 13 == 0 else 1


def _to_segments(dst_ref, x):
    r, w = x.shape
    for j in range(w // LANES):
        dst_ref[:, j] = x[:, j * LANES:(j + 1) * LANES].reshape(r // SUB, SUB, LANES)


def _from_segments(src_ref):
    g, nseg = src_ref.shape[0], src_ref.shape[1]
    return jnp.concatenate([src_ref[:, j].reshape(g * SUB, LANES) for j in range(nseg)], axis=1)


def _bf16_bits(x):
    return pltpu.bitcast(x.astype(BF16).astype(F32), jnp.int32)


def _pack_pairs(x):
    half = x.shape[1] // 2
    return _bf16_bits(x[:, :half]) | lax.shift_right_logical(_bf16_bits(x[:, half:]), 16)


def _unpack_hi(w):
    return pltpu.bitcast(w & -65536, F32)


def _unpack_lo(w):
    return pltpu.bitcast(w << 16, F32)


def _load_rows(x_refs, head_ref, i):
    blocks = [r[0] for r in x_refs]
    blocks[0] = jnp.where(i == 0, head_ref[...], blocks[0])
    return blocks


def _x_specs(tr, d):
    nb = tr // CHUNK
    return [pl.BlockSpec((1, CHUNK, d), functools.partial(
        lambda bi, i, k: (bi, jnp.maximum(i * nb - 1 + k, 0), 0), k=k)) for k in range(nb)]


def _in_proj_kernel(*refs, tr, n_tiles):
    nb = tr // CHUNK
    x_refs, refs = refs[:nb], refs[nb:]
    (head_ref, hp_ref, hn_ref, ng_ref, wr_ref, wm_ref, wg_ref, wgt_ref,
     bg_ref, bgt_ref, cw_ref, cb_ref, cos_ref, sin_ref,
     rq_ref, rk_ref, rv_ref, rg_ref, mq_ref, mk_ref, mv_ref, mo_ref,
     kwf_ref, kwb_ref, gc_ref, gr_ref, ext_ref, u_ref, gts_ref, h_ref) = refs
    i = pl.program_id(1)
    ng = ng_ref[...]
    w = HEADS * DH
    nd = 2 * HEADS
    for k, blk in enumerate(_load_rows(x_refs, head_ref, i)):
        h_ref[k * CHUNK:(k + 1) * CHUNK, :] = blk

    hp = hp_ref[0]
    if tr == CHUNK:
        hp = jnp.where(i == 1, head_ref[CHUNK - HALO:CHUNK, :], hp)
    up = _rms(hp, ng)
    un = _rms(hn_ref[0], ng)
    up = jnp.where(i == 0, 0.0, up)
    un = jnp.where(i == n_tiles - 1, 0.0, un)
    uh = jnp.concatenate([up, un], axis=0).astype(BF16)
    halo = _dot(uh, wm_ref[:, 0:2 * w])
    ext_ref[0:HALO, :] = halo[0:HALO]
    ext_ref[HALO + tr:2 * HALO + tr, :] = halo[HALO:2 * HALO]

    sub = tr // 2
    lane_s = lax.broadcasted_iota(jnp.int32, (sub, LANES), 1)
    for r0 in (0, sub):
        rs = slice(r0, r0 + sub)
        u = _rms(h_ref[rs, :], ng).astype(BF16)
        u_ref[rs, :] = u
        real = lax.broadcasted_iota(jnp.int32, (sub, 1), 0) + (i * tr + r0) >= PAD
        cosf = cos_ref[rs, :]
        sinf = sin_ref[rs, :]
        pq = _dot(u, wr_ref[:, 0:w])
        pk = _dot(u, wr_ref[:, w:2 * w])
        for hd in range(HEADS):
            sl = slice(hd * DH, (hd + 1) * DH)
            xq = pq[:, sl]
            xk = pk[:, sl]
            rq_ref[0, rs, sl] = ((xq * cosf + pltpu.roll(xq, DH // 2, 1) * sinf) * (DH ** -0.5)).astype(BF16)
            rk_ref[0, rs, sl] = (xk * cosf + pltpu.roll(xk, DH // 2, 1) * sinf).astype(BF16)
        rv_ref[0, rs, :] = _dot(u, wr_ref[:, 2 * w:3 * w]).astype(BF16)
        rg_ref[0, rs, :] = _silu(_dot(u, wr_ref[:, 3 * w:4 * w])).astype(BF16)
        ext_ref[HALO + r0:HALO + r0 + sub, :] = _dot(u, wm_ref[:, 0:2 * w])
        mv_ref[0, rs, :] = _dot(u, wm_ref[:, 2 * w:3 * w]).astype(BF16)
        mo_ref[0, rs, :] = _sigmoid(_dot(u, wm_ref[:, 3 * w:4 * w])).astype(BF16)
        gts = _dot(u, wg_ref[...]) + bg_ref[...]
        gts_ref[rs, :] = jnp.where(lane_s < nd, jnp.where(real, gts, NEG),
                                   jnp.where(real & (lane_s < 2 * nd), _log_sigmoid(gts), 0.0))

    rown = lax.broadcasted_iota(jnp.int32, (2 * nd, tr), 0)
    coln = lax.broadcasted_iota(jnp.int32, (1, tr), 1) + i * tr
    realr = coln >= PAD
    gtr = _dot_nt(wgt_ref[...], u_ref[...]) + bgt_ref[...]
    gtr = jnp.where(rown < nd, jnp.where(realr, gtr, NEG),
                    jnp.where(realr, _log_sigmoid(gtr), 0.0))
    ci = lax.broadcasted_iota(jnp.int32, (CHUNK, CHUNK), 0)
    cj = lax.broadcasted_iota(jnp.int32, (CHUNK, CHUNK), 1)
    causal = ci >= cj
    anti = ci <= cj
    low = causal.astype(BF16)
    upp = anti.astype(BF16)
    lane_c = lax.broadcasted_iota(jnp.int32, (CHUNK, LANES), 1)
    lane_1 = lax.broadcasted_iota(jnp.int32, (1, LANES), 1)
    rown_c = lax.broadcasted_iota(jnp.int32, (2 * nd, CHUNK), 0)
    for c in range(tr // CHUNK):
        rs = slice(c * CHUNK, (c + 1) * CHUNK)
        realc = lax.broadcasted_iota(jnp.int32, (CHUNK, 1), 0) + (i * tr + c * CHUNK) >= PAD
        conv = cb_ref[...] + jnp.zeros((CHUNK, 2 * w), F32)
        for k in range(CONV_K):
            off = HALO + c * CHUNK + k - CONV_K // 2
            conv = conv + cw_ref[k:k + 1, :] * ext_ref[off:off + CHUNK, :]
        qk = jnp.where(realc, _silu(conv), 0.0)
        mq_ref[0, rs, :] = (qk[:, 0:w] * (DH ** -0.5)).astype(BF16)
        mk_ref[0, rs, :] = qk[:, w:2 * w].astype(BF16)
        kconv = qk[:, w:2 * w]
        g = gts_ref[rs, :]
        lf = jnp.where(lane_c >= nd, g, 0.0)
        pre = _dot3_left(low, lf)
        suf = _dot3_left(upp, lf)
        bc = pltpu.roll(jnp.where(lane_c < nd + HEADS, pre, suf), LANES - nd, 1)
        blast = jnp.where(lane_1 < HEADS, bc[CHUNK - 1:CHUNK], bc[0:1])
        log_u = blast - bc + g
        a = jnp.max(log_u, axis=0, keepdims=True)
        wu = jnp.exp(log_u - a)
        gl = gtr[:, rs]
        lfr = jnp.where(rown_c >= nd, gl, 0.0)
        prer = _dot3_right(lfr, upp)
        sufr = _dot3_right(lfr, low)
        rr = gl[0:nd] - jnp.concatenate([prer[nd:nd + HEADS], sufr[nd + HEADS:2 * nd]], axis=0)
        for hd in range(HEADS):
            hs = slice(hd * DH, (hd + 1) * DH)
            cols = []
            for dr, (msk, kw_ref) in enumerate(((causal, kwf_ref), (anti, kwb_ref))):
                i8 = dr * HEADS + hd
                rmax = jnp.max(jnp.where(msk, rr[i8:i8 + 1, :], NEG), axis=1, keepdims=True)
                cols += [rmax, bc[:, i8:i8 + 1]]
                kw_ref[0, rs, hs] = (kconv[:, hs] * wu[:, i8:i8 + 1]).astype(BF16)
            gc_ref[0, hd, rs, :] = jnp.where(
                lane_c == 0, cols[0], jnp.where(lane_c == 1, cols[1], jnp.where(lane_c == 2, cols[2], cols[3])))
            scal = [jnp.broadcast_to(x[:, j:j + 1], (1, CHUNK))
                    for j in (hd, HEADS + hd) for x in (a, blast)]
            gr_ref[0, hd, :, rs] = jnp.concatenate(
                [rr[hd:hd + 1], rr[HEADS + hd:HEADS + hd + 1]] + scal + [jnp.zeros((2, CHUNK), F32)], axis=0)


def _in_proj(x, head, norm_g, w_in, b_gates, conv_w, conv_b, cosf, sinf):
    b, s, d = x.shape
    t = CHUNK + s
    tr = _row_tile(t)
    n_tiles = t // tr
    w = HEADS * DH
    wr = w_in[:, 0:4 * w].astype(BF16)
    wm = w_in[:, 4 * w:8 * w].astype(BF16)
    wg = jnp.pad(w_in[:, 8 * w:], ((0, 0), (0, LANES - 4 * HEADS))).astype(BF16)
    wgt = w_in[:, 8 * w:].T.astype(BF16)
    bg = jnp.pad(b_gates, (0, LANES - 4 * HEADS)).reshape(1, LANES)
    bgt = b_gates.reshape(4 * HEADS, 1)
    hb = tr // HALO
    pad_h = CHUNK // HALO
    last_h = s // HALO - 1
    seq = lambda bi, i: (bi, i, 0)
    const2 = lambda bi, i: (0, 0)
    out_bf = jax.ShapeDtypeStruct((b, t, w), BF16)
    kern = functools.partial(_in_proj_kernel, tr=tr, n_tiles=n_tiles)
    return pl.pallas_call(
        kern,
        grid=(b, n_tiles),
        in_specs=_x_specs(tr, d) + [
            pl.BlockSpec((CHUNK, d), const2),
            pl.BlockSpec((1, HALO, d), lambda bi, i: (bi, jnp.maximum(i * hb - pad_h - 1, 0), 0)),
            pl.BlockSpec((1, HALO, d), lambda bi, i: (bi, jnp.minimum((i + 1) * hb - pad_h, last_h), 0)),
            pl.BlockSpec((1, d), const2),
            pl.BlockSpec((d, 4 * w), const2),
            pl.BlockSpec((d, 4 * w), const2),
            pl.BlockSpec((d, LANES), const2),
            pl.BlockSpec((4 * HEADS, d), const2),
            pl.BlockSpec((1, LANES), const2),
            pl.BlockSpec((4 * HEADS, 1), const2),
            pl.BlockSpec((CONV_K, 2 * w), const2),
            pl.BlockSpec((1, 2 * w), const2),
            pl.BlockSpec((tr, DH), lambda bi, i: (i, 0)),
            pl.BlockSpec((tr, DH), lambda bi, i: (i, 0)),
        ],
        out_specs=[pl.BlockSpec((1, tr, w), seq)] * 10 + [
            pl.BlockSpec((1, HEADS, tr, LANES), lambda bi, i: (bi, 0, i, 0)),
            pl.BlockSpec((1, HEADS, 8, tr), lambda bi, i: (bi, 0, 0, i)),
        ],
        out_shape=[out_bf] * 10 + [
            jax.ShapeDtypeStruct((b, HEADS, t, LANES), F32),
            jax.ShapeDtypeStruct((b, HEADS, 8, t), F32),
        ],
        scratch_shapes=[pltpu.VMEM((tr + 2 * HALO, 2 * w), F32), pltpu.VMEM((tr, d), BF16),
                        pltpu.VMEM((tr, LANES), F32), pltpu.VMEM((tr, d), F32)],
        compiler_params=_cparams(("parallel", "arbitrary")),
        name="in_proj",
    )(*([x] * (tr // CHUNK)), head, x, x, norm_g.reshape(1, d), wr, wm, wg, wgt, bg, bgt,
      conv_w, conv_b.reshape(1, 2 * w), cosf, sinf)


def _retention_kernel(lg_ref, q_ref, k_ref, v_ref, o_ref, of_ref, ob_ref, *, n_chunks):
    hd = pl.program_id(1)
    lgf = lg_ref[0, hd]
    lgb = lg_ref[1, hd]
    li = lax.broadcasted_iota(jnp.int32, (CHUNK, CHUNK), 0)
    mi = lax.broadcasted_iota(jnp.int32, (CHUNK, CHUNK), 1)
    diff = (li - mi).astype(F32)
    decay = (jnp.where(diff >= 0, jnp.exp(lgf * jnp.maximum(diff, 0.0)), 0.0)
             + jnp.where(diff <= 0, jnp.exp(lgb * jnp.maximum(-diff, 0.0)), 0.0))
    lcol = lax.broadcasted_iota(jnp.int32, (CHUNK, 1), 0).astype(F32)
    one = jnp.ones((1, 1), F32)
    zeta_f = jnp.exp(lgf * (CHUNK - 1.0 - lcol))
    xi_f = jnp.exp(lgf * (lcol + 1.0))
    g_f = jnp.exp(lgf * CHUNK * one)
    zeta_b = jnp.exp(lgb * lcol)
    xi_b = jnp.exp(lgb * (CHUNK - lcol))
    g_b = jnp.exp(lgb * CHUNK * one)

    def fwd(c, state):
        rows = pl.ds(pl.multiple_of(c * CHUNK, CHUNK), CHUNK)
        q = q_ref[0, rows, :]
        k = k_ref[0, rows, :]
        v = v_ref[0, rows, :]
        s = _dot_nt(q, k) * decay
        intra = _dot(s.astype(BF16), v)
        cross = _dot(q, state.astype(BF16)) * xi_f
        of_ref[rows, :] = intra + cross
        kz = (k.astype(F32) * zeta_f).astype(BF16)
        return g_f * state + _dot_tn(kz, v)

    def bwd(j, state):
        c = n_chunks - 1 - j
        rows = pl.ds(pl.multiple_of(c * CHUNK, CHUNK), CHUNK)
        q = q_ref[0, rows, :]
        k = k_ref[0, rows, :]
        v = v_ref[0, rows, :]
        ob_ref[rows, :] = _dot(q, state.astype(BF16)) * xi_b
        kz = (k.astype(F32) * zeta_b).astype(BF16)
        return g_b * state + _dot_tn(kz, v)

    zero = jnp.zeros((DH, DH), F32)
    lax.fori_loop(0, n_chunks, lambda j, st: (fwd(j, st[0]), bwd(j, st[1])), (zero, zero),
                  unroll=_chunk_unroll(n_chunks))
    o_ref[0] = (of_ref[...] + ob_ref[...]).astype(BF16)


def _retention(lg, rq, rk, rv):
    b, t, _ = rq.shape
    spec = pl.BlockSpec((1, t, DH), lambda bi, hi, lg_ref: (bi, 0, hi))
    return pl.pallas_call(
        functools.partial(_retention_kernel, n_chunks=t // CHUNK),
        grid_spec=pltpu.PrefetchScalarGridSpec(
            num_scalar_prefetch=1,
            grid=(b, HEADS),
            in_specs=[spec, spec, spec],
            out_specs=spec,
            scratch_shapes=[pltpu.VMEM((t, DH), F32)] * 2,
        ),
        out_shape=jax.ShapeDtypeStruct((b, t, HEADS * DH), BF16),
        compiler_params=_cparams(("parallel", "arbitrary")),
        name="retention",
    )(lg, rq, rk, rv)


def _mlstm_kernel(q_ref, k_ref, v_ref, kwf_ref, kwb_ref, gc_ref, gr_ref, o_ref, of_ref, ob_ref, *, n_chunks):
    li = lax.broadcasted_iota(jnp.int32, (CHUNK, CHUNK), 0)
    mi = lax.broadcasted_iota(jnp.int32, (CHUNK, CHUNK), 1)
    causal = li >= mi
    anti = li <= mi

    ones = jnp.ones((CHUNK, DH), BF16)

    def step(c, carry, backward):
        st, m_st = carry
        rows = pl.ds(pl.multiple_of(c * CHUNK, CHUNK), CHUNK)
        q = q_ref[0, rows, :]
        k = k_ref[0, rows, :]
        v_aug = jnp.concatenate([v_ref[0, rows, :], ones], axis=1)
        gcol = gc_ref[0, 0, rows, :]
        grow = gr_ref[0, 0, :, rows]
        if backward:
            kw = kwb_ref[0, rows, :]
            r_max, bc = gcol[:, 2:3], gcol[:, 3:4]
            r_row, a, b_last = grow[1:2, :], grow[4:5, 0:1], grow[5:6, 0:1]
            mask = anti
        else:
            kw = kwf_ref[0, rows, :]
            r_max, bc = gcol[:, 0:1], gcol[:, 1:2]
            r_row, a, b_last = grow[0:1, :], grow[2:3, 0:1], grow[3:4, 0:1]
            mask = causal
        mm = jnp.maximum(jnp.broadcast_to(r_max, (CHUNK, DH)), m_st)
        s = _dot_nt(q, k) * jnp.where(mask, jnp.exp(r_row - mm), 0.0)
        w_inter = jnp.exp(m_st - mm)
        w2 = jnp.concatenate([w_inter, w_inter], axis=1)
        tot = _dot(s.astype(BF16), v_aug) + _dot(q, st.astype(BF16)) * w2
        floor = jnp.exp(-(jnp.broadcast_to(bc, (CHUNK, DH)) + mm))
        hval = tot[:, 0:DH] / jnp.maximum(jnp.abs(tot[:, DH:2 * DH]), floor)
        if backward:
            ob_ref[rows, :] = hval
        else:
            of_ref[rows, :] = hval
        upd = _dot_tn(kw, v_aug)
        m_new = jnp.maximum(b_last + m_st, a)
        f = jnp.exp(b_last + m_st - m_new)
        g = jnp.exp(a - m_new)
        return f * st + g * upd, m_new

    def both(j, carry):
        return step(j, carry[0], False), step(n_chunks - 1 - j, carry[1], True)

    init = (jnp.zeros((DH, 2 * DH), F32), jnp.zeros((1, 1), F32))
    lax.fori_loop(0, n_chunks, both, (init, init), unroll=_chunk_unroll(n_chunks))
    o_ref[0] = (of_ref[...] + ob_ref[...]).astype(BF16)


def _mlstm(mq, mk, mv, kwf, kwb, gcol, grow):
    b, t, _ = mq.shape
    spec = pl.BlockSpec((1, t, DH), lambda bi, hi: (bi, 0, hi))
    return pl.pallas_call(
        functools.partial(_mlstm_kernel, n_chunks=t // CHUNK),
        grid=(b, HEADS),
        in_specs=[spec, spec, spec, spec, spec,
                  pl.BlockSpec((1, 1, t, LANES), lambda bi, hi: (bi, hi, 0, 0)),
                  pl.BlockSpec((1, 1, 8, t), lambda bi, hi: (bi, hi, 0, 0))],
        out_specs=spec,
        out_shape=jax.ShapeDtypeStruct((b, t, HEADS * DH), BF16),
        scratch_shapes=[pltpu.VMEM((t, DH), F32)] * 2,
        compiler_params=_cparams(("parallel", "arbitrary")),
        name="mlstm",
    )(mq, mk, mv, kwf, kwb, gcol, grow)


def _head_norm(y, g):
    outs = []
    for hd in range(HEADS):
        yh = y[:, hd * DH:(hd + 1) * DH]
        mu = jnp.mean(yh, axis=1, keepdims=True)
        yc = yh - mu
        var = jnp.mean(yc * yc, axis=1, keepdims=True)
        outs.append(yc * lax.rsqrt(var + EPS))
    return jnp.concatenate(outs, axis=1) * g


def _out_proj_kernel(*refs, tr):
    nb = tr // CHUNK
    x_refs, refs = refs[:nb], refs[nb:]
    (head_ref, ret_ref, hm_ref, rg_ref, mo_ref, rgn_ref, mgn_ref, wo_ref,
     fg_ref, wrt_ref, h1_ref, u2_ref, aff_ref) = refs
    i = pl.program_id(1)
    row = lax.broadcasted_iota(jnp.int32, (tr, 1), 0) + i * tr
    real = row >= PAD
    w = HEADS * DH
    y_ret = _head_norm(ret_ref[0].astype(F32), rgn_ref[...]) * rg_ref[0].astype(F32)
    y_m = _head_norm(mo_ref[0].astype(F32) * hm_ref[0].astype(F32), mgn_ref[...])
    y_ret = jnp.where(real, y_ret, 0.0).astype(BF16)
    y_m = jnp.where(real, y_m, 0.0).astype(BF16)
    h = jnp.concatenate(_load_rows(x_refs, head_ref, i), axis=0)
    h1 = h + _dot(y_ret, wo_ref[0:w, :]) + _dot(y_m, wo_ref[w:2 * w, :])
    h1_ref[0] = h1
    u2 = _rms(h1, fg_ref[...])
    _to_segments(u2_ref, _pack_pairs(u2))
    logits = _dot_nt(wrt_ref[...], u2.astype(BF16))
    mx = jnp.max(logits, axis=0, keepdims=True)
    ex = jnp.exp(logits - mx)
    aff = ex / jnp.sum(ex, axis=0, keepdims=True)
    coln = lax.broadcasted_iota(jnp.int32, (1, tr), 1) + i * tr
    aff_ref[0] = jnp.where(coln >= PAD, aff, -1.0)


def _out_proj(x, head, ret, hm, rgs, mos, ret_gn_g, mlstm_gn_g, w_out, ffn_norm_g, w_router):
    b, s, d = x.shape
    t = CHUNK + s
    tr = _row_tile(t)
    n_tiles = t // tr
    nseg = d // 2 // LANES
    w = HEADS * DH
    seq = lambda bi, i: (bi, i, 0)
    const2 = lambda bi, i: (0, 0)
    return pl.pallas_call(
        functools.partial(_out_proj_kernel, tr=tr),
        grid=(b, n_tiles),
        in_specs=_x_specs(tr, d) + [
            pl.BlockSpec((CHUNK, d), const2),
            pl.BlockSpec((1, tr, w), seq),
            pl.BlockSpec((1, tr, w), seq),
            pl.BlockSpec((1, tr, w), seq),
            pl.BlockSpec((1, tr, w), seq),
            pl.BlockSpec((1, w), const2),
            pl.BlockSpec((1, w), const2),
            pl.BlockSpec((2 * w, d), const2),
            pl.BlockSpec((1, d), const2),
            pl.BlockSpec((N_EXPERTS, d), const2),
        ],
        out_specs=[
            pl.BlockSpec((1, tr, d), seq),
            pl.BlockSpec((tr // SUB, nseg, SUB, LANES), lambda bi, i: (bi * n_tiles + i, 0, 0, 0)),
            pl.BlockSpec((1, N_EXPERTS, tr), lambda bi, i: (bi, 0, i)),
        ],
        out_shape=[
            jax.ShapeDtypeStruct((b, t, d), F32),
            jax.ShapeDtypeStruct((b * t // SUB, nseg, SUB, LANES), jnp.int32),
            jax.ShapeDtypeStruct((b, N_EXPERTS, t), F32),
        ],
        compiler_params=_cparams(("parallel", "arbitrary")),
        name="out_proj",
    )(*([x] * (tr // CHUNK)), head, ret, hm, rgs, mos, ret_gn_g.reshape(1, w), mlstm_gn_g.reshape(1, w),
      w_out.astype(BF16), ffn_norm_g.reshape(1, d), w_router.T.astype(BF16))


def _select_kernel(aff_ref, pm_ref, dv_ref, cnt_ref, bt_ref, mc_ref, sel_ref, *, t, cap, tr):
    e_n = N_EXPERTS
    nblk = t // LANES
    bits = pltpu.bitcast(aff_ref[0], jnp.int32)
    capf = float(cap)

    def search(_, lohi):
        lo, hi = lohi
        mid = lo + ((hi - lo + 1) >> 1)
        cnt = jnp.sum((bits >= mid).astype(F32), axis=1, keepdims=True)
        ok = cnt >= capf
        return jnp.where(ok, mid, lo), jnp.where(ok, hi, mid - 1)

    lo0 = jnp.zeros((e_n, 1), jnp.int32)
    hi0 = jnp.full((e_n, 1), 0x3F800000, jnp.int32)
    thr, _ = lax.fori_loop(0, 31, search, (lo0, hi0))
    need = capf - jnp.sum((bits > thr).astype(F32), axis=1, keepdims=True)

    ci = lax.broadcasted_iota(jnp.int32, (LANES, LANES), 0)
    cj = lax.broadcasted_iota(jnp.int32, (LANES, LANES), 1)
    upp = (ci <= cj).astype(BF16)
    ei = lax.broadcasted_iota(jnp.int32, (e_n, e_n), 0)
    ej = lax.broadcasted_iota(jnp.int32, (e_n, e_n), 1)
    strict = (ej < ei).astype(BF16)

    carry = jnp.zeros((e_n, 1), F32)
    for j in range(nblk):
        sl = slice(j * LANES, (j + 1) * LANES)
        bb = bits[:, sl]
        eqf = (bb == thr).astype(F32)
        ceq = _dot(eqf.astype(BF16), upp) + carry
        carry = ceq[:, LANES - 1:LANES]
        sel_ref[:, sl] = jnp.where(bb > thr, 1.0, jnp.where(ceq <= need, eqf, 0.0))

    carry = jnp.zeros((e_n, 1), F32)
    tok = lax.broadcasted_iota(jnp.int32, (1, LANES), 1).astype(F32)
    lane_b = lax.broadcasted_iota(jnp.int32, (e_n, LANES), 1)
    btot = jnp.zeros((e_n, LANES), F32)
    tile_max = jnp.zeros((SUB, LANES), F32)
    lane_t = lax.broadcasted_iota(jnp.int32, (SUB, LANES), 1)
    for j in range(nblk):
        sl = slice(j * LANES, (j + 1) * LANES)
        selb = sel_ref[:, sl]
        selbf = selb.astype(BF16)
        pin = _dot(selbf, upp) + carry
        carry = pin[:, LANES - 1:LANES]
        btot = jnp.where(lane_b == j, carry, btot)
        pm_ref[0, :, sl] = jnp.where(selb > 0, pin, 0.0)
        rank = _dot(strict, selbf)
        dv_ref[0, :, sl] = (tok + float(j * LANES)) * float(N_EXPERTS) + rank
        cnt_row = jnp.sum(selb, axis=0, keepdims=True)
        cnt_ref[0, sl, :] = jnp.broadcast_to(cnt_row, (LANES, LANES)).T
        tile_max = jnp.where(lane_t == (j * LANES) // tr,
                             jnp.maximum(tile_max, jnp.max(cnt_row, axis=1, keepdims=True)), tile_max)
    bt_ref[0] = jnp.where(lane_b < nblk, btot, carry).astype(jnp.int32)
    mc_ref[0] = tile_max.astype(jnp.int32)


def _select(aff, cap):
    b, e_n, t = aff.shape
    tr = _row_tile(t)
    assert t // tr <= LANES
    spec = pl.BlockSpec((1, e_n, t), lambda bi: (bi, 0, 0))
    return pl.pallas_call(
        functools.partial(_select_kernel, t=t, cap=cap, tr=tr),
        grid=(b,),
        in_specs=[spec],
        out_specs=[
            spec,
            spec,
            pl.BlockSpec((1, t, LANES), lambda bi: (bi, 0, 0)),
            pl.BlockSpec((1, e_n, LANES), lambda bi: (bi, 0, 0)),
            pl.BlockSpec((1, SUB, LANES), lambda bi: (bi, 0, 0)),
        ],
        out_shape=[
            jax.ShapeDtypeStruct((b, e_n, t), F32),
            jax.ShapeDtypeStruct((b, e_n, t), F32),
            jax.ShapeDtypeStruct((b, t, LANES), F32),
            jax.ShapeDtypeStruct((b, e_n, LANES), jnp.int32),
            jax.ShapeDtypeStruct((b, SUB, LANES), jnp.int32),
        ],
        scratch_shapes=[pltpu.VMEM((e_n, t), F32)],
        compiler_params=_cparams(("parallel",)),
        name="select",
    )(aff)


def _compact_kernel(bt_ref, pm_ref, dv_ref, aff_ref, src_ref, dst_ref, gate_ref, accd_ref, accg_ref,
                    *, t, cap, c_pad, nct, nseg, batch):
    bi = batch
    e_n = N_EXPERTS
    nblk = t // LANES
    lane_e = lax.broadcasted_iota(jnp.int32, (LANES, LANES), 1)
    slot_l = lax.broadcasted_iota(jnp.int32, (1, LANES), 1)

    spr = LANES // nseg
    sh_spr = spr.bit_length() - 1
    sh_grp = (SUB * nseg).bit_length() - 1
    sub_i = lax.broadcasted_iota(jnp.int32, (SUB, LANES), 0)
    lane_i = lax.broadcasted_iota(jnp.int32, (SUB, LANES), 1)
    pick = (lane_i >> sh_spr) == sub_i
    pl_i = lax.broadcasted_iota(jnp.int32, (LANES, LANES), 0)
    pl_j = lax.broadcasted_iota(jnp.int32, (LANES, LANES), 1)
    perm = ((pl_i & (spr - 1)) == ((pl_j >> sh_grp) << 3) + (pl_j & (SUB - 1))).astype(BF16)
    seg_j = (lane_i & (SUB * nseg - 1)) >> 3

    def segment_tile(rows):
        spread = _dot3_right(jnp.where(pick, rows.astype(F32), 0.0), perm).astype(jnp.int32)
        return (((spread >> 3) * nseg + seg_j) << 3) + (spread & (SUB - 1))

    def expert(e):
        base = e * LANES

        def ctile(ct, j0):
            c0 = pl.multiple_of(ct * LANES, LANES)
            j_lo = lax.while_loop(
                lambda j: (j < nblk) & (bt_ref[base + jnp.minimum(j, nblk - 1)] <= c0),
                lambda j: j + 1, j0)
            j_hi = lax.while_loop(
                lambda j: (j < nblk) & (bt_ref[base + jnp.minimum(j, nblk - 1)] < c0 + LANES),
                lambda j: j + 1, j_lo)
            j_end = jnp.minimum(j_hi + 1, nblk)
            want = (lax.broadcasted_iota(jnp.int32, (LANES, 1), 0) + (c0 + 1)).astype(F32)

            def blk(j, acc):
                acc_d, acc_g = acc
                cols = pl.ds(pl.multiple_of(j * LANES, LANES), LANES)
                hit = pm_ref[0, e:e + 1, cols] == want
                acc_d = acc_d + jnp.where(hit, dv_ref[0, e:e + 1, cols], 0.0)
                acc_g = acc_g + jnp.where(hit, aff_ref[0, e:e + 1, cols], 0.0)
                return acc_d, acc_g

            zero = jnp.zeros((LANES, LANES), F32)
            acc_d, acc_g = lax.fori_loop(j_lo, j_end, blk, (zero, zero))
            accd_ref[e * nct + ct] = acc_d
            accg_ref[e * nct + ct] = acc_g
            return j_lo

        lax.fori_loop(0, nct, ctile, 0)

    for e in range(e_n):
        expert(e)

    def finish(ct, _):
        c0 = pl.multiple_of(ct * LANES, LANES)
        p = slot_l + (c0 - cap)
        is_pad = p >= 0
        gate_tile = jnp.zeros((LANES, LANES), F32)
        for e in range(e_n):
            v = jnp.sum(accd_ref[e * nct + ct].T, axis=0, keepdims=True).astype(jnp.int32)
            tok = v >> 4
            rank = v & (N_EXPERTS - 1)
            src = bi * t + jnp.where(is_pad, 0, tok)
            dst = jnp.where(is_pad, N_EXPERTS * t + e * (c_pad - cap) + p, rank * t + tok)
            out_rows = pl.ds(pl.multiple_of((e * nct + ct) * SUB, SUB), SUB)
            src_ref[0, out_rows, :] = segment_tile(src)
            dst_ref[0, out_rows, :] = segment_tile(dst)
            gcol = jnp.sum(accg_ref[e * nct + ct], axis=1, keepdims=True)
            gate_tile = jnp.where(lane_e == e, gcol, gate_tile)
        gate_ref[0, pl.ds(c0, LANES), :] = gate_tile
        return 0

    lax.fori_loop(0, nct, finish, 0)


def _compact(btot, pm, dv, aff, cap, c_pad, nseg, batch):
    _, e_n, t = aff.shape
    assert nseg in (1, 2, 4, 8)
    nct = -(-c_pad // LANES)
    tile_rows = e_n * nct * SUB
    spec = pl.BlockSpec((1, e_n, t), lambda i, bt: (0, 0, 0))
    ospec = pl.BlockSpec((1, tile_rows, LANES), lambda i, bt: (0, 0, 0))
    src, dst, gate = pl.pallas_call(
        functools.partial(_compact_kernel, t=t, cap=cap, c_pad=c_pad, nct=nct, nseg=nseg, batch=batch),
        grid_spec=pltpu.PrefetchScalarGridSpec(
            num_scalar_prefetch=1,
            grid=(1,),
            in_specs=[spec, spec, spec],
            out_specs=[ospec, ospec, pl.BlockSpec((1, nct * LANES, LANES), lambda i, bt: (0, 0, 0))],
            scratch_shapes=[pltpu.VMEM((e_n * nct, LANES, LANES), F32)] * 2,
        ),
        out_shape=[
            jax.ShapeDtypeStruct((1, tile_rows, LANES), jnp.int32),
            jax.ShapeDtypeStruct((1, tile_rows, LANES), jnp.int32),
            jax.ShapeDtypeStruct((1, nct * LANES, LANES), F32),
        ],
        compiler_params=_cparams(("arbitrary",)),
        name="compact",
    )(btot.reshape(e_n * LANES), pm, dv, aff)
    ids = lambda a: a.reshape(e_n, nct, SUB, LANES)[:, :, :nseg].reshape(e_n, -1)[:, :c_pad * nseg].reshape(-1)
    return ids(src), ids(dst), gate


def _sc_mesh():
    return plsc.VectorSubcoreMesh(core_axis_name="core", subcore_axis_name="subcore")


def _row_gather(table, idx):
    m = idx.shape[0]
    d = table.shape[1]

    @functools.partial(pl.kernel, out_type=jax.ShapeDtypeStruct((m, d), table.dtype),
                       mesh=_sc_mesh(), scratch_types=[])
    def gather_kernel(x_hbm, i_hbm, o_hbm):
        def body(i_vmem, o_vmem):
            pltpu.sync_copy(x_hbm.at[i_vmem.at[0]], o_vmem)

        pltpu.emit_pipeline(
            body,
            grid=(m // SC_WINDOW,),
            in_specs=[pl.BlockSpec((1, SC_WINDOW), lambda i: (0, i))],
            out_specs=[pl.BlockSpec((SC_WINDOW, d), lambda i: (i, 0))],
            core_axis_name=("core", "subcore"),
            dimension_semantics=(pltpu.PARALLEL,),
        )(i_hbm, o_hbm)

    return gather_kernel(table, idx.reshape(1, m))


def _row_scatter(rows, idx, n_out):
    m, d = rows.shape

    @functools.partial(pl.kernel, out_type=jax.ShapeDtypeStruct((n_out, d), rows.dtype),
                       mesh=_sc_mesh(), scratch_types=[])
    def scatter_kernel(x_hbm, i_hbm, o_hbm):
        def body(x_vmem, i_vmem):
            pltpu.sync_copy(x_vmem, o_hbm.at[i_vmem.at[0]])

        pltpu.emit_pipeline(
            body,
            grid=(m // SC_WINDOW,),
            in_specs=[pl.BlockSpec((SC_WINDOW, d), lambda i: (i, 0)),
                      pl.BlockSpec((1, SC_WINDOW), lambda i: (0, i))],
            out_specs=[],
            core_axis_name=("core", "subcore"),
            dimension_semantics=(pltpu.PARALLEL,),
        )(x_hbm, i_hbm)

    return scatter_kernel(rows, idx.reshape(1, m))


def _expert_kernel(gate_ref, x_ref, wg_ref, wu_ref, wd_ref, y_ref, xb_ref, acc_ref, *, c_pad, n_f):
    e = pl.program_id(0)
    f = pl.program_id(1)

    @pl.when(f == 0)
    def _():
        w = _from_segments(x_ref)
        xb_ref[...] = jnp.concatenate([_unpack_hi(w), _unpack_lo(w)], axis=1).astype(BF16)
        acc_ref[...] = jnp.zeros(acc_ref.shape, F32)

    wg = wg_ref[0].astype(BF16)
    wu = wu_ref[0].astype(BF16)
    wd = wd_ref[0].astype(BF16)
    half = c_pad // 2
    for r0 in (0, half):
        rows = slice(r0, r0 + half)
        x = xb_ref[rows, :]
        hid = _silu(_dot(x, wg)) * _dot(x, wu)
        part = _dot(hid.astype(BF16), wd)
        acc_ref[rows, :] = acc_ref[rows, :] + part

    @pl.when(f == n_f - 1)
    def _():
        lane = lax.broadcasted_iota(jnp.int32, (c_pad, LANES), 1)
        gcol = jnp.sum(jnp.where(lane == e, gate_ref[0], 0.0), axis=1, keepdims=True)
        _to_segments(y_ref, _pack_pairs(acc_ref[...] * gcol))


def _experts(gate, xs, w_gate, w_up, w_down):
    e_n, d, d_ff = w_gate.shape
    nseg = d // 2 // LANES
    c_pad = xs.shape[0] * SUB // e_n
    fc = 256
    n_f = d_ff // fc
    blk = pl.BlockSpec((c_pad // SUB, nseg, SUB, LANES), lambda e, f: (e, 0, 0, 0))
    return pl.pallas_call(
        functools.partial(_expert_kernel, c_pad=c_pad, n_f=n_f),
        grid=(e_n, n_f),
        in_specs=[
            pl.BlockSpec((1, c_pad, LANES), lambda e, f: (0, 0, 0)),
            blk,
            pl.BlockSpec((1, d, fc), lambda e, f: (e, 0, f)),
            pl.BlockSpec((1, d, fc), lambda e, f: (e, 0, f)),
            pl.BlockSpec((1, fc, d), lambda e, f: (e, f, 0)),
        ],
        out_specs=blk,
        out_shape=jax.ShapeDtypeStruct(xs.shape, jnp.int32),
        scratch_shapes=[pltpu.VMEM((c_pad, d), BF16), pltpu.VMEM((c_pad, d), F32)],
        compiler_params=_cparams(("parallel", "arbitrary")),
        name="experts",
    )(gate, xs, w_gate, w_up, w_down)


LAYERS_PER_STEP = 2


def _combine_kernel(mc_ref, h1_ref, fg_ref, *rest, n_steps, nb, n_tiles):
    cnt_refs = rest[:nb]
    z_refs = rest[nb:nb + nb * LAYERS_PER_STEP]
    o_hbm, acc_ref, obuf_ref, sem = rest[-4:]
    bi = pl.program_id(0)
    i = pl.program_id(1)
    j = pl.program_id(2)
    blocks = obuf_ref.shape[1] // CHUNK

    def tile_copies(b_, i_, slot, fn):
        for k in range(blocks):
            dst = pl.multiple_of((i_ * blocks - 1 + k) * CHUNK, CHUNK)
            copy = pltpu.make_async_copy(obuf_ref.at[slot, pl.ds(k * CHUNK, CHUNK), :],
                                         o_hbm.at[b_, pl.ds(dst, CHUNK), :], sem.at[slot])
            if k == 0:
                pl.when(i_ > 0)(functools.partial(fn, copy))
            else:
                fn(copy)

    def tile_before(steps):
        wrap = i < steps
        return bi - wrap.astype(jnp.int32), i - steps + jnp.where(wrap, n_tiles, 0)

    @pl.when(j == 0)
    def _():
        _to_segments(acc_ref, h1_ref[0])

    def add_layers(k, n_take):
        g, nz = z_refs[0].shape[0], z_refs[0].shape[1]
        cnt = cnt_refs[k][0].reshape(g, 1, SUB, LANES)
        hi = acc_ref[:, 0:nz]
        lo = acc_ref[:, nz:2 * nz]
        for l in range(n_take):
            take = cnt > (j * LAYERS_PER_STEP + l).astype(F32)
            w = z_refs[k * LAYERS_PER_STEP + l][...]
            hi = hi + jnp.where(take, _unpack_hi(w), 0.0)
            lo = lo + jnp.where(take, _unpack_lo(w), 0.0)
        acc_ref[:, 0:nz] = hi
        acc_ref[:, nz:2 * nz] = lo

    live = mc_ref[bi * LANES + i] - j * LAYERS_PER_STEP
    for k in range(nb):
        for n_take in range(1, LAYERS_PER_STEP + 1):
            last = n_take == LAYERS_PER_STEP
            pl.when((bi == k) & ((live >= n_take) if last else (live == n_take)))(
                functools.partial(add_layers, k, n_take))

    @pl.when(j == n_steps - 1)
    def _():
        idx = bi * n_tiles + i
        slot = idx % 2

        @pl.when(idx >= 2)
        def _():
            b2, i2 = tile_before(2)
            tile_copies(b2, i2, slot, lambda c: c.wait())

        obuf_ref[slot] = _rms(_from_segments(acc_ref), fg_ref[...])
        tile_copies(bi, i, slot, lambda c: c.start())

        @pl.when(idx == nb * n_tiles - 1)
        def _():
            if nb * n_tiles >= 2:
                b1, i1 = tile_before(1)
                tile_copies(b1, i1, 1 - slot, lambda c: c.wait())
            tile_copies(bi, i, slot, lambda c: c.wait())


def _combine(maxcnt, h1, zs, cnts, final_norm_g):
    b, t, d = h1.shape
    n_steps = N_EXPERTS // LAYERS_PER_STEP
    tr = _row_tile(t)
    n_tiles = t // tr

    def z_map(k, l):
        def index(bi, i, j, mc):
            layer = jnp.maximum(jnp.minimum(j * LAYERS_PER_STEP + l, mc[bi * LANES + i] - 1), 0)
            return (jnp.where(bi == k, layer * n_tiles + i, 0), 0, 0, 0)
        return index

    def cnt_map(k):
        return lambda bi, i, j, mc: (0, jnp.where(bi == k, i, 0), 0)

    seg_block = (tr // SUB, d // LANES, SUB, LANES)
    z_block = (tr // SUB, d // 2 // LANES, SUB, LANES)
    z_specs = [pl.BlockSpec(z_block, z_map(k, l)) for k in range(b) for l in range(LAYERS_PER_STEP)]
    z_args = [zs[k] for k in range(b) for _ in range(LAYERS_PER_STEP)]
    return pl.pallas_call(
        functools.partial(_combine_kernel, n_steps=n_steps, nb=b, n_tiles=n_tiles),
        grid_spec=pltpu.PrefetchScalarGridSpec(
            num_scalar_prefetch=1,
            grid=(b, n_tiles, n_steps),
            in_specs=[
                pl.BlockSpec((1, tr, d), lambda bi, i, j, mc: (bi, i, 0)),
                pl.BlockSpec((1, d), lambda bi, i, j, mc: (0, 0)),
            ] + [pl.BlockSpec((1, tr, LANES), cnt_map(k)) for k in range(b)] + z_specs,
            out_specs=pl.BlockSpec(memory_space=pl.ANY),
            scratch_shapes=[pltpu.VMEM(seg_block, F32), pltpu.VMEM((2, tr, d), F32),
                            pltpu.SemaphoreType.DMA((2,))],
        ),
        out_shape=jax.ShapeDtypeStruct((b, t - CHUNK, d), F32),
        compiler_params=_cparams(("arbitrary", "arbitrary", "arbitrary")),
        name="combine",
    )(maxcnt, h1, final_norm_g.reshape(1, d), *cnts, *z_args)


def kernel(x, meta_tokens, mix_norm_g, w_in, b_gates, conv_w, conv_b, ret_decay_logit, ret_gn_g,
           mlstm_gn_g, w_out, ffn_norm_g, w_router, w_gate, w_up, w_down, final_norm_g):
    b, s, d = x.shape
    t = CHUNK + s
    n = N_META + s
    cap = 2 * n // N_EXPERTS
    nseg = d // 2 // LANES
    c_pad = next(c for c in range(-(-cap // 16) * 16, cap + 4096, 16)
                 if (N_EXPERTS * c * nseg) % (SC_WINDOW * SC_SUBCORES) == 0)

    head = jnp.concatenate([jnp.zeros((PAD, d), x.dtype), meta_tokens.astype(x.dtype)], axis=0)

    half = DH // 2
    pos = jnp.arange(t, dtype=F32) - PAD
    inv = ROPE_BASE ** (-jnp.arange(half, dtype=F32) / half)
    ang = pos[:, None] * inv[None, :]
    cosf = jnp.concatenate([jnp.cos(ang), jnp.cos(ang)], axis=1)
    sinf = jnp.concatenate([-jnp.sin(ang), jnp.sin(ang)], axis=1)

    assert mix_norm_g.shape[0] == 1, "single-layer block only"
    rq, rk, rv, rgs, mq, mk, mv, mos, kwf, kwb, gcol, grow = _in_proj(
        x, head, mix_norm_g[0], w_in[0], b_gates[0], conv_w[0], conv_b[0], cosf, sinf)
    lg = jax.nn.log_sigmoid(ret_decay_logit[0].astype(F32))
    ret = _retention(lg, rq, rk, rv)
    hm = _mlstm(mq, mk, mv, kwf, kwb, gcol, grow)
    h1, u2, aff = _out_proj(x, head, ret, hm, rgs, mos, ret_gn_g[0], mlstm_gn_g[0], w_out[0],
                            ffn_norm_g[0], w_router[0])
    tiled = lambda a: a.reshape(-1, nseg, SUB, LANES)
    flat = lambda a: a.reshape(-1, LANES)
    z_rows = -(-(N_EXPERTS * t + N_EXPERTS * (c_pad - cap)) // SUB) * SUB
    routed = []
    for bi in range(b):
        aff_b = aff[bi:bi + 1]
        pm, dv, cnt, btot, maxcnt = _select(aff_b, cap)
        src, dst, gate = _compact(btot, pm, dv, aff_b, cap, c_pad, nseg, bi)
        routed.append((_row_gather(flat(u2), src), dst, gate, cnt, maxcnt))
    zs = []
    for xs, dst, gate, _, _ in routed:
        ys = _experts(gate, tiled(xs), w_gate[0], w_up[0], w_down[0])
        zs.append(tiled(_row_scatter(flat(ys), dst, z_rows * nseg)))
    maxcnt = jnp.concatenate([r[4][:, 0, :] for r in routed], axis=0).reshape(-1)
    return _combine(maxcnt, h1, zs, [r[3] for r in routed], final_norm_g)
```

```python
import functools

import jax
import jax.numpy as jnp
from jax import lax
from jax.experimental import pallas as pl
from jax.experimental.pallas import tpu as pltpu
from jax.experimental.pallas import tpu_sc as plsc

F32 = jnp.float32
BF16 = jnp.bfloat16

LANES = 128
CHUNK = 128
N_META = 16
PAD = CHUNK - N_META
HEADS = 4
DH = 128
N_EXPERTS = 16
CONV_K = 5
HALO = 8
SUB = 8
SC_WINDOW = 128
SC_SUBCORES = 32
EPS = 1e-6
NEG = -1e30
ROPE_BASE = 10000.0
VMEM_LIMIT = 56 * 1024 * 1024


def _cparams(sem, vmem=VMEM_LIMIT, **kw):
    return pltpu.CompilerParams(dimension_semantics=sem, vmem_limit_bytes=vmem, **kw)


def _dot(a, b):
    return jnp.dot(a, b, preferred_element_type=F32)


def _dot_nt(a, b):
    return lax.dot_general(a, b, (((1,), (1,)), ((), ())), preferred_element_type=F32)


def _dot_tn(a, b):
    return lax.dot_general(a, b, (((0,), (0,)), ((), ())), preferred_element_type=F32)


def _split3(x):
    hi = x.astype(BF16)
    r = x - hi.astype(F32)
    mid = r.astype(BF16)
    lo = (r - mid.astype(F32)).astype(BF16)
    return hi, mid, lo


def _dot3_left(m_bf, x):
    hi, mid, lo = _split3(x)
    return _dot(m_bf, hi) + _dot(m_bf, mid) + _dot(m_bf, lo)


def _dot3_right(x, m_bf):
    hi, mid, lo = _split3(x)
    return _dot(hi, m_bf) + _dot(mid, m_bf) + _dot(lo, m_bf)


def _rms(x, g):
    return x * lax.rsqrt(jnp.mean(x * x, axis=-1, keepdims=True) + EPS) * g


def _log_sigmoid(x):
    return jnp.minimum(x, 0.0) - jnp.log1p(jnp.exp(-jnp.abs(x)))


def _silu(x):
    return x * (1.0 / (1.0 + jnp.exp(-x)))


def _sigmoid(x):
    return 1.0 / (1.0 + jnp.exp(-x))


def _row_tile(t):
    return 640 if t % 640 == 0 else CHUNK


def _chunk_unroll(n_chunks):
    return 13 if n_chunks % 13 == 0 else 1


def _to_segments(dst_ref, x):
    r, w = x.shape
    for j in range(w // LANES):
        dst_ref[:, j] = x[:, j * LANES:(j + 1) * LANES].reshape(r // SUB, SUB, LANES)


def _from_segments(src_ref):
    g, nseg = src_ref.shape[0], src_ref.shape[1]
    return jnp.concatenate([src_ref[:, j].reshape(g * SUB, LANES) for j in range(nseg)], axis=1)


def _bf16_bits(x):
    return pltpu.bitcast(x.astype(BF16).astype(F32), jnp.int32)


def _pack_pairs(x):
    half = x.shape[1] // 2
    return _bf16_bits(x[:, :half]) | lax.shift_right_logical(_bf16_bits(x[:, half:]), 16)


def _unpack_hi(w):
    return pltpu.bitcast(w & -65536, F32)


def _unpack_lo(w):
    return pltpu.bitcast(w << 16, F32)


def _load_rows(x_refs, head_ref, i):
    blocks = [r[0] for r in x_refs]
    blocks[0] = jnp.where(i == 0, head_ref[...], blocks[0])
    return blocks


def _x_specs(tr, d):
    nb = tr // CHUNK
    return [pl.BlockSpec((1, CHUNK, d), functools.partial(
        lambda bi, i, k: (bi, jnp.maximum(i * nb - 1 + k, 0), 0), k=k)) for k in range(nb)]


def _in_proj_kernel(*refs, tr, n_tiles):
    nb = tr // CHUNK
    x_refs, refs = refs[:nb], refs[nb:]
    (head_ref, hp_ref, hn_ref, ng_ref, wr_ref, wm_ref, wg_ref, wgt_ref,
     bg_ref, bgt_ref, cw_ref, cb_ref, cos_ref, sin_ref,
     rq_ref, rk_ref, rv_ref, rg_ref, mq_ref, mk_ref, mv_ref, mo_ref,
     kwf_ref, kwb_ref, gc_ref, gr_ref, ext_ref, u_ref, gts_ref, h_ref) = refs
    i = pl.program_id(1)
    ng = ng_ref[...]
    w = HEADS * DH
    nd = 2 * HEADS
    for k, blk in enumerate(_load_rows(x_refs, head_ref, i)):
        h_ref[k * CHUNK:(k + 1) * CHUNK, :] = blk

    hp = hp_ref[0]
    if tr == CHUNK:
        hp = jnp.where(i == 1, head_ref[CHUNK - HALO:CHUNK, :], hp)
    up = _rms(hp, ng)
    un = _rms(hn_ref[0], ng)
    up = jnp.where(i == 0, 0.0, up)
    un = jnp.where(i == n_tiles - 1, 0.0, un)
    uh = jnp.concatenate([up, un], axis=0).astype(BF16)
    halo = _dot(uh, wm_ref[:, 0:2 * w])
    ext_ref[0:HALO, :] = halo[0:HALO]
    ext_ref[HALO + tr:2 * HALO + tr, :] = halo[HALO:2 * HALO]

    sub = tr // 2
    lane_s = lax.broadcasted_iota(jnp.int32, (sub, LANES), 1)
    for r0 in (0, sub):
        rs = slice(r0, r0 + sub)
        u = _rms(h_ref[rs, :], ng).astype(BF16)
        u_ref[rs, :] = u
        real = lax.broadcasted_iota(jnp.int32, (sub, 1), 0) + (i * tr + r0) >= PAD
        cosf = cos_ref[rs, :]
        sinf = sin_ref[rs, :]
        pq = _dot(u, wr_ref[:, 0:w])
        pk = _dot(u, wr_ref[:, w:2 * w])
        for hd in range(HEADS):
            sl = slice(hd * DH, (hd + 1) * DH)
            xq = pq[:, sl]
            xk = pk[:, sl]
            rq_ref[0, rs, sl] = ((xq * cosf + pltpu.roll(xq, DH // 2, 1) * sinf) * (DH ** -0.5)).astype(BF16)
            rk_ref[0, rs, sl] = (xk * cosf + pltpu.roll(xk, DH // 2, 1) * sinf).astype(BF16)
        rv_ref[0, rs, :] = _dot(u, wr_ref[:, 2 * w:3 * w]).astype(BF16)
        rg_ref[0, rs, :] = _silu(_dot(u, wr_ref[:, 3 * w:4 * w])).astype(BF16)
        ext_ref[HALO + r0:HALO + r0 + sub, :] = _dot(u, wm_ref[:, 0:2 * w])
        mv_ref[0, rs, :] = _dot(u, wm_ref[:, 2 * w:3 * w]).astype(BF16)
        mo_ref[0, rs, :] = _sigmoid(_dot(u, wm_ref[:, 3 * w:4 * w])).astype(BF16)
        gts = _dot(u, wg_ref[...]) + bg_ref[...]
        gts_ref[rs, :] = jnp.where(lane_s < nd, jnp.where(real, gts, NEG),
                                   jnp.where(real & (lane_s < 2 * nd), _log_sigmoid(gts), 0.0))

    rown = lax.broadcasted_iota(jnp.int32, (2 * nd, tr), 0)
    coln = lax.broadcasted_iota(jnp.int32, (1, tr), 1) + i * tr
    realr = coln >= PAD
    gtr = _dot_nt(wgt_ref[...], u_ref[...]) + bgt_ref[...]
    gtr = jnp.where(rown < nd, jnp.where(realr, gtr, NEG),
                    jnp.where(realr, _log_sigmoid(gtr), 0.0))
    ci = lax.broadcasted_iota(jnp.int32, (CHUNK, CHUNK), 0)
    cj = lax.broadcasted_iota(jnp.int32, (CHUNK, CHUNK), 1)
    causal = ci >= cj
    anti = ci <= cj
    low = causal.astype(BF16)
    upp = anti.astype(BF16)
    lane_c = lax.broadcasted_iota(jnp.int32, (CHUNK, LANES), 1)
    lane_1 = lax.broadcasted_iota(jnp.int32, (1, LANES), 1)
    rown_c = lax.broadcasted_iota(jnp.int32, (2 * nd, CHUNK), 0)
    for c in range(tr // CHUNK):
        rs = slice(c * CHUNK, (c + 1) * CHUNK)
        realc = lax.broadcasted_iota(jnp.int32, (CHUNK, 1), 0) + (i * tr + c * CHUNK) >= PAD
        conv = cb_ref[...] + jnp.zeros((CHUNK, 2 * w), F32)
        for k in range(CONV_K):
            off = HALO + c * CHUNK + k - CONV_K // 2
            conv = conv + cw_ref[k:k + 1, :] * ext_ref[off:off + CHUNK, :]
        qk = jnp.where(realc, _silu(conv), 0.0)
        mq_ref[0, rs, :] = (qk[:, 0:w] * (DH ** -0.5)).astype(BF16)
        mk_ref[0, rs, :] = qk[:, w:2 * w].astype(BF16)
        kconv = qk[:, w:2 * w]
        g = gts_ref[rs, :]
        lf = jnp.where(lane_c >= nd, g, 0.0)
        pre = _dot3_left(low, lf)
        suf = _dot3_left(upp, lf)
        bc = pltpu.roll(jnp.where(lane_c < nd + HEADS, pre, suf), LANES - nd, 1)
        blast = jnp.where(lane_1 < HEADS, bc[CHUNK - 1:CHUNK], bc[0:1])
        log_u = blast - bc + g
        a = jnp.max(log_u, axis=0, keepdims=True)
        wu = jnp.exp(log_u - a)
        gl = gtr[:, rs]
        lfr = jnp.where(rown_c >= nd, gl, 0.0)
        prer = _dot3_right(lfr, upp)
        sufr = _dot3_right(lfr, low)
        rr = gl[0:nd] - jnp.concatenate([prer[nd:nd + HEADS], sufr[nd + HEADS:2 * nd]], axis=0)
        for hd in range(HEADS):
            hs = slice(hd * DH, (hd + 1) * DH)
            cols = []
            for dr, (msk, kw_ref) in enumerate(((causal, kwf_ref), (anti, kwb_ref))):
                i8 = dr * HEADS + hd
                rmax = jnp.max(jnp.where(msk, rr[i8:i8 + 1, :], NEG), axis=1, keepdims=True)
                cols += [rmax, bc[:, i8:i8 + 1]]
                kw_ref[0, rs, hs] = (kconv[:, hs] * wu[:, i8:i8 + 1]).astype(BF16)
            gc_ref[0, hd, rs, :] = jnp.where(
                lane_c == 0, cols[0], jnp.where(lane_c == 1, cols[1], jnp.where(lane_c == 2, cols[2], cols[3])))
            scal = [jnp.broadcast_to(x[:, j:j + 1], (1, CHUNK))
                    for j in (hd, HEADS + hd) for x in (a, blast)]
            gr_ref[0, hd, :, rs] = jnp.concatenate(
                [rr[hd:hd + 1], rr[HEADS + hd:HEADS + hd + 1]] + scal + [jnp.zeros((2, CHUNK), F32)], axis=0)


def _in_proj(x, head, norm_g, w_in, b_gates, conv_w, conv_b, cosf, sinf):
    b, s, d = x.shape
    t = CHUNK + s
    tr = _row_tile(t)
    n_tiles = t // tr
    w = HEADS * DH
    wr = w_in[:, 0:4 * w].astype(BF16)
    wm = w_in[:, 4 * w:8 * w].astype(BF16)
    wg = jnp.pad(w_in[:, 8 * w:], ((0, 0), (0, LANES - 4 * HEADS))).astype(BF16)
    wgt = w_in[:, 8 * w:].T.astype(BF16)
    bg = jnp.pad(b_gates, (0, LANES - 4 * HEADS)).reshape(1, LANES)
    bgt = b_gates.reshape(4 * HEADS, 1)
    hb = tr // HALO
    pad_h = CHUNK // HALO
    last_h = s // HALO - 1
    seq = lambda bi, i: (bi, i, 0)
    const2 = lambda bi, i: (0, 0)
    out_bf = jax.ShapeDtypeStruct((b, t, w), BF16)
    kern = functools.partial(_in_proj_kernel, tr=tr, n_tiles=n_tiles)
    return pl.pallas_call(
        kern,
        grid=(b, n_tiles),
        in_specs=_x_specs(tr, d) + [
            pl.BlockSpec((CHUNK, d), const2),
            pl.BlockSpec((1, HALO, d), lambda bi, i: (bi, jnp.maximum(i * hb - pad_h - 1, 0), 0)),
            pl.BlockSpec((1, HALO, d), lambda bi, i: (bi, jnp.minimum((i + 1) * hb - pad_h, last_h), 0)),
            pl.BlockSpec((1, d), const2),
            pl.BlockSpec((d, 4 * w), const2),
            pl.BlockSpec((d, 4 * w), const2),
            pl.BlockSpec((d, LANES), const2),
            pl.BlockSpec((4 * HEADS, d), const2),
            pl.BlockSpec((1, LANES), const2),
            pl.BlockSpec((4 * HEADS, 1), const2),
            pl.BlockSpec((CONV_K, 2 * w), const2),
            pl.BlockSpec((1, 2 * w), const2),
            pl.BlockSpec((tr, DH), lambda bi, i: (i, 0)),
            pl.BlockSpec((tr, DH), lambda bi, i: (i, 0)),
        ],
        out_specs=[pl.BlockSpec((1, tr, w), seq)] * 10 + [
            pl.BlockSpec((1, HEADS, tr, LANES), lambda bi, i: (bi, 0, i, 0)),
            pl.BlockSpec((1, HEADS, 8, tr), lambda bi, i: (bi, 0, 0, i)),
        ],
        out_shape=[out_bf] * 10 + [
            jax.ShapeDtypeStruct((b, HEADS, t, LANES), F32),
            jax.ShapeDtypeStruct((b, HEADS, 8, t), F32),
        ],
        scratch_shapes=[pltpu.VMEM((tr + 2 * HALO, 2 * w), F32), pltpu.VMEM((tr, d), BF16),
                        pltpu.VMEM((tr, LANES), F32), pltpu.VMEM((tr, d), F32)],
        compiler_params=_cparams(("parallel", "arbitrary")),
        name="in_proj",
    )(*([x] * (tr // CHUNK)), head, x, x, norm_g.reshape(1, d), wr, wm, wg, wgt, bg, bgt,
      conv_w, conv_b.reshape(1, 2 * w), cosf, sinf)


def _retention_kernel(lg_ref, q_ref, k_ref, v_ref, o_ref, of_ref, ob_ref, *, n_chunks):
    hd = pl.program_id(1)
    lgf = lg_ref[0, hd]
    lgb = lg_ref[1, hd]
    li = lax.broadcasted_iota(jnp.int32, (CHUNK, CHUNK), 0)
    mi = lax.broadcasted_iota(jnp.int32, (CHUNK, CHUNK), 1)
    diff = (li - mi).astype(F32)
    decay = (jnp.where(diff >= 0, jnp.exp(lgf * jnp.maximum(diff, 0.0)), 0.0)
             + jnp.where(diff <= 0, jnp.exp(lgb * jnp.maximum(-diff, 0.0)), 0.0))
    lcol = lax.broadcasted_iota(jnp.int32, (CHUNK, 1), 0).astype(F32)
    one = jnp.ones((1, 1), F32)
    zeta_f = jnp.exp(lgf * (CHUNK - 1.0 - lcol))
    xi_f = jnp.exp(lgf * (lcol + 1.0))
    g_f = jnp.exp(lgf * CHUNK * one)
    zeta_b = jnp.exp(lgb * lcol)
    xi_b = jnp.exp(lgb * (CHUNK - lcol))
    g_b = jnp.exp(lgb * CHUNK * one)

    def fwd(c, state):
        rows = pl.ds(pl.multiple_of(c * CHUNK, CHUNK), CHUNK)
        q = q_ref[0, rows, :]
        k = k_ref[0, rows, :]
        v = v_ref[0, rows, :]
        s = _dot_nt(q, k) * decay
        intra = _dot(s.astype(BF16), v)
        cross = _dot(q, state.astype(BF16)) * xi_f
        of_ref[rows, :] = intra + cross
        kz = (k.astype(F32) * zeta_f).astype(BF16)
        return g_f * state + _dot_tn(kz, v)

    def bwd(j, state):
        c = n_chunks - 1 - j
        rows = pl.ds(pl.multiple_of(c * CHUNK, CHUNK), CHUNK)
        q = q_ref[0, rows, :]
        k = k_ref[0, rows, :]
        v = v_ref[0, rows, :]
        ob_ref[rows, :] = _dot(q, state.astype(BF16)) * xi_b
        kz = (k.astype(F32) * zeta_b).astype(BF16)
        return g_b * state + _dot_tn(kz, v)

    zero = jnp.zeros((DH, DH), F32)
    lax.fori_loop(0, n_chunks, lambda j, st: (fwd(j, st[0]), bwd(j, st[1])), (zero, zero),
                  unroll=_chunk_unroll(n_chunks))
    o_ref[0] = (of_ref[...] + ob_ref[...]).astype(BF16)


def _retention(lg, rq, rk, rv):
    b, t, _ = rq.shape
    spec = pl.BlockSpec((1, t, DH), lambda bi, hi, lg_ref: (bi, 0, hi))
    return pl.pallas_call(
        functools.partial(_retention_kernel, n_chunks=t // CHUNK),
        grid_spec=pltpu.PrefetchScalarGridSpec(
            num_scalar_prefetch=1,
            grid=(b, HEADS),
            in_specs=[spec, spec, spec],
            out_specs=spec,
            scratch_shapes=[pltpu.VMEM((t, DH), F32)] * 2,
        ),
        out_shape=jax.ShapeDtypeStruct((b, t, HEADS * DH), BF16),
        compiler_params=_cparams(("parallel", "arbitrary")),
        name="retention",
    )(lg, rq, rk, rv)


def _mlstm_kernel(q_ref, k_ref, v_ref, kwf_ref, kwb_ref, gc_ref, gr_ref, o_ref, of_ref, ob_ref, *, n_chunks):
    li = lax.broadcasted_iota(jnp.int32, (CHUNK, CHUNK), 0)
    mi = lax.broadcasted_iota(jnp.int32, (CHUNK, CHUNK), 1)
    causal = li >= mi
    anti = li <= mi

    ones = jnp.ones((CHUNK, DH), BF16)

    def step(c, carry, backward):
        st, m_st = carry
        rows = pl.ds(pl.multiple_of(c * CHUNK, CHUNK), CHUNK)
        q = q_ref[0, rows, :]
        k = k_ref[0, rows, :]
        v_aug = jnp.concatenate([v_ref[0, rows, :], ones], axis=1)
        gcol = gc_ref[0, 0, rows, :]
        grow = gr_ref[0, 0, :, rows]
        if backward:
            kw = kwb_ref[0, rows, :]
            r_max, bc = gcol[:, 2:3], gcol[:, 3:4]
            r_row, a, b_last = grow[1:2, :], grow[4:5, 0:1], grow[5:6, 0:1]
            mask = anti
        else:
            kw = kwf_ref[0, rows, :]
            r_max, bc = gcol[:, 0:1], gcol[:, 1:2]
            r_row, a, b_last = grow[0:1, :], grow[2:3, 0:1], grow[3:4, 0:1]
            mask = causal
        mm = jnp.maximum(jnp.broadcast_to(r_max, (CHUNK, DH)), m_st)
        s = _dot_nt(q, k) * jnp.where(mask, jnp.exp(r_row - mm), 0.0)
        w_inter = jnp.exp(m_st - mm)
        w2 = jnp.concatenate([w_inter, w_inter], axis=1)
        tot = _dot(s.astype(BF16), v_aug) + _dot(q, st.astype(BF16)) * w2
        floor = jnp.exp(-(jnp.broadcast_to(bc, (CHUNK, DH)) + mm))
        hval = tot[:, 0:DH] / jnp.maximum(jnp.abs(tot[:, DH:2 * DH]), floor)
        if backward:
            ob_ref[rows, :] = hval
        else:
            of_ref[rows, :] = hval
        upd = _dot_tn(kw, v_aug)
        m_new = jnp.maximum(b_last + m_st, a)
        f = jnp.exp(b_last + m_st - m_new)
        g = jnp.exp(a - m_new)
        return f * st + g * upd, m_new

    def both(j, carry):
        return step(j, carry[0], False), step(n_chunks - 1 - j, carry[1], True)

    init = (jnp.zeros((DH, 2 * DH), F32), jnp.zeros((1, 1), F32))
    lax.fori_loop(0, n_chunks, both, (init, init), unroll=_chunk_unroll(n_chunks))
    o_ref[0] = (of_ref[...] + ob_ref[...]).astype(BF16)


def _mlstm(mq, mk, mv, kwf, kwb, gcol, grow):
    b, t, _ = mq.shape
    spec = pl.BlockSpec((1, t, DH), lambda bi, hi: (bi, 0, hi))
    return pl.pallas_call(
        functools.partial(_mlstm_kernel, n_chunks=t // CHUNK),
        grid=(b, HEADS),
        in_specs=[spec, spec, spec, spec, spec,
                  pl.BlockSpec((1, 1, t, LANES), lambda bi, hi: (bi, hi, 0, 0)),
                  pl.BlockSpec((1, 1, 8, t), lambda bi, hi: (bi, hi, 0, 0))],
        out_specs=spec,
        out_shape=jax.ShapeDtypeStruct((b, t, HEADS * DH), BF16),
        scratch_shapes=[pltpu.VMEM((t, DH), F32)] * 2,
        compiler_params=_cparams(("parallel", "arbitrary")),
        name="mlstm",
    )(mq, mk, mv, kwf, kwb, gcol, grow)


def _head_norm(y, g):
    outs = []
    for hd in range(HEADS):
        yh = y[:, hd * DH:(hd + 1) * DH]
        mu = jnp.mean(yh, axis=1, keepdims=True)
        yc = yh - mu
        var = jnp.mean(yc * yc, axis=1, keepdims=True)
        outs.append(yc * lax.rsqrt(var + EPS))
    return jnp.concatenate(outs, axis=1) * g


def _out_proj_kernel(*refs, tr):
    nb = tr // CHUNK
    x_refs, refs = refs[:nb], refs[nb:]
    (head_ref, ret_ref, hm_ref, rg_ref, mo_ref, rgn_ref, mgn_ref, wo_ref,
     fg_ref, wrt_ref, h1_ref, u2_ref, aff_ref) = refs
    i = pl.program_id(1)
    row = lax.broadcasted_iota(jnp.int32, (tr, 1), 0) + i * tr
    real = row >= PAD
    w = HEADS * DH
    y_ret = _head_norm(ret_ref[0].astype(F32), rgn_ref[...]) * rg_ref[0].astype(F32)
    y_m = _head_norm(mo_ref[0].astype(F32) * hm_ref[0].astype(F32), mgn_ref[...])
    y_ret = jnp.where(real, y_ret, 0.0).astype(BF16)
    y_m = jnp.where(real, y_m, 0.0).astype(BF16)
    h = jnp.concatenate(_load_rows(x_refs, head_ref, i), axis=0)
    h1 = h + _dot(y_ret, wo_ref[0:w, :]) + _dot(y_m, wo_ref[w:2 * w, :])
    h1_ref[0] = h1
    u2 = _rms(h1, fg_ref[...])
    _to_segments(u2_ref, _pack_pairs(u2))
    logits = _dot_nt(wrt_ref[...], u2.astype(BF16))
    mx = jnp.max(logits, axis=0, keepdims=True)
    ex = jnp.exp(logits - mx)
    aff = ex / jnp.sum(ex, axis=0, keepdims=True)
    coln = lax.broadcasted_iota(jnp.int32, (1, tr), 1) + i * tr
    aff_ref[0] = jnp.where(coln >= PAD, aff, -1.0)


def _out_proj(x, head, ret, hm, rgs, mos, ret_gn_g, mlstm_gn_g, w_out, ffn_norm_g, w_router):
    b, s, d = x.shape
    t = CHUNK + s
    tr = _row_tile(t)
    n_tiles = t // tr
    nseg = d // 2 // LANES
    w = HEADS * DH
    seq = lambda bi, i: (bi, i, 0)
    const2 = lambda bi, i: (0, 0)
    return pl.pallas_call(
        functools.partial(_out_proj_kernel, tr=tr),
        grid=(b, n_tiles),
        in_specs=_x_specs(tr, d) + [
            pl.BlockSpec((CHUNK, d), const2),
            pl.BlockSpec((1, tr, w), seq),
            pl.BlockSpec((1, tr, w), seq),
            pl.BlockSpec((1, tr, w), seq),
            pl.BlockSpec((1, tr, w), seq),
            pl.BlockSpec((1, w), const2),
            pl.BlockSpec((1, w), const2),
            pl.BlockSpec((2 * w, d), const2),
            pl.BlockSpec((1, d), const2),
            pl.BlockSpec((N_EXPERTS, d), const2),
        ],
        out_specs=[
            pl.BlockSpec((1, tr, d), seq),
            pl.BlockSpec((tr // SUB, nseg, SUB, LANES), lambda bi, i: (bi * n_tiles + i, 0, 0, 0)),
            pl.BlockSpec((1, N_EXPERTS, tr), lambda bi, i: (bi, 0, i)),
        ],
        out_shape=[
            jax.ShapeDtypeStruct((b, t, d), F32),
            jax.ShapeDtypeStruct((b * t // SUB, nseg, SUB, LANES), jnp.int32),
            jax.ShapeDtypeStruct((b, N_EXPERTS, t), F32),
        ],
        compiler_params=_cparams(("parallel", "arbitrary")),
        name="out_proj",
    )(*([x] * (tr // CHUNK)), head, ret, hm, rgs, mos, ret_gn_g.reshape(1, w), mlstm_gn_g.reshape(1, w),
      w_out.astype(BF16), ffn_norm_g.reshape(1, d), w_router.T.astype(BF16))


def _select_kernel(aff_ref, pm_ref, dv_ref, cnt_ref, bt_ref, mc_ref, sel_ref, *, t, cap, tr):
    e_n = N_EXPERTS
    nblk = t // LANES
    bits = pltpu.bitcast(aff_ref[0], jnp.int32)
    capf = float(cap)

    def search(_, lohi):
        lo, hi = lohi
        mid = lo + ((hi - lo + 1) >> 1)
        cnt = jnp.sum((bits >= mid).astype(F32), axis=1, keepdims=True)
        ok = cnt >= capf
        return jnp.where(ok, mid, lo), jnp.where(ok, hi, mid - 1)

    lo0 = jnp.zeros((e_n, 1), jnp.int32)
    hi0 = jnp.full((e_n, 1), 0x3F800000, jnp.int32)
    thr, _ = lax.fori_loop(0, 31, search, (lo0, hi0))
    need = capf - jnp.sum((bits > thr).astype(F32), axis=1, keepdims=True)

    ci = lax.broadcasted_iota(jnp.int32, (LANES, LANES), 0)
    cj = lax.broadcasted_iota(jnp.int32, (LANES, LANES), 1)
    upp = (ci <= cj).astype(BF16)
    ei = lax.broadcasted_iota(jnp.int32, (e_n, e_n), 0)
    ej = lax.broadcasted_iota(jnp.int32, (e_n, e_n), 1)
    strict = (ej < ei).astype(BF16)

    carry = jnp.zeros((e_n, 1), F32)
    for j in range(nblk):
        sl = slice(j * LANES, (j + 1) * LANES)
        bb = bits[:, sl]
        eqf = (bb == thr).astype(F32)
        ceq = _dot(eqf.astype(BF16), upp) + carry
        carry = ceq[:, LANES - 1:LANES]
        sel_ref[:, sl] = jnp.where(bb > thr, 1.0, jnp.where(ceq <= need, eqf, 0.0))

    carry = jnp.zeros((e_n, 1), F32)
    tok = lax.broadcasted_iota(jnp.int32, (1, LANES), 1).astype(F32)
    lane_b = lax.broadcasted_iota(jnp.int32, (e_n, LANES), 1)
    btot = jnp.zeros((e_n, LANES), F32)
    tile_max = jnp.zeros((SUB, LANES), F32)
    lane_t = lax.broadcasted_iota(jnp.int32, (SUB, LANES), 1)
    for j in range(nblk):
        sl = slice(j * LANES, (j + 1) * LANES)
        selb = sel_ref[:, sl]
        selbf = selb.astype(BF16)
        pin = _dot(selbf, upp) + carry
        carry = pin[:, LANES - 1:LANES]
        btot = jnp.where(lane_b == j, carry, btot)
        pm_ref[0, :, sl] = jnp.where(selb > 0, pin, 0.0)
        rank = _dot(strict, selbf)
        dv_ref[0, :, sl] = (tok + float(j * LANES)) * float(N_EXPERTS) + rank
        cnt_row = jnp.sum(selb, axis=0, keepdims=True)
        cnt_ref[0, sl, :] = jnp.broadcast_to(cnt_row, (LANES, LANES)).T
        tile_max = jnp.where(lane_t == (j * LANES) // tr,
                             jnp.maximum(tile_max, jnp.max(cnt_row, axis=1, keepdims=True)), tile_max)
    bt_ref[0] = jnp.where(lane_b < nblk, btot, carry).astype(jnp.int32)
    mc_ref[0] = tile_max.astype(jnp.int32)


def _select(aff, cap):
    b, e_n, t = aff.shape
    tr = _row_tile(t)
    assert t // tr <= LANES
    spec = pl.BlockSpec((1, e_n, t), lambda bi: (bi, 0, 0))
    return pl.pallas_call(
        functools.partial(_select_kernel, t=t, cap=cap, tr=tr),
        grid=(b,),
        in_specs=[spec],
        out_specs=[
            spec,
            spec,
            pl.BlockSpec((1, t, LANES), lambda bi: (bi, 0, 0)),
            pl.BlockSpec((1, e_n, LANES), lambda bi: (bi, 0, 0)),
            pl.BlockSpec((1, SUB, LANES), lambda bi: (bi, 0, 0)),
        ],
        out_shape=[
            jax.ShapeDtypeStruct((b, e_n, t), F32),
            jax.ShapeDtypeStruct((b, e_n, t), F32),
            jax.ShapeDtypeStruct((b, t, LANES), F32),
            jax.ShapeDtypeStruct((b, e_n, LANES), jnp.int32),
            jax.ShapeDtypeStruct((b, SUB, LANES), jnp.int32),
        ],
        scratch_shapes=[pltpu.VMEM((e_n, t), F32)],
        compiler_params=_cparams(("parallel",)),
        name="select",
    )(aff)


def _compact_kernel(bt_ref, pm_ref, dv_ref, aff_ref, src_ref, dst_ref, gate_ref, accd_ref, accg_ref,
                    *, t, cap, c_pad, nct, nseg, batch):
    bi = batch
    e_n = N_EXPERTS
    nblk = t // LANES
    lane_e = lax.broadcasted_iota(jnp.int32, (LANES, LANES), 1)
    slot_l = lax.broadcasted_iota(jnp.int32, (1, LANES), 1)

    spr = LANES // nseg
    sh_spr = spr.bit_length() - 1
    sh_grp = (SUB * nseg).bit_length() - 1
    sub_i = lax.broadcasted_iota(jnp.int32, (SUB, LANES), 0)
    lane_i = lax.broadcasted_iota(jnp.int32, (SUB, LANES), 1)
    pick = (lane_i >> sh_spr) == sub_i
    pl_i = lax.broadcasted_iota(jnp.int32, (LANES, LANES), 0)
    pl_j = lax.broadcasted_iota(jnp.int32, (LANES, LANES), 1)
    perm = ((pl_i & (spr - 1)) == ((pl_j >> sh_grp) << 3) + (pl_j & (SUB - 1))).astype(BF16)
    seg_j = (lane_i & (SUB * nseg - 1)) >> 3

    def segment_tile(rows):
        spread = _dot3_right(jnp.where(pick, rows.astype(F32), 0.0), perm).astype(jnp.int32)
        return (((spread >> 3) * nseg + seg_j) << 3) + (spread & (SUB - 1))

    def expert(e):
        base = e * LANES

        def ctile(ct, j0):
            c0 = pl.multiple_of(ct * LANES, LANES)
            j_lo = lax.while_loop(
                lambda j: (j < nblk) & (bt_ref[base + jnp.minimum(j, nblk - 1)] <= c0),
                lambda j: j + 1, j0)
            j_hi = lax.while_loop(
                lambda j: (j < nblk) & (bt_ref[base + jnp.minimum(j, nblk - 1)] < c0 + LANES),
                lambda j: j + 1, j_lo)
            j_end = jnp.minimum(j_hi + 1, nblk)
            want = (lax.broadcasted_iota(jnp.int32, (LANES, 1), 0) + (c0 + 1)).astype(F32)

            def blk(j, acc):
                acc_d, acc_g = acc
                cols = pl.ds(pl.multiple_of(j * LANES, LANES), LANES)
                hit = pm_ref[0, e:e + 1, cols] == want
                acc_d = acc_d + jnp.where(hit, dv_ref[0, e:e + 1, cols], 0.0)
                acc_g = acc_g + jnp.where(hit, aff_ref[0, e:e + 1, cols], 0.0)
                return acc_d, acc_g

            zero = jnp.zeros((LANES, LANES), F32)
            acc_d, acc_g = lax.fori_loop(j_lo, j_end, blk, (zero, zero))
            accd_ref[e * nct + ct] = acc_d
            accg_ref[e * nct + ct] = acc_g
            return j_lo

        lax.fori_loop(0, nct, ctile, 0)

    for e in range(e_n):
        expert(e)

    def finish(ct, _):
        c0 = pl.multiple_of(ct * LANES, LANES)
        p = slot_l + (c0 - cap)
        is_pad = p >= 0
        gate_tile = jnp.zeros((LANES, LANES), F32)
        for e in range(e_n):
            v = jnp.sum(accd_ref[e * nct + ct].T, axis=0, keepdims=True).astype(jnp.int32)
            tok = v >> 4
            rank = v & (N_EXPERTS - 1)
            src = bi * t + jnp.where(is_pad, 0, tok)
            dst = jnp.where(is_pad, N_EXPERTS * t + e * (c_pad - cap) + p, rank * t + tok)
            out_rows = pl.ds(pl.multiple_of((e * nct + ct) * SUB, SUB), SUB)
            src_ref[0, out_rows, :] = segment_tile(src)
            dst_ref[0, out_rows, :] = segment_tile(dst)
            gcol = jnp.sum(accg_ref[e * nct + ct], axis=1, keepdims=True)
            gate_tile = jnp.where(lane_e == e, gcol, gate_tile)
        gate_ref[0, pl.ds(c0, LANES), :] = gate_tile
        return 0

    lax.fori_loop(0, nct, finish, 0)


def _compact(btot, pm, dv, aff, cap, c_pad, nseg, batch):
    _, e_n, t = aff.shape
    assert nseg in (1, 2, 4, 8)
    nct = -(-c_pad // LANES)
    tile_rows = e_n * nct * SUB
    spec = pl.BlockSpec((1, e_n, t), lambda i, bt: (0, 0, 0))
    ospec = pl.BlockSpec((1, tile_rows, LANES), lambda i, bt: (0, 0, 0))
    src, dst, gate = pl.pallas_call(
        functools.partial(_compact_kernel, t=t, cap=cap, c_pad=c_pad, nct=nct, nseg=nseg, batch=batch),
        grid_spec=pltpu.PrefetchScalarGridSpec(
            num_scalar_prefetch=1,
            grid=(1,),
            in_specs=[spec, spec, spec],
            out_specs=[ospec, ospec, pl.BlockSpec((1, nct * LANES, LANES), lambda i, bt: (0, 0, 0))],
            scratch_shapes=[pltpu.VMEM((e_n * nct, LANES, LANES), F32)] * 2,
        ),
        out_shape=[
            jax.ShapeDtypeStruct((1, tile_rows, LANES), jnp.int32),
            jax.ShapeDtypeStruct((1, tile_rows, LANES), jnp.int32),
            jax.ShapeDtypeStruct((1, nct * LANES, LANES), F32),
        ],
        compiler_params=_cparams(("arbitrary",)),
        name="compact",
    )(btot.reshape(e_n * LANES), pm, dv, aff)
    ids = lambda a: a.reshape(e_n, nct, SUB, LANES)[:, :, :nseg].reshape(e_n, -1)[:, :c_pad * nseg].reshape(-1)
    return ids(src), ids(dst), gate


def _sc_mesh():
    return plsc.VectorSubcoreMesh(core_axis_name="core", subcore_axis_name="subcore")


def _row_gather(table, idx):
    m = idx.shape[0]
    d = table.shape[1]

    @functools.partial(pl.kernel, out_type=jax.ShapeDtypeStruct((m, d), table.dtype),
                       mesh=_sc_mesh(), scratch_types=[])
    def gather_kernel(x_hbm, i_hbm, o_hbm):
        def body(i_vmem, o_vmem):
            pltpu.sync_copy(x_hbm.at[i_vmem.at[0]], o_vmem)

        pltpu.emit_pipeline(
            body,
            grid=(m // SC_WINDOW,),
            in_specs=[pl.BlockSpec((1, SC_WINDOW), lambda i: (0, i))],
            out_specs=[pl.BlockSpec((SC_WINDOW, d), lambda i: (i, 0))],
            core_axis_name=("core", "subcore"),
            dimension_semantics=(pltpu.PARALLEL,),
        )(i_hbm, o_hbm)

    return gather_kernel(table, idx.reshape(1, m))


def _row_scatter(rows, idx, n_out):
    m, d = rows.shape

    @functools.partial(pl.kernel, out_type=jax.ShapeDtypeStruct((n_out, d), rows.dtype),
                       mesh=_sc_mesh(), scratch_types=[])
    def scatter_kernel(x_hbm, i_hbm, o_hbm):
        def body(x_vmem, i_vmem):
            pltpu.sync_copy(x_vmem, o_hbm.at[i_vmem.at[0]])

        pltpu.emit_pipeline(
            body,
            grid=(m // SC_WINDOW,),
            in_specs=[pl.BlockSpec((SC_WINDOW, d), lambda i: (i, 0)),
                      pl.BlockSpec((1, SC_WINDOW), lambda i: (0, i))],
            out_specs=[],
            core_axis_name=("core", "subcore"),
            dimension_semantics=(pltpu.PARALLEL,),
        )(x_hbm, i_hbm)

    return scatter_kernel(rows, idx.reshape(1, m))


def _expert_kernel(gate_ref, x_ref, wg_ref, wu_ref, wd_ref, y_ref, *rest, c_pad, n_f):
    (xb_ref, acc_ref), bf_refs = rest[-2:], rest[:-2]
    e = pl.program_id(0)
    f = pl.program_id(1)

    @pl.when(f == 0)
    def _():
        w = _from_segments(x_ref)
        xb_ref[...] = jnp.concatenate([_unpack_hi(w), _unpack_lo(w)], axis=1).astype(BF16)
        acc_ref[...] = jnp.zeros(acc_ref.shape, F32)

    wg = wg_ref[0].astype(BF16)
    wu = wu_ref[0].astype(BF16)
    wd = wd_ref[0].astype(BF16)
    for ref, val in zip(bf_refs, (wg, wu, wd)):
        ref[0] = val
    half = c_pad // 2
    for r0 in (0, half):
        rows = slice(r0, r0 + half)
        x = xb_ref[rows, :]
        hid = _silu(_dot(x, wg)) * _dot(x, wu)
        part = _dot(hid.astype(BF16), wd)
        acc_ref[rows, :] = acc_ref[rows, :] + part

    @pl.when(f == n_f - 1)
    def _():
        lane = lax.broadcasted_iota(jnp.int32, (c_pad, LANES), 1)
        gcol = jnp.sum(jnp.where(lane == e, gate_ref[0], 0.0), axis=1, keepdims=True)
        _to_segments(y_ref, _pack_pairs(acc_ref[...] * gcol))


def _experts(gate, xs, w_gate, w_up, w_down):
    e_n, d, d_ff = w_gate.shape
    nseg = d // 2 // LANES
    c_pad = xs.shape[0] * SUB // e_n
    fc = 256
    n_f = d_ff // fc
    blk = pl.BlockSpec((c_pad // SUB, nseg, SUB, LANES), lambda e, f: (e, 0, 0, 0))
    w_specs = [
        pl.BlockSpec((1, d, fc), lambda e, f: (e, 0, f)),
        pl.BlockSpec((1, d, fc), lambda e, f: (e, 0, f)),
        pl.BlockSpec((1, fc, d), lambda e, f: (e, f, 0)),
    ]
    emit = w_gate.dtype != BF16
    weights = (w_gate, w_up, w_down)
    out = pl.pallas_call(
        functools.partial(_expert_kernel, c_pad=c_pad, n_f=n_f),
        grid=(e_n, n_f),
        in_specs=[pl.BlockSpec((1, c_pad, LANES), lambda e, f: (0, 0, 0)), blk] + w_specs,
        out_specs=[blk] + (w_specs if emit else []),
        out_shape=[jax.ShapeDtypeStruct(xs.shape, jnp.int32)]
        + ([jax.ShapeDtypeStruct(w.shape, BF16) for w in weights] if emit else []),
        scratch_shapes=[pltpu.VMEM((c_pad, d), BF16), pltpu.VMEM((c_pad, d), F32)],
        compiler_params=_cparams(("parallel", "arbitrary")),
        name="experts",
    )(gate, xs, *weights)
    return out[0], (tuple(out[1:]) if emit else weights)


LAYERS_PER_STEP = 2


def _combine_kernel(mc_ref, h1_ref, fg_ref, *rest, n_steps, nb, n_tiles):
    cnt_refs = rest[:nb]
    z_refs = rest[nb:nb + nb * LAYERS_PER_STEP]
    o_hbm, acc_ref, obuf_ref, sem = rest[-4:]
    bi = pl.program_id(0)
    i = pl.program_id(1)
    j = pl.program_id(2)
    blocks = obuf_ref.shape[1] // CHUNK

    def tile_copies(b_, i_, slot, fn):
        for k in range(blocks):
            dst = pl.multiple_of((i_ * blocks - 1 + k) * CHUNK, CHUNK)
            copy = pltpu.make_async_copy(obuf_ref.at[slot, pl.ds(k * CHUNK, CHUNK), :],
                                         o_hbm.at[b_, pl.ds(dst, CHUNK), :], sem.at[slot])
            if k == 0:
                pl.when(i_ > 0)(functools.partial(fn, copy))
            else:
                fn(copy)

    def tile_before(steps):
        wrap = i < steps
        return bi - wrap.astype(jnp.int32), i - steps + jnp.where(wrap, n_tiles, 0)

    @pl.when(j == 0)
    def _():
        _to_segments(acc_ref, h1_ref[0])

    def add_layers(k, n_take):
        g, nz = z_refs[0].shape[0], z_refs[0].shape[1]
        cnt = cnt_refs[k][0].reshape(g, 1, SUB, LANES)
        hi = acc_ref[:, 0:nz]
        lo = acc_ref[:, nz:2 * nz]
        for l in range(n_take):
            take = cnt > (j * LAYERS_PER_STEP + l).astype(F32)
            w = z_refs[k * LAYERS_PER_STEP + l][...]
            hi = hi + jnp.where(take, _unpack_hi(w), 0.0)
            lo = lo + jnp.where(take, _unpack_lo(w), 0.0)
        acc_ref[:, 0:nz] = hi
        acc_ref[:, nz:2 * nz] = lo

    live = mc_ref[bi * LANES + i] - j * LAYERS_PER_STEP
    for k in range(nb):
        for n_take in range(1, LAYERS_PER_STEP + 1):
            last = n_take == LAYERS_PER_STEP
            pl.when((bi == k) & ((live >= n_take) if last else (live == n_take)))(
                functools.partial(add_layers, k, n_take))

    @pl.when(j == n_steps - 1)
    def _():
        idx = bi * n_tiles + i
        slot = idx % 2

        @pl.when(idx >= 2)
        def _():
            b2, i2 = tile_before(2)
            tile_copies(b2, i2, slot, lambda c: c.wait())

        obuf_ref[slot] = _rms(_from_segments(acc_ref), fg_ref[...])
        tile_copies(bi, i, slot, lambda c: c.start())

        @pl.when(idx == nb * n_tiles - 1)
        def _():
            if nb * n_tiles >= 2:
                b1, i1 = tile_before(1)
                tile_copies(b1, i1, 1 - slot, lambda c: c.wait())
            tile_copies(bi, i, slot, lambda c: c.wait())


def _combine(maxcnt, h1, zs, cnts, final_norm_g):
    b, t, d = h1.shape
    n_steps = N_EXPERTS // LAYERS_PER_STEP
    tr = _row_tile(t)
    n_tiles = t // tr

    def z_map(k, l):
        def index(bi, i, j, mc):
            layer = jnp.maximum(jnp.minimum(j * LAYERS_PER_STEP + l, mc[bi * LANES + i] - 1), 0)
            return (jnp.where(bi == k, layer * n_tiles + i, 0), 0, 0, 0)
        return index

    def cnt_map(k):
        return lambda bi, i, j, mc: (0, jnp.where(bi == k, i, 0), 0)

    seg_block = (tr // SUB, d // LANES, SUB, LANES)
    z_block = (tr // SUB, d // 2 // LANES, SUB, LANES)
    z_specs = [pl.BlockSpec(z_block, z_map(k, l)) for k in range(b) for l in range(LAYERS_PER_STEP)]
    z_args = [zs[k] for k in range(b) for _ in range(LAYERS_PER_STEP)]
    return pl.pallas_call(
        functools.partial(_combine_kernel, n_steps=n_steps, nb=b, n_tiles=n_tiles),
        grid_spec=pltpu.PrefetchScalarGridSpec(
            num_scalar_prefetch=1,
            grid=(b, n_tiles, n_steps),
            in_specs=[
                pl.BlockSpec((1, tr, d), lambda bi, i, j, mc: (bi, i, 0)),
                pl.BlockSpec((1, d), lambda bi, i, j, mc: (0, 0)),
            ] + [pl.BlockSpec((1, tr, LANES), cnt_map(k)) for k in range(b)] + z_specs,
            out_specs=pl.BlockSpec(memory_space=pl.ANY),
            scratch_shapes=[pltpu.VMEM(seg_block, F32), pltpu.VMEM((2, tr, d), F32),
                            pltpu.SemaphoreType.DMA((2,))],
        ),
        out_shape=jax.ShapeDtypeStruct((b, t - CHUNK, d), F32),
        compiler_params=_cparams(("arbitrary", "arbitrary", "arbitrary")),
        name="combine",
    )(maxcnt, h1, final_norm_g.reshape(1, d), *cnts, *z_args)


def kernel(x, meta_tokens, mix_norm_g, w_in, b_gates, conv_w, conv_b, ret_decay_logit, ret_gn_g,
           mlstm_gn_g, w_out, ffn_norm_g, w_router, w_gate, w_up, w_down, final_norm_g):
    b, s, d = x.shape
    t = CHUNK + s
    n = N_META + s
    cap = 2 * n // N_EXPERTS
    nseg = d // 2 // LANES
    c_pad = next(c for c in range(-(-cap // 16) * 16, cap + 4096, 16)
                 if (N_EXPERTS * c * nseg) % (SC_WINDOW * SC_SUBCORES) == 0)

    head = jnp.concatenate([jnp.zeros((PAD, d), x.dtype), meta_tokens.astype(x.dtype)], axis=0)

    half = DH // 2
    pos = jnp.arange(t, dtype=F32) - PAD
    inv = ROPE_BASE ** (-jnp.arange(half, dtype=F32) / half)
    ang = pos[:, None] * inv[None, :]
    cosf = jnp.concatenate([jnp.cos(ang), jnp.cos(ang)], axis=1)
    sinf = jnp.concatenate([-jnp.sin(ang), jnp.sin(ang)], axis=1)

    assert mix_norm_g.shape[0] == 1, "single-layer block only"
    rq, rk, rv, rgs, mq, mk, mv, mos, kwf, kwb, gcol, grow = _in_proj(
        x, head, mix_norm_g[0], w_in[0], b_gates[0], conv_w[0], conv_b[0], cosf, sinf)
    lg = jax.nn.log_sigmoid(ret_decay_logit[0].astype(F32))
    ret = _retention(lg, rq, rk, rv)
    hm = _mlstm(mq, mk, mv, kwf, kwb, gcol, grow)
    h1, u2, aff = _out_proj(x, head, ret, hm, rgs, mos, ret_gn_g[0], mlstm_gn_g[0], w_out[0],
                            ffn_norm_g[0], w_router[0])
    tiled = lambda a: a.reshape(-1, nseg, SUB, LANES)
    flat = lambda a: a.reshape(-1, LANES)
    z_rows = -(-(N_EXPERTS * t + N_EXPERTS * (c_pad - cap)) // SUB) * SUB
    routed = []
    for bi in range(b):
        aff_b = aff[bi:bi + 1]
        pm, dv, cnt, btot, maxcnt = _select(aff_b, cap)
        src, dst, gate = _compact(btot, pm, dv, aff_b, cap, c_pad, nseg, bi)
        routed.append((_row_gather(flat(u2), src), dst, gate, cnt, maxcnt))
    zs = []
    weights = (w_gate[0], w_up[0], w_down[0])
    for xs, dst, gate, _, _ in routed:
        ys, weights = _experts(gate, tiled(xs), *weights)
        zs.append(tiled(_row_scatter(flat(ys), dst, z_rows * nseg)))
    maxcnt = jnp.concatenate([r[4][:, 0, :] for r in routed], axis=0).reshape(-1)
    return _combine(maxcnt, h1, zs, [r[3] for r in routed], final_norm_g)
```

```python
import functools

import jax
import jax.numpy as jnp
from jax import lax
from jax.experimental import pallas as pl
from jax.experimental.pallas import tpu as pltpu
from jax.experimental.pallas import tpu_sc as plsc

F32 = jnp.float32
BF16 = jnp.bfloat16

LANES = 128
CHUNK = 128
N_META = 16
PAD = CHUNK - N_META
HEADS = 4
DH = 128
N_EXPERTS = 16
CONV_K = 5
HALO = 8
SUB = 8
SC_WINDOW = 128
SC_SUBCORES = 32
EPS = 1e-6
NEG = -1e30
ROPE_BASE = 10000.0
VMEM_LIMIT = 56 * 1024 * 1024


def _cparams(sem, vmem=VMEM_LIMIT, **kw):
    return pltpu.CompilerParams(dimension_semantics=sem, vmem_limit_bytes=vmem, **kw)


def _dot(a, b):
    return jnp.dot(a, b, preferred_element_type=F32)


def _dot_nt(a, b):
    return lax.dot_general(a, b, (((1,), (1,)), ((), ())), preferred_element_type=F32)


def _dot_tn(a, b):
    return lax.dot_general(a, b, (((0,), (0,)), ((), ())), preferred_element_type=F32)


def _split3(x):
    hi = x.astype(BF16)
    r = x - hi.astype(F32)
    mid = r.astype(BF16)
    lo = (r - mid.astype(F32)).astype(BF16)
    return hi, mid, lo


def _dot3_left(m_bf, x):
    hi, mid, lo = _split3(x)
    return _dot(m_bf, hi) + _dot(m_bf, mid) + _dot(m_bf, lo)


def _dot3_right(x, m_bf):
    hi, mid, lo = _split3(x)
    return _dot(hi, m_bf) + _dot(mid, m_bf) + _dot(lo, m_bf)


def _rms(x, g):
    return x * lax.rsqrt(jnp.mean(x * x, axis=-1, keepdims=True) + EPS) * g


def _log_sigmoid(x):
    return jnp.minimum(x, 0.0) - jnp.log1p(jnp.exp(-jnp.abs(x)))


def _silu(x):
    return x * (1.0 / (1.0 + jnp.exp(-x)))


def _sigmoid(x):
    return 1.0 / (1.0 + jnp.exp(-x))


def _row_tile(t):
    return 640 if t % 640 == 0 else CHUNK


def _chunk_unroll(n_chunks):
    return 13 if n_chunks % 13 == 0 else 1


def _to_segments(dst_ref, x):
    r, w = x.shape
    for j in range(w // LANES):
        dst_ref[:, j] = x[:, j * LANES:(j + 1) * LANES].reshape(r // SUB, SUB, LANES)


def _from_segments(src_ref):
    g, nseg = src_ref.shape[0], src_ref.shape[1]
    return jnp.concatenate([src_ref[:, j].reshape(g * SUB, LANES) for j in range(nseg)], axis=1)


def _bf16_bits(x):
    return pltpu.bitcast(x.astype(BF16).astype(F32), jnp.int32)


def _pack_pairs(x):
    half = x.shape[1] // 2
    return _bf16_bits(x[:, :half]) | lax.shift_right_logical(_bf16_bits(x[:, half:]), 16)


def _unpack_hi(w):
    return pltpu.bitcast(w & -65536, F32)


def _unpack_lo(w):
    return pltpu.bitcast(w << 16, F32)


def _load_rows(x_refs, head_ref, i):
    blocks = [r[0] for r in x_refs]
    blocks[0] = jnp.where(i == 0, head_ref[...], blocks[0])
    return blocks


def _x_specs(tr, d):
    nb = tr // CHUNK
    return [pl.BlockSpec((1, CHUNK, d), functools.partial(
        lambda bi, i, k: (bi, jnp.maximum(i * nb - 1 + k, 0), 0), k=k)) for k in range(nb)]


def _in_proj_kernel(*refs, tr, n_tiles):
    nb = tr // CHUNK
    x_refs, refs = refs[:nb], refs[nb:]
    (head_ref, hp_ref, hn_ref, ng_ref, wr_ref, wm_ref, wg_ref, wgt_ref,
     bg_ref, bgt_ref, cw_ref, cb_ref, cos_ref, sin_ref,
     rq_ref, rk_ref, rv_ref, rg_ref, mq_ref, mk_ref, mv_ref, mo_ref,
     kwf_ref, kwb_ref, gc_ref, gr_ref, ext_ref, u_ref, gts_ref, h_ref) = refs
    i = pl.program_id(1)
    ng = ng_ref[...]
    w = HEADS * DH
    nd = 2 * HEADS
    for k, blk in enumerate(_load_rows(x_refs, head_ref, i)):
        h_ref[k * CHUNK:(k + 1) * CHUNK, :] = blk

    hp = hp_ref[0]
    if tr == CHUNK:
        hp = jnp.where(i == 1, head_ref[CHUNK - HALO:CHUNK, :], hp)
    up = _rms(hp, ng)
    un = _rms(hn_ref[0], ng)
    up = jnp.where(i == 0, 0.0, up)
    un = jnp.where(i == n_tiles - 1, 0.0, un)
    uh = jnp.concatenate([up, un], axis=0).astype(BF16)
    halo = _dot(uh, wm_ref[:, 0:2 * w])
    ext_ref[0:HALO, :] = halo[0:HALO]
    ext_ref[HALO + tr:2 * HALO + tr, :] = halo[HALO:2 * HALO]

    sub = tr // 2
    lane_s = lax.broadcasted_iota(jnp.int32, (sub, LANES), 1)
    for r0 in (0, sub):
        rs = slice(r0, r0 + sub)
        u = _rms(h_ref[rs, :], ng).astype(BF16)
        u_ref[rs, :] = u
        real = lax.broadcasted_iota(jnp.int32, (sub, 1), 0) + (i * tr + r0) >= PAD
        cosf = cos_ref[rs, :]
        sinf = sin_ref[rs, :]
        pq = _dot(u, wr_ref[:, 0:w])
        pk = _dot(u, wr_ref[:, w:2 * w])
        for hd in range(HEADS):
            sl = slice(hd * DH, (hd + 1) * DH)
            xq = pq[:, sl]
            xk = pk[:, sl]
            rq_ref[0, rs, sl] = ((xq * cosf + pltpu.roll(xq, DH // 2, 1) * sinf) * (DH ** -0.5)).astype(BF16)
            rk_ref[0, rs, sl] = (xk * cosf + pltpu.roll(xk, DH // 2, 1) * sinf).astype(BF16)
        rv_ref[0, rs, :] = _dot(u, wr_ref[:, 2 * w:3 * w]).astype(BF16)
        rg_ref[0, rs, :] = _silu(_dot(u, wr_ref[:, 3 * w:4 * w])).astype(BF16)
        ext_ref[HALO + r0:HALO + r0 + sub, :] = _dot(u, wm_ref[:, 0:2 * w])
        mv_ref[0, rs, :] = _dot(u, wm_ref[:, 2 * w:3 * w]).astype(BF16)
        mo_ref[0, rs, :] = _sigmoid(_dot(u, wm_ref[:, 3 * w:4 * w])).astype(BF16)
        gts = _dot(u, wg_ref[...]) + bg_ref[...]
        gts_ref[rs, :] = jnp.where(lane_s < nd, jnp.where(real, gts, NEG),
                                   jnp.where(real & (lane_s < 2 * nd), _log_sigmoid(gts), 0.0))

    rown = lax.broadcasted_iota(jnp.int32, (2 * nd, tr), 0)
    coln = lax.broadcasted_iota(jnp.int32, (1, tr), 1) + i * tr
    realr = coln >= PAD
    gtr = _dot_nt(wgt_ref[...], u_ref[...]) + bgt_ref[...]
    gtr = jnp.where(rown < nd, jnp.where(realr, gtr, NEG),
                    jnp.where(realr, _log_sigmoid(gtr), 0.0))
    ci = lax.broadcasted_iota(jnp.int32, (CHUNK, CHUNK), 0)
    cj = lax.broadcasted_iota(jnp.int32, (CHUNK, CHUNK), 1)
    causal = ci >= cj
    anti = ci <= cj
    low = causal.astype(BF16)
    upp = anti.astype(BF16)
    lane_c = lax.broadcasted_iota(jnp.int32, (CHUNK, LANES), 1)
    lane_1 = lax.broadcasted_iota(jnp.int32, (1, LANES), 1)
    rown_c = lax.broadcasted_iota(jnp.int32, (2 * nd, CHUNK), 0)
    for c in range(tr // CHUNK):
        rs = slice(c * CHUNK, (c + 1) * CHUNK)
        realc = lax.broadcasted_iota(jnp.int32, (CHUNK, 1), 0) + (i * tr + c * CHUNK) >= PAD
        conv = cb_ref[...] + jnp.zeros((CHUNK, 2 * w), F32)
        for k in range(CONV_K):
            off = HALO + c * CHUNK + k - CONV_K // 2
            conv = conv + cw_ref[k:k + 1, :] * ext_ref[off:off + CHUNK, :]
        qk = jnp.where(realc, _silu(conv), 0.0)
        mq_ref[0, rs, :] = (qk[:, 0:w] * (DH ** -0.5)).astype(BF16)
        mk_ref[0, rs, :] = qk[:, w:2 * w].astype(BF16)
        kconv = qk[:, w:2 * w]
        g = gts_ref[rs, :]
        lf = jnp.where(lane_c >= nd, g, 0.0)
        pre = _dot3_left(low, lf)
        suf = _dot3_left(upp, lf)
        bc = pltpu.roll(jnp.where(lane_c < nd + HEADS, pre, suf), LANES - nd, 1)
        blast = jnp.where(lane_1 < HEADS, bc[CHUNK - 1:CHUNK], bc[0:1])
        log_u = blast - bc + g
        a = jnp.max(log_u, axis=0, keepdims=True)
        wu = jnp.exp(log_u - a)
        gl = gtr[:, rs]
        lfr = jnp.where(rown_c >= nd, gl, 0.0)
        prer = _dot3_right(lfr, upp)
        sufr = _dot3_right(lfr, low)
        rr = gl[0:nd] - jnp.concatenate([prer[nd:nd + HEADS], sufr[nd + HEADS:2 * nd]], axis=0)
        for hd in range(HEADS):
            hs = slice(hd * DH, (hd + 1) * DH)
            cols = []
            for dr, (msk, kw_ref) in enumerate(((causal, kwf_ref), (anti, kwb_ref))):
                i8 = dr * HEADS + hd
                rmax = jnp.max(jnp.where(msk, rr[i8:i8 + 1, :], NEG), axis=1, keepdims=True)
                cols += [rmax, bc[:, i8:i8 + 1]]
                kw_ref[0, rs, hs] = (kconv[:, hs] * wu[:, i8:i8 + 1]).astype(BF16)
            gc_ref[0, hd, rs, :] = jnp.where(
                lane_c == 0, cols[0], jnp.where(lane_c == 1, cols[1], jnp.where(lane_c == 2, cols[2], cols[3])))
            scal = [jnp.broadcast_to(x[:, j:j + 1], (1, CHUNK))
                    for j in (hd, HEADS + hd) for x in (a, blast)]
            gr_ref[0, hd, :, rs] = jnp.concatenate(
                [rr[hd:hd + 1], rr[HEADS + hd:HEADS + hd + 1]] + scal + [jnp.zeros((2, CHUNK), F32)], axis=0)


def _in_proj(x, head, norm_g, w_in, b_gates, conv_w, conv_b, cosf, sinf):
    b, s, d = x.shape
    t = CHUNK + s
    tr = _row_tile(t)
    n_tiles = t // tr
    w = HEADS * DH
    wr = w_in[:, 0:4 * w].astype(BF16)
    wm = w_in[:, 4 * w:8 * w].astype(BF16)
    wg = jnp.pad(w_in[:, 8 * w:], ((0, 0), (0, LANES - 4 * HEADS))).astype(BF16)
    wgt = w_in[:, 8 * w:].T.astype(BF16)
    bg = jnp.pad(b_gates, (0, LANES - 4 * HEADS)).reshape(1, LANES)
    bgt = b_gates.reshape(4 * HEADS, 1)
    hb = tr // HALO
    pad_h = CHUNK // HALO
    last_h = s // HALO - 1
    seq = lambda bi, i: (bi, i, 0)
    const2 = lambda bi, i: (0, 0)
    out_bf = jax.ShapeDtypeStruct((b, t, w), BF16)
    kern = functools.partial(_in_proj_kernel, tr=tr, n_tiles=n_tiles)
    return pl.pallas_call(
        kern,
        grid=(b, n_tiles),
        in_specs=_x_specs(tr, d) + [
            pl.BlockSpec((CHUNK, d), const2),
            pl.BlockSpec((1, HALO, d), lambda bi, i: (bi, jnp.maximum(i * hb - pad_h - 1, 0), 0)),
            pl.BlockSpec((1, HALO, d), lambda bi, i: (bi, jnp.minimum((i + 1) * hb - pad_h, last_h), 0)),
            pl.BlockSpec((1, d), const2),
            pl.BlockSpec((d, 4 * w), const2),
            pl.BlockSpec((d, 4 * w), const2),
            pl.BlockSpec((d, LANES), const2),
            pl.BlockSpec((4 * HEADS, d), const2),
            pl.BlockSpec((1, LANES), const2),
            pl.BlockSpec((4 * HEADS, 1), const2),
            pl.BlockSpec((CONV_K, 2 * w), const2),
            pl.BlockSpec((1, 2 * w), const2),
            pl.BlockSpec((tr, DH), lambda bi, i: (i, 0)),
            pl.BlockSpec((tr, DH), lambda bi, i: (i, 0)),
        ],
        out_specs=[pl.BlockSpec((1, tr, w), seq)] * 10 + [
            pl.BlockSpec((1, HEADS, tr, LANES), lambda bi, i: (bi, 0, i, 0)),
            pl.BlockSpec((1, HEADS, 8, tr), lambda bi, i: (bi, 0, 0, i)),
        ],
        out_shape=[out_bf] * 10 + [
            jax.ShapeDtypeStruct((b, HEADS, t, LANES), F32),
            jax.ShapeDtypeStruct((b, HEADS, 8, t), F32),
        ],
        scratch_shapes=[pltpu.VMEM((tr + 2 * HALO, 2 * w), F32), pltpu.VMEM((tr, d), BF16),
                        pltpu.VMEM((tr, LANES), F32), pltpu.VMEM((tr, d), F32)],
        compiler_params=_cparams(("parallel", "arbitrary")),
        name="in_proj",
    )(*([x] * (tr // CHUNK)), head, x, x, norm_g.reshape(1, d), wr, wm, wg, wgt, bg, bgt,
      conv_w, conv_b.reshape(1, 2 * w), cosf, sinf)


def _retention_kernel(lg_ref, q_ref, k_ref, v_ref, o_ref, of_ref, ob_ref, *, n_chunks):
    hd = pl.program_id(1)
    lgf = lg_ref[0, hd]
    lgb = lg_ref[1, hd]
    li = lax.broadcasted_iota(jnp.int32, (CHUNK, CHUNK), 0)
    mi = lax.broadcasted_iota(jnp.int32, (CHUNK, CHUNK), 1)
    diff = (li - mi).astype(F32)
    decay = (jnp.where(diff >= 0, jnp.exp(lgf * jnp.maximum(diff, 0.0)), 0.0)
             + jnp.where(diff <= 0, jnp.exp(lgb * jnp.maximum(-diff, 0.0)), 0.0))
    lcol = lax.broadcasted_iota(jnp.int32, (CHUNK, 1), 0).astype(F32)
    one = jnp.ones((1, 1), F32)
    zeta_f = jnp.exp(lgf * (CHUNK - 1.0 - lcol))
    xi_f = jnp.exp(lgf * (lcol + 1.0))
    g_f = jnp.exp(lgf * CHUNK * one)
    zeta_b = jnp.exp(lgb * lcol)
    xi_b = jnp.exp(lgb * (CHUNK - lcol))
    g_b = jnp.exp(lgb * CHUNK * one)

    def fwd(c, state):
        rows = pl.ds(pl.multiple_of(c * CHUNK, CHUNK), CHUNK)
        q = q_ref[0, rows, :]
        k = k_ref[0, rows, :]
        v = v_ref[0, rows, :]
        s = _dot_nt(q, k) * decay
        intra = _dot(s.astype(BF16), v)
        cross = _dot(q, state.astype(BF16)) * xi_f
        of_ref[rows, :] = intra + cross
        kz = (k.astype(F32) * zeta_f).astype(BF16)
        return g_f * state + _dot_tn(kz, v)

    def bwd(j, state):
        c = n_chunks - 1 - j
        rows = pl.ds(pl.multiple_of(c * CHUNK, CHUNK), CHUNK)
        q = q_ref[0, rows, :]
        k = k_ref[0, rows, :]
        v = v_ref[0, rows, :]
        ob_ref[rows, :] = _dot(q, state.astype(BF16)) * xi_b
        kz = (k.astype(F32) * zeta_b).astype(BF16)
        return g_b * state + _dot_tn(kz, v)

    zero = jnp.zeros((DH, DH), F32)
    lax.fori_loop(0, n_chunks, lambda j, st: (fwd(j, st[0]), bwd(j, st[1])), (zero, zero),
                  unroll=_chunk_unroll(n_chunks))
    o_ref[0] = (of_ref[...] + ob_ref[...]).astype(BF16)


def _retention(lg, rq, rk, rv):
    b, t, _ = rq.shape
    spec = pl.BlockSpec((1, t, DH), lambda bi, hi, lg_ref: (bi, 0, hi))
    return pl.pallas_call(
        functools.partial(_retention_kernel, n_chunks=t // CHUNK),
        grid_spec=pltpu.PrefetchScalarGridSpec(
            num_scalar_prefetch=1,
            grid=(b, HEADS),
            in_specs=[spec, spec, spec],
            out_specs=spec,
            scratch_shapes=[pltpu.VMEM((t, DH), F32)] * 2,
        ),
        out_shape=jax.ShapeDtypeStruct((b, t, HEADS * DH), BF16),
        compiler_params=_cparams(("parallel", "arbitrary")),
        name="retention",
    )(lg, rq, rk, rv)


def _mlstm_kernel(q_ref, k_ref, v_ref, kwf_ref, kwb_ref, gc_ref, gr_ref, o_ref, of_ref, ob_ref, *, n_chunks):
    li = lax.broadcasted_iota(jnp.int32, (CHUNK, CHUNK), 0)
    mi = lax.broadcasted_iota(jnp.int32, (CHUNK, CHUNK), 1)
    causal = li >= mi
    anti = li <= mi

    ones = jnp.ones((CHUNK, DH), BF16)

    def step(c, carry, backward):
        st, m_st = carry
        rows = pl.ds(pl.multiple_of(c * CHUNK, CHUNK), CHUNK)
        q = q_ref[0, rows, :]
        k = k_ref[0, rows, :]
        v_aug = jnp.concatenate([v_ref[0, rows, :], ones], axis=1)
        gcol = gc_ref[0, 0, rows, :]
        grow = gr_ref[0, 0, :, rows]
        if backward:
            kw = kwb_ref[0, rows, :]
            r_max, bc = gcol[:, 2:3], gcol[:, 3:4]
            r_row, a, b_last = grow[1:2, :], grow[4:5, 0:1], grow[5:6, 0:1]
            mask = anti
        else:
            kw = kwf_ref[0, rows, :]
            r_max, bc = gcol[:, 0:1], gcol[:, 1:2]
            r_row, a, b_last = grow[0:1, :], grow[2:3, 0:1], grow[3:4, 0:1]
            mask = causal
        mm = jnp.maximum(jnp.broadcast_to(r_max, (CHUNK, DH)), m_st)
        s = _dot_nt(q, k) * jnp.where(mask, jnp.exp(r_row - mm), 0.0)
        w_inter = jnp.exp(m_st - mm)
        w2 = jnp.concatenate([w_inter, w_inter], axis=1)
        tot = _dot(s.astype(BF16), v_aug) + _dot(q, st.astype(BF16)) * w2
        floor = jnp.exp(-(jnp.broadcast_to(bc, (CHUNK, DH)) + mm))
        hval = tot[:, 0:DH] / jnp.maximum(jnp.abs(tot[:, DH:2 * DH]), floor)
        if backward:
            ob_ref[rows, :] = hval
        else:
            of_ref[rows, :] = hval
        upd = _dot_tn(kw, v_aug)
        m_new = jnp.maximum(b_last + m_st, a)
        f = jnp.exp(b_last + m_st - m_new)
        g = jnp.exp(a - m_new)
        return f * st + g * upd, m_new

    def both(j, carry):
        return step(j, carry[0], False), step(n_chunks - 1 - j, carry[1], True)

    init = (jnp.zeros((DH, 2 * DH), F32), jnp.zeros((1, 1), F32))
    lax.fori_loop(0, n_chunks, both, (init, init), unroll=_chunk_unroll(n_chunks))
    o_ref[0] = (of_ref[...] + ob_ref[...]).astype(BF16)


def _mlstm(mq, mk, mv, kwf, kwb, gcol, grow):
    b, t, _ = mq.shape
    spec = pl.BlockSpec((1, t, DH), lambda bi, hi: (bi, 0, hi))
    return pl.pallas_call(
        functools.partial(_mlstm_kernel, n_chunks=t // CHUNK),
        grid=(b, HEADS),
        in_specs=[spec, spec, spec, spec, spec,
                  pl.BlockSpec((1, 1, t, LANES), lambda bi, hi: (bi, hi, 0, 0)),
                  pl.BlockSpec((1, 1, 8, t), lambda bi, hi: (bi, hi, 0, 0))],
        out_specs=spec,
        out_shape=jax.ShapeDtypeStruct((b, t, HEADS * DH), BF16),
        scratch_shapes=[pltpu.VMEM((t, DH), F32)] * 2,
        compiler_params=_cparams(("parallel", "arbitrary")),
        name="mlstm",
    )(mq, mk, mv, kwf, kwb, gcol, grow)


def _head_norm(y, g):
    outs = []
    for hd in range(HEADS):
        yh = y[:, hd * DH:(hd + 1) * DH]
        mu = jnp.mean(yh, axis=1, keepdims=True)
        yc = yh - mu
        var = jnp.mean(yc * yc, axis=1, keepdims=True)
        outs.append(yc * lax.rsqrt(var + EPS))
    return jnp.concatenate(outs, axis=1) * g


def _out_proj_kernel(*refs, tr):
    nb = tr // CHUNK
    x_refs, refs = refs[:nb], refs[nb:]
    (head_ref, ret_ref, hm_ref, rg_ref, mo_ref, rgn_ref, mgn_ref, wo_ref,
     fg_ref, wrt_ref, h1_ref, u2_ref, aff_ref) = refs
    i = pl.program_id(1)
    row = lax.broadcasted_iota(jnp.int32, (tr, 1), 0) + i * tr
    real = row >= PAD
    w = HEADS * DH
    y_ret = _head_norm(ret_ref[0].astype(F32), rgn_ref[...]) * rg_ref[0].astype(F32)
    y_m = _head_norm(mo_ref[0].astype(F32) * hm_ref[0].astype(F32), mgn_ref[...])
    y_ret = jnp.where(real, y_ret, 0.0).astype(BF16)
    y_m = jnp.where(real, y_m, 0.0).astype(BF16)
    h = jnp.concatenate(_load_rows(x_refs, head_ref, i), axis=0)
    h1 = h + _dot(y_ret, wo_ref[0:w, :]) + _dot(y_m, wo_ref[w:2 * w, :])
    h1_ref[0] = h1
    u2 = _rms(h1, fg_ref[...])
    _to_segments(u2_ref, _pack_pairs(u2))
    logits = _dot_nt(wrt_ref[...], u2.astype(BF16))
    mx = jnp.max(logits, axis=0, keepdims=True)
    ex = jnp.exp(logits - mx)
    aff = ex / jnp.sum(ex, axis=0, keepdims=True)
    coln = lax.broadcasted_iota(jnp.int32, (1, tr), 1) + i * tr
    aff_ref[0] = jnp.where(coln >= PAD, aff, -1.0)


def _out_proj(x, head, ret, hm, rgs, mos, ret_gn_g, mlstm_gn_g, w_out, ffn_norm_g, w_router):
    b, s, d = x.shape
    t = CHUNK + s
    tr = _row_tile(t)
    n_tiles = t // tr
    nseg = d // 2 // LANES
    w = HEADS * DH
    seq = lambda bi, i: (bi, i, 0)
    const2 = lambda bi, i: (0, 0)
    return pl.pallas_call(
        functools.partial(_out_proj_kernel, tr=tr),
        grid=(b, n_tiles),
        in_specs=_x_specs(tr, d) + [
            pl.BlockSpec((CHUNK, d), const2),
            pl.BlockSpec((1, tr, w), seq),
            pl.BlockSpec((1, tr, w), seq),
            pl.BlockSpec((1, tr, w), seq),
            pl.BlockSpec((1, tr, w), seq),
            pl.BlockSpec((1, w), const2),
            pl.BlockSpec((1, w), const2),
            pl.BlockSpec((2 * w, d), const2),
            pl.BlockSpec((1, d), const2),
            pl.BlockSpec((N_EXPERTS, d), const2),
        ],
        out_specs=[
            pl.BlockSpec((1, tr, d), seq),
            pl.BlockSpec((tr // SUB, nseg, SUB, LANES), lambda bi, i: (bi * n_tiles + i, 0, 0, 0)),
            pl.BlockSpec((1, N_EXPERTS, tr), lambda bi, i: (bi, 0, i)),
        ],
        out_shape=[
            jax.ShapeDtypeStruct((b, t, d), F32),
            jax.ShapeDtypeStruct((b * t // SUB, nseg, SUB, LANES), jnp.int32),
            jax.ShapeDtypeStruct((b, N_EXPERTS, t), F32),
        ],
        compiler_params=_cparams(("parallel", "arbitrary")),
        name="out_proj",
    )(*([x] * (tr // CHUNK)), head, ret, hm, rgs, mos, ret_gn_g.reshape(1, w), mlstm_gn_g.reshape(1, w),
      w_out.astype(BF16), ffn_norm_g.reshape(1, d), w_router.T.astype(BF16))


def _select_kernel(aff_ref, pm_ref, dv_ref, cnt_ref, bt_ref, mc_ref, sel_ref, *, t, cap, tr):
    e_n = N_EXPERTS
    nblk = t // LANES
    bits = pltpu.bitcast(aff_ref[0], jnp.int32)
    capf = float(cap)

    def search(_, lohi):
        lo, hi = lohi
        mid = lo + ((hi - lo + 1) >> 1)
        cnt = jnp.sum((bits >= mid).astype(F32), axis=1, keepdims=True)
        ok = cnt >= capf
        return jnp.where(ok, mid, lo), jnp.where(ok, hi, mid - 1)

    lo0 = jnp.zeros((e_n, 1), jnp.int32)
    hi0 = jnp.full((e_n, 1), 0x3F800000, jnp.int32)
    thr, _ = lax.fori_loop(0, 31, search, (lo0, hi0))
    need = capf - jnp.sum((bits > thr).astype(F32), axis=1, keepdims=True)

    ci = lax.broadcasted_iota(jnp.int32, (LANES, LANES), 0)
    cj = lax.broadcasted_iota(jnp.int32, (LANES, LANES), 1)
    upp = (ci <= cj).astype(BF16)
    ei = lax.broadcasted_iota(jnp.int32, (e_n, e_n), 0)
    ej = lax.broadcasted_iota(jnp.int32, (e_n, e_n), 1)
    strict = (ej < ei).astype(BF16)

    carry = jnp.zeros((e_n, 1), F32)
    for j in range(nblk):
        sl = slice(j * LANES, (j + 1) * LANES)
        bb = bits[:, sl]
        eqf = (bb == thr).astype(F32)
        ceq = _dot(eqf.astype(BF16), upp) + carry
        carry = ceq[:, LANES - 1:LANES]
        sel_ref[:, sl] = jnp.where(bb > thr, 1.0, jnp.where(ceq <= need, eqf, 0.0))

    carry = jnp.zeros((e_n, 1), F32)
    tok = lax.broadcasted_iota(jnp.int32, (1, LANES), 1).astype(F32)
    lane_b = lax.broadcasted_iota(jnp.int32, (e_n, LANES), 1)
    btot = jnp.zeros((e_n, LANES), F32)
    tile_max = jnp.zeros((SUB, LANES), F32)
    lane_t = lax.broadcasted_iota(jnp.int32, (SUB, LANES), 1)
    for j in range(nblk):
        sl = slice(j * LANES, (j + 1) * LANES)
        selb = sel_ref[:, sl]
        selbf = selb.astype(BF16)
        pin = _dot(selbf, upp) + carry
        carry = pin[:, LANES - 1:LANES]
        btot = jnp.where(lane_b == j, carry, btot)
        pm_ref[0, :, sl] = jnp.where(selb > 0, pin, 0.0)
        rank = _dot(strict, selbf)
        dv_ref[0, :, sl] = (tok + float(j * LANES)) * float(N_EXPERTS) + rank
        cnt_row = jnp.sum(selb, axis=0, keepdims=True)
        cnt_ref[0, sl, :] = jnp.broadcast_to(cnt_row, (LANES, LANES)).T
        tile_max = jnp.where(lane_t == (j * LANES) // tr,
                             jnp.maximum(tile_max, jnp.max(cnt_row, axis=1, keepdims=True)), tile_max)
    bt_ref[0] = jnp.where(lane_b < nblk, btot, carry).astype(jnp.int32)
    mc_ref[0] = tile_max.astype(jnp.int32)


def _select(aff, cap):
    b, e_n, t = aff.shape
    tr = _row_tile(t)
    assert t // tr <= LANES
    spec = pl.BlockSpec((1, e_n, t), lambda bi: (bi, 0, 0))
    return pl.pallas_call(
        functools.partial(_select_kernel, t=t, cap=cap, tr=tr),
        grid=(b,),
        in_specs=[spec],
        out_specs=[
            spec,
            spec,
            pl.BlockSpec((1, t, LANES), lambda bi: (bi, 0, 0)),
            pl.BlockSpec((1, e_n, LANES), lambda bi: (bi, 0, 0)),
            pl.BlockSpec((1, SUB, LANES), lambda bi: (bi, 0, 0)),
        ],
        out_shape=[
            jax.ShapeDtypeStruct((b, e_n, t), F32),
            jax.ShapeDtypeStruct((b, e_n, t), F32),
            jax.ShapeDtypeStruct((b, t, LANES), F32),
            jax.ShapeDtypeStruct((b, e_n, LANES), jnp.int32),
            jax.ShapeDtypeStruct((b, SUB, LANES), jnp.int32),
        ],
        scratch_shapes=[pltpu.VMEM((e_n, t), F32)],
        compiler_params=_cparams(("parallel",)),
        name="select",
    )(aff)


def _compact_kernel(bt_ref, pm_ref, dv_ref, aff_ref, src_ref, dst_ref, gate_ref, accd_ref, accg_ref,
                    *, t, cap, c_pad, nct, nseg, batch):
    bi = batch
    e_n = N_EXPERTS
    nblk = t // LANES
    lane_e = lax.broadcasted_iota(jnp.int32, (LANES, LANES), 1)
    slot_l = lax.broadcasted_iota(jnp.int32, (1, LANES), 1)

    spr = LANES // nseg
    sh_spr = spr.bit_length() - 1
    sh_grp = (SUB * nseg).bit_length() - 1
    sub_i = lax.broadcasted_iota(jnp.int32, (SUB, LANES), 0)
    lane_i = lax.broadcasted_iota(jnp.int32, (SUB, LANES), 1)
    pick = (lane_i >> sh_spr) == sub_i
    pl_i = lax.broadcasted_iota(jnp.int32, (LANES, LANES), 0)
    pl_j = lax.broadcasted_iota(jnp.int32, (LANES, LANES), 1)
    perm = ((pl_i & (spr - 1)) == ((pl_j >> sh_grp) << 3) + (pl_j & (SUB - 1))).astype(BF16)
    seg_j = (lane_i & (SUB * nseg - 1)) >> 3

    def segment_tile(rows):
        spread = _dot3_right(jnp.where(pick, rows.astype(F32), 0.0), perm).astype(jnp.int32)
        return (((spread >> 3) * nseg + seg_j) << 3) + (spread & (SUB - 1))

    def expert(e):
        base = e * LANES

        def ctile(ct, j0):
            c0 = pl.multiple_of(ct * LANES, LANES)
            j_lo = lax.while_loop(
                lambda j: (j < nblk) & (bt_ref[base + jnp.minimum(j, nblk - 1)] <= c0),
                lambda j: j + 1, j0)
            j_hi = lax.while_loop(
                lambda j: (j < nblk) & (bt_ref[base + jnp.minimum(j, nblk - 1)] < c0 + LANES),
                lambda j: j + 1, j_lo)
            j_end = jnp.minimum(j_hi + 1, nblk)
            want = (lax.broadcasted_iota(jnp.int32, (LANES, 1), 0) + (c0 + 1)).astype(F32)

            def blk(j, acc):
                acc_d, acc_g = acc
                cols = pl.ds(pl.multiple_of(j * LANES, LANES), LANES)
                hit = pm_ref[0, e:e + 1, cols] == want
                acc_d = acc_d + jnp.where(hit, dv_ref[0, e:e + 1, cols], 0.0)
                acc_g = acc_g + jnp.where(hit, aff_ref[0, e:e + 1, cols], 0.0)
                return acc_d, acc_g

            zero = jnp.zeros((LANES, LANES), F32)
            acc_d, acc_g = lax.fori_loop(j_lo, j_end, blk, (zero, zero))
            accd_ref[e * nct + ct] = acc_d
            accg_ref[e * nct + ct] = acc_g
            return j_lo

        lax.fori_loop(0, nct, ctile, 0)

    for e in range(e_n):
        expert(e)

    def finish(ct, _):
        c0 = pl.multiple_of(ct * LANES, LANES)
        p = slot_l + (c0 - cap)
        is_pad = p >= 0
        gate_tile = jnp.zeros((LANES, LANES), F32)
        for e in range(e_n):
            v = jnp.sum(accd_ref[e * nct + ct].T, axis=0, keepdims=True).astype(jnp.int32)
            tok = v >> 4
            rank = v & (N_EXPERTS - 1)
            src = bi * t + jnp.where(is_pad, 0, tok)
            dst = jnp.where(is_pad, N_EXPERTS * t + e * (c_pad - cap) + p, rank * t + tok)
            out_rows = pl.ds(pl.multiple_of((e * nct + ct) * SUB, SUB), SUB)
            src_ref[0, out_rows, :] = segment_tile(src)
            dst_ref[0, out_rows, :] = segment_tile(dst)
            gcol = jnp.sum(accg_ref[e * nct + ct], axis=1, keepdims=True)
            gate_tile = jnp.where(lane_e == e, gcol, gate_tile)
        gate_ref[0, pl.ds(c0, LANES), :] = gate_tile
        return 0

    lax.fori_loop(0, nct, finish, 0)


def _compact(btot, pm, dv, aff, cap, c_pad, nseg, batch):
    _, e_n, t = aff.shape
    assert nseg in (1, 2, 4, 8)
    nct = -(-c_pad // LANES)
    tile_rows = e_n * nct * SUB
    spec = pl.BlockSpec((1, e_n, t), lambda i, bt: (0, 0, 0))
    ospec = pl.BlockSpec((1, tile_rows, LANES), lambda i, bt: (0, 0, 0))
    src, dst, gate = pl.pallas_call(
        functools.partial(_compact_kernel, t=t, cap=cap, c_pad=c_pad, nct=nct, nseg=nseg, batch=batch),
        grid_spec=pltpu.PrefetchScalarGridSpec(
            num_scalar_prefetch=1,
            grid=(1,),
            in_specs=[spec, spec, spec],
            out_specs=[ospec, ospec, pl.BlockSpec((1, nct * LANES, LANES), lambda i, bt: (0, 0, 0))],
            scratch_shapes=[pltpu.VMEM((e_n * nct, LANES, LANES), F32)] * 2,
        ),
        out_shape=[
            jax.ShapeDtypeStruct((1, tile_rows, LANES), jnp.int32),
            jax.ShapeDtypeStruct((1, tile_rows, LANES), jnp.int32),
            jax.ShapeDtypeStruct((1, nct * LANES, LANES), F32),
        ],
        compiler_params=_cparams(("arbitrary",)),
        name="compact",
    )(btot.reshape(e_n * LANES), pm, dv, aff)
    ids = lambda a: a.reshape(e_n, nct, SUB, LANES)[:, :, :nseg].reshape(e_n, -1)[:, :c_pad * nseg].reshape(-1)
    return ids(src), ids(dst), gate


def _sc_mesh():
    return plsc.VectorSubcoreMesh(core_axis_name="core", subcore_axis_name="subcore")


def _row_gather(table, idx):
    m = idx.shape[0]
    d = table.shape[1]

    @functools.partial(pl.kernel, out_type=jax.ShapeDtypeStruct((m, d), table.dtype),
                       mesh=_sc_mesh(), scratch_types=[])
    def gather_kernel(x_hbm, i_hbm, o_hbm):
        def body(i_vmem, o_vmem):
            pltpu.sync_copy(x_hbm.at[i_vmem.at[0]], o_vmem)

        pltpu.emit_pipeline(
            body,
            grid=(m // SC_WINDOW,),
            in_specs=[pl.BlockSpec((1, SC_WINDOW), lambda i: (0, i))],
            out_specs=[pl.BlockSpec((SC_WINDOW, d), lambda i: (i, 0))],
            core_axis_name=("core", "subcore"),
            dimension_semantics=(pltpu.PARALLEL,),
        )(i_hbm, o_hbm)

    return gather_kernel(table, idx.reshape(1, m))


def _row_scatter(rows, idx, n_out):
    m, d = rows.shape

    @functools.partial(pl.kernel, out_type=jax.ShapeDtypeStruct((n_out, d), rows.dtype),
                       mesh=_sc_mesh(), scratch_types=[])
    def scatter_kernel(x_hbm, i_hbm, o_hbm):
        def body(x_vmem, i_vmem):
            pltpu.sync_copy(x_vmem, o_hbm.at[i_vmem.at[0]])

        pltpu.emit_pipeline(
            body,
            grid=(m // SC_WINDOW,),
            in_specs=[pl.BlockSpec((SC_WINDOW, d), lambda i: (i, 0)),
                      pl.BlockSpec((1, SC_WINDOW), lambda i: (0, i))],
            out_specs=[],
            core_axis_name=("core", "subcore"),
            dimension_semantics=(pltpu.PARALLEL,),
        )(x_hbm, i_hbm)

    return scatter_kernel(rows, idx.reshape(1, m))


def _expert_kernel(gate_ref, x_ref, wg_ref, wu_ref, wd_ref, y_ref, xb_ref, acc_ref, *, c_pad, n_f):
    e = pl.program_id(0)
    f = pl.program_id(1)

    @pl.when(f == 0)
    def _():
        w = _from_segments(x_ref)
        xb_ref[...] = jnp.concatenate([_unpack_hi(w), _unpack_lo(w)], axis=1).astype(BF16)
        acc_ref[...] = jnp.zeros(acc_ref.shape, F32)

    wg = wg_ref[0].astype(BF16)
    wu = wu_ref[0].astype(BF16)
    wd = wd_ref[0].astype(BF16)
    half = c_pad // 2
    for r0 in (0, half):
        rows = slice(r0, r0 + half)
        x = xb_ref[rows, :]
        hid = _silu(_dot(x, wg)) * _dot(x, wu)
        part = _dot(hid.astype(BF16), wd)
        acc_ref[rows, :] = acc_ref[rows, :] + part

    @pl.when(f == n_f - 1)
    def _():
        lane = lax.broadcasted_iota(jnp.int32, (c_pad, LANES), 1)
        gcol = jnp.sum(jnp.where(lane == e, gate_ref[0], 0.0), axis=1, keepdims=True)
        _to_segments(y_ref, _pack_pairs(acc_ref[...] * gcol))


def _experts(gate, xs, w_gate, w_up, w_down):
    e_n, d, d_ff = w_gate.shape
    nseg = d // 2 // LANES
    c_pad = xs.shape[0] * SUB // e_n
    fc = 256
    n_f = d_ff // fc
    blk = pl.BlockSpec((c_pad // SUB, nseg, SUB, LANES), lambda e, f: (e, 0, 0, 0))
    return pl.pallas_call(
        functools.partial(_expert_kernel, c_pad=c_pad, n_f=n_f),
        grid=(e_n, n_f),
        in_specs=[
            pl.BlockSpec((1, c_pad, LANES), lambda e, f: (0, 0, 0)),
            blk,
            pl.BlockSpec((1, d, fc), lambda e, f: (e, 0, f)),
            pl.BlockSpec((1, d, fc), lambda e, f: (e, 0, f)),
            pl.BlockSpec((1, fc, d), lambda e, f: (e, f, 0)),
        ],
        out_specs=blk,
        out_shape=jax.ShapeDtypeStruct(xs.shape, jnp.int32),
        scratch_shapes=[pltpu.VMEM((c_pad, d), BF16), pltpu.VMEM((c_pad, d), F32)],
        compiler_params=_cparams(("parallel", "arbitrary")),
        name="experts",
    )(gate, xs, w_gate, w_up, w_down)


def _combine_kernel(mc_ref, h1_ref, fg_ref, *rest, nb, n_tiles):
    cnt_refs = rest[:nb]
    z_hbm = rest[nb:2 * nb]
    o_hbm, acc_ref, obuf_ref, zbuf_ref, sem, zsem = rest[-6:]
    bi = pl.program_id(0)
    i = pl.program_id(1)
    blocks = obuf_ref.shape[1] // CHUNK
    g, nz = zbuf_ref.shape[1], zbuf_ref.shape[2]

    def tile_copies(b_, i_, slot, fn):
        for k in range(blocks):
            dst = pl.multiple_of((i_ * blocks - 1 + k) * CHUNK, CHUNK)
            copy = pltpu.make_async_copy(obuf_ref.at[slot, pl.ds(k * CHUNK, CHUNK), :],
                                         o_hbm.at[b_, pl.ds(dst, CHUNK), :], sem.at[slot])
            if k == 0:
                pl.when(i_ > 0)(functools.partial(fn, copy))
            else:
                fn(copy)

    def tile_before(steps):
        wrap = i < steps
        return bi - wrap.astype(jnp.int32), i - steps + jnp.where(wrap, n_tiles, 0)

    _to_segments(acc_ref, h1_ref[0])
    n_live = mc_ref[bi * LANES + i]

    def add_layers(k):
        def layer_copy(l, zslot):
            first = pl.multiple_of((l * n_tiles + i) * g, g)
            return pltpu.make_async_copy(z_hbm[k].at[pl.ds(first, g)], zbuf_ref.at[zslot], zsem.at[zslot])

        layer_copy(0, 0).start()
        cnt = cnt_refs[k][0].reshape(g, 1, SUB, LANES)

        def body(l, _):
            zslot = l % 2
            layer_copy(l, zslot).wait()

            @pl.when(l + 1 < n_live)
            def _():
                layer_copy(l + 1, 1 - zslot).start()

            take = cnt > l.astype(F32)
            w = zbuf_ref[zslot]
            acc_ref[:, 0:nz] = acc_ref[:, 0:nz] + jnp.where(take, _unpack_hi(w), 0.0)
            acc_ref[:, nz:2 * nz] = acc_ref[:, nz:2 * nz] + jnp.where(take, _unpack_lo(w), 0.0)
            return 0

        lax.fori_loop(0, n_live, body, 0)

    for k in range(nb):
        pl.when((bi == k) & (n_live > 0))(functools.partial(add_layers, k))

    idx = bi * n_tiles + i
    slot = idx % 2

    @pl.when(idx >= 2)
    def _():
        b2, i2 = tile_before(2)
        tile_copies(b2, i2, slot, lambda c: c.wait())

    obuf_ref[slot] = _rms(_from_segments(acc_ref), fg_ref[...])
    tile_copies(bi, i, slot, lambda c: c.start())

    @pl.when(idx == nb * n_tiles - 1)
    def _():
        if nb * n_tiles >= 2:
            b1, i1 = tile_before(1)
            tile_copies(b1, i1, 1 - slot, lambda c: c.wait())
        tile_copies(bi, i, slot, lambda c: c.wait())


def _combine(maxcnt, h1, zs, cnts, final_norm_g):
    b, t, d = h1.shape
    tr = _row_tile(t)
    n_tiles = t // tr

    def cnt_map(k):
        return lambda bi, i, mc: (0, jnp.where(bi == k, i, 0), 0)

    seg_block = (tr // SUB, d // LANES, SUB, LANES)
    z_block = (tr // SUB, d // 2 // LANES, SUB, LANES)
    return pl.pallas_call(
        functools.partial(_combine_kernel, nb=b, n_tiles=n_tiles),
        grid_spec=pltpu.PrefetchScalarGridSpec(
            num_scalar_prefetch=1,
            grid=(b, n_tiles),
            in_specs=[
                pl.BlockSpec((1, tr, d), lambda bi, i, mc: (bi, i, 0)),
                pl.BlockSpec((1, d), lambda bi, i, mc: (0, 0)),
            ] + [pl.BlockSpec((1, tr, LANES), cnt_map(k)) for k in range(b)]
            + [pl.BlockSpec(memory_space=pl.ANY)] * b,
            out_specs=pl.BlockSpec(memory_space=pl.ANY),
            scratch_shapes=[pltpu.VMEM(seg_block, F32), pltpu.VMEM((2, tr, d), F32),
                            pltpu.VMEM((2,) + z_block, jnp.int32),
                            pltpu.SemaphoreType.DMA((2,)), pltpu.SemaphoreType.DMA((2,))],
        ),
        out_shape=jax.ShapeDtypeStruct((b, t - CHUNK, d), F32),
        compiler_params=_cparams(("arbitrary", "arbitrary")),
        name="combine",
    )(maxcnt, h1, final_norm_g.reshape(1, d), *cnts, *zs)


def kernel(x, meta_tokens, mix_norm_g, w_in, b_gates, conv_w, conv_b, ret_decay_logit, ret_gn_g,
           mlstm_gn_g, w_out, ffn_norm_g, w_router, w_gate, w_up, w_down, final_norm_g):
    b, s, d = x.shape
    t = CHUNK + s
    n = N_META + s
    cap = 2 * n // N_EXPERTS
    nseg = d // 2 // LANES
    c_pad = next(c for c in range(-(-cap // 16) * 16, cap + 4096, 16)
                 if (N_EXPERTS * c * nseg) % (SC_WINDOW * SC_SUBCORES) == 0)

    head = jnp.concatenate([jnp.zeros((PAD, d), x.dtype), meta_tokens.astype(x.dtype)], axis=0)

    half = DH // 2
    pos = jnp.arange(t, dtype=F32) - PAD
    inv = ROPE_BASE ** (-jnp.arange(half, dtype=F32) / half)
    ang = pos[:, None] * inv[None, :]
    cosf = jnp.concatenate([jnp.cos(ang), jnp.cos(ang)], axis=1)
    sinf = jnp.concatenate([-jnp.sin(ang), jnp.sin(ang)], axis=1)

    assert mix_norm_g.shape[0] == 1, "single-layer block only"
    rq, rk, rv, rgs, mq, mk, mv, mos, kwf, kwb, gcol, grow = _in_proj(
        x, head, mix_norm_g[0], w_in[0], b_gates[0], conv_w[0], conv_b[0], cosf, sinf)
    lg = jax.nn.log_sigmoid(ret_decay_logit[0].astype(F32))
    ret = _retention(lg, rq, rk, rv)
    hm = _mlstm(mq, mk, mv, kwf, kwb, gcol, grow)
    h1, u2, aff = _out_proj(x, head, ret, hm, rgs, mos, ret_gn_g[0], mlstm_gn_g[0], w_out[0],
                            ffn_norm_g[0], w_router[0])
    tiled = lambda a: a.reshape(-1, nseg, SUB, LANES)
    flat = lambda a: a.reshape(-1, LANES)
    z_rows = -(-(N_EXPERTS * t + N_EXPERTS * (c_pad - cap)) // SUB) * SUB
    routed = []
    for bi in range(b):
        aff_b = aff[bi:bi + 1]
        pm, dv, cnt, btot, maxcnt = _select(aff_b, cap)
        src, dst, gate = _compact(btot, pm, dv, aff_b, cap, c_pad, nseg, bi)
        routed.append((_row_gather(flat(u2), src), dst, gate, cnt, maxcnt))
    zs = []
    for xs, dst, gate, _, _ in routed:
        ys = _experts(gate, tiled(xs), w_gate[0], w_up[0], w_down[0])
        zs.append(tiled(_row_scatter(flat(ys), dst, z_rows * nseg)))
    maxcnt = jnp.concatenate([r[4][:, 0, :] for r in routed], axis=0).reshape(-1)
    return _combine(maxcnt, h1, zs, [r[3] for r in routed], final_norm_g)
```

```python
import functools

import jax
import jax.numpy as jnp
from jax import lax
from jax.experimental import pallas as pl
from jax.experimental.pallas import tpu as pltpu
from jax.experimental.pallas import tpu_sc as plsc

F32 = jnp.float32
BF16 = jnp.bfloat16

LANES = 128
CHUNK = 128
N_META = 16
PAD = CHUNK - N_META
HEADS = 4
DH = 128
N_EXPERTS = 16
CONV_K = 5
HALO = 8
SUB = 8
SC_WINDOW = 128
SC_SUBCORES = 32
EPS = 1e-6
NEG = -1e30
ROPE_BASE = 10000.0
VMEM_LIMIT = 56 * 1024 * 1024


def _cparams(sem, vmem=VMEM_LIMIT, **kw):
    return pltpu.CompilerParams(dimension_semantics=sem, vmem_limit_bytes=vmem, **kw)


def _dot(a, b):
    return jnp.dot(a, b, preferred_element_type=F32)


def _dot_nt(a, b):
    return lax.dot_general(a, b, (((1,), (1,)), ((), ())), preferred_element_type=F32)


def _dot_tn(a, b):
    return lax.dot_general(a, b, (((0,), (0,)), ((), ())), preferred_element_type=F32)


def _split3(x):
    hi = x.astype(BF16)
    r = x - hi.astype(F32)
    mid = r.astype(BF16)
    lo = (r - mid.astype(F32)).astype(BF16)
    return hi, mid, lo


def _dot3_left(m_bf, x):
    hi, mid, lo = _split3(x)
    return _dot(m_bf, hi) + _dot(m_bf, mid) + _dot(m_bf, lo)


def _dot3_right(x, m_bf):
    hi, mid, lo = _split3(x)
    return _dot(hi, m_bf) + _dot(mid, m_bf) + _dot(lo, m_bf)


def _rms(x, g):
    return x * lax.rsqrt(jnp.mean(x * x, axis=-1, keepdims=True) + EPS) * g


def _log_sigmoid(x):
    return jnp.minimum(x, 0.0) - jnp.log1p(jnp.exp(-jnp.abs(x)))


def _silu(x):
    return x * (1.0 / (1.0 + jnp.exp(-x)))


def _sigmoid(x):
    return 1.0 / (1.0 + jnp.exp(-x))


def _row_tile(t):
    return 640 if t % 640 == 0 else CHUNK


def _chunk_unroll(n_chunks):
    return 13 if n_chunks % 13 == 0 else 1


def _to_segments(dst_ref, x):
    r, w = x.shape
    for j in range(w // LANES):
        dst_ref[:, j] = x[:, j * LANES:(j + 1) * LANES].reshape(r // SUB, SUB, LANES)


def _from_segments(src_ref):
    g, nseg = src_ref.shape[0], src_ref.shape[1]
    return jnp.concatenate([src_ref[:, j].reshape(g * SUB, LANES) for j in range(nseg)], axis=1)


def _bf16_bits(x):
    return pltpu.bitcast(x.astype(BF16).astype(F32), jnp.int32)


def _pack_pairs(x):
    half = x.shape[1] // 2
    return _bf16_bits(x[:, :half]) | lax.shift_right_logical(_bf16_bits(x[:, half:]), 16)


def _unpack_hi(w):
    return pltpu.bitcast(w & -65536, F32)


def _unpack_lo(w):
    return pltpu.bitcast(w << 16, F32)


def _load_rows(x_refs, head_ref, i):
    blocks = [r[0] for r in x_refs]
    blocks[0] = jnp.where(i == 0, head_ref[...], blocks[0])
    return blocks


def _x_specs(tr, d):
    nb = tr // CHUNK
    return [pl.BlockSpec((1, CHUNK, d), functools.partial(
        lambda bi, i, k: (bi, jnp.maximum(i * nb - 1 + k, 0), 0), k=k)) for k in range(nb)]


def _in_proj_kernel(*refs, tr, n_tiles):
    nb = tr // CHUNK
    x_refs, refs = refs[:nb], refs[nb:]
    (head_ref, hp_ref, hn_ref, ng_ref, wr_ref, wm_ref, wg_ref, wgt_ref,
     bg_ref, bgt_ref, cw_ref, cb_ref, cos_ref, sin_ref,
     rq_ref, rk_ref, rv_ref, rg_ref, mq_ref, mk_ref, mv_ref, mo_ref,
     kwf_ref, kwb_ref, gc_ref, gr_ref, ext_ref, u_ref, gts_ref, h_ref) = refs
    i = pl.program_id(1)
    ng = ng_ref[...]
    w = HEADS * DH
    nd = 2 * HEADS
    for k, blk in enumerate(_load_rows(x_refs, head_ref, i)):
        h_ref[k * CHUNK:(k + 1) * CHUNK, :] = blk

    hp = hp_ref[0]
    if tr == CHUNK:
        hp = jnp.where(i == 1, head_ref[CHUNK - HALO:CHUNK, :], hp)
    up = _rms(hp, ng)
    un = _rms(hn_ref[0], ng)
    up = jnp.where(i == 0, 0.0, up)
    un = jnp.where(i == n_tiles - 1, 0.0, un)
    uh = jnp.concatenate([up, un], axis=0).astype(BF16)
    halo = _dot(uh, wm_ref[:, 0:2 * w])
    ext_ref[0:HALO, :] = halo[0:HALO]
    ext_ref[HALO + tr:2 * HALO + tr, :] = halo[HALO:2 * HALO]

    sub = tr // 2
    lane_s = lax.broadcasted_iota(jnp.int32, (sub, LANES), 1)
    for r0 in (0, sub):
        rs = slice(r0, r0 + sub)
        u = _rms(h_ref[rs, :], ng).astype(BF16)
        u_ref[rs, :] = u
        real = lax.broadcasted_iota(jnp.int32, (sub, 1), 0) + (i * tr + r0) >= PAD
        cosf = cos_ref[rs, :]
        sinf = sin_ref[rs, :]
        pq = _dot(u, wr_ref[:, 0:w])
        pk = _dot(u, wr_ref[:, w:2 * w])
        for hd in range(HEADS):
            sl = slice(hd * DH, (hd + 1) * DH)
            xq = pq[:, sl]
            xk = pk[:, sl]
            rq_ref[0, rs, sl] = ((xq * cosf + pltpu.roll(xq, DH // 2, 1) * sinf) * (DH ** -0.5)).astype(BF16)
            rk_ref[0, rs, sl] = (xk * cosf + pltpu.roll(xk, DH // 2, 1) * sinf).astype(BF16)
        rv_ref[0, rs, :] = _dot(u, wr_ref[:, 2 * w:3 * w]).astype(BF16)
        rg_ref[0, rs, :] = _silu(_dot(u, wr_ref[:, 3 * w:4 * w])).astype(BF16)
        ext_ref[HALO + r0:HALO + r0 + sub, :] = _dot(u, wm_ref[:, 0:2 * w])
        mv_ref[0, rs, :] = _dot(u, wm_ref[:, 2 * w:3 * w]).astype(BF16)
        mo_ref[0, rs, :] = _sigmoid(_dot(u, wm_ref[:, 3 * w:4 * w])).astype(BF16)
        gts = _dot(u, wg_ref[...]) + bg_ref[...]
        gts_ref[rs, :] = jnp.where(lane_s < nd, jnp.where(real, gts, NEG),
                                   jnp.where(real & (lane_s < 2 * nd), _log_sigmoid(gts), 0.0))

    rown = lax.broadcasted_iota(jnp.int32, (2 * nd, tr), 0)
    coln = lax.broadcasted_iota(jnp.int32, (1, tr), 1) + i * tr
    realr = coln >= PAD
    gtr = _dot_nt(wgt_ref[...], u_ref[...]) + bgt_ref[...]
    gtr = jnp.where(rown < nd, jnp.where(realr, gtr, NEG),
                    jnp.where(realr, _log_sigmoid(gtr), 0.0))
    ci = lax.broadcasted_iota(jnp.int32, (CHUNK, CHUNK), 0)
    cj = lax.broadcasted_iota(jnp.int32, (CHUNK, CHUNK), 1)
    causal = ci >= cj
    anti = ci <= cj
    low = causal.astype(BF16)
    upp = anti.astype(BF16)
    lane_c = lax.broadcasted_iota(jnp.int32, (CHUNK, LANES), 1)
    lane_1 = lax.broadcasted_iota(jnp.int32, (1, LANES), 1)
    rown_c = lax.broadcasted_iota(jnp.int32, (2 * nd, CHUNK), 0)
    for c in range(tr // CHUNK):
        rs = slice(c * CHUNK, (c + 1) * CHUNK)
        realc = lax.broadcasted_iota(jnp.int32, (CHUNK, 1), 0) + (i * tr + c * CHUNK) >= PAD
        conv = cb_ref[...] + jnp.zeros((CHUNK, 2 * w), F32)
        for k in range(CONV_K):
            off = HALO + c * CHUNK + k - CONV_K // 2
            conv = conv + cw_ref[k:k + 1, :] * ext_ref[off:off + CHUNK, :]
        qk = jnp.where(realc, _silu(conv), 0.0)
        mq_ref[0, rs, :] = (qk[:, 0:w] * (DH ** -0.5)).astype(BF16)
        mk_ref[0, rs, :] = qk[:, w:2 * w].astype(BF16)
        kconv = qk[:, w:2 * w]
        g = gts_ref[rs, :]
        lf = jnp.where(lane_c >= nd, g, 0.0)
        pre = _dot3_left(low, lf)
        suf = _dot3_left(upp, lf)
        bc = pltpu.roll(jnp.where(lane_c < nd + HEADS, pre, suf), LANES - nd, 1)
        blast = jnp.where(lane_1 < HEADS, bc[CHUNK - 1:CHUNK], bc[0:1])
        log_u = blast - bc + g
        a = jnp.max(log_u, axis=0, keepdims=True)
        wu = jnp.exp(log_u - a)
        gl = gtr[:, rs]
        lfr = jnp.where(rown_c >= nd, gl, 0.0)
        prer = _dot3_right(lfr, upp)
        sufr = _dot3_right(lfr, low)
        rr = gl[0:nd] - jnp.concatenate([prer[nd:nd + HEADS], sufr[nd + HEADS:2 * nd]], axis=0)
        for hd in range(HEADS):
            hs = slice(hd * DH, (hd + 1) * DH)
            cols = []
            for dr, (msk, kw_ref) in enumerate(((causal, kwf_ref), (anti, kwb_ref))):
                i8 = dr * HEADS + hd
                rmax = jnp.max(jnp.where(msk, rr[i8:i8 + 1, :], NEG), axis=1, keepdims=True)
                cols += [rmax, bc[:, i8:i8 + 1]]
                kw_ref[0, rs, hs] = (kconv[:, hs] * wu[:, i8:i8 + 1]).astype(BF16)
            gc_ref[0, hd, rs, :] = jnp.where(
                lane_c == 0, cols[0], jnp.where(lane_c == 1, cols[1], jnp.where(lane_c == 2, cols[2], cols[3])))
            scal = [jnp.broadcast_to(x[:, j:j + 1], (1, CHUNK))
                    for j in (hd, HEADS + hd) for x in (a, blast)]
            gr_ref[0, hd, :, rs] = jnp.concatenate(
                [rr[hd:hd + 1], rr[HEADS + hd:HEADS + hd + 1]] + scal + [jnp.zeros((2, CHUNK), F32)], axis=0)


def _in_proj(x, head, norm_g, w_in, b_gates, conv_w, conv_b, cosf, sinf):
    b, s, d = x.shape
    t = CHUNK + s
    tr = _row_tile(t)
    n_tiles = t // tr
    w = HEADS * DH
    wr = w_in[:, 0:4 * w].astype(BF16)
    wm = w_in[:, 4 * w:8 * w].astype(BF16)
    wg = jnp.pad(w_in[:, 8 * w:], ((0, 0), (0, LANES - 4 * HEADS))).astype(BF16)
    wgt = w_in[:, 8 * w:].T.astype(BF16)
    bg = jnp.pad(b_gates, (0, LANES - 4 * HEADS)).reshape(1, LANES)
    bgt = b_gates.reshape(4 * HEADS, 1)
    hb = tr // HALO
    pad_h = CHUNK // HALO
    last_h = s // HALO - 1
    seq = lambda bi, i: (bi, i, 0)
    const2 = lambda bi, i: (0, 0)
    out_bf = jax.ShapeDtypeStruct((b, t, w), BF16)
    kern = functools.partial(_in_proj_kernel, tr=tr, n_tiles=n_tiles)
    return pl.pallas_call(
        kern,
        grid=(b, n_tiles),
        in_specs=_x_specs(tr, d) + [
            pl.BlockSpec((CHUNK, d), const2),
            pl.BlockSpec((1, HALO, d), lambda bi, i: (bi, jnp.maximum(i * hb - pad_h - 1, 0), 0)),
            pl.BlockSpec((1, HALO, d), lambda bi, i: (bi, jnp.minimum((i + 1) * hb - pad_h, last_h), 0)),
            pl.BlockSpec((1, d), const2),
            pl.BlockSpec((d, 4 * w), const2),
            pl.BlockSpec((d, 4 * w), const2),
            pl.BlockSpec((d, LANES), const2),
            pl.BlockSpec((4 * HEADS, d), const2),
            pl.BlockSpec((1, LANES), const2),
            pl.BlockSpec((4 * HEADS, 1), const2),
            pl.BlockSpec((CONV_K, 2 * w), const2),
            pl.BlockSpec((1, 2 * w), const2),
            pl.BlockSpec((tr, DH), lambda bi, i: (i, 0)),
            pl.BlockSpec((tr, DH), lambda bi, i: (i, 0)),
        ],
        out_specs=[pl.BlockSpec((1, tr, w), seq)] * 10 + [
            pl.BlockSpec((1, HEADS, tr, LANES), lambda bi, i: (bi, 0, i, 0)),
            pl.BlockSpec((1, HEADS, 8, tr), lambda bi, i: (bi, 0, 0, i)),
        ],
        out_shape=[out_bf] * 10 + [
            jax.ShapeDtypeStruct((b, HEADS, t, LANES), F32),
            jax.ShapeDtypeStruct((b, HEADS, 8, t), F32),
        ],
        scratch_shapes=[pltpu.VMEM((tr + 2 * HALO, 2 * w), F32), pltpu.VMEM((tr, d), BF16),
                        pltpu.VMEM((tr, LANES), F32), pltpu.VMEM((tr, d), F32)],
        compiler_params=_cparams(("parallel", "arbitrary")),
        name="in_proj",
    )(*([x] * (tr // CHUNK)), head, x, x, norm_g.reshape(1, d), wr, wm, wg, wgt, bg, bgt,
      conv_w, conv_b.reshape(1, 2 * w), cosf, sinf)


def _retention_kernel(lg_ref, q_ref, k_ref, v_ref, o_ref, of_ref, ob_ref, *, n_chunks):
    hd = pl.program_id(1)
    lgf = lg_ref[0, hd]
    lgb = lg_ref[1, hd]
    li = lax.broadcasted_iota(jnp.int32, (CHUNK, CHUNK), 0)
    mi = lax.broadcasted_iota(jnp.int32, (CHUNK, CHUNK), 1)
    diff = (li - mi).astype(F32)
    decay = (jnp.where(diff >= 0, jnp.exp(lgf * jnp.maximum(diff, 0.0)), 0.0)
             + jnp.where(diff <= 0, jnp.exp(lgb * jnp.maximum(-diff, 0.0)), 0.0))
    lcol = lax.broadcasted_iota(jnp.int32, (CHUNK, 1), 0).astype(F32)
    one = jnp.ones((1, 1), F32)
    zeta_f = jnp.exp(lgf * (CHUNK - 1.0 - lcol))
    xi_f = jnp.exp(lgf * (lcol + 1.0))
    g_f = jnp.exp(lgf * CHUNK * one)
    zeta_b = jnp.exp(lgb * lcol)
    xi_b = jnp.exp(lgb * (CHUNK - lcol))
    g_b = jnp.exp(lgb * CHUNK * one)

    def fwd(c, state):
        rows = pl.ds(pl.multiple_of(c * CHUNK, CHUNK), CHUNK)
        q = q_ref[0, rows, :]
        k = k_ref[0, rows, :]
        v = v_ref[0, rows, :]
        s = _dot_nt(q, k) * decay
        intra = _dot(s.astype(BF16), v)
        cross = _dot(q, state.astype(BF16)) * xi_f
        of_ref[rows, :] = intra + cross
        kz = (k.astype(F32) * zeta_f).astype(BF16)
        return g_f * state + _dot_tn(kz, v)

    def bwd(j, state):
        c = n_chunks - 1 - j
        rows = pl.ds(pl.multiple_of(c * CHUNK, CHUNK), CHUNK)
        q = q_ref[0, rows, :]
        k = k_ref[0, rows, :]
        v = v_ref[0, rows, :]
        ob_ref[rows, :] = _dot(q, state.astype(BF16)) * xi_b
        kz = (k.astype(F32) * zeta_b).astype(BF16)
        return g_b * state + _dot_tn(kz, v)

    zero = jnp.zeros((DH, DH), F32)
    lax.fori_loop(0, n_chunks, lambda j, st: (fwd(j, st[0]), bwd(j, st[1])), (zero, zero),
                  unroll=_chunk_unroll(n_chunks))
    o_ref[0] = (of_ref[...] + ob_ref[...]).astype(BF16)


def _retention(lg, rq, rk, rv):
    b, t, _ = rq.shape
    spec = pl.BlockSpec((1, t, DH), lambda bi, hi, lg_ref: (bi, 0, hi))
    return pl.pallas_call(
        functools.partial(_retention_kernel, n_chunks=t // CHUNK),
        grid_spec=pltpu.PrefetchScalarGridSpec(
            num_scalar_prefetch=1,
            grid=(b, HEADS),
            in_specs=[spec, spec, spec],
            out_specs=spec,
            scratch_shapes=[pltpu.VMEM((t, DH), F32)] * 2,
        ),
        out_shape=jax.ShapeDtypeStruct((b, t, HEADS * DH), BF16),
        compiler_params=_cparams(("parallel", "arbitrary")),
        name="retention",
    )(lg, rq, rk, rv)


def _mlstm_kernel(q_ref, k_ref, v_ref, kwf_ref, kwb_ref, gc_ref, gr_ref, o_ref, of_ref, ob_ref, *, n_chunks):
    li = lax.broadcasted_iota(jnp.int32, (CHUNK, CHUNK), 0)
    mi = lax.broadcasted_iota(jnp.int32, (CHUNK, CHUNK), 1)
    causal = li >= mi
    anti = li <= mi

    ones = jnp.ones((CHUNK, DH), BF16)

    def step(c, carry, backward):
        st, m_st = carry
        rows = pl.ds(pl.multiple_of(c * CHUNK, CHUNK), CHUNK)
        q = q_ref[0, rows, :]
        k = k_ref[0, rows, :]
        v_aug = jnp.concatenate([v_ref[0, rows, :], ones], axis=1)
        gcol = gc_ref[0, 0, rows, :]
        grow = gr_ref[0, 0, :, rows]
        if backward:
            kw = kwb_ref[0, rows, :]
            r_max, bc = gcol[:, 2:3], gcol[:, 3:4]
            r_row, a, b_last = grow[1:2, :], grow[4:5, 0:1], grow[5:6, 0:1]
            mask = anti
        else:
            kw = kwf_ref[0, rows, :]
            r_max, bc = gcol[:, 0:1], gcol[:, 1:2]
            r_row, a, b_last = grow[0:1, :], grow[2:3, 0:1], grow[3:4, 0:1]
            mask = causal
        mm = jnp.maximum(jnp.broadcast_to(r_max, (CHUNK, DH)), m_st)
        s = _dot_nt(q, k) * jnp.where(mask, jnp.exp(r_row - mm), 0.0)
        w_inter = jnp.exp(m_st - mm)
        w2 = jnp.concatenate([w_inter, w_inter], axis=1)
        tot = _dot(s.astype(BF16), v_aug) + _dot(q, st.astype(BF16)) * w2
        floor = jnp.exp(-(jnp.broadcast_to(bc, (CHUNK, DH)) + mm))
        hval = tot[:, 0:DH] / jnp.maximum(jnp.abs(tot[:, DH:2 * DH]), floor)
        if backward:
            ob_ref[rows, :] = hval
        else:
            of_ref[rows, :] = hval
        upd = _dot_tn(kw, v_aug)
        m_new = jnp.maximum(b_last + m_st, a)
        f = jnp.exp(b_last + m_st - m_new)
        g = jnp.exp(a - m_new)
        return f * st + g * upd, m_new

    def both(j, carry):
        return step(j, carry[0], False), step(n_chunks - 1 - j, carry[1], True)

    init = (jnp.zeros((DH, 2 * DH), F32), jnp.zeros((1, 1), F32))
    lax.fori_loop(0, n_chunks, both, (init, init), unroll=_chunk_unroll(n_chunks))
    o_ref[0] = (of_ref[...] + ob_ref[...]).astype(BF16)


def _mlstm(mq, mk, mv, kwf, kwb, gcol, grow):
    b, t, _ = mq.shape
    spec = pl.BlockSpec((1, t, DH), lambda bi, hi: (bi, 0, hi))
    return pl.pallas_call(
        functools.partial(_mlstm_kernel, n_chunks=t // CHUNK),
        grid=(b, HEADS),
        in_specs=[spec, spec, spec, spec, spec,
                  pl.BlockSpec((1, 1, t, LANES), lambda bi, hi: (bi, hi, 0, 0)),
                  pl.BlockSpec((1, 1, 8, t), lambda bi, hi: (bi, hi, 0, 0))],
        out_specs=spec,
        out_shape=jax.ShapeDtypeStruct((b, t, HEADS * DH), BF16),
        scratch_shapes=[pltpu.VMEM((t, DH), F32)] * 2,
        compiler_params=_cparams(("parallel", "arbitrary")),
        name="mlstm",
    )(mq, mk, mv, kwf, kwb, gcol, grow)


def _head_norm(y, g):
    outs = []
    for hd in range(HEADS):
        yh = y[:, hd * DH:(hd + 1) * DH]
        mu = jnp.mean(yh, axis=1, keepdims=True)
        yc = yh - mu
        var = jnp.mean(yc * yc, axis=1, keepdims=True)
        outs.append(yc * lax.rsqrt(var + EPS))
    return jnp.concatenate(outs, axis=1) * g


def _out_proj_kernel(*refs, tr):
    nb = tr // CHUNK
    x_refs, refs = refs[:nb], refs[nb:]
    (head_ref, ret_ref, hm_ref, rg_ref, mo_ref, rgn_ref, mgn_ref, wo_ref,
     fg_ref, wrt_ref, h1_ref, u2_ref, aff_ref) = refs
    i = pl.program_id(1)
    row = lax.broadcasted_iota(jnp.int32, (tr, 1), 0) + i * tr
    real = row >= PAD
    w = HEADS * DH
    y_ret = _head_norm(ret_ref[0].astype(F32), rgn_ref[...]) * rg_ref[0].astype(F32)
    y_m = _head_norm(mo_ref[0].astype(F32) * hm_ref[0].astype(F32), mgn_ref[...])
    y_ret = jnp.where(real, y_ret, 0.0).astype(BF16)
    y_m = jnp.where(real, y_m, 0.0).astype(BF16)
    h = jnp.concatenate(_load_rows(x_refs, head_ref, i), axis=0)
    h1 = h + _dot(y_ret, wo_ref[0:w, :]) + _dot(y_m, wo_ref[w:2 * w, :])
    h1_ref[0] = h1
    u2 = _rms(h1, fg_ref[...])
    _to_segments(u2_ref, _pack_pairs(u2))
    logits = _dot_nt(wrt_ref[...], u2.astype(BF16))
    mx = jnp.max(logits, axis=0, keepdims=True)
    ex = jnp.exp(logits - mx)
    aff = ex / jnp.sum(ex, axis=0, keepdims=True)
    coln = lax.broadcasted_iota(jnp.int32, (1, tr), 1) + i * tr
    aff_ref[0] = jnp.where(coln >= PAD, aff, -1.0)


def _out_proj(x, head, ret, hm, rgs, mos, ret_gn_g, mlstm_gn_g, w_out, ffn_norm_g, w_router):
    b, s, d = x.shape
    t = CHUNK + s
    tr = _row_tile(t)
    n_tiles = t // tr
    nseg = d // 2 // LANES
    w = HEADS * DH
    seq = lambda bi, i: (bi, i, 0)
    const2 = lambda bi, i: (0, 0)
    return pl.pallas_call(
        functools.partial(_out_proj_kernel, tr=tr),
        grid=(b, n_tiles),
        in_specs=_x_specs(tr, d) + [
            pl.BlockSpec((CHUNK, d), const2),
            pl.BlockSpec((1, tr, w), seq),
            pl.BlockSpec((1, tr, w), seq),
            pl.BlockSpec((1, tr, w), seq),
            pl.BlockSpec((1, tr, w), seq),
            pl.BlockSpec((1, w), const2),
            pl.BlockSpec((1, w), const2),
            pl.BlockSpec((2 * w, d), const2),
            pl.BlockSpec((1, d), const2),
            pl.BlockSpec((N_EXPERTS, d), const2),
        ],
        out_specs=[
            pl.BlockSpec((1, tr, d), seq),
            pl.BlockSpec((tr // SUB, nseg, SUB, LANES), lambda bi, i: (bi * n_tiles + i, 0, 0, 0)),
            pl.BlockSpec((1, N_EXPERTS, tr), lambda bi, i: (bi, 0, i)),
        ],
        out_shape=[
            jax.ShapeDtypeStruct((b, t, d), F32),
            jax.ShapeDtypeStruct((b * t // SUB, nseg, SUB, LANES), jnp.int32),
            jax.ShapeDtypeStruct((b, N_EXPERTS, t), F32),
        ],
        compiler_params=_cparams(("parallel", "arbitrary")),
        name="out_proj",
    )(*([x] * (tr // CHUNK)), head, ret, hm, rgs, mos, ret_gn_g.reshape(1, w), mlstm_gn_g.reshape(1, w),
      w_out.astype(BF16), ffn_norm_g.reshape(1, d), w_router.T.astype(BF16))


def _select_kernel(aff_ref, pm_ref, dv_ref, cnt_ref, bt_ref, mc_ref, sel_ref, *, t, cap, tr):
    e_n = N_EXPERTS
    nblk = t // LANES
    bits = pltpu.bitcast(aff_ref[0], jnp.int32)
    capf = float(cap)

    def search(_, lohi):
        lo, hi = lohi
        mid = lo + ((hi - lo + 1) >> 1)
        cnt = jnp.sum((bits >= mid).astype(F32), axis=1, keepdims=True)
        ok = cnt >= capf
        return jnp.where(ok, mid, lo), jnp.where(ok, hi, mid - 1)

    lo0 = jnp.zeros((e_n, 1), jnp.int32)
    hi0 = jnp.full((e_n, 1), 0x3F800000, jnp.int32)
    thr, _ = lax.fori_loop(0, 31, search, (lo0, hi0))
    need = capf - jnp.sum((bits > thr).astype(F32), axis=1, keepdims=True)

    ci = lax.broadcasted_iota(jnp.int32, (LANES, LANES), 0)
    cj = lax.broadcasted_iota(jnp.int32, (LANES, LANES), 1)
    upp = (ci <= cj).astype(BF16)
    ei = lax.broadcasted_iota(jnp.int32, (e_n, e_n), 0)
    ej = lax.broadcasted_iota(jnp.int32, (e_n, e_n), 1)
    strict = (ej < ei).astype(BF16)

    carry = jnp.zeros((e_n, 1), F32)
    for j in range(nblk):
        sl = slice(j * LANES, (j + 1) * LANES)
        bb = bits[:, sl]
        eqf = (bb == thr).astype(F32)
        ceq = _dot(eqf.astype(BF16), upp) + carry
        carry = ceq[:, LANES - 1:LANES]
        sel_ref[:, sl] = jnp.where(bb > thr, 1.0, jnp.where(ceq <= need, eqf, 0.0))

    carry = jnp.zeros((e_n, 1), F32)
    tok = lax.broadcasted_iota(jnp.int32, (1, LANES), 1).astype(F32)
    lane_b = lax.broadcasted_iota(jnp.int32, (e_n, LANES), 1)
    btot = jnp.zeros((e_n, LANES), F32)
    tile_max = jnp.zeros((SUB, LANES), F32)
    lane_t = lax.broadcasted_iota(jnp.int32, (SUB, LANES), 1)
    for j in range(nblk):
        sl = slice(j * LANES, (j + 1) * LANES)
        selb = sel_ref[:, sl]
        selbf = selb.astype(BF16)
        pin = _dot(selbf, upp) + carry
        carry = pin[:, LANES - 1:LANES]
        btot = jnp.where(lane_b == j, carry, btot)
        pm_ref[0, :, sl] = jnp.where(selb > 0, pin, 0.0)
        rank = _dot(strict, selbf)
        dv_ref[0, :, sl] = (tok + float(j * LANES)) * float(N_EXPERTS) + rank
        cnt_row = jnp.sum(selb, axis=0, keepdims=True)
        cnt_ref[0, sl, :] = jnp.broadcast_to(cnt_row, (LANES, LANES)).T
        tile_max = jnp.where(lane_t == (j * LANES) // tr,
                             jnp.maximum(tile_max, jnp.max(cnt_row, axis=1, keepdims=True)), tile_max)
    bt_ref[0] = jnp.where(lane_b < nblk, btot, carry).astype(jnp.int32)
    mc_ref[0] = tile_max.astype(jnp.int32)


def _select(aff, cap):
    b, e_n, t = aff.shape
    tr = _row_tile(t)
    assert t // tr <= LANES
    spec = pl.BlockSpec((1, e_n, t), lambda bi: (bi, 0, 0))
    return pl.pallas_call(
        functools.partial(_select_kernel, t=t, cap=cap, tr=tr),
        grid=(b,),
        in_specs=[spec],
        out_specs=[
            spec,
            spec,
            pl.BlockSpec((1, t, LANES), lambda bi: (bi, 0, 0)),
            pl.BlockSpec((1, e_n, LANES), lambda bi: (bi, 0, 0)),
            pl.BlockSpec((1, SUB, LANES), lambda bi: (bi, 0, 0)),
        ],
        out_shape=[
            jax.ShapeDtypeStruct((b, e_n, t), F32),
            jax.ShapeDtypeStruct((b, e_n, t), F32),
            jax.ShapeDtypeStruct((b, t, LANES), F32),
            jax.ShapeDtypeStruct((b, e_n, LANES), jnp.int32),
            jax.ShapeDtypeStruct((b, SUB, LANES), jnp.int32),
        ],
        scratch_shapes=[pltpu.VMEM((e_n, t), F32)],
        compiler_params=_cparams(("parallel",)),
        name="select",
    )(aff)


def _compact_kernel(bt_ref, pm_ref, dv_ref, aff_ref, src_ref, dst_ref, gate_ref, accd_ref, accg_ref,
                    *, t, cap, c_pad, nct, nseg, batch):
    bi = batch
    e_n = N_EXPERTS
    nblk = t // LANES
    lane_e = lax.broadcasted_iota(jnp.int32, (LANES, LANES), 1)
    slot_l = lax.broadcasted_iota(jnp.int32, (1, LANES), 1)

    spr = LANES // nseg
    sh_spr = spr.bit_length() - 1
    sh_grp = (SUB * nseg).bit_length() - 1
    sub_i = lax.broadcasted_iota(jnp.int32, (SUB, LANES), 0)
    lane_i = lax.broadcasted_iota(jnp.int32, (SUB, LANES), 1)
    pick = (lane_i >> sh_spr) == sub_i
    pl_i = lax.broadcasted_iota(jnp.int32, (LANES, LANES), 0)
    pl_j = lax.broadcasted_iota(jnp.int32, (LANES, LANES), 1)
    perm = ((pl_i & (spr - 1)) == ((pl_j >> sh_grp) << 3) + (pl_j & (SUB - 1))).astype(BF16)
    seg_j = (lane_i & (SUB * nseg - 1)) >> 3

    def segment_tile(rows):
        spread = _dot3_right(jnp.where(pick, rows.astype(F32), 0.0), perm).astype(jnp.int32)
        return (((spread >> 3) * nseg + seg_j) << 3) + (spread & (SUB - 1))

    def expert(e):
        base = e * LANES

        def ctile(ct, j0):
            c0 = pl.multiple_of(ct * LANES, LANES)
            j_lo = lax.while_loop(
                lambda j: (j < nblk) & (bt_ref[base + jnp.minimum(j, nblk - 1)] <= c0),
                lambda j: j + 1, j0)
            j_hi = lax.while_loop(
                lambda j: (j < nblk) & (bt_ref[base + jnp.minimum(j, nblk - 1)] < c0 + LANES),
                lambda j: j + 1, j_lo)
            j_end = jnp.minimum(j_hi + 1, nblk)
            want = (lax.broadcasted_iota(jnp.int32, (LANES, 1), 0) + (c0 + 1)).astype(F32)

            def blk(j, acc):
                acc_d, acc_g = acc
                cols = pl.ds(pl.multiple_of(j * LANES, LANES), LANES)
                hit = pm_ref[0, e:e + 1, cols] == want
                acc_d = acc_d + jnp.where(hit, dv_ref[0, e:e + 1, cols], 0.0)
                acc_g = acc_g + jnp.where(hit, aff_ref[0, e:e + 1, cols], 0.0)
                return acc_d, acc_g

            zero = jnp.zeros((LANES, LANES), F32)
            acc_d, acc_g = lax.fori_loop(j_lo, j_end, blk, (zero, zero))
            accd_ref[e * nct + ct] = acc_d
            accg_ref[e * nct + ct] = acc_g
            return j_lo

        lax.fori_loop(0, nct, ctile, 0)

    for e in range(e_n):
        expert(e)

    def finish(ct, _):
        c0 = pl.multiple_of(ct * LANES, LANES)
        p = slot_l + (c0 - cap)
        is_pad = p >= 0
        gate_tile = jnp.zeros((LANES, LANES), F32)
        for e in range(e_n):
            v = jnp.sum(accd_ref[e * nct + ct].T, axis=0, keepdims=True).astype(jnp.int32)
            tok = v >> 4
            rank = v & (N_EXPERTS - 1)
            src = bi * t + jnp.where(is_pad, 0, tok)
            dst = jnp.where(is_pad, N_EXPERTS * t + e * (c_pad - cap) + p, rank * t + tok)
            out_rows = pl.ds(pl.multiple_of((e * nct + ct) * SUB, SUB), SUB)
            src_ref[0, out_rows, :] = segment_tile(src)
            dst_ref[0, out_rows, :] = segment_tile(dst)
            gcol = jnp.sum(accg_ref[e * nct + ct], axis=1, keepdims=True)
            gate_tile = jnp.where(lane_e == e, gcol, gate_tile)
        gate_ref[0, pl.ds(c0, LANES), :] = gate_tile
        return 0

    lax.fori_loop(0, nct, finish, 0)


def _compact(btot, pm, dv, aff, cap, c_pad, nseg, batch):
    _, e_n, t = aff.shape
    assert nseg in (1, 2, 4, 8)
    nct = -(-c_pad // LANES)
    tile_rows = e_n * nct * SUB
    spec = pl.BlockSpec((1, e_n, t), lambda i, bt: (0, 0, 0))
    ospec = pl.BlockSpec((1, tile_rows, LANES), lambda i, bt: (0, 0, 0))
    src, dst, gate = pl.pallas_call(
        functools.partial(_compact_kernel, t=t, cap=cap, c_pad=c_pad, nct=nct, nseg=nseg, batch=batch),
        grid_spec=pltpu.PrefetchScalarGridSpec(
            num_scalar_prefetch=1,
            grid=(1,),
            in_specs=[spec, spec, spec],
            out_specs=[ospec, ospec, pl.BlockSpec((1, nct * LANES, LANES), lambda i, bt: (0, 0, 0))],
            scratch_shapes=[pltpu.VMEM((e_n * nct, LANES, LANES), F32)] * 2,
        ),
        out_shape=[
            jax.ShapeDtypeStruct((1, tile_rows, LANES), jnp.int32),
            jax.ShapeDtypeStruct((1, tile_rows, LANES), jnp.int32),
            jax.ShapeDtypeStruct((1, nct * LANES, LANES), F32),
        ],
        compiler_params=_cparams(("arbitrary",)),
        name="compact",
    )(btot.reshape(e_n * LANES), pm, dv, aff)
    ids = lambda a: a.reshape(e_n, nct, SUB, LANES)[:, :, :nseg].reshape(e_n, -1)[:, :c_pad * nseg].reshape(-1)
    return ids(src), ids(dst), gate


def _sc_mesh():
    return plsc.VectorSubcoreMesh(core_axis_name="core", subcore_axis_name="subcore")


def _row_gather(table, idx):
    m = idx.shape[0]
    d = table.shape[1]

    @functools.partial(pl.kernel, out_type=jax.ShapeDtypeStruct((m, d), table.dtype),
                       mesh=_sc_mesh(), scratch_types=[])
    def gather_kernel(x_hbm, i_hbm, o_hbm):
        def body(i_vmem, o_vmem):
            pltpu.sync_copy(x_hbm.at[i_vmem.at[0]], o_vmem)

        pltpu.emit_pipeline(
            body,
            grid=(m // SC_WINDOW,),
            in_specs=[pl.BlockSpec((1, SC_WINDOW), lambda i: (0, i))],
            out_specs=[pl.BlockSpec((SC_WINDOW, d), lambda i: (i, 0))],
            core_axis_name=("core", "subcore"),
            dimension_semantics=(pltpu.PARALLEL,),
        )(i_hbm, o_hbm)

    return gather_kernel(table, idx.reshape(1, m))


def _row_scatter(rows, idx, n_out):
    m, d = rows.shape

    @functools.partial(pl.kernel, out_type=jax.ShapeDtypeStruct((n_out, d), rows.dtype),
                       mesh=_sc_mesh(), scratch_types=[])
    def scatter_kernel(x_hbm, i_hbm, o_hbm):
        def body(x_vmem, i_vmem):
            pltpu.sync_copy(x_vmem, o_hbm.at[i_vmem.at[0]])

        pltpu.emit_pipeline(
            body,
            grid=(m // SC_WINDOW,),
            in_specs=[pl.BlockSpec((SC_WINDOW, d), lambda i: (i, 0)),
                      pl.BlockSpec((1, SC_WINDOW), lambda i: (0, i))],
            out_specs=[],
            core_axis_name=("core", "subcore"),
            dimension_semantics=(pltpu.PARALLEL,),
        )(x_hbm, i_hbm)

    return scatter_kernel(rows, idx.reshape(1, m))


def _expert_kernel(gate_ref, x_ref, wg_ref, wu_ref, wd_ref, y_ref, xb_ref, acc_ref, *, c_pad, n_f):
    e = pl.program_id(0)
    f = pl.program_id(1)

    @pl.when(f == 0)
    def _():
        w = _from_segments(x_ref)
        xb_ref[...] = jnp.concatenate([_unpack_hi(w), _unpack_lo(w)], axis=1).astype(BF16)
        acc_ref[...] = jnp.zeros(acc_ref.shape, F32)

    wg = wg_ref[0].astype(BF16)
    wu = wu_ref[0].astype(BF16)
    wd = wd_ref[0].astype(BF16)
    half = c_pad // 2
    for r0 in (0, half):
        rows = slice(r0, r0 + half)
        x = xb_ref[rows, :]
        hid = _silu(_dot(x, wg)) * _dot(x, wu)
        part = _dot(hid.astype(BF16), wd)
        acc_ref[rows, :] = acc_ref[rows, :] + part

    @pl.when(f == n_f - 1)
    def _():
        lane = lax.broadcasted_iota(jnp.int32, (c_pad, LANES), 1)
        gcol = jnp.sum(jnp.where(lane == e, gate_ref[0], 0.0), axis=1, keepdims=True)
        _to_segments(y_ref, _pack_pairs(acc_ref[...] * gcol))


def _experts(gate, xs, w_gate, w_up, w_down):
    e_n, d, d_ff = w_gate.shape
    nseg = d // 2 // LANES
    c_pad = xs.shape[0] * SUB // e_n
    fc = 256
    n_f = d_ff // fc
    blk = pl.BlockSpec((c_pad // SUB, nseg, SUB, LANES), lambda e, f: (e, 0, 0, 0))
    return pl.pallas_call(
        functools.partial(_expert_kernel, c_pad=c_pad, n_f=n_f),
        grid=(e_n, n_f),
        in_specs=[
            pl.BlockSpec((1, c_pad, LANES), lambda e, f: (0, 0, 0)),
            blk,
            pl.BlockSpec((1, d, fc), lambda e, f: (e, 0, f)),
            pl.BlockSpec((1, d, fc), lambda e, f: (e, 0, f)),
            pl.BlockSpec((1, fc, d), lambda e, f: (e, f, 0)),
        ],
        out_specs=blk,
        out_shape=jax.ShapeDtypeStruct(xs.shape, jnp.int32),
        scratch_shapes=[pltpu.VMEM((c_pad, d), BF16), pltpu.VMEM((c_pad, d), F32)],
        compiler_params=_cparams(("parallel", "arbitrary")),
        name="experts",
    )(gate, xs, w_gate, w_up, w_down)


LAYERS_PER_STEP = 3


def _combine_kernel(mc_ref, h1_ref, fg_ref, *rest, n_steps, nb, n_tiles):
    cnt_refs = rest[:nb]
    z_refs = rest[nb:nb + nb * LAYERS_PER_STEP]
    o_hbm, acc_ref, obuf_ref, sem = rest[-4:]
    bi = pl.program_id(0)
    i = pl.program_id(1)
    j = pl.program_id(2)
    blocks = obuf_ref.shape[1] // CHUNK

    def tile_copies(b_, i_, slot, fn):
        for k in range(blocks):
            dst = pl.multiple_of((i_ * blocks - 1 + k) * CHUNK, CHUNK)
            copy = pltpu.make_async_copy(obuf_ref.at[slot, pl.ds(k * CHUNK, CHUNK), :],
                                         o_hbm.at[b_, pl.ds(dst, CHUNK), :], sem.at[slot])
            if k == 0:
                pl.when(i_ > 0)(functools.partial(fn, copy))
            else:
                fn(copy)

    def tile_before(steps):
        wrap = i < steps
        return bi - wrap.astype(jnp.int32), i - steps + jnp.where(wrap, n_tiles, 0)

    @pl.when(j == 0)
    def _():
        _to_segments(acc_ref, h1_ref[0])

    def add_layers(k, n_take):
        g, nz = z_refs[0].shape[0], z_refs[0].shape[1]
        cnt = cnt_refs[k][0].reshape(g, 1, SUB, LANES)
        hi = acc_ref[:, 0:nz]
        lo = acc_ref[:, nz:2 * nz]
        for l in range(n_take):
            take = cnt > (j * LAYERS_PER_STEP + l).astype(F32)
            w = z_refs[k * LAYERS_PER_STEP + l][...]
            hi = hi + jnp.where(take, _unpack_hi(w), 0.0)
            lo = lo + jnp.where(take, _unpack_lo(w), 0.0)
        acc_ref[:, 0:nz] = hi
        acc_ref[:, nz:2 * nz] = lo

    live = mc_ref[bi * LANES + i] - j * LAYERS_PER_STEP
    for k in range(nb):
        for n_take in range(1, LAYERS_PER_STEP + 1):
            last = n_take == LAYERS_PER_STEP
            pl.when((bi == k) & ((live >= n_take) if last else (live == n_take)))(
                functools.partial(add_layers, k, n_take))

    @pl.when(j == n_steps - 1)
    def _():
        idx = bi * n_tiles + i
        slot = idx % 2

        @pl.when(idx >= 2)
        def _():
            b2, i2 = tile_before(2)
            tile_copies(b2, i2, slot, lambda c: c.wait())

        obuf_ref[slot] = _rms(_from_segments(acc_ref), fg_ref[...])
        tile_copies(bi, i, slot, lambda c: c.start())

        @pl.when(idx == nb * n_tiles - 1)
        def _():
            if nb * n_tiles >= 2:
                b1, i1 = tile_before(1)
                tile_copies(b1, i1, 1 - slot, lambda c: c.wait())
            tile_copies(bi, i, slot, lambda c: c.wait())


def _combine(maxcnt, h1, zs, cnts, final_norm_g):
    b, t, d = h1.shape
    n_steps = -(-N_EXPERTS // LAYERS_PER_STEP)
    tr = _row_tile(t)
    n_tiles = t // tr

    def z_map(k, l):
        def index(bi, i, j, mc):
            layer = jnp.maximum(jnp.minimum(j * LAYERS_PER_STEP + l, mc[bi * LANES + i] - 1), 0)
            return (jnp.where(bi == k, layer * n_tiles + i, 0), 0, 0, 0)
        return index

    def cnt_map(k):
        return lambda bi, i, j, mc: (0, jnp.where(bi == k, i, 0), 0)

    seg_block = (tr // SUB, d // LANES, SUB, LANES)
    z_block = (tr // SUB, d // 2 // LANES, SUB, LANES)
    z_specs = [pl.BlockSpec(z_block, z_map(k, l)) for k in range(b) for l in range(LAYERS_PER_STEP)]
    z_args = [zs[k] for k in range(b) for _ in range(LAYERS_PER_STEP)]
    return pl.pallas_call(
        functools.partial(_combine_kernel, n_steps=n_steps, nb=b, n_tiles=n_tiles),
        grid_spec=pltpu.PrefetchScalarGridSpec(
            num_scalar_prefetch=1,
            grid=(b, n_tiles, n_steps),
            in_specs=[
                pl.BlockSpec((1, tr, d), lambda bi, i, j, mc: (bi, i, 0)),
                pl.BlockSpec((1, d), lambda bi, i, j, mc: (0, 0)),
            ] + [pl.BlockSpec((1, tr, LANES), cnt_map(k)) for k in range(b)] + z_specs,
            out_specs=pl.BlockSpec(memory_space=pl.ANY),
            scratch_shapes=[pltpu.VMEM(seg_block, F32), pltpu.VMEM((2, tr, d), F32),
                            pltpu.SemaphoreType.DMA((2,))],
        ),
        out_shape=jax.ShapeDtypeStruct((b, t - CHUNK, d), F32),
        compiler_params=_cparams(("arbitrary", "arbitrary", "arbitrary")),
        name="combine",
    )(maxcnt, h1, final_norm_g.reshape(1, d), *cnts, *z_args)


def kernel(x, meta_tokens, mix_norm_g, w_in, b_gates, conv_w, conv_b, ret_decay_logit, ret_gn_g,
           mlstm_gn_g, w_out, ffn_norm_g, w_router, w_gate, w_up, w_down, final_norm_g):
    b, s, d = x.shape
    t = CHUNK + s
    n = N_META + s
    cap = 2 * n // N_EXPERTS
    nseg = d // 2 // LANES
    c_pad = next(c for c in range(-(-cap // 16) * 16, cap + 4096, 16)
                 if (N_EXPERTS * c * nseg) % (SC_WINDOW * SC_SUBCORES) == 0)

    head = jnp.concatenate([jnp.zeros((PAD, d), x.dtype), meta_tokens.astype(x.dtype)], axis=0)

    half = DH // 2
    pos = jnp.arange(t, dtype=F32) - PAD
    inv = ROPE_BASE ** (-jnp.arange(half, dtype=F32) / half)
    ang = pos[:, None] * inv[None, :]
    cosf = jnp.concatenate([jnp.cos(ang), jnp.cos(ang)], axis=1)
    sinf = jnp.concatenate([-jnp.sin(ang), jnp.sin(ang)], axis=1)

    assert mix_norm_g.shape[0] == 1, "single-layer block only"
    rq, rk, rv, rgs, mq, mk, mv, mos, kwf, kwb, gcol, grow = _in_proj(
        x, head, mix_norm_g[0], w_in[0], b_gates[0], conv_w[0], conv_b[0], cosf, sinf)
    lg = jax.nn.log_sigmoid(ret_decay_logit[0].astype(F32))
    ret = _retention(lg, rq, rk, rv)
    hm = _mlstm(mq, mk, mv, kwf, kwb, gcol, grow)
    h1, u2, aff = _out_proj(x, head, ret, hm, rgs, mos, ret_gn_g[0], mlstm_gn_g[0], w_out[0],
                            ffn_norm_g[0], w_router[0])
    tiled = lambda a: a.reshape(-1, nseg, SUB, LANES)
    flat = lambda a: a.reshape(-1, LANES)
    z_rows = -(-(N_EXPERTS * t + N_EXPERTS * (c_pad - cap)) // SUB) * SUB
    routed = []
    for bi in range(b):
        aff_b = aff[bi:bi + 1]
        pm, dv, cnt, btot, maxcnt = _select(aff_b, cap)
        src, dst, gate = _compact(btot, pm, dv, aff_b, cap, c_pad, nseg, bi)
        routed.append((_row_gather(flat(u2), src), dst, gate, cnt, maxcnt))
    zs = []
    for xs, dst, gate, _, _ in routed:
        ys = _experts(gate, tiled(xs), w_gate[0], w_up[0], w_down[0])
        zs.append(tiled(_row_scatter(flat(ys), dst, z_rows * nseg)))
    maxcnt = jnp.concatenate([r[4][:, 0, :] for r in routed], axis=0).reshape(-1)
    return _combine(maxcnt, h1, zs, [r[3] for r in routed], final_norm_g)
```

```python
import functools

import jax
import jax.numpy as jnp
from jax import lax
from jax.experimental import pallas as pl
from jax.experimental.pallas import tpu as pltpu
from jax.experimental.pallas import tpu_sc as plsc

F32 = jnp.float32
BF16 = jnp.bfloat16

LANES = 128
CHUNK = 128
N_META = 16
PAD = CHUNK - N_META
HEADS = 4
DH = 128
N_EXPERTS = 16
CONV_K = 5
HALO = 8
SUB = 8
SC_WINDOW = 128
SC_SUBCORES = 32
EPS = 1e-6
NEG = -1e30
ROPE_BASE = 10000.0
VMEM_LIMIT = 56 * 1024 * 1024


def _cparams(sem, vmem=VMEM_LIMIT, **kw):
    return pltpu.CompilerParams(dimension_semantics=sem, vmem_limit_bytes=vmem, **kw)


def _dot(a, b):
    return jnp.dot(a, b, preferred_element_type=F32)


def _dot_nt(a, b):
    return lax.dot_general(a, b, (((1,), (1,)), ((), ())), preferred_element_type=F32)


def _dot_tn(a, b):
    return lax.dot_general(a, b, (((0,), (0,)), ((), ())), preferred_element_type=F32)


def _split3(x):
    hi = x.astype(BF16)
    r = x - hi.astype(F32)
    mid = r.astype(BF16)
    lo = (r - mid.astype(F32)).astype(BF16)
    return hi, mid, lo


def _dot3_left(m_bf, x):
    hi, mid, lo = _split3(x)
    return _dot(m_bf, hi) + _dot(m_bf, mid) + _dot(m_bf, lo)


def _dot3_right(x, m_bf):
    hi, mid, lo = _split3(x)
    return _dot(hi, m_bf) + _dot(mid, m_bf) + _dot(lo, m_bf)


def _rms(x, g):
    return x * lax.rsqrt(jnp.mean(x * x, axis=-1, keepdims=True) + EPS) * g


def _log_sigmoid(x):
    return jnp.minimum(x, 0.0) - jnp.log1p(jnp.exp(-jnp.abs(x)))


def _silu(x):
    return x * (1.0 / (1.0 + jnp.exp(-x)))


def _sigmoid(x):
    return 1.0 / (1.0 + jnp.exp(-x))


def _row_tile(t):
    return 640 if t % 640 == 0 else CHUNK


def _chunk_unroll(n_chunks):
    return 13 if n_chunks % 13 == 0 else 1


def _to_segments(dst_ref, x):
    r, w = x.shape
    for j in range(w // LANES):
        dst_ref[:, j] = x[:, j * LANES:(j + 1) * LANES].reshape(r // SUB, SUB, LANES)


def _from_segments(src_ref):
    g, nseg = src_ref.shape[0], src_ref.shape[1]
    return jnp.concatenate([src_ref[:, j].reshape(g * SUB, LANES) for j in range(nseg)], axis=1)


def _bf16_bits(x):
    return pltpu.bitcast(x.astype(BF16).astype(F32), jnp.int32)


def _pack_pairs(x):
    half = x.shape[1] // 2
    return _bf16_bits(x[:, :half]) | lax.shift_right_logical(_bf16_bits(x[:, half:]), 16)


def _unpack_hi(w):
    return pltpu.bitcast(w & -65536, F32)


def _unpack_lo(w):
    return pltpu.bitcast(w << 16, F32)


def _load_rows(x_refs, head_ref, i):
    blocks = [r[0] for r in x_refs]
    blocks[0] = jnp.where(i == 0, head_ref[...], blocks[0])
    return blocks


def _x_specs(tr, d):
    nb = tr // CHUNK
    return [pl.BlockSpec((1, CHUNK, d), functools.partial(
        lambda bi, i, k: (bi, jnp.maximum(i * nb - 1 + k, 0), 0), k=k)) for k in range(nb)]


def _in_proj_kernel(*refs, tr, n_tiles):
    nb = tr // CHUNK
    x_refs, refs = refs[:nb], refs[nb:]
    (head_ref, hp_ref, hn_ref, ng_ref, wr_ref, wm_ref, wg_ref, wgt_ref,
     bg_ref, bgt_ref, cw_ref, cb_ref, cos_ref, sin_ref,
     rq_ref, rk_ref, rv_ref, rg_ref, mq_ref, mk_ref, mv_ref, mo_ref,
     kwf_ref, kwb_ref, gc_ref, gr_ref, ext_ref, u_ref, gts_ref, h_ref) = refs
    i = pl.program_id(1)
    ng = ng_ref[...]
    w = HEADS * DH
    nd = 2 * HEADS
    for k, blk in enumerate(_load_rows(x_refs, head_ref, i)):
        h_ref[k * CHUNK:(k + 1) * CHUNK, :] = blk

    hp = hp_ref[0]
    if tr == CHUNK:
        hp = jnp.where(i == 1, head_ref[CHUNK - HALO:CHUNK, :], hp)
    up = _rms(hp, ng)
    un = _rms(hn_ref[0], ng)
    up = jnp.where(i == 0, 0.0, up)
    un = jnp.where(i == n_tiles - 1, 0.0, un)
    uh = jnp.concatenate([up, un], axis=0).astype(BF16)
    halo = _dot(uh, wm_ref[:, 0:2 * w])
    ext_ref[0:HALO, :] = halo[0:HALO]
    ext_ref[HALO + tr:2 * HALO + tr, :] = halo[HALO:2 * HALO]

    sub = tr // 2
    lane_s = lax.broadcasted_iota(jnp.int32, (sub, LANES), 1)
    for r0 in (0, sub):
        rs = slice(r0, r0 + sub)
        u = _rms(h_ref[rs, :], ng).astype(BF16)
        u_ref[rs, :] = u
        real = lax.broadcasted_iota(jnp.int32, (sub, 1), 0) + (i * tr + r0) >= PAD
        cosf = cos_ref[rs, :]
        sinf = sin_ref[rs, :]
        pq = _dot(u, wr_ref[:, 0:w])
        pk = _dot(u, wr_ref[:, w:2 * w])
        for hd in range(HEADS):
            sl = slice(hd * DH, (hd + 1) * DH)
            xq = pq[:, sl]
            xk = pk[:, sl]
            rq_ref[0, rs, sl] = ((xq * cosf + pltpu.roll(xq, DH // 2, 1) * sinf) * (DH ** -0.5)).astype(BF16)
            rk_ref[0, rs, sl] = (xk * cosf + pltpu.roll(xk, DH // 2, 1) * sinf).astype(BF16)
        rv_ref[0, rs, :] = _dot(u, wr_ref[:, 2 * w:3 * w]).astype(BF16)
        rg_ref[0, rs, :] = _silu(_dot(u, wr_ref[:, 3 * w:4 * w])).astype(BF16)
        ext_ref[HALO + r0:HALO + r0 + sub, :] = _dot(u, wm_ref[:, 0:2 * w])
        mv_ref[0, rs, :] = _dot(u, wm_ref[:, 2 * w:3 * w]).astype(BF16)
        mo_ref[0, rs, :] = _sigmoid(_dot(u, wm_ref[:, 3 * w:4 * w])).astype(BF16)
        gts = _dot(u, wg_ref[...]) + bg_ref[...]
        gts_ref[rs, :] = jnp.where(lane_s < nd, jnp.where(real, gts, NEG),
                                   jnp.where(real & (lane_s < 2 * nd), _log_sigmoid(gts), 0.0))

    rown = lax.broadcasted_iota(jnp.int32, (2 * nd, tr), 0)
    coln = lax.broadcasted_iota(jnp.int32, (1, tr), 1) + i * tr
    realr = coln >= PAD
    gtr = _dot_nt(wgt_ref[...], u_ref[...]) + bgt_ref[...]
    gtr = jnp.where(rown < nd, jnp.where(realr, gtr, NEG),
                    jnp.where(realr, _log_sigmoid(gtr), 0.0))
    ci = lax.broadcasted_iota(jnp.int32, (CHUNK, CHUNK), 0)
    cj = lax.broadcasted_iota(jnp.int32, (CHUNK, CHUNK), 1)
    causal = ci >= cj
    anti = ci <= cj
    low = causal.astype(BF16)
    upp = anti.astype(BF16)
    lane_c = lax.broadcasted_iota(jnp.int32, (CHUNK, LANES), 1)
    lane_1 = lax.broadcasted_iota(jnp.int32, (1, LANES), 1)
    rown_c = lax.broadcasted_iota(jnp.int32, (2 * nd, CHUNK), 0)
    for c in range(tr // CHUNK):
        rs = slice(c * CHUNK, (c + 1) * CHUNK)
        realc = lax.broadcasted_iota(jnp.int32, (CHUNK, 1), 0) + (i * tr + c * CHUNK) >= PAD
        conv = cb_ref[...] + jnp.zeros((CHUNK, 2 * w), F32)
        for k in range(CONV_K):
            off = HALO + c * CHUNK + k - CONV_K // 2
            conv = conv + cw_ref[k:k + 1, :] * ext_ref[off:off + CHUNK, :]
        qk = jnp.where(realc, _silu(conv), 0.0)
        mq_ref[0, rs, :] = (qk[:, 0:w] * (DH ** -0.5)).astype(BF16)
        mk_ref[0, rs, :] = qk[:, w:2 * w].astype(BF16)
        kconv = qk[:, w:2 * w]
        g = gts_ref[rs, :]
        lf = jnp.where(lane_c >= nd, g, 0.0)
        pre = _dot3_left(low, lf)
        suf = _dot3_left(upp, lf)
        bc = pltpu.roll(jnp.where(lane_c < nd + HEADS, pre, suf), LANES - nd, 1)
        blast = jnp.where(lane_1 < HEADS, bc[CHUNK - 1:CHUNK], bc[0:1])
        log_u = blast - bc + g
        a = jnp.max(log_u, axis=0, keepdims=True)
        wu = jnp.exp(log_u - a)
        gl = gtr[:, rs]
        lfr = jnp.where(rown_c >= nd, gl, 0.0)
        prer = _dot3_right(lfr, upp)
        sufr = _dot3_right(lfr, low)
        rr = gl[0:nd] - jnp.concatenate([prer[nd:nd + HEADS], sufr[nd + HEADS:2 * nd]], axis=0)
        for hd in range(HEADS):
            hs = slice(hd * DH, (hd + 1) * DH)
            cols = []
            for dr, (msk, kw_ref) in enumerate(((causal, kwf_ref), (anti, kwb_ref))):
                i8 = dr * HEADS + hd
                rmax = jnp.max(jnp.where(msk, rr[i8:i8 + 1, :], NEG), axis=1, keepdims=True)
                cols += [rmax, bc[:, i8:i8 + 1]]
                kw_ref[0, rs, hs] = (kconv[:, hs] * wu[:, i8:i8 + 1]).astype(BF16)
            gc_ref[0, hd, rs, :] = jnp.where(
                lane_c == 0, cols[0], jnp.where(lane_c == 1, cols[1], jnp.where(lane_c == 2, cols[2], cols[3])))
            scal = [jnp.broadcast_to(x[:, j:j + 1], (1, CHUNK))
                    for j in (hd, HEADS + hd) for x in (a, blast)]
            gr_ref[0, hd, :, rs] = jnp.concatenate(
                [rr[hd:hd + 1], rr[HEADS + hd:HEADS + hd + 1]] + scal + [jnp.zeros((2, CHUNK), F32)], axis=0)


def _in_proj(x, head, norm_g, w_in, b_gates, conv_w, conv_b, cosf, sinf):
    b, s, d = x.shape
    t = CHUNK + s
    tr = _row_tile(t)
    n_tiles = t // tr
    w = HEADS * DH
    wr = w_in[:, 0:4 * w].astype(BF16)
    wm = w_in[:, 4 * w:8 * w].astype(BF16)
    wg = jnp.pad(w_in[:, 8 * w:], ((0, 0), (0, LANES - 4 * HEADS))).astype(BF16)
    wgt = w_in[:, 8 * w:].T.astype(BF16)
    bg = jnp.pad(b_gates, (0, LANES - 4 * HEADS)).reshape(1, LANES)
    bgt = b_gates.reshape(4 * HEADS, 1)
    hb = tr // HALO
    pad_h = CHUNK // HALO
    last_h = s // HALO - 1
    seq = lambda bi, i: (bi, i, 0)
    const2 = lambda bi, i: (0, 0)
    out_bf = jax.ShapeDtypeStruct((b, t, w), BF16)
    kern = functools.partial(_in_proj_kernel, tr=tr, n_tiles=n_tiles)
    return pl.pallas_call(
        kern,
        grid=(b, n_tiles),
        in_specs=_x_specs(tr, d) + [
            pl.BlockSpec((CHUNK, d), const2),
            pl.BlockSpec((1, HALO, d), lambda bi, i: (bi, jnp.maximum(i * hb - pad_h - 1, 0), 0)),
            pl.BlockSpec((1, HALO, d), lambda bi, i: (bi, jnp.minimum((i + 1) * hb - pad_h, last_h), 0)),
            pl.BlockSpec((1, d), const2),
            pl.BlockSpec((d, 4 * w), const2),
            pl.BlockSpec((d, 4 * w), const2),
            pl.BlockSpec((d, LANES), const2),
            pl.BlockSpec((4 * HEADS, d), const2),
            pl.BlockSpec((1, LANES), const2),
            pl.BlockSpec((4 * HEADS, 1), const2),
            pl.BlockSpec((CONV_K, 2 * w), const2),
            pl.BlockSpec((1, 2 * w), const2),
            pl.BlockSpec((tr, DH), lambda bi, i: (i, 0)),
            pl.BlockSpec((tr, DH), lambda bi, i: (i, 0)),
        ],
        out_specs=[pl.BlockSpec((1, tr, w), seq)] * 10 + [
            pl.BlockSpec((1, HEADS, tr, LANES), lambda bi, i: (bi, 0, i, 0)),
            pl.BlockSpec((1, HEADS, 8, tr), lambda bi, i: (bi, 0, 0, i)),
        ],
        out_shape=[out_bf] * 10 + [
            jax.ShapeDtypeStruct((b, HEADS, t, LANES), F32),
            jax.ShapeDtypeStruct((b, HEADS, 8, t), F32),
        ],
        scratch_shapes=[pltpu.VMEM((tr + 2 * HALO, 2 * w), F32), pltpu.VMEM((tr, d), BF16),
                        pltpu.VMEM((tr, LANES), F32), pltpu.VMEM((tr, d), F32)],
        compiler_params=_cparams(("parallel", "arbitrary")),
        name="in_proj",
    )(*([x] * (tr // CHUNK)), head, x, x, norm_g.reshape(1, d), wr, wm, wg, wgt, bg, bgt,
      conv_w, conv_b.reshape(1, 2 * w), cosf, sinf)


def _retention_kernel(lg_ref, q_ref, k_ref, v_ref, o_ref, of_ref, ob_ref, *, n_chunks):
    hd = pl.program_id(1)
    lgf = lg_ref[0, hd]
    lgb = lg_ref[1, hd]
    li = lax.broadcasted_iota(jnp.int32, (CHUNK, CHUNK), 0)
    mi = lax.broadcasted_iota(jnp.int32, (CHUNK, CHUNK), 1)
    diff = (li - mi).astype(F32)
    decay = (jnp.where(diff >= 0, jnp.exp(lgf * jnp.maximum(diff, 0.0)), 0.0)
             + jnp.where(diff <= 0, jnp.exp(lgb * jnp.maximum(-diff, 0.0)), 0.0))
    lcol = lax.broadcasted_iota(jnp.int32, (CHUNK, 1), 0).astype(F32)
    one = jnp.ones((1, 1), F32)
    zeta_f = jnp.exp(lgf * (CHUNK - 1.0 - lcol))
    xi_f = jnp.exp(lgf * (lcol + 1.0))
    g_f = jnp.exp(lgf * CHUNK * one)
    zeta_b = jnp.exp(lgb * lcol)
    xi_b = jnp.exp(lgb * (CHUNK - lcol))
    g_b = jnp.exp(lgb * CHUNK * one)

    def fwd(c, state):
        rows = pl.ds(pl.multiple_of(c * CHUNK, CHUNK), CHUNK)
        q = q_ref[0, rows, :]
        k = k_ref[0, rows, :]
        v = v_ref[0, rows, :]
        s = _dot_nt(q, k) * decay
        intra = _dot(s.astype(BF16), v)
        cross = _dot(q, state.astype(BF16)) * xi_f
        of_ref[rows, :] = intra + cross
        kz = (k.astype(F32) * zeta_f).astype(BF16)
        return g_f * state + _dot_tn(kz, v)

    def bwd(j, state):
        c = n_chunks - 1 - j
        rows = pl.ds(pl.multiple_of(c * CHUNK, CHUNK), CHUNK)
        q = q_ref[0, rows, :]
        k = k_ref[0, rows, :]
        v = v_ref[0, rows, :]
        ob_ref[rows, :] = _dot(q, state.astype(BF16)) * xi_b
        kz = (k.astype(F32) * zeta_b).astype(BF16)
        return g_b * state + _dot_tn(kz, v)

    zero = jnp.zeros((DH, DH), F32)
    lax.fori_loop(0, n_chunks, lambda j, st: (fwd(j, st[0]), bwd(j, st[1])), (zero, zero),
                  unroll=_chunk_unroll(n_chunks))
    o_ref[0] = (of_ref[...] + ob_ref[...]).astype(BF16)


def _retention(lg, rq, rk, rv):
    b, t, _ = rq.shape
    spec = pl.BlockSpec((1, t, DH), lambda bi, hi, lg_ref: (bi, 0, hi))
    return pl.pallas_call(
        functools.partial(_retention_kernel, n_chunks=t // CHUNK),
        grid_spec=pltpu.PrefetchScalarGridSpec(
            num_scalar_prefetch=1,
            grid=(b, HEADS),
            in_specs=[spec, spec, spec],
            out_specs=spec,
            scratch_shapes=[pltpu.VMEM((t, DH), F32)] * 2,
        ),
        out_shape=jax.ShapeDtypeStruct((b, t, HEADS * DH), BF16),
        compiler_params=_cparams(("parallel", "arbitrary")),
        name="retention",
    )(lg, rq, rk, rv)


def _mlstm_kernel(q_ref, k_ref, v_ref, kwf_ref, kwb_ref, gc_ref, gr_ref, o_ref, of_ref, ob_ref, *, n_chunks):
    li = lax.broadcasted_iota(jnp.int32, (CHUNK, CHUNK), 0)
    mi = lax.broadcasted_iota(jnp.int32, (CHUNK, CHUNK), 1)
    causal = li >= mi
    anti = li <= mi

    ones = jnp.ones((CHUNK, DH), BF16)

    def step(c, carry, backward):
        st, m_st = carry
        rows = pl.ds(pl.multiple_of(c * CHUNK, CHUNK), CHUNK)
        q = q_ref[0, rows, :]
        k = k_ref[0, rows, :]
        v_aug = jnp.concatenate([v_ref[0, rows, :], ones], axis=1)
        gcol = gc_ref[0, 0, rows, :]
        grow = gr_ref[0, 0, :, rows]
        if backward:
            kw = kwb_ref[0, rows, :]
            r_max, bc = gcol[:, 2:3], gcol[:, 3:4]
            r_row, a, b_last = grow[1:2, :], grow[4:5, 0:1], grow[5:6, 0:1]
            mask = anti
        else:
            kw = kwf_ref[0, rows, :]
            r_max, bc = gcol[:, 0:1], gcol[:, 1:2]
            r_row, a, b_last = grow[0:1, :], grow[2:3, 0:1], grow[3:4, 0:1]
            mask = causal
        mm = jnp.maximum(jnp.broadcast_to(r_max, (CHUNK, DH)), m_st)
        s = _dot_nt(q, k) * jnp.where(mask, jnp.exp(r_row - mm), 0.0)
        w_inter = jnp.exp(m_st - mm)
        w2 = jnp.concatenate([w_inter, w_inter], axis=1)
        tot = _dot(s.astype(BF16), v_aug) + _dot(q, st.astype(BF16)) * w2
        floor = jnp.exp(-(jnp.broadcast_to(bc, (CHUNK, DH)) + mm))
        hval = tot[:, 0:DH] / jnp.maximum(jnp.abs(tot[:, DH:2 * DH]), floor)
        if backward:
            ob_ref[rows, :] = hval
        else:
            of_ref[rows, :] = hval
        upd = _dot_tn(kw, v_aug)
        m_new = jnp.maximum(b_last + m_st, a)
        f = jnp.exp(b_last + m_st - m_new)
        g = jnp.exp(a - m_new)
        return f * st + g * upd, m_new

    def both(j, carry):
        return step(j, carry[0], False), step(n_chunks - 1 - j, carry[1], True)

    init = (jnp.zeros((DH, 2 * DH), F32), jnp.zeros((1, 1), F32))
    lax.fori_loop(0, n_chunks, both, (init, init), unroll=_chunk_unroll(n_chunks))
    o_ref[0] = (of_ref[...] + ob_ref[...]).astype(BF16)


def _mlstm(mq, mk, mv, kwf, kwb, gcol, grow):
    b, t, _ = mq.shape
    spec = pl.BlockSpec((1, t, DH), lambda bi, hi: (bi, 0, hi))
    return pl.pallas_call(
        functools.partial(_mlstm_kernel, n_chunks=t // CHUNK),
        grid=(b, HEADS),
        in_specs=[spec, spec, spec, spec, spec,
                  pl.BlockSpec((1, 1, t, LANES), lambda bi, hi: (bi, hi, 0, 0)),
                  pl.BlockSpec((1, 1, 8, t), lambda bi, hi: (bi, hi, 0, 0))],
        out_specs=spec,
        out_shape=jax.ShapeDtypeStruct((b, t, HEADS * DH), BF16),
        scratch_shapes=[pltpu.VMEM((t, DH), F32)] * 2,
        compiler_params=_cparams(("parallel", "arbitrary")),
        name="mlstm",
    )(mq, mk, mv, kwf, kwb, gcol, grow)


def _head_norm(y, g):
    outs = []
    for hd in range(HEADS):
        yh = y[:, hd * DH:(hd + 1) * DH]
        mu = jnp.mean(yh, axis=1, keepdims=True)
        yc = yh - mu
        var = jnp.mean(yc * yc, axis=1, keepdims=True)
        outs.append(yc * lax.rsqrt(var + EPS))
    return jnp.concatenate(outs, axis=1) * g


def _out_proj_kernel(*refs, tr):
    nb = tr // CHUNK
    x_refs, refs = refs[:nb], refs[nb:]
    (head_ref, ret_ref, hm_ref, rg_ref, mo_ref, rgn_ref, mgn_ref, wo_ref,
     fg_ref, wrt_ref, h1_ref, u2_ref, aff_ref) = refs
    i = pl.program_id(1)
    row = lax.broadcasted_iota(jnp.int32, (tr, 1), 0) + i * tr
    real = row >= PAD
    w = HEADS * DH
    y_ret = _head_norm(ret_ref[0].astype(F32), rgn_ref[...]) * rg_ref[0].astype(F32)
    y_m = _head_norm(mo_ref[0].astype(F32) * hm_ref[0].astype(F32), mgn_ref[...])
    y_ret = jnp.where(real, y_ret, 0.0).astype(BF16)
    y_m = jnp.where(real, y_m, 0.0).astype(BF16)
    h = jnp.concatenate(_load_rows(x_refs, head_ref, i), axis=0)
    h1 = h + _dot(y_ret, wo_ref[0:w, :]) + _dot(y_m, wo_ref[w:2 * w, :])
    h1_ref[0] = h1
    u2 = _rms(h1, fg_ref[...])
    _to_segments(u2_ref, _pack_pairs(u2))
    logits = _dot_nt(wrt_ref[...], u2.astype(BF16))
    mx = jnp.max(logits, axis=0, keepdims=True)
    ex = jnp.exp(logits - mx)
    aff = ex / jnp.sum(ex, axis=0, keepdims=True)
    coln = lax.broadcasted_iota(jnp.int32, (1, tr), 1) + i * tr
    aff_ref[0] = jnp.where(coln >= PAD, aff, -1.0)


def _out_proj(x, head, ret, hm, rgs, mos, ret_gn_g, mlstm_gn_g, w_out, ffn_norm_g, w_router):
    b, s, d = x.shape
    t = CHUNK + s
    tr = _row_tile(t)
    n_tiles = t // tr
    nseg = d // 2 // LANES
    w = HEADS * DH
    seq = lambda bi, i: (bi, i, 0)
    const2 = lambda bi, i: (0, 0)
    return pl.pallas_call(
        functools.partial(_out_proj_kernel, tr=tr),
        grid=(b, n_tiles),
        in_specs=_x_specs(tr, d) + [
            pl.BlockSpec((CHUNK, d), const2),
            pl.BlockSpec((1, tr, w), seq),
            pl.BlockSpec((1, tr, w), seq),
            pl.BlockSpec((1, tr, w), seq),
            pl.BlockSpec((1, tr, w), seq),
            pl.BlockSpec((1, w), const2),
            pl.BlockSpec((1, w), const2),
            pl.BlockSpec((2 * w, d), const2),
            pl.BlockSpec((1, d), const2),
            pl.BlockSpec((N_EXPERTS, d), const2),
        ],
        out_specs=[
            pl.BlockSpec((1, tr, d), seq),
            pl.BlockSpec((tr // SUB, nseg, SUB, LANES), lambda bi, i: (bi * n_tiles + i, 0, 0, 0)),
            pl.BlockSpec((1, N_EXPERTS, tr), lambda bi, i: (bi, 0, i)),
        ],
        out_shape=[
            jax.ShapeDtypeStruct((b, t, d), F32),
            jax.ShapeDtypeStruct((b * t // SUB, nseg, SUB, LANES), jnp.int32),
            jax.ShapeDtypeStruct((b, N_EXPERTS, t), F32),
        ],
        compiler_params=_cparams(("parallel", "arbitrary")),
        name="out_proj",
    )(*([x] * (tr // CHUNK)), head, ret, hm, rgs, mos, ret_gn_g.reshape(1, w), mlstm_gn_g.reshape(1, w),
      w_out.astype(BF16), ffn_norm_g.reshape(1, d), w_router.T.astype(BF16))


def _select_kernel(aff_ref, pm_ref, dv_ref, cnt_ref, bt_ref, mc_ref, sel_ref, *, t, cap, tr):
    e_n = N_EXPERTS
    nblk = t // LANES
    bits = pltpu.bitcast(aff_ref[0], jnp.int32)
    capf = float(cap)

    def search(_, lohi):
        lo, hi = lohi
        mid = lo + ((hi - lo + 1) >> 1)
        cnt = jnp.sum((bits >= mid).astype(F32), axis=1, keepdims=True)
        ok = cnt >= capf
        return jnp.where(ok, mid, lo), jnp.where(ok, hi, mid - 1)

    lo0 = jnp.zeros((e_n, 1), jnp.int32)
    hi0 = jnp.full((e_n, 1), 0x3F800000, jnp.int32)
    thr, _ = lax.fori_loop(0, 31, search, (lo0, hi0))
    need = capf - jnp.sum((bits > thr).astype(F32), axis=1, keepdims=True)

    ci = lax.broadcasted_iota(jnp.int32, (LANES, LANES), 0)
    cj = lax.broadcasted_iota(jnp.int32, (LANES, LANES), 1)
    upp = (ci <= cj).astype(BF16)
    ei = lax.broadcasted_iota(jnp.int32, (e_n, e_n), 0)
    ej = lax.broadcasted_iota(jnp.int32, (e_n, e_n), 1)
    strict = (ej < ei).astype(BF16)

    carry = jnp.zeros((e_n, 1), F32)
    for j in range(nblk):
        sl = slice(j * LANES, (j + 1) * LANES)
        bb = bits[:, sl]
        eqf = (bb == thr).astype(F32)
        ceq = _dot(eqf.astype(BF16), upp) + carry
        carry = ceq[:, LANES - 1:LANES]
        sel_ref[:, sl] = jnp.where(bb > thr, 1.0, jnp.where(ceq <= need, eqf, 0.0))

    carry = jnp.zeros((e_n, 1), F32)
    tok = lax.broadcasted_iota(jnp.int32, (1, LANES), 1).astype(F32)
    lane_b = lax.broadcasted_iota(jnp.int32, (e_n, LANES), 1)
    btot = jnp.zeros((e_n, LANES), F32)
    tile_max = jnp.zeros((SUB, LANES), F32)
    lane_t = lax.broadcasted_iota(jnp.int32, (SUB, LANES), 1)
    for j in range(nblk):
        sl = slice(j * LANES, (j + 1) * LANES)
        selb = sel_ref[:, sl]
        selbf = selb.astype(BF16)
        pin = _dot(selbf, upp) + carry
        carry = pin[:, LANES - 1:LANES]
        btot = jnp.where(lane_b == j, carry, btot)
        pm_ref[0, :, sl] = jnp.where(selb > 0, pin, 0.0)
        rank = _dot(strict, selbf)
        dv_ref[0, :, sl] = (tok + float(j * LANES)) * float(N_EXPERTS) + rank
        cnt_row = jnp.sum(selb, axis=0, keepdims=True)
        cnt_ref[0, sl, :] = jnp.broadcast_to(cnt_row, (LANES, LANES)).T
        tile_max = jnp.where(lane_t == (j * LANES) // tr,
                             jnp.maximum(tile_max, jnp.max(cnt_row, axis=1, keepdims=True)), tile_max)
    bt_ref[0] = jnp.where(lane_b < nblk, btot, carry).astype(jnp.int32)
    mc_ref[0] = tile_max.astype(jnp.int32)


def _select(aff, cap):
    b, e_n, t = aff.shape
    tr = _row_tile(t)
    assert t // tr <= LANES
    spec = pl.BlockSpec((1, e_n, t), lambda bi: (bi, 0, 0))
    return pl.pallas_call(
        functools.partial(_select_kernel, t=t, cap=cap, tr=tr),
        grid=(b,),
        in_specs=[spec],
        out_specs=[
            spec,
            spec,
            pl.BlockSpec((1, t, LANES), lambda bi: (bi, 0, 0)),
            pl.BlockSpec((1, e_n, LANES), lambda bi: (bi, 0, 0)),
            pl.BlockSpec((1, SUB, LANES), lambda bi: (bi, 0, 0)),
        ],
        out_shape=[
            jax.ShapeDtypeStruct((b, e_n, t), F32),
            jax.ShapeDtypeStruct((b, e_n, t), F32),
            jax.ShapeDtypeStruct((b, t, LANES), F32),
            jax.ShapeDtypeStruct((b, e_n, LANES), jnp.int32),
            jax.ShapeDtypeStruct((b, SUB, LANES), jnp.int32),
        ],
        scratch_shapes=[pltpu.VMEM((e_n, t), F32)],
        compiler_params=_cparams(("parallel",)),
        name="select",
    )(aff)


def _compact_kernel(bt_ref, pm_ref, dv_ref, aff_ref, src_ref, dst_ref, gate_ref, accd_ref, accg_ref,
                    *, t, cap, c_pad, nct, nseg, batch):
    bi = batch
    e_n = N_EXPERTS
    nblk = t // LANES
    lane_e = lax.broadcasted_iota(jnp.int32, (LANES, LANES), 1)
    slot_l = lax.broadcasted_iota(jnp.int32, (1, LANES), 1)

    spr = LANES // nseg
    sh_spr = spr.bit_length() - 1
    sh_grp = (SUB * nseg).bit_length() - 1
    sub_i = lax.broadcasted_iota(jnp.int32, (SUB, LANES), 0)
    lane_i = lax.broadcasted_iota(jnp.int32, (SUB, LANES), 1)
    pick = (lane_i >> sh_spr) == sub_i
    pl_i = lax.broadcasted_iota(jnp.int32, (LANES, LANES), 0)
    pl_j = lax.broadcasted_iota(jnp.int32, (LANES, LANES), 1)
    perm = ((pl_i & (spr - 1)) == ((pl_j >> sh_grp) << 3) + (pl_j & (SUB - 1))).astype(BF16)
    seg_j = (lane_i & (SUB * nseg - 1)) >> 3

    def segment_tile(rows):
        spread = _dot3_right(jnp.where(pick, rows.astype(F32), 0.0), perm).astype(jnp.int32)
        return (((spread >> 3) * nseg + seg_j) << 3) + (spread & (SUB - 1))

    def expert(e):
        base = e * LANES

        def ctile(ct, j0):
            c0 = pl.multiple_of(ct * LANES, LANES)
            j_lo = lax.while_loop(
                lambda j: (j < nblk) & (bt_ref[base + jnp.minimum(j, nblk - 1)] <= c0),
                lambda j: j + 1, j0)
            j_hi = lax.while_loop(
                lambda j: (j < nblk) & (bt_ref[base + jnp.minimum(j, nblk - 1)] < c0 + LANES),
                lambda j: j + 1, j_lo)
            j_end = jnp.minimum(j_hi + 1, nblk)
            want = (lax.broadcasted_iota(jnp.int32, (LANES, 1), 0) + (c0 + 1)).astype(F32)

            def blk(j, acc):
                acc_d, acc_g = acc
                cols = pl.ds(pl.multiple_of(j * LANES, LANES), LANES)
                hit = pm_ref[0, e:e + 1, cols] == want
                acc_d = acc_d + jnp.where(hit, dv_ref[0, e:e + 1, cols], 0.0)
                acc_g = acc_g + jnp.where(hit, aff_ref[0, e:e + 1, cols], 0.0)
                return acc_d, acc_g

            zero = jnp.zeros((LANES, LANES), F32)
            acc_d, acc_g = lax.fori_loop(j_lo, j_end, blk, (zero, zero))
            accd_ref[e * nct + ct] = acc_d
            accg_ref[e * nct + ct] = acc_g
            return j_lo

        lax.fori_loop(0, nct, ctile, 0)

    for e in range(e_n):
        expert(e)

    def finish(ct, _):
        c0 = pl.multiple_of(ct * LANES, LANES)
        p = slot_l + (c0 - cap)
        is_pad = p >= 0
        gate_tile = jnp.zeros((LANES, LANES), F32)
        for e in range(e_n):
            v = jnp.sum(accd_ref[e * nct + ct].T, axis=0, keepdims=True).astype(jnp.int32)
            tok = v >> 4
            rank = v & (N_EXPERTS - 1)
            src = bi * t + jnp.where(is_pad, 0, tok)
            dst = jnp.where(is_pad, N_EXPERTS * t + e * (c_pad - cap) + p, rank * t + tok)
            out_rows = pl.ds(pl.multiple_of((e * nct + ct) * SUB, SUB), SUB)
            src_ref[0, out_rows, :] = segment_tile(src)
            dst_ref[0, out_rows, :] = segment_tile(dst)
            gcol = jnp.sum(accg_ref[e * nct + ct], axis=1, keepdims=True)
            gate_tile = jnp.where(lane_e == e, gcol, gate_tile)
        gate_ref[0, pl.ds(c0, LANES), :] = gate_tile
        return 0

    lax.fori_loop(0, nct, finish, 0)


def _compact(btot, pm, dv, aff, cap, c_pad, nseg, batch):
    _, e_n, t = aff.shape
    assert nseg in (1, 2, 4, 8)
    nct = -(-c_pad // LANES)
    tile_rows = e_n * nct * SUB
    spec = pl.BlockSpec((1, e_n, t), lambda i, bt: (0, 0, 0))
    ospec = pl.BlockSpec((1, tile_rows, LANES), lambda i, bt: (0, 0, 0))
    src, dst, gate = pl.pallas_call(
        functools.partial(_compact_kernel, t=t, cap=cap, c_pad=c_pad, nct=nct, nseg=nseg, batch=batch),
        grid_spec=pltpu.PrefetchScalarGridSpec(
            num_scalar_prefetch=1,
            grid=(1,),
            in_specs=[spec, spec, spec],
            out_specs=[ospec, ospec, pl.BlockSpec((1, nct * LANES, LANES), lambda i, bt: (0, 0, 0))],
            scratch_shapes=[pltpu.VMEM((e_n * nct, LANES, LANES), F32)] * 2,
        ),
        out_shape=[
            jax.ShapeDtypeStruct((1, tile_rows, LANES), jnp.int32),
            jax.ShapeDtypeStruct((1, tile_rows, LANES), jnp.int32),
            jax.ShapeDtypeStruct((1, nct * LANES, LANES), F32),
        ],
        compiler_params=_cparams(("arbitrary",)),
        name="compact",
    )(btot.reshape(e_n * LANES), pm, dv, aff)
    ids = lambda a: a.reshape(e_n, nct, SUB, LANES)[:, :, :nseg].reshape(e_n, -1)[:, :c_pad * nseg].reshape(-1)
    return ids(src), ids(dst), gate


def _sc_mesh():
    return plsc.VectorSubcoreMesh(core_axis_name="core", subcore_axis_name="subcore")


def _row_gather(table, idx):
    m = idx.shape[0]
    d = table.shape[1]

    @functools.partial(pl.kernel, out_type=jax.ShapeDtypeStruct((m, d), table.dtype),
                       mesh=_sc_mesh(), scratch_types=[])
    def gather_kernel(x_hbm, i_hbm, o_hbm):
        def body(i_vmem, o_vmem):
            pltpu.sync_copy(x_hbm.at[i_vmem.at[0]], o_vmem)

        pltpu.emit_pipeline(
            body,
            grid=(m // SC_WINDOW,),
            in_specs=[pl.BlockSpec((1, SC_WINDOW), lambda i: (0, i))],
            out_specs=[pl.BlockSpec((SC_WINDOW, d), lambda i: (i, 0))],
            core_axis_name=("core", "subcore"),
            dimension_semantics=(pltpu.PARALLEL,),
        )(i_hbm, o_hbm)

    return gather_kernel(table, idx.reshape(1, m))


def _row_scatter(rows, idx, n_out):
    m, d = rows.shape

    @functools.partial(pl.kernel, out_type=jax.ShapeDtypeStruct((n_out, d), rows.dtype),
                       mesh=_sc_mesh(), scratch_types=[])
    def scatter_kernel(x_hbm, i_hbm, o_hbm):
        def body(x_vmem, i_vmem):
            pltpu.sync_copy(x_vmem, o_hbm.at[i_vmem.at[0]])

        pltpu.emit_pipeline(
            body,
            grid=(m // SC_WINDOW,),
            in_specs=[pl.BlockSpec((SC_WINDOW, d), lambda i: (i, 0)),
                      pl.BlockSpec((1, SC_WINDOW), lambda i: (0, i))],
            out_specs=[],
            core_axis_name=("core", "subcore"),
            dimension_semantics=(pltpu.PARALLEL,),
        )(x_hbm, i_hbm)

    return scatter_kernel(rows, idx.reshape(1, m))


def _expert_kernel(gate_ref, x_ref, wg_ref, wu_ref, wd_ref, y_ref, xb_ref, acc_ref, *, c_pad, n_f):
    e = pl.program_id(0)
    f = pl.program_id(1)

    @pl.when(f == 0)
    def _():
        w = _from_segments(x_ref)
        xb_ref[...] = jnp.concatenate([_unpack_hi(w), _unpack_lo(w)], axis=1).astype(BF16)
        acc_ref[...] = jnp.zeros(acc_ref.shape, F32)

    wg = wg_ref[0].astype(BF16)
    wu = wu_ref[0].astype(BF16)
    wd = wd_ref[0].astype(BF16)
    half = c_pad // 2
    for r0 in (0, half):
        rows = slice(r0, r0 + half)
        x = xb_ref[rows, :]
        hid = _silu(_dot(x, wg)) * _dot(x, wu)
        part = _dot(hid.astype(BF16), wd)
        acc_ref[rows, :] = acc_ref[rows, :] + part

    @pl.when(f == n_f - 1)
    def _():
        lane = lax.broadcasted_iota(jnp.int32, (c_pad, LANES), 1)
        gcol = jnp.sum(jnp.where(lane == e, gate_ref[0], 0.0), axis=1, keepdims=True)
        _to_segments(y_ref, _pack_pairs(acc_ref[...] * gcol))


def _experts(gate, xs, w_gate, w_up, w_down):
    e_n, d, d_ff = w_gate.shape
    nseg = d // 2 // LANES
    c_pad = xs.shape[0] * SUB // e_n
    fc = 256
    n_f = d_ff // fc
    blk = pl.BlockSpec((c_pad // SUB, nseg, SUB, LANES), lambda e, f: (e, 0, 0, 0))
    return pl.pallas_call(
        functools.partial(_expert_kernel, c_pad=c_pad, n_f=n_f),
        grid=(e_n, n_f),
        in_specs=[
            pl.BlockSpec((1, c_pad, LANES), lambda e, f: (0, 0, 0)),
            blk,
            pl.BlockSpec((1, d, fc), lambda e, f: (e, 0, f)),
            pl.BlockSpec((1, d, fc), lambda e, f: (e, 0, f)),
            pl.BlockSpec((1, fc, d), lambda e, f: (e, f, 0)),
        ],
        out_specs=blk,
        out_shape=jax.ShapeDtypeStruct(xs.shape, jnp.int32),
        scratch_shapes=[pltpu.VMEM((c_pad, d), BF16), pltpu.VMEM((c_pad, d), F32)],
        compiler_params=_cparams(("parallel", "arbitrary")),
        name="experts",
    )(gate, xs, w_gate, w_up, w_down)


LAYERS_PER_STEP = 2


def _combine_kernel(mc_ref, h1_ref, fg_ref, *rest, n_steps, nb, n_tiles):
    cnt_refs = rest[:nb]
    z_refs = rest[nb:nb + nb * LAYERS_PER_STEP]
    o_hbm, acc_ref, obuf_ref, sem = rest[-4:]
    bi = pl.program_id(0)
    i = pl.program_id(1)
    j = pl.program_id(2)
    blocks = obuf_ref.shape[1] // CHUNK

    def tile_copies(b_, i_, slot, fn):
        for k in range(blocks):
            dst = pl.multiple_of((i_ * blocks - 1 + k) * CHUNK, CHUNK)
            copy = pltpu.make_async_copy(obuf_ref.at[slot, pl.ds(k * CHUNK, CHUNK), :],
                                         o_hbm.at[b_, pl.ds(dst, CHUNK), :], sem.at[slot])
            if k == 0:
                pl.when(i_ > 0)(functools.partial(fn, copy))
            else:
                fn(copy)

    def tile_before(steps):
        wrap = i < steps
        return bi - wrap.astype(jnp.int32), i - steps + jnp.where(wrap, n_tiles, 0)

    @pl.when(j == 0)
    def _():
        _to_segments(acc_ref, h1_ref[0])

    def add_layers(k, n_take):
        g, nz = z_refs[0].shape[0], z_refs[0].shape[1]
        cnt = cnt_refs[k][0].reshape(g, 1, SUB, LANES)
        hi = acc_ref[:, 0:nz]
        lo = acc_ref[:, nz:2 * nz]
        for l in range(n_take):
            take = cnt > (j * LAYERS_PER_STEP + l).astype(F32)
            w = z_refs[k * LAYERS_PER_STEP + l][...]
            hi = hi + jnp.where(take, _unpack_hi(w), 0.0)
            lo = lo + jnp.where(take, _unpack_lo(w), 0.0)
        acc_ref[:, 0:nz] = hi
        acc_ref[:, nz:2 * nz] = lo

    live = mc_ref[bi * LANES + i] - j * LAYERS_PER_STEP
    for k in range(nb):
        for n_take in range(1, LAYERS_PER_STEP + 1):
            last = n_take == LAYERS_PER_STEP
            pl.when((bi == k) & ((live >= n_take) if last else (live == n_take)))(
                functools.partial(add_layers, k, n_take))

    @pl.when(j == n_steps - 1)
    def _():
        idx = bi * n_tiles + i
        slot = idx % 2

        @pl.when(idx >= 2)
        def _():
            b2, i2 = tile_before(2)
            tile_copies(b2, i2, slot, lambda c: c.wait())

        obuf_ref[slot] = _rms(_from_segments(acc_ref), fg_ref[...])
        tile_copies(bi, i, slot, lambda c: c.start(priority=1))

        @pl.when(idx == nb * n_tiles - 1)
        def _():
            if nb * n_tiles >= 2:
                b1, i1 = tile_before(1)
                tile_copies(b1, i1, 1 - slot, lambda c: c.wait())
            tile_copies(bi, i, slot, lambda c: c.wait())


def _combine(maxcnt, h1, zs, cnts, final_norm_g):
    b, t, d = h1.shape
    n_steps = N_EXPERTS // LAYERS_PER_STEP
    tr = _row_tile(t)
    n_tiles = t // tr

    def z_map(k, l):
        def index(bi, i, j, mc):
            layer = jnp.maximum(jnp.minimum(j * LAYERS_PER_STEP + l, mc[bi * LANES + i] - 1), 0)
            return (jnp.where(bi == k, layer * n_tiles + i, 0), 0, 0, 0)
        return index

    def cnt_map(k):
        return lambda bi, i, j, mc: (0, jnp.where(bi == k, i, 0), 0)

    seg_block = (tr // SUB, d // LANES, SUB, LANES)
    z_block = (tr // SUB, d // 2 // LANES, SUB, LANES)
    z_specs = [pl.BlockSpec(z_block, z_map(k, l)) for k in range(b) for l in range(LAYERS_PER_STEP)]
    z_args = [zs[k] for k in range(b) for _ in range(LAYERS_PER_STEP)]
    return pl.pallas_call(
        functools.partial(_combine_kernel, n_steps=n_steps, nb=b, n_tiles=n_tiles),
        grid_spec=pltpu.PrefetchScalarGridSpec(
            num_scalar_prefetch=1,
            grid=(b, n_tiles, n_steps),
            in_specs=[
                pl.BlockSpec((1, tr, d), lambda bi, i, j, mc: (bi, i, 0)),
                pl.BlockSpec((1, d), lambda bi, i, j, mc: (0, 0)),
            ] + [pl.BlockSpec((1, tr, LANES), cnt_map(k)) for k in range(b)] + z_specs,
            out_specs=pl.BlockSpec(memory_space=pl.ANY),
            scratch_shapes=[pltpu.VMEM(seg_block, F32), pltpu.VMEM((2, tr, d), F32),
                            pltpu.SemaphoreType.DMA((2,))],
        ),
        out_shape=jax.ShapeDtypeStruct((b, t - CHUNK, d), F32),
        compiler_params=_cparams(("arbitrary", "arbitrary", "arbitrary")),
        name="combine",
    )(maxcnt, h1, final_norm_g.reshape(1, d), *cnts, *z_args)


def kernel(x, meta_tokens, mix_norm_g, w_in, b_gates, conv_w, conv_b, ret_decay_logit, ret_gn_g,
           mlstm_gn_g, w_out, ffn_norm_g, w_router, w_gate, w_up, w_down, final_norm_g):
    b, s, d = x.shape
    t = CHUNK + s
    n = N_META + s
    cap = 2 * n // N_EXPERTS
    nseg = d // 2 // LANES
    c_pad = next(c for c in range(-(-cap // 16) * 16, cap + 4096, 16)
                 if (N_EXPERTS * c * nseg) % (SC_WINDOW * SC_SUBCORES) == 0)

    head = jnp.concatenate([jnp.zeros((PAD, d), x.dtype), meta_tokens.astype(x.dtype)], axis=0)

    half = DH // 2
    pos = jnp.arange(t, dtype=F32) - PAD
    inv = ROPE_BASE ** (-jnp.arange(half, dtype=F32) / half)
    ang = pos[:, None] * inv[None, :]
    cosf = jnp.concatenate([jnp.cos(ang), jnp.cos(ang)], axis=1)
    sinf = jnp.concatenate([-jnp.sin(ang), jnp.sin(ang)], axis=1)

    assert mix_norm_g.shape[0] == 1, "single-layer block only"
    rq, rk, rv, rgs, mq, mk, mv, mos, kwf, kwb, gcol, grow = _in_proj(
        x, head, mix_norm_g[0], w_in[0], b_gates[0], conv_w[0], conv_b[0], cosf, sinf)
    lg = jax.nn.log_sigmoid(ret_decay_logit[0].astype(F32))
    ret = _retention(lg, rq, rk, rv)
    hm = _mlstm(mq, mk, mv, kwf, kwb, gcol, grow)
    h1, u2, aff = _out_proj(x, head, ret, hm, rgs, mos, ret_gn_g[0], mlstm_gn_g[0], w_out[0],
                            ffn_norm_g[0], w_router[0])
    tiled = lambda a: a.reshape(-1, nseg, SUB, LANES)
    flat = lambda a: a.reshape(-1, LANES)
    z_rows = -(-(N_EXPERTS * t + N_EXPERTS * (c_pad - cap)) // SUB) * SUB
    routed = []
    for bi in range(b):
        aff_b = aff[bi:bi + 1]
        pm, dv, cnt, btot, maxcnt = _select(aff_b, cap)
        src, dst, gate = _compact(btot, pm, dv, aff_b, cap, c_pad, nseg, bi)
        routed.append((_row_gather(flat(u2), src), dst, gate, cnt, maxcnt))
    zs = []
    for xs, dst, gate, _, _ in routed:
        ys = _experts(gate, tiled(xs), w_gate[0], w_up[0], w_down[0])
        zs.append(tiled(_row_scatter(flat(ys), dst, z_rows * nseg)))
    maxcnt = jnp.concatenate([r[4][:, 0, :] for r in routed], axis=0).reshape(-1)
    return _combine(maxcnt, h1, zs, [r[3] for r in routed], final_norm_g)
```
